```python
import jax
import jax.numpy as jnp
from jax import lax
import numpy as np

D_MODEL = 1024
BATCH = 16
SEQ = 2048
DEPTH = 2

CTX_LEN = 256
GRID_W = 64
N_MOD = 9
D_FF = ((8 * D_MODEL // 3 + 127) // 128) * 128
MIX_WIDTH = D_MODEL
POOL_WIDTH = D_MODEL // 4
POOL_WINDOWS = (2, 4, 8, 16)
POOL_GROUPS = len(POOL_WINDOWS)
POOL_GROUP = POOL_WIDTH // POOL_GROUPS
RET_WIDTH = D_MODEL // 2
RET_HEADS = 4
RET_HEAD_DIM = RET_WIDTH // RET_HEADS
RET_CHUNK = 128
CONV_WIDTH = D_MODEL // 4
CONV_K = 31
IN_WIDTH = POOL_WIDTH + 4 * RET_WIDTH + 2 * CONV_WIDTH
Q_OFF = POOL_WIDTH
K_OFF = Q_OFF + RET_WIDTH
V_OFF = K_OFF + RET_WIDTH
G_OFF = V_OFF + RET_WIDTH
C_OFF = G_OFF + RET_WIDTH
SPLITS = (Q_OFF, K_OFF, V_OFF, G_OFF, C_OFF)
ROPE_BASE = 10000.0
EPS = 1e-6
F32 = jnp.float32

kernel_name = 'hybrid_pool_retention_conv_dit'


def _rmsnorm(x, g):
    x32 = x.astype(F32)
    y = x32 * lax.rsqrt(jnp.mean(x32 * x32, axis=-1, keepdims=True) + EPS)
    return (y * g.astype(F32)).astype(x.dtype)


def _modulate(h, shift, scale):
    return h * (1 + scale) + shift


def _swiglu(h, w1, w3, w2):
    return (jax.nn.silu(h @ w1) * (h @ w3)) @ w2


def _multiscale_pool(p, pool_w, pool_scale):
    B, L, _ = p.shape
    p32 = p.astype(F32)
    cs = jnp.concatenate([jnp.zeros((B, 1, POOL_WIDTH), F32), jnp.cumsum(p32, axis=1)], axis=1)
    t = jnp.arange(L)
    outs = []
    for gi, w in enumerate(POOL_WINDOWS):
        lo = jnp.clip(t - w // 2, 0, L)
        hi = jnp.clip(t + w // 2, 0, L)
        csg = cs[:, :, gi * POOL_GROUP:(gi + 1) * POOL_GROUP]
        cnt = (hi - lo).astype(F32)[None, :, None]
        outs.append((csg[:, hi] - csg[:, lo]) / cnt - p32[:, :, gi * POOL_GROUP:(gi + 1) * POOL_GROUP])
    pooled = jnp.stack(outs, axis=2)
    mixed = jnp.einsum('blgc,gcd->blgd', pooled, pool_w.astype(F32)).reshape(B, L, POOL_WIDTH)
    return (mixed * pool_scale.astype(F32)).astype(p.dtype)


def _conv_module(u, dw, db, ln_g, ln_b):
    a, gt = jnp.split(u, 2, axis=-1)
    z = a * jax.nn.sigmoid(gt)
    z = lax.conv_general_dilated(z, dw[:, None, :].astype(z.dtype), window_strides=(1,),
                                 padding=[(CONV_K // 2, CONV_K // 2)],
                                 dimension_numbers=('NWC', 'WIO', 'NWC'),
                                 feature_group_count=CONV_WIDTH) + db
    z32 = z.astype(F32)
    mu = jnp.mean(z32, axis=-1, keepdims=True)
    var = jnp.mean(jnp.square(z32 - mu), axis=-1, keepdims=True)
    z32 = (z32 - mu) * lax.rsqrt(var + EPS) * ln_g.astype(F32) + ln_b.astype(F32)
    return jax.nn.silu(z32).astype(u.dtype)


def _heads(t):
    return t.reshape(t.shape[0], t.shape[1], RET_HEADS, RET_HEAD_DIM)


def _rotate(u, ang):
    u1, u2 = jnp.split(u, 2, axis=-1)
    cos = jnp.cos(ang)[None, :, None, :]
    sin = jnp.sin(ang)[None, :, None, :]
    return jnp.concatenate([u1 * cos - u2 * sin, u1 * sin + u2 * cos], axis=-1)


def _axial_rope(t, row, col):
    n_freq = RET_HEAD_DIM // 4
    inv = ROPE_BASE ** (-jnp.arange(n_freq, dtype=F32) / n_freq)
    tr, tc = jnp.split(t.astype(F32), 2, axis=-1)
    return jnp.concatenate([_rotate(tr, row[:, None] * inv[None]),
                            _rotate(tc, col[:, None] * inv[None])], axis=-1)


def _retention_dir(q, k, v, log_gamma, s0, strict):
    B, L, H, Dk = q.shape
    Dv = v.shape[-1]
    n = L // RET_CHUNK
    qc = q.astype(F32).reshape(B, n, RET_CHUNK, H, Dk)
    kc = k.astype(F32).reshape(B, n, RET_CHUNK, H, Dk)
    vc = v.astype(F32).reshape(B, n, RET_CHUNK, H, Dv)
    pos = jnp.arange(RET_CHUNK, dtype=F32)
    diff = pos[:, None] - pos[None, :]
    mask = (diff > 0) if strict else (diff >= 0)
    decay = jnp.where(mask[None], jnp.exp(jnp.maximum(diff, 0.0)[None] * log_gamma[:, None, None]), 0.0)
    scores = jnp.einsum('bnihd,bnjhd->bnhij', qc, kc) * decay[None, None]
    intra = jnp.einsum('bnhij,bnjhe->bnihe', scores, vc)
    w_k = jnp.exp((RET_CHUNK - 1 - pos)[:, None] * log_gamma[None, :])
    kv = jnp.einsum('bnjhd,jh,bnjhe->bnhde', kc, w_k, vc)
    chunk_decay = jnp.exp(RET_CHUNK * log_gamma)[None, :, None, None]

    def step(s, kv_n):
        return chunk_decay * s + kv_n, s

    s_final, s_starts = lax.scan(step, s0, jnp.moveaxis(kv, 1, 0))
    s_starts = jnp.moveaxis(s_starts, 0, 1)
    w_q = jnp.exp((pos + 1.0)[:, None] * log_gamma[None, :])
    cross = jnp.einsum('bnihd,ih,bnhde->bnihe', qc, w_q, s_starts)
    return (intra + cross).reshape(B, L, H, Dv), s_final


def _context_states(k, v, log_gamma_f, log_gamma_b):
    Lc = k.shape[1]
    pos = jnp.arange(Lc, dtype=F32)
    w_f = jnp.exp((Lc - 1 - pos)[:, None] * log_gamma_f[None, :])
    w_b = jnp.exp(pos[:, None] * log_gamma_b[None, :])
    k32 = k.astype(F32)
    v32 = v.astype(F32)
    s_f = jnp.einsum('bjhd,jh,bjhe->bhde', k32, w_f, v32)
    s_b = jnp.einsum('bjhd,jh,bjhe->bhde', k32, w_b, v32)
    return s_f, s_b


def _retention_readout(o, g, gn_g):
    B, L, H, Dv = o.shape
    mu = jnp.mean(o, axis=-1, keepdims=True)
    var = jnp.mean(jnp.square(o - mu), axis=-1, keepdims=True)
    y = ((o - mu) * lax.rsqrt(var + EPS)).reshape(B, L, H * Dv) * gn_g.astype(F32)
    return (y * jax.nn.silu(g.astype(F32))).astype(g.dtype)


def _mixer(hx, hy, w_in, w_out, pool_w, pool_scale, dec_f, dec_b, gn_g,
           conv_dw, conv_b, conv_ln_g, conv_ln_b, need_ctx_out):
    B, L, _ = hx.shape
    lg_f = jax.nn.log_sigmoid(dec_f.astype(F32))
    lg_b = jax.nn.log_sigmoid(dec_b.astype(F32))
    k_scale = RET_HEAD_DIM ** -0.5
    rows = L // GRID_W
    row = jnp.repeat(jnp.arange(rows, dtype=F32), GRID_W)
    col = jnp.tile(jnp.arange(GRID_W, dtype=F32), rows)

    if need_ctx_out:
        pool_y, qy, ky, vy, gy, conv_y = jnp.split(hy @ w_in, SPLITS, axis=-1)
        qy = _heads(qy)
        ky = _heads(ky) * k_scale
        vy = _heads(vy)
        s0 = jnp.zeros((hy.shape[0], RET_HEADS, RET_HEAD_DIM, RET_HEAD_DIM), F32)
        oy_f, s_f = _retention_dir(qy, ky, vy, lg_f, s0, False)
        oy_b, s_b = _retention_dir(jnp.flip(qy, 1), jnp.flip(ky, 1), jnp.flip(vy, 1), lg_b, s0, True)
        ret_y = _retention_readout(oy_f + jnp.flip(oy_b, 1), gy, gn_g)
        cat_y = jnp.concatenate([_multiscale_pool(pool_y, pool_w, pool_scale), ret_y,
                                 _conv_module(conv_y, conv_dw, conv_b, conv_ln_g, conv_ln_b)], axis=-1)
        out_y = cat_y @ w_out
    else:
        ky, vy = jnp.split(hy @ w_in[:, K_OFF:G_OFF], 2, axis=-1)
        s_f, s_b = _context_states(_heads(ky) * k_scale, _heads(vy), lg_f, lg_b)
        out_y = None

    pool_x, qx, kx, vx, gx, conv_x = jnp.split(hx @ w_in, SPLITS, axis=-1)
    qx = _axial_rope(_heads(qx), row, col)
    kx = _axial_rope(_heads(kx), row, col) * k_scale
    vx = _heads(vx)
    ox_f, _ = _retention_dir(qx, kx, vx, lg_f, s_f, False)
    ox_b, _ = _retention_dir(jnp.flip(qx, 1), jnp.flip(kx, 1), jnp.flip(vx, 1), lg_b, s_b, True)
    ret_x = _retention_readout(ox_f + jnp.flip(ox_b, 1), gx, gn_g)
    cat_x = jnp.concatenate([_multiscale_pool(pool_x, pool_w, pool_scale), ret_x,
                             _conv_module(conv_x, conv_dw, conv_b, conv_ln_g, conv_ln_b)], axis=-1)
    return cat_x @ w_out, out_y


def _fwd_setup_inputs(seed: int = 0) -> dict:
    key = jax.random.key(seed)
    ks = jax.random.split(key, 24)

    def nrm(k, shape, scale):
        return jax.random.normal(k, shape, F32) * scale

    decay_base = jnp.log(2.0 ** (5.0 + jnp.arange(RET_HEADS, dtype=F32)) - 1.0)
    return {
        'x': nrm(ks[0], (BATCH, SEQ, D_MODEL), 1.0),
        'c': nrm(ks[1], (BATCH, D_MODEL), 1.0),
        'ctx': nrm(ks[2], (BATCH, CTX_LEN, D_MODEL), 1.0),
        'c_ctx': nrm(ks[3], (D_MODEL,), 1.0),
        'w_mod': nrm(ks[4], (DEPTH, D_MODEL, N_MOD * D_MODEL), 0.02),
        'b_mod': nrm(ks[5], (DEPTH, N_MOD * D_MODEL), 0.02),
        'norm_g': 1.0 + nrm(ks[6], (DEPTH, 3, D_MODEL), 0.02),
        'ffn_w1': nrm(ks[7], (DEPTH, 2, D_MODEL, D_FF), D_MODEL ** -0.5),
        'ffn_w3': nrm(ks[8], (DEPTH, 2, D_MODEL, D_FF), D_MODEL ** -0.5),
        'ffn_w2': nrm(ks[9], (DEPTH, 2, D_FF, D_MODEL), D_FF ** -0.5),
        'w_in': nrm(ks[10], (DEPTH, D_MODEL, IN_WIDTH), D_MODEL ** -0.5),
        'w_out': nrm(ks[11], (DEPTH, MIX_WIDTH, D_MODEL), MIX_WIDTH ** -0.5),
        'pool_w': nrm(ks[12], (DEPTH, POOL_GROUPS, POOL_GROUP, POOL_GROUP), POOL_GROUP ** -0.5),
        'pool_scale': 1.0 + nrm(ks[13], (DEPTH, POOL_WIDTH), 0.02),
        'ret_decay_fwd': decay_base + nrm(ks[14], (DEPTH, RET_HEADS), 0.1),
        'ret_decay_bwd': decay_base + nrm(ks[15], (DEPTH, RET_HEADS), 0.1),
        'ret_gn_g': 1.0 + nrm(ks[16], (DEPTH, RET_WIDTH), 0.02),
        'conv_dw': nrm(ks[17], (DEPTH, CONV_K, CONV_WIDTH), CONV_K ** -0.5),
        'conv_b': nrm(ks[18], (DEPTH, CONV_WIDTH), 0.02),
        'conv_ln_g': 1.0 + nrm(ks[19], (DEPTH, CONV_WIDTH), 0.02),
        'conv_ln_b': nrm(ks[20], (DEPTH, CONV_WIDTH), 0.02),
        'final_g': 1.0 + nrm(ks[21], (D_MODEL,), 0.02),
    }


def _fwd_reference(x, c, ctx, c_ctx, w_mod, b_mod, norm_g, ffn_w1, ffn_w3, ffn_w2, w_in, w_out,
              pool_w, pool_scale, ret_decay_fwd, ret_decay_bwd, ret_gn_g, conv_dw, conv_b,
              conv_ln_g, conv_ln_b, final_g):
    y = ctx
    for l in range(DEPTH):
        last = l == DEPTH - 1
        mx = jnp.split((jax.nn.silu(c) @ w_mod[l] + b_mod[l])[:, None, :], N_MOD, axis=-1)
        my = jnp.split((jax.nn.silu(c_ctx) @ w_mod[l] + b_mod[l])[None, None, :], N_MOD, axis=-1)

        x = x + 0.5 * mx[2] * _swiglu(_modulate(_rmsnorm(x, norm_g[l, 0]), mx[0], mx[1]),
                                      ffn_w1[l, 0], ffn_w3[l, 0], ffn_w2[l, 0])
        y = y + 0.5 * my[2] * _swiglu(_modulate(_rmsnorm(y, norm_g[l, 0]), my[0], my[1]),
                                      ffn_w1[l, 0], ffn_w3[l, 0], ffn_w2[l, 0])

        hx = _modulate(_rmsnorm(x, norm_g[l, 1]), mx[3], mx[4])
        hy = _modulate(_rmsnorm(y, norm_g[l, 1]), my[3], my[4])
        ox, oy = _mixer(hx, hy, w_in[l], w_out[l], pool_w[l], pool_scale[l], ret_decay_fwd[l],
                        ret_decay_bwd[l], ret_gn_g[l], conv_dw[l], conv_b[l], conv_ln_g[l],
                        conv_ln_b[l], not last)
        x = x + mx[5] * ox
        if not last:
            y = y + my[5] * oy

        x = x + 0.5 * mx[8] * _swiglu(_modulate(_rmsnorm(x, norm_g[l, 2]), mx[6], mx[7]),
                                      ffn_w1[l, 1], ffn_w3[l, 1], ffn_w2[l, 1])
        if not last:
            y = y + 0.5 * my[8] * _swiglu(_modulate(_rmsnorm(y, norm_g[l, 2]), my[6], my[7]),
                                          ffn_w1[l, 1], ffn_w3[l, 1], ffn_w2[l, 1])
    return _rmsnorm(x, final_g)


import jax as _jax
import jax.numpy as _jnp

TWIN_FORMAT = 'train_step'
FWD_PARAMS = ['x', 'c', 'ctx', 'c_ctx', 'w_mod', 'b_mod', 'norm_g', 'ffn_w1', 'ffn_w3', 'ffn_w2', 'w_in', 'w_out', 'pool_w', 'pool_scale', 'ret_decay_fwd', 'ret_decay_bwd', 'ret_gn_g', 'conv_dw', 'conv_b', 'conv_ln_g', 'conv_ln_b', 'final_g']
TWIN_WEIGHTS = ['c_ctx', 'w_mod', 'b_mod', 'norm_g', 'ffn_w1', 'ffn_w3', 'ffn_w2', 'w_in', 'w_out', 'pool_w', 'pool_scale', 'ret_decay_fwd', 'ret_decay_bwd', 'ret_gn_g', 'conv_dw', 'conv_b', 'conv_ln_g', 'conv_ln_b', 'final_g']
TWIN_DIFF_INPUT = 'x'
TWIN_INPUTS = ['x', 'c', 'ctx', 'c_ctx', 'w_mod', 'b_mod', 'norm_g', 'ffn_w1', 'ffn_w3', 'ffn_w2', 'w_in', 'w_out', 'pool_w', 'pool_scale', 'ret_decay_fwd', 'ret_decay_bwd', 'ret_gn_g', 'conv_dw', 'conv_b', 'conv_ln_g', 'conv_ln_b', 'final_g', 'loss_target', 'm_c_ctx', 'm_w_mod', 'm_b_mod', 'm_norm_g', 'm_ffn_w1', 'm_ffn_w3', 'm_ffn_w2', 'm_w_in', 'm_w_out', 'm_pool_w', 'm_pool_scale', 'm_ret_decay_fwd', 'm_ret_decay_bwd', 'm_ret_gn_g', 'm_conv_dw', 'm_conv_b', 'm_conv_ln_g', 'm_conv_ln_b', 'm_final_g', 'v_c_ctx', 'v_w_mod', 'v_b_mod', 'v_norm_g', 'v_ffn_w1', 'v_ffn_w3', 'v_ffn_w2', 'v_w_in', 'v_w_out', 'v_pool_w', 'v_pool_scale', 'v_ret_decay_fwd', 'v_ret_decay_bwd', 'v_ret_gn_g', 'v_conv_dw', 'v_conv_b', 'v_conv_ln_g', 'v_conv_ln_b', 'v_final_g']
TWIN_OUTPUTS = ['loss', 'grad_x', 'grad_c_ctx', 'grad_w_mod', 'grad_b_mod', 'grad_norm_g', 'grad_ffn_w1', 'grad_ffn_w3', 'grad_ffn_w2', 'grad_w_in', 'grad_w_out', 'grad_pool_w', 'grad_pool_scale', 'grad_ret_decay_fwd', 'grad_ret_decay_bwd', 'grad_ret_gn_g', 'grad_conv_dw', 'grad_conv_b', 'grad_conv_ln_g', 'grad_conv_ln_b', 'grad_final_g', 'delta_c_ctx', 'delta_w_mod', 'delta_b_mod', 'delta_norm_g', 'delta_ffn_w1', 'delta_ffn_w3', 'delta_ffn_w2', 'delta_w_in', 'delta_w_out', 'delta_pool_w', 'delta_pool_scale', 'delta_ret_decay_fwd', 'delta_ret_decay_bwd', 'delta_ret_gn_g', 'delta_conv_dw', 'delta_conv_b', 'delta_conv_ln_g', 'delta_conv_ln_b', 'delta_final_g', 'new_m_c_ctx', 'new_m_w_mod', 'new_m_b_mod', 'new_m_norm_g', 'new_m_ffn_w1', 'new_m_ffn_w3', 'new_m_ffn_w2', 'new_m_w_in', 'new_m_w_out', 'new_m_pool_w', 'new_m_pool_scale', 'new_m_ret_decay_fwd', 'new_m_ret_decay_bwd', 'new_m_ret_gn_g', 'new_m_conv_dw', 'new_m_conv_b', 'new_m_conv_ln_g', 'new_m_conv_ln_b', 'new_m_final_g', 'new_v_c_ctx', 'new_v_w_mod', 'new_v_b_mod', 'new_v_norm_g', 'new_v_ffn_w1', 'new_v_ffn_w3', 'new_v_ffn_w2', 'new_v_w_in', 'new_v_w_out', 'new_v_pool_w', 'new_v_pool_scale', 'new_v_ret_decay_fwd', 'new_v_ret_decay_bwd', 'new_v_ret_gn_g', 'new_v_conv_dw', 'new_v_conv_b', 'new_v_conv_ln_g', 'new_v_conv_ln_b', 'new_v_final_g']
TWIN_LEAF_KINDS = {'loss': 'loss', 'grad_x': 'grad_x', 'grad_c_ctx': 'grad_w', 'grad_w_mod': 'grad_w', 'grad_b_mod': 'grad_w', 'grad_norm_g': 'grad_w', 'grad_ffn_w1': 'grad_w', 'grad_ffn_w3': 'grad_w', 'grad_ffn_w2': 'grad_w', 'grad_w_in': 'grad_w', 'grad_w_out': 'grad_w', 'grad_pool_w': 'grad_w', 'grad_pool_scale': 'grad_w', 'grad_ret_decay_fwd': 'grad_w', 'grad_ret_decay_bwd': 'grad_w', 'grad_ret_gn_g': 'grad_w', 'grad_conv_dw': 'grad_w', 'grad_conv_b': 'grad_w', 'grad_conv_ln_g': 'grad_w', 'grad_conv_ln_b': 'grad_w', 'grad_final_g': 'grad_w', 'delta_c_ctx': 'delta_w', 'delta_w_mod': 'delta_w', 'delta_b_mod': 'delta_w', 'delta_norm_g': 'delta_w', 'delta_ffn_w1': 'delta_w', 'delta_ffn_w3': 'delta_w', 'delta_ffn_w2': 'delta_w', 'delta_w_in': 'delta_w', 'delta_w_out': 'delta_w', 'delta_pool_w': 'delta_w', 'delta_pool_scale': 'delta_w', 'delta_ret_decay_fwd': 'delta_w', 'delta_ret_decay_bwd': 'delta_w', 'delta_ret_gn_g': 'delta_w', 'delta_conv_dw': 'delta_w', 'delta_conv_b': 'delta_w', 'delta_conv_ln_g': 'delta_w', 'delta_conv_ln_b': 'delta_w', 'delta_final_g': 'delta_w', 'new_m_c_ctx': 'new_m', 'new_m_w_mod': 'new_m', 'new_m_b_mod': 'new_m', 'new_m_norm_g': 'new_m', 'new_m_ffn_w1': 'new_m', 'new_m_ffn_w3': 'new_m', 'new_m_ffn_w2': 'new_m', 'new_m_w_in': 'new_m', 'new_m_w_out': 'new_m', 'new_m_pool_w': 'new_m', 'new_m_pool_scale': 'new_m', 'new_m_ret_decay_fwd': 'new_m', 'new_m_ret_decay_bwd': 'new_m', 'new_m_ret_gn_g': 'new_m', 'new_m_conv_dw': 'new_m', 'new_m_conv_b': 'new_m', 'new_m_conv_ln_g': 'new_m', 'new_m_conv_ln_b': 'new_m', 'new_m_final_g': 'new_m', 'new_v_c_ctx': 'new_v', 'new_v_w_mod': 'new_v', 'new_v_b_mod': 'new_v', 'new_v_norm_g': 'new_v', 'new_v_ffn_w1': 'new_v', 'new_v_ffn_w3': 'new_v', 'new_v_ffn_w2': 'new_v', 'new_v_w_in': 'new_v', 'new_v_w_out': 'new_v', 'new_v_pool_w': 'new_v', 'new_v_pool_scale': 'new_v', 'new_v_ret_decay_fwd': 'new_v', 'new_v_ret_decay_bwd': 'new_v', 'new_v_ret_gn_g': 'new_v', 'new_v_conv_dw': 'new_v', 'new_v_conv_b': 'new_v', 'new_v_conv_ln_g': 'new_v', 'new_v_conv_ln_b': 'new_v', 'new_v_final_g': 'new_v'}


def _forward(args):
    return _fwd_reference(*[args[k] for k in FWD_PARAMS])


def _output_shape():
    out = _jax.eval_shape(lambda: _forward(_fwd_setup_inputs(0)))
    return out.shape, out.dtype

N_MICROBATCH = 1
ADAM_LR = 0.001
ADAM_B1 = 0.9
ADAM_B2 = 0.999
ADAM_EPS = 1e-08
ADAM_WD = 0.01
ADAM_STEP = 10
PER_EXAMPLE_BATCH_AXIS = {'x': 0, 'c': 0, 'ctx': 0, 'loss_target': 0}
SHARED_INPUTS = []
_WEIGHT_DTYPES = {'c_ctx': _jnp.float32, 'w_mod': _jnp.float32, 'b_mod': _jnp.float32, 'norm_g': _jnp.float32, 'ffn_w1': _jnp.float32, 'ffn_w3': _jnp.float32, 'ffn_w2': _jnp.float32, 'w_in': _jnp.float32, 'w_out': _jnp.float32, 'pool_w': _jnp.float32, 'pool_scale': _jnp.float32, 'ret_decay_fwd': _jnp.float32, 'ret_decay_bwd': _jnp.float32, 'ret_gn_g': _jnp.float32, 'conv_dw': _jnp.float32, 'conv_b': _jnp.float32, 'conv_ln_g': _jnp.float32, 'conv_ln_b': _jnp.float32, 'final_g': _jnp.float32}
MOMENT_SCALE = {'c_ctx': 3.303522e-02, 'w_mod': 4.830645e-02, 'b_mod': 7.924729e-02, 'norm_g': 5.242929e-02, 'ffn_w1': 1.716134e-02, 'ffn_w3': 1.664064e-02, 'ffn_w2': 2.758765e-02, 'w_in': 5.267859e-02, 'w_out': 5.007232e-02, 'pool_w': 6.281948e-02, 'pool_scale': 6.475157e-02, 'ret_decay_fwd': 2.583360e-01, 'ret_decay_bwd': 2.075841e-01, 'ret_gn_g': 4.790825e-02, 'conv_dw': 4.247471e-02, 'conv_b': 7.471865e-02, 'conv_ln_g': 5.252554e-02, 'conv_ln_b': 4.605094e-02, 'final_g': 3.208057e+01}


def _to_microbatches(a, axis):
    t = _jnp.moveaxis(a, axis, 0)
    t = t.reshape((N_MICROBATCH, t.shape[0] // N_MICROBATCH) + t.shape[1:])
    return _jnp.moveaxis(t, 1, axis + 1)


def setup_inputs(seed: int = 0) -> dict:
    inp = _fwd_setup_inputs(seed)
    key = _jax.random.fold_in(_jax.random.key(seed), 7919)
    shape, _ = _output_shape()
    out = dict(inp)
    out["loss_target"] = _jax.random.normal(_jax.random.fold_in(key, 0), shape, _jnp.float32)
    for i, name in enumerate(TWIN_WEIGHTS):
        w = inp[name].astype(_jnp.float32)
        if MOMENT_SCALE is None:
            s = _jnp.sqrt(_jnp.mean(_jnp.square(w)) + 1e-30)
        else:
            s = MOMENT_SCALE[name]
        km, kv = _jax.random.split(_jax.random.fold_in(key, i + 1))
        out[name] = w
        out["m_" + name] = s * _jax.random.normal(km, w.shape, _jnp.float32)
        out["v_" + name] = (s * s) * _jax.random.uniform(kv, w.shape, _jnp.float32, 0.5, 1.5)
    if N_MICROBATCH > 1:
        for name, axis in PER_EXAMPLE_BATCH_AXIS.items():
            out[name] = _to_microbatches(out[name], axis)
    return {'x': out['x'], 'c': out['c'], 'ctx': out['ctx'], 'c_ctx': out['c_ctx'], 'w_mod': out['w_mod'], 'b_mod': out['b_mod'], 'norm_g': out['norm_g'], 'ffn_w1': out['ffn_w1'], 'ffn_w3': out['ffn_w3'], 'ffn_w2': out['ffn_w2'], 'w_in': out['w_in'], 'w_out': out['w_out'], 'pool_w': out['pool_w'], 'pool_scale': out['pool_scale'], 'ret_decay_fwd': out['ret_decay_fwd'], 'ret_decay_bwd': out['ret_decay_bwd'], 'ret_gn_g': out['ret_gn_g'], 'conv_dw': out['conv_dw'], 'conv_b': out['conv_b'], 'conv_ln_g': out['conv_ln_g'], 'conv_ln_b': out['conv_ln_b'], 'final_g': out['final_g'], 'loss_target': out['loss_target'], 'm_c_ctx': out['m_c_ctx'], 'm_w_mod': out['m_w_mod'], 'm_b_mod': out['m_b_mod'], 'm_norm_g': out['m_norm_g'], 'm_ffn_w1': out['m_ffn_w1'], 'm_ffn_w3': out['m_ffn_w3'], 'm_ffn_w2': out['m_ffn_w2'], 'm_w_in': out['m_w_in'], 'm_w_out': out['m_w_out'], 'm_pool_w': out['m_pool_w'], 'm_pool_scale': out['m_pool_scale'], 'm_ret_decay_fwd': out['m_ret_decay_fwd'], 'm_ret_decay_bwd': out['m_ret_decay_bwd'], 'm_ret_gn_g': out['m_ret_gn_g'], 'm_conv_dw': out['m_conv_dw'], 'm_conv_b': out['m_conv_b'], 'm_conv_ln_g': out['m_conv_ln_g'], 'm_conv_ln_b': out['m_conv_ln_b'], 'm_final_g': out['m_final_g'], 'v_c_ctx': out['v_c_ctx'], 'v_w_mod': out['v_w_mod'], 'v_b_mod': out['v_b_mod'], 'v_norm_g': out['v_norm_g'], 'v_ffn_w1': out['v_ffn_w1'], 'v_ffn_w3': out['v_ffn_w3'], 'v_ffn_w2': out['v_ffn_w2'], 'v_w_in': out['v_w_in'], 'v_w_out': out['v_w_out'], 'v_pool_w': out['v_pool_w'], 'v_pool_scale': out['v_pool_scale'], 'v_ret_decay_fwd': out['v_ret_decay_fwd'], 'v_ret_decay_bwd': out['v_ret_decay_bwd'], 'v_ret_gn_g': out['v_ret_gn_g'], 'v_conv_dw': out['v_conv_dw'], 'v_conv_b': out['v_conv_b'], 'v_conv_ln_g': out['v_conv_ln_g'], 'v_conv_ln_b': out['v_conv_ln_b'], 'v_final_g': out['v_final_g']}


def _loss(weights, diff, rest, loss_target):
    with _jax.named_scope("forward"):
        args = {**rest, TWIN_DIFF_INPUT: diff, **{k: w.astype(_WEIGHT_DTYPES[k]) for k, w in weights.items()}}
        y = _forward(args)
    with _jax.named_scope("loss_head"):
        err = _jnp.square(y.astype(_jnp.float32) - loss_target)
        return 0.5 * _jnp.sum(_jnp.mean(err, axis=-1)) if err.ndim else 0.5 * err


def _adamw(w, g, m, v):
    m = ADAM_B1 * m + (1.0 - ADAM_B1) * g
    v = ADAM_B2 * v + (1.0 - ADAM_B2) * _jnp.square(g)
    m_hat = m / (1.0 - ADAM_B1 ** ADAM_STEP)
    v_hat = v / (1.0 - ADAM_B2 ** ADAM_STEP)
    delta = -ADAM_LR * (m_hat / (_jnp.sqrt(v_hat) + ADAM_EPS) + ADAM_WD * w)
    return delta, m, v


def reference(x, c, ctx, c_ctx, w_mod, b_mod, norm_g, ffn_w1, ffn_w3, ffn_w2, w_in, w_out, pool_w, pool_scale, ret_decay_fwd, ret_decay_bwd, ret_gn_g, conv_dw, conv_b, conv_ln_g, conv_ln_b, final_g, loss_target, m_c_ctx, m_w_mod, m_b_mod, m_norm_g, m_ffn_w1, m_ffn_w3, m_ffn_w2, m_w_in, m_w_out, m_pool_w, m_pool_scale, m_ret_decay_fwd, m_ret_decay_bwd, m_ret_gn_g, m_conv_dw, m_conv_b, m_conv_ln_g, m_conv_ln_b, m_final_g, v_c_ctx, v_w_mod, v_b_mod, v_norm_g, v_ffn_w1, v_ffn_w3, v_ffn_w2, v_w_in, v_w_out, v_pool_w, v_pool_scale, v_ret_decay_fwd, v_ret_decay_bwd, v_ret_gn_g, v_conv_dw, v_conv_b, v_conv_ln_g, v_conv_ln_b, v_final_g):
    given = dict(x=x, c=c, ctx=ctx, c_ctx=c_ctx, w_mod=w_mod, b_mod=b_mod, norm_g=norm_g, ffn_w1=ffn_w1, ffn_w3=ffn_w3, ffn_w2=ffn_w2, w_in=w_in, w_out=w_out, pool_w=pool_w, pool_scale=pool_scale, ret_decay_fwd=ret_decay_fwd, ret_decay_bwd=ret_decay_bwd, ret_gn_g=ret_gn_g, conv_dw=conv_dw, conv_b=conv_b, conv_ln_g=conv_ln_g, conv_ln_b=conv_ln_b, final_g=final_g, loss_target=loss_target, m_c_ctx=m_c_ctx, m_w_mod=m_w_mod, m_b_mod=m_b_mod, m_norm_g=m_norm_g, m_ffn_w1=m_ffn_w1, m_ffn_w3=m_ffn_w3, m_ffn_w2=m_ffn_w2, m_w_in=m_w_in, m_w_out=m_w_out, m_pool_w=m_pool_w, m_pool_scale=m_pool_scale, m_ret_decay_fwd=m_ret_decay_fwd, m_ret_decay_bwd=m_ret_decay_bwd, m_ret_gn_g=m_ret_gn_g, m_conv_dw=m_conv_dw, m_conv_b=m_conv_b, m_conv_ln_g=m_conv_ln_g, m_conv_ln_b=m_conv_ln_b, m_final_g=m_final_g, v_c_ctx=v_c_ctx, v_w_mod=v_w_mod, v_b_mod=v_b_mod, v_norm_g=v_norm_g, v_ffn_w1=v_ffn_w1, v_ffn_w3=v_ffn_w3, v_ffn_w2=v_ffn_w2, v_w_in=v_w_in, v_w_out=v_w_out, v_pool_w=v_pool_w, v_pool_scale=v_pool_scale, v_ret_decay_fwd=v_ret_decay_fwd, v_ret_decay_bwd=v_ret_decay_bwd, v_ret_gn_g=v_ret_gn_g, v_conv_dw=v_conv_dw, v_conv_b=v_conv_b, v_conv_ln_g=v_conv_ln_g, v_conv_ln_b=v_conv_ln_b, v_final_g=v_final_g)
    weights = {n: given[n] for n in TWIN_WEIGHTS}
    shared = {n: given[n] for n in SHARED_INPUTS}
    per_example = {n: given[n] for n in ['x', 'c', 'ctx']}
    grad_fn = _jax.value_and_grad(_loss, argnums=(0, 1))

    def one_microbatch(ex, loss_target):
        ex = dict(ex)
        diff = ex.pop(TWIN_DIFF_INPUT)
        return grad_fn(weights, diff, {**shared, **ex}, loss_target)

    if N_MICROBATCH == 1:
        loss, (grad_w, grad_x) = one_microbatch(per_example, given["loss_target"])
    else:
        def body(carry, xs):
            loss_sum, grad_sum = carry
            l_k, (gw_k, gx_k) = one_microbatch(xs[0], xs[1])
            with _jax.named_scope("update"):
                return (loss_sum + l_k, _jax.tree.map(_jnp.add, grad_sum, gw_k)), gx_k

        init = (_jnp.zeros((), _jnp.float32), _jax.tree.map(_jnp.zeros_like, weights))
        (loss, grad_w), grad_x = _jax.lax.scan(body, init, (per_example, given["loss_target"]))
    with _jax.named_scope("update"):
        delta_w, new_m, new_v = {}, {}, {}
        for n in TWIN_WEIGHTS:
            delta_w[n], new_m[n], new_v[n] = _adamw(weights[n], grad_w[n], given["m_" + n], given["v_" + n])
    return (loss, grad_x, *[grad_w[n] for n in TWIN_WEIGHTS], *[delta_w[n] for n in TWIN_WEIGHTS],
            *[new_m[n] for n in TWIN_WEIGHTS], *[new_v[n] for n in TWIN_WEIGHTS])
```

```python
import functools

import jax
import jax.numpy as jnp
from jax import lax
from jax.experimental import pallas as pl
from jax.experimental.pallas import tpu as pltpu

F32 = jnp.float32
BF16 = jnp.bfloat16

D = 1024
F = 2816
FH = 1408
N_MOD = 9
LC = 256
TM = 256
HD = 128
NH = 4
RW = 512
PW = 256
CONV_K = 31
GRID_W = 64
EPS = 1e-6
K_SCALE = HD ** -0.5
N_DEV = 8
N_CHIP = 4
SLAB = F // N_CHIP
HSLAB = SLAB // 2
OSLAB = D // N_CHIP
HOSLAB = OSLAB // 2
N704 = 14
N256 = 2
ROWS = N704 * HSLAB + N256 * HOSLAB
VMEM_BIG = 60 * 1024 * 1024
MESH = pl.DeviceIdType.MESH

ADAM_LR = 0.001
ADAM_B1 = 0.9
ADAM_B2 = 0.999
ADAM_EPS = 1e-08
ADAM_WD = 0.01
ADAM_STEP = 10


def _nt(a, b):
    return lax.dot_general(a, b, (((1,), (1,)), ((), ())), preferred_element_type=F32)


def _nn(a, b):
    return lax.dot_general(a, b, (((1,), (0,)), ((), ())), preferred_element_type=F32)


def _tn(a, b):
    return lax.dot_general(a, b, (((0,), (0,)), ((), ())), preferred_element_type=F32)


def _params(vmem=None, sem=None):
    return pltpu.CompilerParams(dimension_semantics=sem, vmem_limit_bytes=vmem)


def _rms_mod(z, g, shift, scale):
    y = z * lax.rsqrt(jnp.mean(z * z, axis=-1, keepdims=True) + EPS)
    return (y * g) * (1.0 + scale) + shift


def _acc(ref, val, first):
    @pl.when(first)
    def _():
        ref[...] = val

    @pl.when(jnp.logical_not(first))
    def _():
        ref[...] += val


def _tok(width):
    return pl.BlockSpec((None, TM, width), lambda b, t: (b, t, 0))


def _modspec():
    return pl.BlockSpec((None, None, 3, D), lambda b, t: (b, jnp.minimum(t, 1), 0, 0))


def _const(shape):
    nd = len(shape)
    return pl.BlockSpec(shape, lambda b, t: (0,) * nd)


def _wspec(shape):
    nd = len(shape)
    return pl.BlockSpec(shape, lambda b, t: (0,) * nd, pipeline_mode=pl.Buffered(1))


def _ffn_fwd(z, mod, g, w1t, w3t, w2):
    B, S, _ = z.shape

    def body(z_ref, mod_ref, g_ref, w1_ref, w3_ref, w2_ref, zo_ref, f_ref):
        zt = z_ref[...]
        h = _rms_mod(zt, g_ref[...], mod_ref[0:1, :], mod_ref[1:2, :]).astype(BF16)
        f = jnp.zeros((TM, D), F32)
        for c in range(F // FH):
            rows = slice(c * FH, (c + 1) * FH)
            u1 = _nt(h, w1_ref[rows, :])
            u3 = _nt(h, w3_ref[rows, :])
            a = (u1 * jax.nn.sigmoid(u1) * u3).astype(BF16)
            f = f + _nn(a, w2_ref[rows, :])
        f_ref[...] = f
        zo_ref[...] = zt + 0.5 * mod_ref[2:3, :] * f

    return pl.pallas_call(
        body, name="ffn_fwd", grid=(B, S // TM),
        in_specs=[_tok(D), _modspec(), _const((1, D)), _wspec((F, D)), _wspec((F, D)), _wspec((F, D))],
        out_specs=[_tok(D), _tok(D)],
        out_shape=[jax.ShapeDtypeStruct((B, S, D), F32)] * 2,
        compiler_params=_params(VMEM_BIG, ("arbitrary", "arbitrary")),
    )(z, mod, g, w1t, w3t, w2)


def _ffn_bwd(z, dzo, f, mod, g, w1t, w3t, w2):
    B, S, _ = z.shape

    def body(z_ref, dzo_ref, f_ref, mod_ref, g_ref, w1_ref, w3_ref, w2_ref,
             dz_ref, h_ref, du1_ref, du3_ref, a_ref, do_ref, dmod_ref, dg_ref):
        b, t = pl.program_id(0), pl.program_id(1)
        zt = z_ref[...]
        dzo = dzo_ref[...]
        gate = mod_ref[2:3, :]
        h32, vjp_h = jax.vjp(_rms_mod, zt, g_ref[...], mod_ref[0:1, :], mod_ref[1:2, :])
        h = h32.astype(BF16)
        h_ref[...] = h
        do = (0.5 * gate * dzo).astype(BF16)
        do_ref[...] = do
        dgate = jnp.sum(0.5 * f_ref[...] * dzo, axis=0, keepdims=True)
        dh = jnp.zeros((TM, D), F32)
        for c in range(F // FH):
            rows = slice(c * FH, (c + 1) * FH)
            u1 = _nt(h, w1_ref[rows, :])
            u3 = _nt(h, w3_ref[rows, :])
            sg = jax.nn.sigmoid(u1)
            s = u1 * sg
            a_ref[:, rows] = (s * u3).astype(BF16)
            da = _nt(do, w2_ref[rows, :])
            du3 = (da * s).astype(BF16)
            du1 = (da * u3 * (sg * (1.0 + u1 * (1.0 - sg)))).astype(BF16)
            du1_ref[:, rows] = du1
            du3_ref[:, rows] = du3
            dh = dh + _nn(du1, w1_ref[rows, :]) + _nn(du3, w3_ref[rows, :])
        dz_h, dg, dshift, dscale = vjp_h(dh)
        dz_ref[...] = dzo + dz_h
        _acc(dmod_ref, jnp.concatenate([dshift, dscale, dgate], axis=0), t <= 1)
        _acc(dg_ref, dg, jnp.logical_and(b == 0, t == 0))

    T = B * S
    outs = pl.pallas_call(
        body, name="ffn_bwd", grid=(B, S // TM),
        in_specs=[_tok(D), _tok(D), _tok(D), _modspec(), _const((1, D)), _wspec((F, D)), _wspec((F, D)), _wspec((F, D))],
        out_specs=[_tok(D), _tok(D), _tok(F), _tok(F), _tok(F), _tok(D), _modspec(), _const((1, D))],
        out_shape=[jax.ShapeDtypeStruct((B, S, D), F32), jax.ShapeDtypeStruct((B, S, D), BF16),
                   jax.ShapeDtypeStruct((B, S, F), BF16), jax.ShapeDtypeStruct((B, S, F), BF16),
                   jax.ShapeDtypeStruct((B, S, F), BF16), jax.ShapeDtypeStruct((B, S, D), BF16),
                   jax.ShapeDtypeStruct((B, 2, 3, D), F32), jax.ShapeDtypeStruct((1, D), F32)],
        compiler_params=_params(VMEM_BIG, ("arbitrary", "arbitrary")),
    )(z, dzo, f, mod, g, w1t, w3t, w2)
    dz, h, du1, du3, a, do, dmod, dg = outs
    gw1t = _tn_matmul(du1.reshape(T, F), h.reshape(T, D))
    gw3t = _tn_matmul(du3.reshape(T, F), h.reshape(T, D))
    gw2 = _tn_matmul(a.reshape(T, F), do.reshape(T, D))
    return dz, dmod, dg, gw1t, gw3t, gw2


def _tn_matmul(a, b):
    T, M = a.shape
    N = b.shape[1]
    MB = FH if M > FH else M
    TT = 512 if T % 512 == 0 else TM
    nt = T // TT

    def body(a_ref, b_ref, o_ref, acc_ref):
        t = pl.program_id(1)
        prod = _tn(a_ref[...], b_ref[...])
        _acc(acc_ref, prod, t == 0)

        @pl.when(t == nt - 1)
        def _():
            o_ref[...] = acc_ref[...].astype(BF16)

    return pl.pallas_call(
        body, name="tn_matmul", grid=(M // MB, nt),
        in_specs=[pl.BlockSpec((TT, MB), lambda i, t: (t, i)), pl.BlockSpec((TT, N), lambda i, t: (t, 0))],
        out_specs=pl.BlockSpec((MB, N), lambda i, t: (i, 0)),
        out_shape=jax.ShapeDtypeStruct((M, N), BF16),
        scratch_shapes=[pltpu.VMEM((MB, N), F32)],
        compiler_params=_params(VMEM_BIG, ("arbitrary", "arbitrary")),
    )(a, b)


def _swap32(x):
    n = x.shape[1]
    lane = lax.broadcasted_iota(jnp.int32, x.shape, 1)
    return jnp.where((lane % 64) < 32, pltpu.roll(x, n - 32, 1), pltpu.roll(x, 32, 1))


def _rope(x, cos, sin):
    return x * cos + _swap32(x) * sin


def _rope_t(dy, cos, sin):
    return dy * cos + _swap32(dy * sin)


def _rope_tables(S):
    L = S - LC
    n_freq = HD // 4
    inv = 10000.0 ** (-jnp.arange(n_freq, dtype=F32) / n_freq)
    i = jnp.arange(L)
    row = (i // GRID_W).astype(F32)
    col = (i % GRID_W).astype(F32)
    ang_r = row[:, None] * inv[None]
    ang_c = col[:, None] * inv[None]
    ang = jnp.concatenate([ang_r, ang_r, ang_c, ang_c], axis=1)
    ang = jnp.concatenate([jnp.zeros((LC, HD), F32), ang], axis=0)
    sign = jnp.where((jnp.arange(HD) % 64) < 32, -1.0, 1.0).astype(F32)
    return jnp.cos(ang), jnp.sin(ang) * sign[None]


def _tabspec():
    return pl.BlockSpec((TM, HD), lambda b, t: (t, 0))


def _mix_in_fwd(z, mod, g, wint, cos, sin):
    B, S, _ = z.shape

    def body(z_ref, mod_ref, g_ref, w_ref, cos_ref, sin_ref, pp_ref, q_ref, k_ref, v_ref, gg_ref, pc_ref):
        h = _rms_mod(z_ref[...], g_ref[...], mod_ref[0:1, :], mod_ref[1:2, :]).astype(BF16)
        p = _nt(h, w_ref[...])
        cos = jnp.tile(cos_ref[...], (1, NH))
        sin = jnp.tile(sin_ref[...], (1, NH))
        pp_ref[...] = p[:, 0:PW]
        q_ref[...] = _rope(p[:, PW:PW + RW], cos, sin)
        k_ref[...] = _rope(p[:, PW + RW:PW + 2 * RW], cos, sin) * K_SCALE
        v_ref[...] = p[:, PW + 2 * RW:PW + 3 * RW]
        gg_ref[...] = p[:, PW + 3 * RW:PW + 4 * RW]
        pc_ref[...] = p[:, PW + 4 * RW:]

    return pl.pallas_call(
        body, name="mix_in_fwd", grid=(B, S // TM),
        in_specs=[_tok(D), _modspec(), _const((1, D)), _wspec((F, D)), _tabspec(), _tabspec()],
        out_specs=[_tok(PW), _tok(RW), _tok(RW), _tok(RW), _tok(RW), _tok(2 * PW)],
        out_shape=[jax.ShapeDtypeStruct((B, S, PW), F32)] + [jax.ShapeDtypeStruct((B, S, RW), F32)] * 5,
        compiler_params=_params(VMEM_BIG, ("arbitrary", "arbitrary")),
    )(z, mod, g, wint, cos, sin)


def _mix_in_bwd(z, dzo, dpp, dq, dk, dv, dgg, dpc, mod, g, wint):
    B, S, _ = z.shape

    def body(z_ref, dzo_ref, dpp_ref, dq_ref, dk_ref, dv_ref, dgg_ref, dpc_ref, mod_ref, g_ref, w_ref,
             dz_ref, h_ref, dp_ref, dmod_ref, dg_ref):
        b, t = pl.program_id(0), pl.program_id(1)
        h32, vjp_h = jax.vjp(_rms_mod, z_ref[...], g_ref[...], mod_ref[0:1, :], mod_ref[1:2, :])
        h_ref[...] = h32.astype(BF16)
        dp = jnp.concatenate([dpp_ref[...], dq_ref[...], dk_ref[...], dv_ref[...], dgg_ref[...], dpc_ref[...]],
                             axis=1).astype(BF16)
        dp_ref[...] = dp
        dh = _nn(dp, w_ref[...])
        dz_h, dg, dshift, dscale = vjp_h(dh)
        dz_ref[...] = dzo_ref[...] + dz_h
        _acc(dmod_ref, jnp.concatenate([dshift, dscale, jnp.zeros_like(dshift)], axis=0), t <= 1)
        _acc(dg_ref, dg, jnp.logical_and(b == 0, t == 0))

    return pl.pallas_call(
        body, name="mix_in_bwd", grid=(B, S // TM),
        in_specs=[_tok(D), _tok(D), _tok(PW), _tok(RW), _tok(RW), _tok(RW), _tok(RW), _tok(2 * PW),
                  _modspec(), _const((1, D)), _wspec((F, D))],
        out_specs=[_tok(D), _tok(D), _tok(F), _modspec(), _const((1, D))],
        out_shape=[jax.ShapeDtypeStruct((B, S, D), F32), jax.ShapeDtypeStruct((B, S, D), BF16),
                   jax.ShapeDtypeStruct((B, S, F), BF16), jax.ShapeDtypeStruct((B, 2, 3, D), F32),
                   jax.ShapeDtypeStruct((1, D), F32)],
        compiler_params=_params(VMEM_BIG, ("arbitrary", "arbitrary")),
    )(z, dzo, dpp, dq, dk, dv, dgg, dpc, mod, g, wint)


def _rope_bwd(dqa, dqb, dka, dkb, cos, sin):
    B, S, _ = dqa.shape

    def body(dqa_ref, dqb_ref, dka_ref, dkb_ref, cos_ref, sin_ref, dq_ref, dk_ref):
        cos = jnp.tile(cos_ref[...], (1, NH))
        sin = jnp.tile(sin_ref[...], (1, NH))
        dq_ref[...] = _rope_t(dqa_ref[...] + dqb_ref[...], cos, sin)
        dk_ref[...] = _rope_t(dka_ref[...] + dkb_ref[...], cos, sin) * K_SCALE

    return pl.pallas_call(
        body, name="rope_bwd", grid=(B, S // TM),
        in_specs=[_tok(RW)] * 4 + [_tabspec(), _tabspec()],
        out_specs=[_tok(RW), _tok(RW)],
        out_shape=[jax.ShapeDtypeStruct((B, S, RW), F32)] * 2,
        compiler_params=_params(None, ("arbitrary", "arbitrary")),
    )(dqa, dqb, dka, dkb, cos, sin)


def _log_sigmoid(x):
    return jnp.minimum(x, 0.0) - jnp.log(1.0 + jnp.exp(-jnp.abs(x)))


def _retention(a, b, c, dec_a, dec_b, sched_a, sched_b):
    B, S, _ = a.shape
    C = TM

    def body(a_ref, b_ref, c_ref, da_ref, db_ref, oa_ref, ob_ref):
        ii = lax.broadcasted_iota(jnp.int32, (C, C), 0)
        jj = lax.broadcasted_iota(jnp.int32, (C, C), 1)
        pos = lax.broadcasted_iota(jnp.int32, (C, 1), 0).astype(F32)
        for dec_ref, o_ref, (order, causal, strict) in ((da_ref, oa_ref, sched_a), (db_ref, ob_ref, sched_b)):
            lg = _log_sigmoid(dec_ref[...])
            lg1 = lg[:, 0:1]
            dist = ((ii - jj) if causal else (jj - ii)).astype(F32)
            mask = (dist > 0.0) if strict else (dist >= 0.0)
            decay = jnp.where(mask, jnp.exp(jnp.maximum(dist, 0.0) * lg1), 0.0)
            p = pos if causal else (C - 1.0 - pos)
            w_q = jnp.exp((p + 1.0) * lg1)
            w_k = jnp.exp((C - 1.0 - p) * lg1)
            chunk_decay = jnp.exp(C * lg)
            state = jnp.zeros((HD, HD), F32)
            for n in order:
                rows = pl.ds(n * C, C)
                at, bt, ct = a_ref[rows, :], b_ref[rows, :], c_ref[rows, :]
                cb = ct.astype(BF16)
                scores = _nt(at.astype(BF16), bt.astype(BF16)) * decay
                o = _nn(scores.astype(BF16), cb)
                o = o + _nn((at * w_q).astype(BF16), state.astype(BF16))
                o_ref[rows, :] = o
                state = chunk_decay * state + _tn((bt * w_k).astype(BF16), cb)

    seq = pl.BlockSpec((None, S, HD), lambda b, h: (b, 0, h))
    dspec = pl.BlockSpec((None, 1, HD), lambda b, h: (h, 0, 0))
    return pl.pallas_call(
        body, name="retention", grid=(B, NH),
        in_specs=[seq, seq, seq, dspec, dspec], out_specs=[seq, seq],
        out_shape=[jax.ShapeDtypeStruct((B, S, RW), F32)] * 2,
        compiler_params=_params(VMEM_BIG, ("arbitrary", "arbitrary")),
    )(a, b, c, dec_a, dec_b)


def _retention_ddecay(q, k, v, do, dec_a, dec_b, sched_a, sched_b):
    B, S, _ = q.shape
    C = TM

    def body(q_ref, k_ref, v_ref, do_ref, da_ref, db_ref, o_ref):
        ii = lax.broadcasted_iota(jnp.int32, (C, C), 0)
        jj = lax.broadcasted_iota(jnp.int32, (C, C), 1)
        pos = lax.broadcasted_iota(jnp.int32, (C, 1), 0).astype(F32)
        vals = []
        for dec_ref, (order, causal, strict) in ((da_ref, sched_a), (db_ref, sched_b)):
            x = dec_ref[...]
            lg = _log_sigmoid(x)
            lg1 = lg[:, 0:1]
            dist = ((ii - jj) if causal else (jj - ii)).astype(F32)
            mask = (dist > 0.0) if strict else (dist >= 0.0)
            ddecay = jnp.where(mask, dist * jnp.exp(jnp.maximum(dist, 0.0) * lg1), 0.0)
            p = pos if causal else (C - 1.0 - pos)
            w_q = jnp.exp((p + 1.0) * lg1)
            w_k = jnp.exp((C - 1.0 - p) * lg1)
            chunk_decay = jnp.exp(C * lg)
            state = jnp.zeros((HD, HD), F32)
            dstate = jnp.zeros((HD, HD), F32)
            tot = jnp.zeros((), F32)
            for n in order:
                rows = pl.ds(n * C, C)
                qt, kt, vt, dot = q_ref[rows, :], k_ref[rows, :], v_ref[rows, :], do_ref[rows, :]
                vb = vt.astype(BF16)
                scores = _nt(qt.astype(BF16), kt.astype(BF16))
                dscores = _nt(dot.astype(BF16), vb)
                qw = (qt * w_q).astype(BF16)
                cross = _nn(qw, state.astype(BF16))
                dcross = _nn(qw, dstate.astype(BF16))
                tot = tot + jnp.sum(scores * dscores * ddecay) + jnp.sum(((p + 1.0) * cross + dcross) * dot)
                kv = _tn((kt * w_k).astype(BF16), vb)
                dkv = _tn((kt * ((C - 1.0 - p) * w_k)).astype(BF16), vb)
                dstate = chunk_decay * (dstate + C * state) + dkv
                state = chunk_decay * state + kv
            vals.append(tot * jax.nn.sigmoid(-x))
        row = lax.broadcasted_iota(jnp.int32, (8, HD), 0)
        tile = jnp.where(row == 0, vals[0], 0.0) + jnp.where(row == 1, vals[1], 0.0)
        _acc(o_ref, tile, pl.program_id(1) == 0)

    seq = pl.BlockSpec((None, S, HD), lambda h, b: (b, 0, h))
    dspec = pl.BlockSpec((None, 1, HD), lambda h, b: (h, 0, 0))
    return pl.pallas_call(
        body, name="retention_ddecay", grid=(NH, B),
        in_specs=[seq, seq, seq, seq, dspec, dspec], out_specs=pl.BlockSpec((None, 8, HD), lambda h, b: (h, 0, 0)),
        out_shape=jax.ShapeDtypeStruct((NH, 8, HD), F32),
        compiler_params=_params(VMEM_BIG, ("arbitrary", "arbitrary")),
    )(q, k, v, do, dec_a, dec_b)


def _schedules(S):
    n = S // TM
    lat_up = tuple(range(1, n))
    lat_down = tuple(range(n - 1, 0, -1))
    fwd = (((0,) + lat_up, True, False), ((0,) + lat_down, False, True))
    bwd = ((lat_down + (0,), False, False), (lat_up + (0,), True, True))
    return fwd, bwd


def _shift_rows(x, d):
    if d == 0:
        return x
    S = x.shape[0]
    t = lax.broadcasted_iota(jnp.int32, x.shape, 0)
    tt = t + d
    lo = jnp.where(t < LC, 0, LC)
    hi = jnp.where(t < LC, LC, S)
    return jnp.where((tt >= lo) & (tt < hi), pltpu.roll(x, (-d) % S, 0), 0.0)


@functools.partial(jax.custom_vjp, nondiff_argnums=(1,))
def _shift(x, d):
    return _shift_rows(x, d)


_shift.defvjp(lambda x, d: (_shift_rows(x, d), None), lambda d, _, g: (_shift_rows(g, -d),))


def _pool_fn(p, bd, pscale):
    lane = lax.broadcasted_iota(jnp.int32, p.shape, 1)
    grp = lane // (PW // 4)
    half = jnp.where(grp == 0, 1, jnp.where(grp == 1, 2, jnp.where(grp == 2, 4, 8)))
    ones = jnp.ones(p.shape, F32)
    acc = jnp.zeros(p.shape, F32)
    cnt = jnp.zeros(p.shape, F32)
    for d in range(-8, 8):
        inwin = ((d >= -half) & (d < half)).astype(F32)
        acc = acc + _shift(p, d) * inwin
        cnt = cnt + _shift_rows(ones, d) * inwin
    pooled = acc / cnt - p
    mixed = _nn(pooled.astype(BF16), bd.astype(BF16))
    return mixed * pscale


def _dwconv_raw(zc, dw):
    y = jnp.zeros(zc.shape, F32)
    for k in range(CONV_K):
        y = y + _shift_rows(zc, k - CONV_K // 2) * dw[k:k + 1, :]
    return y


@jax.custom_vjp
def _dwconv(zc, dw):
    return _dwconv_raw(zc, dw)


def _dwconv_fwd(zc, dw):
    return _dwconv_raw(zc, dw), (zc, dw)


def _dwconv_bwd(res, g):
    zc, dw = res
    dz = jnp.zeros(zc.shape, F32)
    ddw = jnp.zeros(dw.shape, F32)
    row = lax.broadcasted_iota(jnp.int32, dw.shape, 0)
    for k in range(CONV_K):
        dz = dz + _shift_rows(g, CONV_K // 2 - k) * dw[k:k + 1, :]
        r = jnp.sum(g * _shift_rows(zc, k - CONV_K // 2), axis=0, keepdims=True)
        ddw = ddw + jnp.where(row == k, r, 0.0)
    return dz, ddw


_dwconv.defvjp(_dwconv_fwd, _dwconv_bwd)


def _conv_fn(u, dw, db):
    zc = u[:, :PW] * jax.nn.sigmoid(u[:, PW:])
    return _dwconv(zc, dw) + db


def _ln_swish(y, lng, lnb):
    mu = jnp.mean(y, axis=-1, keepdims=True)
    yc = y - mu
    var = jnp.mean(yc * yc, axis=-1, keepdims=True)
    yn = yc * lax.rsqrt(var + EPS) * lng + lnb
    return yn * jax.nn.sigmoid(yn)


def _seq(shape, single=False):
    return pl.BlockSpec((None,) + shape, lambda b: (b, 0, 0), pipeline_mode=pl.Buffered(1) if single else None)


def _c1(shape):
    nd = len(shape)
    return pl.BlockSpec(shape, lambda b: (0,) * nd)


def _seq_apply(fn, name, xs, consts, width):
    B, S, w = xs.shape

    def body(x_ref, *refs):
        refs[-1][...] = fn(x_ref[...], *[r[...] for r in refs[:-1]])

    return pl.pallas_call(
        body, name=name, grid=(B,),
        in_specs=[_seq((S, w))] + [_c1(c.shape) for c in consts], out_specs=_seq((S, width)),
        out_shape=jax.ShapeDtypeStruct((B, S, width), F32),
        compiler_params=_params(VMEM_BIG, ("arbitrary",)),
    )(xs, *consts)


def _seq_vjp(fn, name, xs, consts, dout):
    B, S, w = xs.shape
    n = len(consts)

    def body(x_ref, d_ref, *refs):
        first = pl.program_id(0) == 0
        _, vjp = jax.vjp(fn, x_ref[...], *[r[...] for r in refs[:n]])
        grads = vjp(d_ref[...])
        refs[n][...] = grads[0]
        for ref, val in zip(refs[n + 1:], grads[1:]):
            _acc(ref, val, first)

    return pl.pallas_call(
        body, name=name, grid=(B,),
        in_specs=[_seq((S, w), True), _seq((S, dout.shape[2]), True)] + [_c1(c.shape) for c in consts],
        out_specs=[_seq((S, w))] + [_c1(c.shape) for c in consts],
        out_shape=[jax.ShapeDtypeStruct((B, S, w), F32)] + [jax.ShapeDtypeStruct(c.shape, F32) for c in consts],
        compiler_params=_params(VMEM_BIG, ("arbitrary",)),
    )(xs, dout, *consts)


def _pool_conv_fwd(pp, pc, bd, pscale, dw, db):
    return (_seq_apply(_pool_fn, "pool_fwd", pp, (bd, pscale), PW),
            _seq_apply(_conv_fn, "conv_fwd", pc, (dw, db), PW))


def _pool_conv_bwd(pp, pc, dpo, dco, bd, pscale, dw, db):
    dpp, dbd, dps = _seq_vjp(_pool_fn, "pool_bwd", pp, (bd, pscale), dpo)
    dpc, ddw, ddb = _seq_vjp(_conv_fn, "conv_bwd", pc, (dw, db), dco)
    return dpp, dpc, dbd, dps, ddw, ddb


def _cat_fn(po, oa, ob, gg, co, gng, lng, lnb):
    o = oa + ob
    outs = []
    for h in range(NH):
        oh = o[:, h * HD:(h + 1) * HD]
        mu = jnp.mean(oh, axis=-1, keepdims=True)
        oc = oh - mu
        var = jnp.mean(oc * oc, axis=-1, keepdims=True)
        outs.append(oc * lax.rsqrt(var + EPS))
    ret = jnp.concatenate(outs, axis=1) * gng * (gg * jax.nn.sigmoid(gg))
    return jnp.concatenate([po, ret, _ln_swish(co, lng, lnb)], axis=1)


def _mix_out_fwd(z, po, oa, ob, gg, co, ro, mod, wout):
    B, S, _ = z.shape

    def body(z_ref, po_ref, oa_ref, ob_ref, gg_ref, co_ref, gn_ref, lg_ref, lb_ref, mod_ref, w_ref, zo_ref, out_ref):
        cat = _cat_fn(po_ref[...], oa_ref[...], ob_ref[...], gg_ref[...], co_ref[...], gn_ref[...], lg_ref[...], lb_ref[...])
        out = _nn(cat.astype(BF16), w_ref[...])
        out_ref[...] = out
        zo_ref[...] = z_ref[...] + mod_ref[2:3, :] * out

    return pl.pallas_call(
        body, name="mix_out_fwd", grid=(B, S // TM),
        in_specs=[_tok(D), _tok(PW), _tok(RW), _tok(RW), _tok(RW), _tok(PW), _const((1, RW)), _const((1, PW)),
                  _const((1, PW)), _modspec(), _wspec((D, D))],
        out_specs=[_tok(D), _tok(D)],
        out_shape=[jax.ShapeDtypeStruct((B, S, D), F32)] * 2,
        compiler_params=_params(None, ("arbitrary", "arbitrary")),
    )(z, po, oa, ob, gg, co, *ro, mod, wout)


def _mix_out_bwd(dzo, out, po, oa, ob, gg, co, ro, mod, wout):
    B, S, _ = dzo.shape

    def body(dzo_ref, out_ref, po_ref, oa_ref, ob_ref, gg_ref, co_ref, gn_ref, lg_ref, lb_ref, mod_ref, w_ref,
             dpo_ref, do_ref, dgg_ref, dco_ref, cat_ref, dout_ref, dmod_ref, dgn_ref, dlg_ref, dlb_ref):
        b, t = pl.program_id(0), pl.program_id(1)
        dzo = dzo_ref[...]
        cat, vjp = jax.vjp(_cat_fn, po_ref[...], oa_ref[...], ob_ref[...], gg_ref[...], co_ref[...], gn_ref[...],
                           lg_ref[...], lb_ref[...])
        cat_ref[...] = cat.astype(BF16)
        dout = (mod_ref[2:3, :] * dzo).astype(BF16)
        dout_ref[...] = dout
        dgate = jnp.sum(out_ref[...] * dzo, axis=0, keepdims=True)
        dcat = _nt(dout, w_ref[...])
        dpo, doa, _, dgg, dco, dgn, dlg, dlb = vjp(dcat)
        dpo_ref[...] = dpo
        do_ref[...] = doa
        dgg_ref[...] = dgg
        dco_ref[...] = dco
        zero = jnp.zeros_like(dgate)
        _acc(dmod_ref, jnp.concatenate([zero, zero, dgate], axis=0), t <= 1)
        first = jnp.logical_and(b == 0, t == 0)
        _acc(dgn_ref, dgn, first)
        _acc(dlg_ref, dlg, first)
        _acc(dlb_ref, dlb, first)

    return pl.pallas_call(
        body, name="mix_out_bwd", grid=(B, S // TM),
        in_specs=[_tok(D), _tok(D), _tok(PW), _tok(RW), _tok(RW), _tok(RW), _tok(PW), _const((1, RW)), _const((1, PW)),
                  _const((1, PW)), _modspec(), _wspec((D, D))],
        out_specs=[_tok(PW), _tok(RW), _tok(RW), _tok(PW), _tok(D), _tok(D), _modspec(), _const((1, RW)),
                   _const((1, PW)), _const((1, PW))],
        out_shape=[jax.ShapeDtypeStruct((B, S, PW), F32), jax.ShapeDtypeStruct((B, S, RW), F32),
                   jax.ShapeDtypeStruct((B, S, RW), F32), jax.ShapeDtypeStruct((B, S, PW), F32),
                   jax.ShapeDtypeStruct((B, S, D), BF16), jax.ShapeDtypeStruct((B, S, D), BF16),
                   jax.ShapeDtypeStruct((B, 2, 3, D), F32), jax.ShapeDtypeStruct((1, RW), F32),
                   jax.ShapeDtypeStruct((1, PW), F32), jax.ShapeDtypeStruct((1, PW), F32)],
        compiler_params=_params(None, ("arbitrary", "arbitrary")),
    )(dzo, out, po, oa, ob, gg, co, *ro, mod, wout)


def _rms(z, g):
    return z * lax.rsqrt(jnp.mean(z * z, axis=-1, keepdims=True) + EPS) * g


def _head(z, target, fg):
    B, S, _ = z.shape

    def body(z_ref, t_ref, g_ref, dz_ref, dg_ref, loss_ref):
        b, t = pl.program_id(0), pl.program_id(1)
        first = jnp.logical_and(b == 0, t == 0)

        @pl.when(t == 0)
        def _():
            dz_ref[...] = jnp.zeros((TM, D), F32)

        @pl.when(first)
        def _():
            dg_ref[...] = jnp.zeros((1, D), F32)
            loss_ref[...] = jnp.zeros((8, 128), F32)

        @pl.when(t > 0)
        def _():
            y, vjp = jax.vjp(_rms, z_ref[...], g_ref[...])
            err = y - t_ref[...]
            dz, dg = vjp(err * (1.0 / D))
            dz_ref[...] = dz
            dg_ref[...] += dg
            loss_ref[...] += 0.5 * jnp.sum(err * err) * (1.0 / D)

    return pl.pallas_call(
        body, name="head", grid=(B, S // TM),
        in_specs=[_tok(D), pl.BlockSpec((None, TM, D), lambda b, t: (b, jnp.maximum(t - 1, 0), 0)), _const((1, D))],
        out_specs=[_tok(D), _const((1, D)), _const((8, 128))],
        out_shape=[jax.ShapeDtypeStruct((B, S, D), F32), jax.ShapeDtypeStruct((1, D), F32),
                   jax.ShapeDtypeStruct((8, 128), F32)],
        compiler_params=_params(None, ("arbitrary", "arbitrary")),
    )(z, target, fg)


MROWS = 24
MCOL = 768


def _silu(x):
    return x * jax.nn.sigmoid(x)


def _mod_fwd(c24, wmod, bmod):
    ncol = wmod.shape[2]

    def body(c_ref, w_ref, b_ref, o_ref):
        sc = _silu(c_ref[...]).astype(BF16)
        o_ref[...] = _nn(sc, w_ref[...].astype(BF16)) + b_ref[...]

    return pl.pallas_call(
        body, name="mod_fwd", grid=(2, ncol // MCOL),
        in_specs=[pl.BlockSpec((MROWS, D), lambda l, j: (0, 0)), pl.BlockSpec((None, D, MCOL), lambda l, j: (l, 0, j)),
                  pl.BlockSpec((None, 1, MCOL), lambda l, j: (l, 0, j))],
        out_specs=pl.BlockSpec((None, MROWS, MCOL), lambda l, j: (l, 0, j)),
        out_shape=jax.ShapeDtypeStruct((2, MROWS, ncol), F32),
        compiler_params=_params(None, ("arbitrary", "arbitrary")),
    )(c24, wmod, bmod)


def _mod_bwd(c24, dmod, wmod):
    ncol = wmod.shape[2]

    def body(c_ref, d_ref, w_ref, dw_ref, dsc_ref):
        l, j = pl.program_id(0), pl.program_id(1)
        sc = _silu(c_ref[...]).astype(BF16)
        dm = d_ref[...].astype(BF16)
        dw_ref[...] = _tn(sc, dm)
        _acc(dsc_ref, _nt(dm, w_ref[...].astype(BF16)), jnp.logical_and(l == 0, j == 0))

    return pl.pallas_call(
        body, name="mod_bwd", grid=(2, ncol // MCOL),
        in_specs=[pl.BlockSpec((MROWS, D), lambda l, j: (0, 0)), pl.BlockSpec((None, MROWS, MCOL), lambda l, j: (l, 0, j)),
                  pl.BlockSpec((None, D, MCOL), lambda l, j: (l, 0, j))],
        out_specs=[pl.BlockSpec((None, D, MCOL), lambda l, j: (l, 0, j)), pl.BlockSpec((MROWS, D), lambda l, j: (0, 0))],
        out_shape=[jax.ShapeDtypeStruct((2, D, ncol), F32), jax.ShapeDtypeStruct((MROWS, D), F32)],
        compiler_params=_params(None, ("arbitrary", "arbitrary")),
    )(c24, dmod, wmod)


def _bmod_cctx_grad(dmod_full, dsc_parts, cctx):
    def body(d_ref, p_ref, c_ref, db_ref, dc_ref):
        db_ref[...] = jnp.sum(d_ref[...], axis=1, keepdims=True)
        tot = jnp.zeros((8, D), F32)
        for s in range(N_CHIP):
            tot = tot + p_ref[s]
        x = c_ref[...]
        sg = jax.nn.sigmoid(x)
        dc_ref[...] = jnp.sum(tot, axis=0, keepdims=True) * (sg * (1.0 + x * (1.0 - sg)))

    return pl.pallas_call(
        body, name="bmod_cctx_grad",
        out_shape=[jax.ShapeDtypeStruct((2, 1, N_MOD * D), F32), jax.ShapeDtypeStruct((1, D), F32)],
    )(dmod_full, dsc_parts, cctx)


def _adam_math(w, g, m, v):
    m = ADAM_B1 * m + (1.0 - ADAM_B1) * g
    v = ADAM_B2 * v + (1.0 - ADAM_B2) * (g * g)
    m_hat = m / (1.0 - ADAM_B1 ** ADAM_STEP)
    v_hat = v / (1.0 - ADAM_B2 ** ADAM_STEP)
    delta = -ADAM_LR * (m_hat / (jnp.sqrt(v_hat) + ADAM_EPS) + ADAM_WD * w)
    return delta, m, v


def _adam(w, g, m, v):
    R, Cc = w.shape
    if R * Cc * 4 <= (1 << 20):
        RB = R
    else:
        RB = 1 << (((1 << 18) // Cc).bit_length() - 1)
        assert R % RB == 0

    def body(w_ref, g_ref, m_ref, v_ref, d_ref, mo_ref, vo_ref):
        d, mn, vn = _adam_math(w_ref[...], g_ref[...], m_ref[...], v_ref[...])
        d_ref[...] = d
        mo_ref[...] = mn
        vo_ref[...] = vn

    spec = pl.BlockSpec((RB, Cc), lambda i: (i, 0))
    return pl.pallas_call(
        body, name="adam", grid=(R // RB,), in_specs=[spec] * 4, out_specs=[spec] * 3,
        out_shape=[jax.ShapeDtypeStruct((R, Cc), F32)] * 3,
        compiler_params=_params(None, ("arbitrary",)),
    )(w, g, m, v)


def _sum_devices(parts):
    K = parts.shape[1]

    def body(p_ref, o_ref):
        tot = p_ref[0]
        for i in range(1, N_DEV):
            tot = tot + p_ref[i]
        o_ref[...] = tot

    return pl.pallas_call(body, name="sum_devices", out_shape=jax.ShapeDtypeStruct((K, 128), F32))(parts)


def _sum_grads(r):
    RB = 432

    def body(r_ref, o_ref):
        tot = r_ref[0].astype(F32)
        for i in range(1, N_DEV):
            tot = tot + r_ref[i].astype(F32)
        o_ref[...] = tot

    return pl.pallas_call(
        body, name="sum_grads", grid=(ROWS // RB,),
        in_specs=[pl.BlockSpec((N_DEV, RB, D), lambda i: (0, i, 0))],
        out_specs=pl.BlockSpec((RB, D), lambda i: (i, 0)),
        out_shape=jax.ShapeDtypeStruct((ROWS, D), F32),
        compiler_params=_params(None, ("arbitrary",)),
    )(r)


def _coords():
    return lax.axis_index("x"), lax.axis_index("y"), lax.axis_index("c")


_FLIPS = [(fx, fy, fc) for fx in (0, 1) for fy in (0, 1) for fc in (0, 1)][1:]


def _all_gather_small(buf):
    K = buf.shape[0]

    def body(in_ref, out_ref, send_sems, recv_sems, local_sem):
        x, y, c = _coords()
        me = 4 * x + 2 * y + c
        mine = pltpu.make_async_copy(in_ref, out_ref.at[me], local_sem)
        mine.start()
        sends = []
        for k, (fx, fy, fc) in enumerate(_FLIPS):
            peer = (x ^ fx, y ^ fy, c ^ fc)
            cp = pltpu.make_async_remote_copy(src_ref=in_ref, dst_ref=out_ref.at[me], send_sem=send_sems.at[k],
                                              recv_sem=recv_sems.at[k], device_id=peer, device_id_type=MESH)
            cp.start()
            sends.append(cp)
        for k, (fx, fy, fc) in enumerate(_FLIPS):
            src = 4 * (x ^ fx) + 2 * (y ^ fy) + (c ^ fc)
            pltpu.make_async_remote_copy(src_ref=in_ref, dst_ref=out_ref.at[src], send_sem=send_sems.at[k],
                                         recv_sem=recv_sems.at[k], device_id=(x, y, c), device_id_type=MESH).wait_recv()
        for cp in sends:
            cp.wait_send()
        mine.wait()

    return pl.pallas_call(
        body, name="all_gather_small",
        in_specs=[pl.BlockSpec(memory_space=pltpu.VMEM)], out_specs=pl.BlockSpec(memory_space=pltpu.VMEM),
        out_shape=jax.ShapeDtypeStruct((N_DEV, K, 128), F32),
        scratch_shapes=[pltpu.SemaphoreType.DMA((7,)), pltpu.SemaphoreType.DMA((7,)), pltpu.SemaphoreType.DMA],
        compiler_params=_params(VMEM_BIG),
    )(buf)


_CHIP_FLIPS = [(1, 0), (0, 1), (1, 1)]
_ANY = pl.BlockSpec(memory_space=pl.ANY)


def _gather_weights(w704, w256):
    def body(a_ref, b_ref, oa_ref, ob_ref, send_sems, recv_sems, local_sems):
        x, y, c = _coords()
        s_me = 2 * x + y
        locals_ = [pltpu.make_async_copy(a_ref, oa_ref.at[:, s_me], local_sems.at[0]),
                   pltpu.make_async_copy(b_ref, ob_ref.at[:, s_me], local_sems.at[1])]
        for cp in locals_:
            cp.start()
        sends = []
        for j, (fx, fy) in enumerate(_CHIP_FLIPS):
            peer = (x ^ fx, y ^ fy, c)
            for i, (src, dst) in enumerate(((a_ref, oa_ref), (b_ref, ob_ref))):
                cp = pltpu.make_async_remote_copy(src_ref=src, dst_ref=dst.at[:, s_me], send_sem=send_sems.at[2 * j + i],
                                                  recv_sem=recv_sems.at[2 * j + i], device_id=peer, device_id_type=MESH)
                cp.start()
                sends.append(cp)
        for j, (fx, fy) in enumerate(_CHIP_FLIPS):
            s_src = 2 * (x ^ fx) + (y ^ fy)
            for i, (src, dst) in enumerate(((a_ref, oa_ref), (b_ref, ob_ref))):
                pltpu.make_async_remote_copy(src_ref=src, dst_ref=dst.at[:, s_src], send_sem=send_sems.at[2 * j + i],
                                             recv_sem=recv_sems.at[2 * j + i], device_id=(x, y, c),
                                             device_id_type=MESH).wait_recv()
        for cp in sends:
            cp.wait_send()
        for cp in locals_:
            cp.wait()

    return pl.pallas_call(
        body, name="gather_weights", in_specs=[_ANY, _ANY], out_specs=[_ANY, _ANY],
        out_shape=[jax.ShapeDtypeStruct((N704, N_CHIP, SLAB, D), BF16), jax.ShapeDtypeStruct((N256, N_CHIP, OSLAB, D), BF16)],
        scratch_shapes=[pltpu.SemaphoreType.DMA((6,)), pltpu.SemaphoreType.DMA((6,)), pltpu.SemaphoreType.DMA((2,))],
    )(w704, w256)


def _scatter_grads(g704, g256):
    n7, n2 = len(g704), len(g256)

    def body(*refs):
        srcs = refs[:n7 + n2]
        r_ref = refs[n7 + n2]
        send_sems, recv_sems, local_sem = refs[n7 + n2 + 1:]
        x, y, c = _coords()
        me = 4 * x + 2 * y + c

        def pieces(s, h):
            out, off = [], 0
            for i, ref in enumerate(srcs):
                n = HSLAB if i < n7 else HOSLAB
                out.append((ref.at[s, h], off, n))
                off += n
            return out

        whole_a, whole_b = r_ref.at[0], r_ref.at[1]
        for src, off, n in pieces(2 * x + y, c):
            pltpu.make_async_copy(src, r_ref.at[me, pl.ds(off, n)], local_sem).start()
        for k, (fx, fy, fc) in enumerate(_FLIPS):
            px, py, pc = x ^ fx, y ^ fy, c ^ fc
            for src, off, n in pieces(2 * px + py, pc):
                pltpu.make_async_remote_copy(src_ref=src, dst_ref=r_ref.at[me, pl.ds(off, n)], send_sem=send_sems.at[k],
                                             recv_sem=recv_sems.at[k], device_id=(px, py, pc), device_id_type=MESH).start()
        for k in range(7):
            pltpu.make_async_remote_copy(src_ref=whole_a, dst_ref=whole_b, send_sem=send_sems.at[k], recv_sem=recv_sems.at[k],
                                         device_id=(x, y, c), device_id_type=MESH).wait_recv()
        for k in range(7):
            pltpu.make_async_remote_copy(src_ref=whole_a, dst_ref=whole_b, send_sem=send_sems.at[k], recv_sem=recv_sems.at[k],
                                         device_id=(x, y, c), device_id_type=MESH).wait_send()
        pltpu.make_async_copy(whole_a, whole_b, local_sem).wait()

    return pl.pallas_call(
        body, name="scatter_grads", in_specs=[_ANY] * (n7 + n2), out_specs=_ANY,
        out_shape=jax.ShapeDtypeStruct((N_DEV, ROWS, D), BF16),
        scratch_shapes=[pltpu.SemaphoreType.DMA((7,)), pltpu.SemaphoreType.DMA((7,)), pltpu.SemaphoreType.DMA],
    )(*g704, *g256)


def _swap_halves(mine):
    def body(in_ref, out_ref, send_sem, recv_sem, local_sem):
        x, y, c = _coords()
        loc = pltpu.make_async_copy(in_ref, out_ref.at[c], local_sem)
        loc.start()
        cp = pltpu.make_async_remote_copy(src_ref=in_ref, dst_ref=out_ref.at[c], send_sem=send_sem, recv_sem=recv_sem,
                                          device_id=(x, y, 1 - c), device_id_type=MESH)
        cp.start()
        pltpu.make_async_remote_copy(src_ref=in_ref, dst_ref=out_ref.at[1 - c], send_sem=send_sem, recv_sem=recv_sem,
                                     device_id=(x, y, c), device_id_type=MESH).wait_recv()
        cp.wait_send()
        loc.wait()

    return pl.pallas_call(
        body, name="swap_halves", in_specs=[_ANY], out_specs=_ANY,
        out_shape=jax.ShapeDtypeStruct((2, ROWS, D), F32),
        scratch_shapes=[pltpu.SemaphoreType.DMA, pltpu.SemaphoreType.DMA, pltpu.SemaphoreType.DMA],
    )(mine)


def _pack(arrays):
    flat = jnp.concatenate([a.reshape(-1).astype(F32) for a in arrays])
    pad = (-flat.shape[0]) % 1024
    return jnp.pad(flat, (0, pad)).reshape(-1, 128)


def _unpack(buf, shapes):
    flat = buf.reshape(-1)
    out, off = [], 0
    for s in shapes:
        n = 1
        for d in s:
            n *= d
        out.append(flat[off:off + n].reshape(s))
        off += n
    return out


def _block_diag(pw):
    bd = jnp.zeros((PW, PW), F32)
    g = PW // 4
    for i in range(4):
        bd = bd.at[i * g:(i + 1) * g, i * g:(i + 1) * g].set(pw[i])
    return bd


def _lanes(v):
    return jnp.broadcast_to(v.reshape(NH, 1, 1), (NH, 1, HD))


def _layer_fwd(z, mod, normg, w, small):
    S = z.shape[1]
    cos, sin = _rope_tables(S)
    fwd_sched, _ = _schedules(S)
    z1, f_a = _ffn_fwd(z, mod[:, :, 0], normg[0], w["w1t"][0], w["w3t"][0], w["w2"][0])
    pp, q, k, v, gg, pc = _mix_in_fwd(z1, mod[:, :, 1], normg[1], w["wint"], cos, sin)
    po, co = _pool_conv_fwd(pp, pc, small["bd"], small["pscale"], small["dw"], small["db"])
    ro = (small["gng"], small["lng"], small["lnb"])
    oa, ob = _retention(q, k, v, small["dec_f"], small["dec_b"], *fwd_sched)
    z2, out = _mix_out_fwd(z1, po, oa, ob, gg, co, ro, mod[:, :, 1], w["wout"])
    z3, f_b = _ffn_fwd(z2, mod[:, :, 2], normg[2], w["w1t"][1], w["w3t"][1], w["w2"][1])
    saved = dict(z=z, f_a=f_a, z1=z1, pp=pp, q=q, k=k, v=v, gg=gg, pc=pc, po=po, co=co, oa=oa, ob=ob, out=out, z2=z2, f_b=f_b)
    return z3, saved


def _layer_bwd(dz3, sv, mod, normg, w, small):
    S = dz3.shape[1]
    B = dz3.shape[0]
    T = B * S
    cos, sin = _rope_tables(S)
    fwd_sched, bwd_sched = _schedules(S)
    dz2, dmod_b, dg_b, gw1t_b, gw3t_b, gw2_b = _ffn_bwd(sv["z2"], dz3, sv["f_b"], mod[:, :, 2], normg[2],
                                                          w["w1t"][1], w["w3t"][1], w["w2"][1])
    ro = (small["gng"], small["lng"], small["lnb"])
    dpo, do, dgg, dco, cat, dout, dmod_gate, dgng, dlng, dlnb = _mix_out_bwd(
        dz2, sv["out"], sv["po"], sv["oa"], sv["ob"], sv["gg"], sv["co"], ro, mod[:, :, 1], w["wout"])
    gwout = _tn_matmul(cat.reshape(T, D), dout.reshape(T, D))
    dqa, dqb = _retention(do, sv["v"], sv["k"], small["dec_f"], small["dec_b"], *fwd_sched)
    dka, dkb = _retention(sv["v"], do, sv["q"], small["dec_f"], small["dec_b"], *bwd_sched)
    dva, dvb = _retention(sv["k"], sv["q"], do, small["dec_f"], small["dec_b"], *bwd_sched)
    ddec = _retention_ddecay(sv["q"], sv["k"], sv["v"], do, small["dec_f"], small["dec_b"], *fwd_sched)
    dq, dk = _rope_bwd(dqa, dqb, dka, dkb, cos, sin)
    dpp, dpc, dbd, dps, ddw, ddb = _pool_conv_bwd(sv["pp"], sv["pc"], dpo, dco, small["bd"], small["pscale"],
                                                   small["dw"], small["db"])
    dz1, h, dp, dmod_m, dg_m = _mix_in_bwd(sv["z1"], dz2, dpp, dq, dk, dva + dvb, dgg, dpc, mod[:, :, 1], normg[1], w["wint"])
    gwint = _tn_matmul(dp.reshape(T, F), h.reshape(T, D))
    dz, dmod_a, dg_a, gw1t_a, gw3t_a, gw2_a = _ffn_bwd(sv["z"], dz1, sv["f_a"], mod[:, :, 0], normg[0],
                                                        w["w1t"][0], w["w3t"][0], w["w2"][0])
    dmod = jnp.stack([dmod_a, dmod_m + dmod_gate, dmod_b], axis=2)
    dnormg = jnp.stack([dg_a, dg_m, dg_b], axis=0)
    big = dict(w1t=[gw1t_a, gw1t_b], w3t=[gw3t_a, gw3t_b], w2=[gw2_a, gw2_b], wint=gwint, wout=gwout)
    g = PW // 4
    dpool_w = jnp.stack([dbd[i * g:(i + 1) * g, i * g:(i + 1) * g] for i in range(4)], axis=0)
    sm = dict(pool_w=dpool_w, pool_scale=dps[0], dec_f=ddec[:, 0, 0], dec_b=ddec[:, 1, 0], gng=dgng[0],
              conv_dw=ddw[0:CONV_K], conv_b=ddb[0], conv_ln_g=dlng[0], conv_ln_b=dlnb[0])
    return dz, dmod, dnormg, big, sm


def _small_params(pool_w, pool_scale, dec_f, dec_b, gng, conv_dw, conv_b, lng, lnb):
    return dict(bd=_block_diag(pool_w), pscale=pool_scale.reshape(1, PW), dec_f=_lanes(dec_f), dec_b=_lanes(dec_b),
                gng=gng.reshape(1, RW), dw=jnp.pad(conv_dw, ((0, 1), (0, 0))), db=conv_b.reshape(1, PW),
                lng=lng.reshape(1, PW), lnb=lnb.reshape(1, PW))


_WEIGHTS = ["c_ctx", "w_mod", "b_mod", "norm_g", "ffn_w1", "ffn_w3", "ffn_w2", "w_in", "w_out", "pool_w", "pool_scale",
            "ret_decay_fwd", "ret_decay_bwd", "ret_gn_g", "conv_dw", "conv_b", "conv_ln_g", "conv_ln_b", "final_g"]
_BIG = ["w_mod", "ffn_w1", "ffn_w3", "ffn_w2", "w_in", "w_out"]
_SMALL = [n for n in _WEIGHTS if n not in _BIG]


def _adam_any(w, g, m, v):
    shape = w.shape
    cols = shape[-1] if w.ndim >= 2 else 128
    outs = _adam(w.reshape(-1, cols), g.reshape(-1, cols), m.reshape(-1, cols), v.reshape(-1, cols))
    return [o.reshape(shape) for o in outs]


def _step(a):
    x, c, ctx = a["x"], a["c"], a["ctx"]
    B = x.shape[0]
    nex = N_DEV * B
    assert nex + B <= MROWS and ctx.shape[1] == LC and x.shape[1] % TM == 0
    xi, yi, ci = _coords()
    me = 4 * xi + 2 * yi + ci
    chip = 2 * xi + yi
    ncol = a["w_mod"].shape[2]

    shapes1 = [(B, D), (2, 3, D // N_CHIP), (2, CONV_K, PW // N_CHIP)]
    g1 = _all_gather_small(_pack([c, a["norm_g"], a["conv_dw"]]))
    per = [_unpack(g1[d], shapes1) for d in range(N_DEV)]
    c_all = jnp.concatenate([per[d][0] for d in range(N_DEV)], axis=0)
    norm_g_full = jnp.concatenate([per[2 * s][1] for s in range(N_CHIP)], axis=-1)
    conv_dw_full = jnp.concatenate([per[2 * s][2] for s in range(N_CHIP)], axis=-1)
    cctx = a["c_ctx"].reshape(1, D)
    c24 = jnp.concatenate([c_all] + [cctx] * B + [jnp.zeros((MROWS - nex - B, D), F32)], axis=0)

    bsh = lax.dynamic_slice(a["b_mod"], (0, chip * ncol), (2, ncol)).reshape(2, 1, ncol)
    mod_raw = _mod_fwd(c24, a["w_mod"], bsh)
    g2 = _all_gather_small(_pack([mod_raw]))
    mod_full = jnp.concatenate([_unpack(g2[2 * s], [(2, MROWS, ncol)])[0] for s in range(N_CHIP)], axis=-1)
    mods = []
    for l in range(2):
        lat = lax.dynamic_slice(mod_full[l], (B * me, 0), (B, N_MOD * D))
        cx = jnp.broadcast_to(mod_full[l, nex][None], (B, N_MOD * D))
        mods.append(jnp.stack([cx, lat], axis=1).reshape(B, 2, 3, 3, D))

    def t_bf16(w):
        return jnp.swapaxes(w, -1, -2).astype(BF16)

    w1t, w3t, w2 = t_bf16(a["ffn_w1"]), t_bf16(a["ffn_w3"]), a["ffn_w2"].astype(BF16)
    wint, wout = t_bf16(a["w_in"]), a["w_out"].astype(BF16)
    l704 = []
    for l in range(2):
        for i in range(2):
            l704 += [w1t[l, i], w3t[l, i], w2[l, i]]
    l704 += [wint[0], wint[1]]
    wg, wo = _gather_weights(jnp.stack(l704, axis=0), wout)
    wg = wg.reshape(N704, F, D)
    wo = wo.reshape(N256, D, D)
    ws = [dict(w1t=[wg[6 * l], wg[6 * l + 3]], w3t=[wg[6 * l + 1], wg[6 * l + 4]], w2=[wg[6 * l + 2], wg[6 * l + 5]],
               wint=wg[12 + l], wout=wo[l]) for l in range(2)]
    smalls = [_small_params(a["pool_w"][l], a["pool_scale"][l], a["ret_decay_fwd"][l], a["ret_decay_bwd"][l],
                            a["ret_gn_g"][l], conv_dw_full[l], a["conv_b"][l], a["conv_ln_g"][l], a["conv_ln_b"][l])
              for l in range(2)]
    normgs = [norm_g_full[l].reshape(3, 1, D) for l in range(2)]

    z = jnp.concatenate([ctx, x], axis=1)
    saved = []
    for l in range(2):
        z, sv = _layer_fwd(z, mods[l], normgs[l], ws[l], smalls[l])
        saved.append(sv)
    dz, dfinal_g, loss_part = _head(z, a["loss_target"], a["final_g"].reshape(1, D))
    back = [None, None]
    for l in (1, 0):
        dz, dmod, dnormg, big, sm = _layer_bwd(dz, saved[l], mods[l], normgs[l], ws[l], smalls[l])
        back[l] = (dmod, dnormg, big, sm)
    grad_x = dz[:, LC:]

    g704 = []
    for l in range(2):
        big = back[l][2]
        for i in range(2):
            g704 += [big["w1t"][i], big["w3t"][i], big["w2"][i]]
    g704 += [back[0][2]["wint"], back[1][2]["wint"]]
    g704 = [g.reshape(N_CHIP, 2, HSLAB, D) for g in g704]
    g256 = [back[l][2]["wout"].reshape(N_CHIP, 2, HOSLAB, D) for l in range(2)]
    both = _swap_halves(_sum_grads(_scatter_grads(g704, g256)))

    def slab(i):
        return both[:, i * HSLAB:(i + 1) * HSLAB].reshape(SLAB, D)

    def oslab(j):
        off = N704 * HSLAB + j * HOSLAB
        return both[:, off:off + HOSLAB].reshape(OSLAB, D)

    grads = {}
    grads["ffn_w1"] = jnp.stack([jnp.stack([slab(6 * l + 3 * i).T for i in range(2)]) for l in range(2)])
    grads["ffn_w3"] = jnp.stack([jnp.stack([slab(6 * l + 3 * i + 1).T for i in range(2)]) for l in range(2)])
    grads["ffn_w2"] = jnp.stack([jnp.stack([slab(6 * l + 3 * i + 2) for i in range(2)]) for l in range(2)])
    grads["w_in"] = jnp.stack([slab(12 + l).T for l in range(2)])
    grads["w_out"] = jnp.stack([oslab(l) for l in range(2)])

    dmods = [back[l][0].reshape(B, 2, N_MOD * D) for l in range(2)]
    pack_a = _pack([jnp.stack([dm[:, 1] for dm in dmods])])
    ka = pack_a.shape[0]
    sm = [back[l][3] for l in range(2)]
    sum_list = [jnp.stack([dm[:, 0] for dm in dmods]), jnp.stack([back[l][1][:, 0] for l in range(2)])]
    sm_keys = ["pool_w", "pool_scale", "dec_f", "dec_b", "gng", "conv_dw", "conv_b", "conv_ln_g", "conv_ln_b"]
    sum_list += [jnp.stack([sm[l][k] for l in range(2)]) for k in sm_keys]
    sum_list += [dfinal_g[0], loss_part[0, 0:1]]
    sum_shapes = [s.shape for s in sum_list]
    g3 = _all_gather_small(jnp.concatenate([pack_a, _pack(sum_list)], axis=0))
    dmx_all = jnp.concatenate([_unpack(g3[d, :ka], [(2, B, N_MOD * D)])[0] for d in range(N_DEV)], axis=1)
    summed = _unpack(_sum_devices(g3[:, ka:]), sum_shapes)
    dmy, dnorm_full = summed[0], summed[1]
    sgrad = dict(zip(sm_keys, summed[2:2 + len(sm_keys)]))
    loss = summed[-1].reshape(())

    dmod24 = jnp.concatenate([dmx_all, dmy, jnp.zeros((2, MROWS - nex - B, N_MOD * D), F32)], axis=1)
    dmod_my = lax.dynamic_slice(dmod24, (0, 0, chip * ncol), (2, MROWS, ncol))
    grads["w_mod"], dsc = _mod_bwd(c24, dmod_my, a["w_mod"])
    g4 = _all_gather_small(_pack([dsc[nex:nex + 8]]))
    dsc_parts = jnp.stack([_unpack(g4[2 * s], [(8, D)])[0] for s in range(N_CHIP)])
    dbmod, dcctx = _bmod_cctx_grad(dmod24, dsc_parts, cctx)

    grads["c_ctx"] = dcctx[0]
    grads["b_mod"] = dbmod.reshape(2, N_MOD * D)
    grads["norm_g"] = lax.dynamic_slice(dnorm_full, (0, 0, chip * (D // N_CHIP)), (2, 3, D // N_CHIP))
    grads["pool_w"] = sgrad["pool_w"]
    grads["pool_scale"] = sgrad["pool_scale"]
    grads["ret_decay_fwd"] = sgrad["dec_f"]
    grads["ret_decay_bwd"] = sgrad["dec_b"]
    grads["ret_gn_g"] = sgrad["gng"]
    grads["conv_dw"] = lax.dynamic_slice(sgrad["conv_dw"], (0, 0, chip * (PW // N_CHIP)), (2, CONV_K, PW // N_CHIP))
    grads["conv_b"] = sgrad["conv_b"]
    grads["conv_ln_g"] = sgrad["conv_ln_g"]
    grads["conv_ln_b"] = sgrad["conv_ln_b"]
    grads["final_g"] = summed[-2]

    delta, new_m, new_v = {}, {}, {}
    for n in _BIG:
        delta[n], new_m[n], new_v[n] = _adam_any(a[n], grads[n], a["m_" + n], a["v_" + n])
    shapes_s = [a[n].shape for n in _SMALL]
    packed = _adam(_pack([a[n] for n in _SMALL]), _pack([grads[n] for n in _SMALL]),
                   _pack([a["m_" + n] for n in _SMALL]), _pack([a["v_" + n] for n in _SMALL]))
    for res, out in zip(packed, (delta, new_m, new_v)):
        for n, val in zip(_SMALL, _unpack(res, shapes_s)):
            out[n] = val
    return (loss, grad_x, *[grads[n] for n in _WEIGHTS], *[delta[n] for n in _WEIGHTS],
            *[new_m[n] for n in _WEIGHTS], *[new_v[n] for n in _WEIGHTS])


def kernel(x, c, ctx, c_ctx, w_mod, b_mod, norm_g, ffn_w1, ffn_w3, ffn_w2, w_in, w_out, pool_w, pool_scale, ret_decay_fwd, ret_decay_bwd, ret_gn_g, conv_dw, conv_b, conv_ln_g, conv_ln_b, final_g, loss_target, m_c_ctx, m_w_mod, m_b_mod, m_norm_g, m_ffn_w1, m_ffn_w3, m_ffn_w2, m_w_in, m_w_out, m_pool_w, m_pool_scale, m_ret_decay_fwd, m_ret_decay_bwd, m_ret_gn_g, m_conv_dw, m_conv_b, m_conv_ln_g, m_conv_ln_b, m_final_g, v_c_ctx, v_w_mod, v_b_mod, v_norm_g, v_ffn_w1, v_ffn_w3, v_ffn_w2, v_w_in, v_w_out, v_pool_w, v_pool_scale, v_ret_decay_fwd, v_ret_decay_bwd, v_ret_gn_g, v_conv_dw, v_conv_b, v_conv_ln_g, v_conv_ln_b, v_final_g):
    return _step(dict(locals()))
```

```python
import functools

import jax
import jax.numpy as jnp
from jax import lax
from jax.experimental import pallas as pl
from jax.experimental.pallas import tpu as pltpu

F32 = jnp.float32
BF16 = jnp.bfloat16

D = 1024
F = 2816
FH = 1408
N_MOD = 9
LC = 256
TM = 256
HD = 128
NH = 4
RW = 512
PW = 256
CONV_K = 31
GRID_W = 64
EPS = 1e-6
K_SCALE = HD ** -0.5
N_DEV = 8
N_CHIP = 4
SLAB = F // N_CHIP
HSLAB = SLAB // 2
OSLAB = D // N_CHIP
HOSLAB = OSLAB // 2
N704 = 14
N256 = 2
ROWS = N704 * HSLAB + N256 * HOSLAB
VMEM_BIG = 60 * 1024 * 1024
MESH = pl.DeviceIdType.MESH

ADAM_LR = 0.001
ADAM_B1 = 0.9
ADAM_B2 = 0.999
ADAM_EPS = 1e-08
ADAM_WD = 0.01
ADAM_STEP = 10


def _nt(a, b):
    return lax.dot_general(a, b, (((1,), (1,)), ((), ())), preferred_element_type=F32)


def _nn(a, b):
    return lax.dot_general(a, b, (((1,), (0,)), ((), ())), preferred_element_type=F32)


def _tn(a, b):
    return lax.dot_general(a, b, (((0,), (0,)), ((), ())), preferred_element_type=F32)


def _params(vmem=None, sem=None):
    return pltpu.CompilerParams(dimension_semantics=sem, vmem_limit_bytes=vmem)


def _rms_mod(z, g, shift, scale):
    y = z * lax.rsqrt(jnp.mean(z * z, axis=-1, keepdims=True) + EPS)
    return (y * g) * (1.0 + scale) + shift


def _acc(ref, val, first):
    @pl.when(first)
    def _():
        ref[...] = val

    @pl.when(jnp.logical_not(first))
    def _():
        ref[...] += val


def _tok(width):
    return pl.BlockSpec((None, TM, width), lambda b, t: (b, t, 0))


def _modspec():
    return pl.BlockSpec((None, None, 3, D), lambda b, t: (b, jnp.minimum(t, 1), 0, 0))


def _const(shape):
    nd = len(shape)
    return pl.BlockSpec(shape, lambda b, t: (0,) * nd)


def _wspec(w):
    stack, idx = w
    return pl.BlockSpec((None,) + stack.shape[1:], lambda b, t: (idx, 0, 0), pipeline_mode=pl.Buffered(1))


def _ffn_fwd(z, mod, g, w1t, w3t, w2):
    B, S, _ = z.shape

    def body(z_ref, mod_ref, g_ref, w1_ref, w3_ref, w2_ref, zo_ref, f_ref):
        zt = z_ref[...]
        h = _rms_mod(zt, g_ref[...], mod_ref[0:1, :], mod_ref[1:2, :]).astype(BF16)
        f = jnp.zeros((TM, D), F32)
        for c in range(F // FH):
            rows = slice(c * FH, (c + 1) * FH)
            u1 = _nt(h, w1_ref[rows, :])
            u3 = _nt(h, w3_ref[rows, :])
            a = (u1 * jax.nn.sigmoid(u1) * u3).astype(BF16)
            f = f + _nn(a, w2_ref[rows, :])
        f_ref[...] = f
        zo_ref[...] = zt + 0.5 * mod_ref[2:3, :] * f

    return pl.pallas_call(
        body, name="ffn_fwd", grid=(B, S // TM),
        in_specs=[_tok(D), _modspec(), _const((1, D)), _wspec(w1t), _wspec(w3t), _wspec(w2)],
        out_specs=[_tok(D), _tok(D)],
        out_shape=[jax.ShapeDtypeStruct((B, S, D), F32)] * 2,
        compiler_params=_params(VMEM_BIG, ("arbitrary", "arbitrary")),
    )(z, mod, g, w1t[0], w3t[0], w2[0])


def _ffn_bwd(z, dzo, f, mod, g, w1t, w3t, w2):
    B, S, _ = z.shape

    def body(z_ref, dzo_ref, f_ref, mod_ref, g_ref, w1_ref, w3_ref, w2_ref,
             dz_ref, h_ref, du1_ref, du3_ref, a_ref, do_ref, dmod_ref, dg_ref):
        b, t = pl.program_id(0), pl.program_id(1)
        zt = z_ref[...]
        dzo = dzo_ref[...]
        gate = mod_ref[2:3, :]
        h32, vjp_h = jax.vjp(_rms_mod, zt, g_ref[...], mod_ref[0:1, :], mod_ref[1:2, :])
        h = h32.astype(BF16)
        h_ref[...] = h
        do = (0.5 * gate * dzo).astype(BF16)
        do_ref[...] = do
        dgate = jnp.sum(0.5 * f_ref[...] * dzo, axis=0, keepdims=True)
        dh = jnp.zeros((TM, D), F32)
        for c in range(F // FH):
            rows = slice(c * FH, (c + 1) * FH)
            u1 = _nt(h, w1_ref[rows, :])
            u3 = _nt(h, w3_ref[rows, :])
            sg = jax.nn.sigmoid(u1)
            s = u1 * sg
            a_ref[:, rows] = (s * u3).astype(BF16)
            da = _nt(do, w2_ref[rows, :])
            du3 = (da * s).astype(BF16)
            du1 = (da * u3 * (sg * (1.0 + u1 * (1.0 - sg)))).astype(BF16)
            du1_ref[:, rows] = du1
            du3_ref[:, rows] = du3
            dh = dh + _nn(du1, w1_ref[rows, :]) + _nn(du3, w3_ref[rows, :])
        dz_h, dg, dshift, dscale = vjp_h(dh)
        dz_ref[...] = dzo + dz_h
        _acc(dmod_ref, jnp.concatenate([dshift, dscale, dgate], axis=0), t <= 1)
        _acc(dg_ref, dg, jnp.logical_and(b == 0, t == 0))

    T = B * S
    outs = pl.pallas_call(
        body, name="ffn_bwd", grid=(B, S // TM),
        in_specs=[_tok(D), _tok(D), _tok(D), _modspec(), _const((1, D)), _wspec(w1t), _wspec(w3t), _wspec(w2)],
        out_specs=[_tok(D), _tok(D), _tok(F), _tok(F), _tok(F), _tok(D), _modspec(), _const((1, D))],
        out_shape=[jax.ShapeDtypeStruct((B, S, D), F32), jax.ShapeDtypeStruct((B, S, D), BF16),
                   jax.ShapeDtypeStruct((B, S, F), BF16), jax.ShapeDtypeStruct((B, S, F), BF16),
                   jax.ShapeDtypeStruct((B, S, F), BF16), jax.ShapeDtypeStruct((B, S, D), BF16),
                   jax.ShapeDtypeStruct((B, 2, 3, D), F32), jax.ShapeDtypeStruct((1, D), F32)],
        compiler_params=_params(VMEM_BIG, ("arbitrary", "arbitrary")),
    )(z, dzo, f, mod, g, w1t[0], w3t[0], w2[0])
    dz, h, du1, du3, a, do, dmod, dg = outs
    gw1t = _tn_matmul(du1.reshape(T, F), h.reshape(T, D))
    gw3t = _tn_matmul(du3.reshape(T, F), h.reshape(T, D))
    gw2 = _tn_matmul(a.reshape(T, F), do.reshape(T, D))
    return dz, dmod, dg, gw1t, gw3t, gw2


def _tn_matmul(a, b):
    T, M = a.shape
    N = b.shape[1]
    MB = FH if M > FH else M
    TT = 512 if T % 512 == 0 else TM
    nt = T // TT

    def body(a_ref, b_ref, o_ref, acc_ref):
        t = pl.program_id(1)
        prod = _tn(a_ref[...], b_ref[...])
        _acc(acc_ref, prod, t == 0)

        @pl.when(t == nt - 1)
        def _():
            o_ref[...] = acc_ref[...].astype(BF16)

    return pl.pallas_call(
        body, name="tn_matmul", grid=(M // MB, nt),
        in_specs=[pl.BlockSpec((TT, MB), lambda i, t: (t, i)), pl.BlockSpec((TT, N), lambda i, t: (t, 0))],
        out_specs=pl.BlockSpec((MB, N), lambda i, t: (i, 0)),
        out_shape=jax.ShapeDtypeStruct((M, N), BF16),
        scratch_shapes=[pltpu.VMEM((MB, N), F32)],
        compiler_params=_params(VMEM_BIG, ("arbitrary", "arbitrary")),
    )(a, b)


def _swap32(x):
    n = x.shape[1]
    lane = lax.broadcasted_iota(jnp.int32, x.shape, 1)
    return jnp.where((lane % 64) < 32, pltpu.roll(x, n - 32, 1), pltpu.roll(x, 32, 1))


def _rope(x, cos, sin):
    return x * cos + _swap32(x) * sin


def _rope_t(dy, cos, sin):
    return dy * cos + _swap32(dy * sin)


def _rope_tables(S):
    L = S - LC
    n_freq = HD // 4
    inv = 10000.0 ** (-jnp.arange(n_freq, dtype=F32) / n_freq)
    i = jnp.arange(L)
    row = (i // GRID_W).astype(F32)
    col = (i % GRID_W).astype(F32)
    ang_r = row[:, None] * inv[None]
    ang_c = col[:, None] * inv[None]
    ang = jnp.concatenate([ang_r, ang_r, ang_c, ang_c], axis=1)
    ang = jnp.concatenate([jnp.zeros((LC, HD), F32), ang], axis=0)
    sign = jnp.where((jnp.arange(HD) % 64) < 32, -1.0, 1.0).astype(F32)
    return jnp.cos(ang), jnp.sin(ang) * sign[None]


def _tabspec():
    return pl.BlockSpec((TM, HD), lambda b, t: (t, 0))


def _mix_in_fwd(z, mod, g, wint, cos, sin):
    B, S, _ = z.shape

    def body(z_ref, mod_ref, g_ref, w_ref, cos_ref, sin_ref, pp_ref, q_ref, k_ref, v_ref, gg_ref, pc_ref):
        h = _rms_mod(z_ref[...], g_ref[...], mod_ref[0:1, :], mod_ref[1:2, :]).astype(BF16)
        p = _nt(h, w_ref[...])
        cos = jnp.tile(cos_ref[...], (1, NH))
        sin = jnp.tile(sin_ref[...], (1, NH))
        pp_ref[...] = p[:, 0:PW]
        q_ref[...] = _rope(p[:, PW:PW + RW], cos, sin)
        k_ref[...] = _rope(p[:, PW + RW:PW + 2 * RW], cos, sin) * K_SCALE
        v_ref[...] = p[:, PW + 2 * RW:PW + 3 * RW]
        gg_ref[...] = p[:, PW + 3 * RW:PW + 4 * RW]
        pc_ref[...] = p[:, PW + 4 * RW:]

    return pl.pallas_call(
        body, name="mix_in_fwd", grid=(B, S // TM),
        in_specs=[_tok(D), _modspec(), _const((1, D)), _wspec(wint), _tabspec(), _tabspec()],
        out_specs=[_tok(PW), _tok(RW), _tok(RW), _tok(RW), _tok(RW), _tok(2 * PW)],
        out_shape=[jax.ShapeDtypeStruct((B, S, PW), F32)] + [jax.ShapeDtypeStruct((B, S, RW), F32)] * 5,
        compiler_params=_params(VMEM_BIG, ("arbitrary", "arbitrary")),
    )(z, mod, g, wint[0], cos, sin)


def _mix_in_bwd(z, dzo, dpp, dq, dk, dv, dgg, dpc, mod, g, wint):
    B, S, _ = z.shape

    def body(z_ref, dzo_ref, dpp_ref, dq_ref, dk_ref, dv_ref, dgg_ref, dpc_ref, mod_ref, g_ref, w_ref,
             dz_ref, h_ref, dp_ref, dmod_ref, dg_ref):
        b, t = pl.program_id(0), pl.program_id(1)
        h32, vjp_h = jax.vjp(_rms_mod, z_ref[...], g_ref[...], mod_ref[0:1, :], mod_ref[1:2, :])
        h_ref[...] = h32.astype(BF16)
        dp = jnp.concatenate([dpp_ref[...], dq_ref[...], dk_ref[...], dv_ref[...], dgg_ref[...], dpc_ref[...]],
                             axis=1).astype(BF16)
        dp_ref[...] = dp
        dh = _nn(dp, w_ref[...])
        dz_h, dg, dshift, dscale = vjp_h(dh)
        dz_ref[...] = dzo_ref[...] + dz_h
        _acc(dmod_ref, jnp.concatenate([dshift, dscale, jnp.zeros_like(dshift)], axis=0), t <= 1)
        _acc(dg_ref, dg, jnp.logical_and(b == 0, t == 0))

    return pl.pallas_call(
        body, name="mix_in_bwd", grid=(B, S // TM),
        in_specs=[_tok(D), _tok(D), _tok(PW), _tok(RW), _tok(RW), _tok(RW), _tok(RW), _tok(2 * PW),
                  _modspec(), _const((1, D)), _wspec(wint)],
        out_specs=[_tok(D), _tok(D), _tok(F), _modspec(), _const((1, D))],
        out_shape=[jax.ShapeDtypeStruct((B, S, D), F32), jax.ShapeDtypeStruct((B, S, D), BF16),
                   jax.ShapeDtypeStruct((B, S, F), BF16), jax.ShapeDtypeStruct((B, 2, 3, D), F32),
                   jax.ShapeDtypeStruct((1, D), F32)],
        compiler_params=_params(VMEM_BIG, ("arbitrary", "arbitrary")),
    )(z, dzo, dpp, dq, dk, dv, dgg, dpc, mod, g, wint[0])


def _rope_bwd(dqa, dqb, dka, dkb, cos, sin):
    B, S, _ = dqa.shape

    def body(dqa_ref, dqb_ref, dka_ref, dkb_ref, cos_ref, sin_ref, dq_ref, dk_ref):
        cos = jnp.tile(cos_ref[...], (1, NH))
        sin = jnp.tile(sin_ref[...], (1, NH))
        dq_ref[...] = _rope_t(dqa_ref[...] + dqb_ref[...], cos, sin)
        dk_ref[...] = _rope_t(dka_ref[...] + dkb_ref[...], cos, sin) * K_SCALE

    return pl.pallas_call(
        body, name="rope_bwd", grid=(B, S // TM),
        in_specs=[_tok(RW)] * 4 + [_tabspec(), _tabspec()],
        out_specs=[_tok(RW), _tok(RW)],
        out_shape=[jax.ShapeDtypeStruct((B, S, RW), F32)] * 2,
        compiler_params=_params(None, ("arbitrary", "arbitrary")),
    )(dqa, dqb, dka, dkb, cos, sin)


def _log_sigmoid(x):
    return jnp.minimum(x, 0.0) - jnp.log(1.0 + jnp.exp(-jnp.abs(x)))


def _retention(a, b, c, dec_a, dec_b, sched_a, sched_b):
    B, S, _ = a.shape
    C = TM

    def body(a_ref, b_ref, c_ref, da_ref, db_ref, oa_ref, ob_ref):
        ii = lax.broadcasted_iota(jnp.int32, (C, C), 0)
        jj = lax.broadcasted_iota(jnp.int32, (C, C), 1)
        pos = lax.broadcasted_iota(jnp.int32, (C, 1), 0).astype(F32)
        for dec_ref, o_ref, (order, causal, strict) in ((da_ref, oa_ref, sched_a), (db_ref, ob_ref, sched_b)):
            lg = _log_sigmoid(dec_ref[...])
            lg1 = lg[:, 0:1]
            dist = ((ii - jj) if causal else (jj - ii)).astype(F32)
            mask = (dist > 0.0) if strict else (dist >= 0.0)
            decay = jnp.where(mask, jnp.exp(jnp.maximum(dist, 0.0) * lg1), 0.0)
            p = pos if causal else (C - 1.0 - pos)
            w_q = jnp.exp((p + 1.0) * lg1)
            w_k = jnp.exp((C - 1.0 - p) * lg1)
            chunk_decay = jnp.exp(C * lg)
            state = jnp.zeros((HD, HD), F32)
            for n in order:
                rows = pl.ds(n * C, C)
                at, bt, ct = a_ref[rows, :], b_ref[rows, :], c_ref[rows, :]
                cb = ct.astype(BF16)
                scores = _nt(at.astype(BF16), bt.astype(BF16)) * decay
                o = _nn(scores.astype(BF16), cb)
                o = o + _nn((at * w_q).astype(BF16), state.astype(BF16))
                o_ref[rows, :] = o
                state = chunk_decay * state + _tn((bt * w_k).astype(BF16), cb)

    seq = pl.BlockSpec((None, S, HD), lambda b, h: (b, 0, h))
    dspec = pl.BlockSpec((None, 1, HD), lambda b, h: (h, 0, 0))
    return pl.pallas_call(
        body, name="retention", grid=(B, NH),
        in_specs=[seq, seq, seq, dspec, dspec], out_specs=[seq, seq],
        out_shape=[jax.ShapeDtypeStruct((B, S, RW), F32)] * 2,
        compiler_params=_params(VMEM_BIG, ("arbitrary", "arbitrary")),
    )(a, b, c, dec_a, dec_b)


def _retention_ddecay(q, k, v, do, dec_a, dec_b, sched_a, sched_b):
    B, S, _ = q.shape
    C = TM

    def body(q_ref, k_ref, v_ref, do_ref, da_ref, db_ref, o_ref):
        ii = lax.broadcasted_iota(jnp.int32, (C, C), 0)
        jj = lax.broadcasted_iota(jnp.int32, (C, C), 1)
        pos = lax.broadcasted_iota(jnp.int32, (C, 1), 0).astype(F32)
        vals = []
        for dec_ref, (order, causal, strict) in ((da_ref, sched_a), (db_ref, sched_b)):
            x = dec_ref[...]
            lg = _log_sigmoid(x)
            lg1 = lg[:, 0:1]
            dist = ((ii - jj) if causal else (jj - ii)).astype(F32)
            mask = (dist > 0.0) if strict else (dist >= 0.0)
            ddecay = jnp.where(mask, dist * jnp.exp(jnp.maximum(dist, 0.0) * lg1), 0.0)
            p = pos if causal else (C - 1.0 - pos)
            w_q = jnp.exp((p + 1.0) * lg1)
            w_k = jnp.exp((C - 1.0 - p) * lg1)
            chunk_decay = jnp.exp(C * lg)
            state = jnp.zeros((HD, HD), F32)
            dstate = jnp.zeros((HD, HD), F32)
            tot = jnp.zeros((), F32)
            for n in order:
                rows = pl.ds(n * C, C)
                qt, kt, vt, dot = q_ref[rows, :], k_ref[rows, :], v_ref[rows, :], do_ref[rows, :]
                vb = vt.astype(BF16)
                scores = _nt(qt.astype(BF16), kt.astype(BF16))
                dscores = _nt(dot.astype(BF16), vb)
                qw = (qt * w_q).astype(BF16)
                cross = _nn(qw, state.astype(BF16))
                dcross = _nn(qw, dstate.astype(BF16))
                tot = tot + jnp.sum(scores * dscores * ddecay) + jnp.sum(((p + 1.0) * cross + dcross) * dot)
                kv = _tn((kt * w_k).astype(BF16), vb)
                dkv = _tn((kt * ((C - 1.0 - p) * w_k)).astype(BF16), vb)
                dstate = chunk_decay * (dstate + C * state) + dkv
                state = chunk_decay * state + kv
            vals.append(tot * jax.nn.sigmoid(-x))
        row = lax.broadcasted_iota(jnp.int32, (8, HD), 0)
        tile = jnp.where(row == 0, vals[0], 0.0) + jnp.where(row == 1, vals[1], 0.0)
        _acc(o_ref, tile, pl.program_id(1) == 0)

    seq = pl.BlockSpec((None, S, HD), lambda h, b: (b, 0, h))
    dspec = pl.BlockSpec((None, 1, HD), lambda h, b: (h, 0, 0))
    return pl.pallas_call(
        body, name="retention_ddecay", grid=(NH, B),
        in_specs=[seq, seq, seq, seq, dspec, dspec], out_specs=pl.BlockSpec((None, 8, HD), lambda h, b: (h, 0, 0)),
        out_shape=jax.ShapeDtypeStruct((NH, 8, HD), F32),
        compiler_params=_params(VMEM_BIG, ("arbitrary", "arbitrary")),
    )(q, k, v, do, dec_a, dec_b)


def _schedules(S):
    n = S // TM
    lat_up = tuple(range(1, n))
    lat_down = tuple(range(n - 1, 0, -1))
    fwd = (((0,) + lat_up, True, False), ((0,) + lat_down, False, True))
    bwd = ((lat_down + (0,), False, False), (lat_up + (0,), True, True))
    return fwd, bwd


def _shift_rows(x, d):
    if d == 0:
        return x
    S = x.shape[0]
    t = lax.broadcasted_iota(jnp.int32, x.shape, 0)
    tt = t + d
    lo = jnp.where(t < LC, 0, LC)
    hi = jnp.where(t < LC, LC, S)
    return jnp.where((tt >= lo) & (tt < hi), pltpu.roll(x, (-d) % S, 0), 0.0)


@functools.partial(jax.custom_vjp, nondiff_argnums=(1,))
def _shift(x, d):
    return _shift_rows(x, d)


_shift.defvjp(lambda x, d: (_shift_rows(x, d), None), lambda d, _, g: (_shift_rows(g, -d),))


def _pool_fn(p, bd, pscale):
    lane = lax.broadcasted_iota(jnp.int32, p.shape, 1)
    grp = lane // (PW // 4)
    half = jnp.where(grp == 0, 1, jnp.where(grp == 1, 2, jnp.where(grp == 2, 4, 8)))
    ones = jnp.ones(p.shape, F32)
    acc = jnp.zeros(p.shape, F32)
    cnt = jnp.zeros(p.shape, F32)
    for d in range(-8, 8):
        inwin = ((d >= -half) & (d < half)).astype(F32)
        acc = acc + _shift(p, d) * inwin
        cnt = cnt + _shift_rows(ones, d) * inwin
    pooled = acc / cnt - p
    mixed = _nn(pooled.astype(BF16), bd.astype(BF16))
    return mixed * pscale


def _dwconv_raw(zc, dw):
    y = jnp.zeros(zc.shape, F32)
    for k in range(CONV_K):
        y = y + _shift_rows(zc, k - CONV_K // 2) * dw[k:k + 1, :]
    return y


@jax.custom_vjp
def _dwconv(zc, dw):
    return _dwconv_raw(zc, dw)


def _dwconv_fwd(zc, dw):
    return _dwconv_raw(zc, dw), (zc, dw)


def _dwconv_bwd(res, g):
    zc, dw = res
    dz = jnp.zeros(zc.shape, F32)
    ddw = jnp.zeros(dw.shape, F32)
    row = lax.broadcasted_iota(jnp.int32, dw.shape, 0)
    for k in range(CONV_K):
        dz = dz + _shift_rows(g, CONV_K // 2 - k) * dw[k:k + 1, :]
        r = jnp.sum(g * _shift_rows(zc, k - CONV_K // 2), axis=0, keepdims=True)
        ddw = ddw + jnp.where(row == k, r, 0.0)
    return dz, ddw


_dwconv.defvjp(_dwconv_fwd, _dwconv_bwd)


def _conv_fn(u, dw, db):
    zc = u[:, :PW] * jax.nn.sigmoid(u[:, PW:])
    return _dwconv(zc, dw) + db


def _ln_swish(y, lng, lnb):
    mu = jnp.mean(y, axis=-1, keepdims=True)
    yc = y - mu
    var = jnp.mean(yc * yc, axis=-1, keepdims=True)
    yn = yc * lax.rsqrt(var + EPS) * lng + lnb
    return yn * jax.nn.sigmoid(yn)


def _seq(shape, single=False):
    return pl.BlockSpec((None,) + shape, lambda b: (b, 0, 0), pipeline_mode=pl.Buffered(1) if single else None)


def _c1(shape):
    nd = len(shape)
    return pl.BlockSpec(shape, lambda b: (0,) * nd)


def _seq_apply(fn, name, xs, consts, width):
    B, S, w = xs.shape

    def body(x_ref, *refs):
        refs[-1][...] = fn(x_ref[...], *[r[...] for r in refs[:-1]])

    return pl.pallas_call(
        body, name=name, grid=(B,),
        in_specs=[_seq((S, w))] + [_c1(c.shape) for c in consts], out_specs=_seq((S, width)),
        out_shape=jax.ShapeDtypeStruct((B, S, width), F32),
        compiler_params=_params(VMEM_BIG, ("arbitrary",)),
    )(xs, *consts)


def _seq_vjp(fn, name, xs, consts, dout):
    B, S, w = xs.shape
    n = len(consts)

    def body(x_ref, d_ref, *refs):
        first = pl.program_id(0) == 0
        _, vjp = jax.vjp(fn, x_ref[...], *[r[...] for r in refs[:n]])
        grads = vjp(d_ref[...])
        refs[n][...] = grads[0]
        for ref, val in zip(refs[n + 1:], grads[1:]):
            _acc(ref, val, first)

    return pl.pallas_call(
        body, name=name, grid=(B,),
        in_specs=[_seq((S, w), True), _seq((S, dout.shape[2]), True)] + [_c1(c.shape) for c in consts],
        out_specs=[_seq((S, w))] + [_c1(c.shape) for c in consts],
        out_shape=[jax.ShapeDtypeStruct((B, S, w), F32)] + [jax.ShapeDtypeStruct(c.shape, F32) for c in consts],
        compiler_params=_params(VMEM_BIG, ("arbitrary",)),
    )(xs, dout, *consts)


def _pool_conv_fwd(pp, pc, bd, pscale, dw, db):
    return (_seq_apply(_pool_fn, "pool_fwd", pp, (bd, pscale), PW),
            _seq_apply(_conv_fn, "conv_fwd", pc, (dw, db), PW))


def _pool_conv_bwd(pp, pc, dpo, dco, bd, pscale, dw, db):
    dpp, dbd, dps = _seq_vjp(_pool_fn, "pool_bwd", pp, (bd, pscale), dpo)
    dpc, ddw, ddb = _seq_vjp(_conv_fn, "conv_bwd", pc, (dw, db), dco)
    return dpp, dpc, dbd, dps, ddw, ddb


def _cat_fn(po, oa, ob, gg, co, gng, lng, lnb):
    o = oa + ob
    outs = []
    for h in range(NH):
        oh = o[:, h * HD:(h + 1) * HD]
        mu = jnp.mean(oh, axis=-1, keepdims=True)
        oc = oh - mu
        var = jnp.mean(oc * oc, axis=-1, keepdims=True)
        outs.append(oc * lax.rsqrt(var + EPS))
    ret = jnp.concatenate(outs, axis=1) * gng * (gg * jax.nn.sigmoid(gg))
    return jnp.concatenate([po, ret, _ln_swish(co, lng, lnb)], axis=1)


def _mix_out_fwd(z, po, oa, ob, gg, co, ro, mod, wout):
    B, S, _ = z.shape

    def body(z_ref, po_ref, oa_ref, ob_ref, gg_ref, co_ref, gn_ref, lg_ref, lb_ref, mod_ref, w_ref, zo_ref, out_ref):
        cat = _cat_fn(po_ref[...], oa_ref[...], ob_ref[...], gg_ref[...], co_ref[...], gn_ref[...], lg_ref[...], lb_ref[...])
        out = _nn(cat.astype(BF16), w_ref[...])
        out_ref[...] = out
        zo_ref[...] = z_ref[...] + mod_ref[2:3, :] * out

    return pl.pallas_call(
        body, name="mix_out_fwd", grid=(B, S // TM),
        in_specs=[_tok(D), _tok(PW), _tok(RW), _tok(RW), _tok(RW), _tok(PW), _const((1, RW)), _const((1, PW)),
                  _const((1, PW)), _modspec(), _wspec(wout)],
        out_specs=[_tok(D), _tok(D)],
        out_shape=[jax.ShapeDtypeStruct((B, S, D), F32)] * 2,
        compiler_params=_params(None, ("arbitrary", "arbitrary")),
    )(z, po, oa, ob, gg, co, *ro, mod, wout[0])


def _mix_out_bwd(dzo, out, po, oa, ob, gg, co, ro, mod, wout):
    B, S, _ = dzo.shape

    def body(dzo_ref, out_ref, po_ref, oa_ref, ob_ref, gg_ref, co_ref, gn_ref, lg_ref, lb_ref, mod_ref, w_ref,
             dpo_ref, do_ref, dgg_ref, dco_ref, cat_ref, dout_ref, dmod_ref, dgn_ref, dlg_ref, dlb_ref):
        b, t = pl.program_id(0), pl.program_id(1)
        dzo = dzo_ref[...]
        cat, vjp = jax.vjp(_cat_fn, po_ref[...], oa_ref[...], ob_ref[...], gg_ref[...], co_ref[...], gn_ref[...],
                           lg_ref[...], lb_ref[...])
        cat_ref[...] = cat.astype(BF16)
        dout = (mod_ref[2:3, :] * dzo).astype(BF16)
        dout_ref[...] = dout
        dgate = jnp.sum(out_ref[...] * dzo, axis=0, keepdims=True)
        dcat = _nt(dout, w_ref[...])
        dpo, doa, _, dgg, dco, dgn, dlg, dlb = vjp(dcat)
        dpo_ref[...] = dpo
        do_ref[...] = doa
        dgg_ref[...] = dgg
        dco_ref[...] = dco
        zero = jnp.zeros_like(dgate)
        _acc(dmod_ref, jnp.concatenate([zero, zero, dgate], axis=0), t <= 1)
        first = jnp.logical_and(b == 0, t == 0)
        _acc(dgn_ref, dgn, first)
        _acc(dlg_ref, dlg, first)
        _acc(dlb_ref, dlb, first)

    return pl.pallas_call(
        body, name="mix_out_bwd", grid=(B, S // TM),
        in_specs=[_tok(D), _tok(D), _tok(PW), _tok(RW), _tok(RW), _tok(RW), _tok(PW), _const((1, RW)), _const((1, PW)),
                  _const((1, PW)), _modspec(), _wspec(wout)],
        out_specs=[_tok(PW), _tok(RW), _tok(RW), _tok(PW), _tok(D), _tok(D), _modspec(), _const((1, RW)),
                   _const((1, PW)), _const((1, PW))],
        out_shape=[jax.ShapeDtypeStruct((B, S, PW), F32), jax.ShapeDtypeStruct((B, S, RW), F32),
                   jax.ShapeDtypeStruct((B, S, RW), F32), jax.ShapeDtypeStruct((B, S, PW), F32),
                   jax.ShapeDtypeStruct((B, S, D), BF16), jax.ShapeDtypeStruct((B, S, D), BF16),
                   jax.ShapeDtypeStruct((B, 2, 3, D), F32), jax.ShapeDtypeStruct((1, RW), F32),
                   jax.ShapeDtypeStruct((1, PW), F32), jax.ShapeDtypeStruct((1, PW), F32)],
        compiler_params=_params(None, ("arbitrary", "arbitrary")),
    )(dzo, out, po, oa, ob, gg, co, *ro, mod, wout[0])


def _rms(z, g):
    return z * lax.rsqrt(jnp.mean(z * z, axis=-1, keepdims=True) + EPS) * g


def _head(z, target, fg):
    B, S, _ = z.shape

    def body(z_ref, t_ref, g_ref, dz_ref, dg_ref, loss_ref):
        b, t = pl.program_id(0), pl.program_id(1)
        first = jnp.logical_and(b == 0, t == 0)

        @pl.when(t == 0)
        def _():
            dz_ref[...] = jnp.zeros((TM, D), F32)

        @pl.when(first)
        def _():
            dg_ref[...] = jnp.zeros((1, D), F32)
            loss_ref[...] = jnp.zeros((8, 128), F32)

        @pl.when(t > 0)
        def _():
            y, vjp = jax.vjp(_rms, z_ref[...], g_ref[...])
            err = y - t_ref[...]
            dz, dg = vjp(err * (1.0 / D))
            dz_ref[...] = dz
            dg_ref[...] += dg
            loss_ref[...] += 0.5 * jnp.sum(err * err) * (1.0 / D)

    return pl.pallas_call(
        body, name="head", grid=(B, S // TM),
        in_specs=[_tok(D), pl.BlockSpec((None, TM, D), lambda b, t: (b, jnp.maximum(t - 1, 0), 0)), _const((1, D))],
        out_specs=[_tok(D), _const((1, D)), _const((8, 128))],
        out_shape=[jax.ShapeDtypeStruct((B, S, D), F32), jax.ShapeDtypeStruct((1, D), F32),
                   jax.ShapeDtypeStruct((8, 128), F32)],
        compiler_params=_params(None, ("arbitrary", "arbitrary")),
    )(z, target, fg)


MROWS = 24
MCOL = 768


def _silu(x):
    return x * jax.nn.sigmoid(x)


def _mod_fwd(c24, wmod, bmod):
    ncol = wmod.shape[2]

    def body(c_ref, w_ref, b_ref, o_ref):
        sc = _silu(c_ref[...]).astype(BF16)
        o_ref[...] = _nn(sc, w_ref[...].astype(BF16)) + b_ref[...]

    return pl.pallas_call(
        body, name="mod_fwd", grid=(2, ncol // MCOL),
        in_specs=[pl.BlockSpec((MROWS, D), lambda l, j: (0, 0)), pl.BlockSpec((None, D, MCOL), lambda l, j: (l, 0, j)),
                  pl.BlockSpec((None, 1, MCOL), lambda l, j: (l, 0, j))],
        out_specs=pl.BlockSpec((None, MROWS, MCOL), lambda l, j: (l, 0, j)),
        out_shape=jax.ShapeDtypeStruct((2, MROWS, ncol), F32),
        compiler_params=_params(None, ("arbitrary", "arbitrary")),
    )(c24, wmod, bmod)


def _mod_bwd(c24, dmod, wmod):
    ncol = wmod.shape[2]

    def body(c_ref, d_ref, w_ref, dw_ref, dsc_ref):
        l, j = pl.program_id(0), pl.program_id(1)
        sc = _silu(c_ref[...]).astype(BF16)
        dm = d_ref[...].astype(BF16)
        dw_ref[...] = _tn(sc, dm)
        _acc(dsc_ref, _nt(dm, w_ref[...].astype(BF16)), jnp.logical_and(l == 0, j == 0))

    return pl.pallas_call(
        body, name="mod_bwd", grid=(2, ncol // MCOL),
        in_specs=[pl.BlockSpec((MROWS, D), lambda l, j: (0, 0)), pl.BlockSpec((None, MROWS, MCOL), lambda l, j: (l, 0, j)),
                  pl.BlockSpec((None, D, MCOL), lambda l, j: (l, 0, j))],
        out_specs=[pl.BlockSpec((None, D, MCOL), lambda l, j: (l, 0, j)), pl.BlockSpec((MROWS, D), lambda l, j: (0, 0))],
        out_shape=[jax.ShapeDtypeStruct((2, D, ncol), F32), jax.ShapeDtypeStruct((MROWS, D), F32)],
        compiler_params=_params(None, ("arbitrary", "arbitrary")),
    )(c24, dmod, wmod)


def _bmod_cctx_grad(dmod_full, dsc_parts, cctx):
    def body(d_ref, p_ref, c_ref, db_ref, dc_ref):
        db_ref[...] = jnp.sum(d_ref[...], axis=1, keepdims=True)
        tot = jnp.zeros((8, D), F32)
        for s in range(N_CHIP):
            tot = tot + p_ref[s]
        x = c_ref[...]
        sg = jax.nn.sigmoid(x)
        dc_ref[...] = jnp.sum(tot, axis=0, keepdims=True) * (sg * (1.0 + x * (1.0 - sg)))

    return pl.pallas_call(
        body, name="bmod_cctx_grad",
        out_shape=[jax.ShapeDtypeStruct((2, 1, N_MOD * D), F32), jax.ShapeDtypeStruct((1, D), F32)],
    )(dmod_full, dsc_parts, cctx)


def _adam_math(w, g, m, v):
    m = ADAM_B1 * m + (1.0 - ADAM_B1) * g
    v = ADAM_B2 * v + (1.0 - ADAM_B2) * (g * g)
    m_hat = m / (1.0 - ADAM_B1 ** ADAM_STEP)
    v_hat = v / (1.0 - ADAM_B2 ** ADAM_STEP)
    delta = -ADAM_LR * (m_hat / (jnp.sqrt(v_hat) + ADAM_EPS) + ADAM_WD * w)
    return delta, m, v


def _adam(w, g, m, v):
    R, Cc = w.shape
    if R * Cc * 4 <= (1 << 20):
        RB = R
    else:
        RB = 1 << (((1 << 18) // Cc).bit_length() - 1)
        assert R % RB == 0

    def body(w_ref, g_ref, m_ref, v_ref, d_ref, mo_ref, vo_ref):
        d, mn, vn = _adam_math(w_ref[...], g_ref[...], m_ref[...], v_ref[...])
        d_ref[...] = d
        mo_ref[...] = mn
        vo_ref[...] = vn

    spec = pl.BlockSpec((RB, Cc), lambda i: (i, 0))
    return pl.pallas_call(
        body, name="adam", grid=(R // RB,), in_specs=[spec] * 4, out_specs=[spec] * 3,
        out_shape=[jax.ShapeDtypeStruct((R, Cc), F32)] * 3,
        compiler_params=_params(None, ("arbitrary",)),
    )(w, g, m, v)


def _sum_devices(parts):
    K = parts.shape[1]

    def body(p_ref, o_ref):
        tot = p_ref[0]
        for i in range(1, N_DEV):
            tot = tot + p_ref[i]
        o_ref[...] = tot

    return pl.pallas_call(body, name="sum_devices", out_shape=jax.ShapeDtypeStruct((K, 128), F32))(parts)


SUM_ROWS = 432


def _pair_sum(mine, theirs):
    def body(a_ref, b_ref, o_ref):
        o_ref[...] = (a_ref[...].astype(F32) + b_ref[...].astype(F32)).astype(BF16)

    spec = pl.BlockSpec((None, SUM_ROWS, D), lambda s, i: (s, i, 0))
    return pl.pallas_call(
        body, name="pair_sum", grid=(N_CHIP, ROWS // SUM_ROWS), in_specs=[spec, spec], out_specs=spec,
        out_shape=jax.ShapeDtypeStruct((N_CHIP, ROWS, D), BF16),
        compiler_params=_params(None, ("arbitrary", "arbitrary")),
    )(mine, theirs)


def _sum_chips(q):
    def body(q_ref, o_ref):
        tot = q_ref[0].astype(F32)
        for i in range(1, N_CHIP):
            tot = tot + q_ref[i].astype(F32)
        o_ref[...] = tot

    return pl.pallas_call(
        body, name="sum_chips", grid=(ROWS // SUM_ROWS,),
        in_specs=[pl.BlockSpec((N_CHIP, SUM_ROWS, D), lambda i: (0, i, 0))],
        out_specs=pl.BlockSpec((SUM_ROWS, D), lambda i: (i, 0)),
        out_shape=jax.ShapeDtypeStruct((ROWS, D), F32),
        compiler_params=_params(None, ("arbitrary",)),
    )(q)


def _coords():
    return lax.axis_index("x"), lax.axis_index("y"), lax.axis_index("c")


_FLIPS = [(fx, fy, fc) for fx in (0, 1) for fy in (0, 1) for fc in (0, 1)][1:]


def _all_gather_small(buf):
    K = buf.shape[0]

    def body(in_ref, out_ref, send_sems, recv_sems, local_sem):
        x, y, c = _coords()
        me = 4 * x + 2 * y + c
        mine = pltpu.make_async_copy(in_ref, out_ref.at[me], local_sem)
        mine.start()
        sends = []
        for k, (fx, fy, fc) in enumerate(_FLIPS):
            peer = (x ^ fx, y ^ fy, c ^ fc)
            cp = pltpu.make_async_remote_copy(src_ref=in_ref, dst_ref=out_ref.at[me], send_sem=send_sems.at[k],
                                              recv_sem=recv_sems.at[k], device_id=peer, device_id_type=MESH)
            cp.start()
            sends.append(cp)
        for k, (fx, fy, fc) in enumerate(_FLIPS):
            src = 4 * (x ^ fx) + 2 * (y ^ fy) + (c ^ fc)
            pltpu.make_async_remote_copy(src_ref=in_ref, dst_ref=out_ref.at[src], send_sem=send_sems.at[k],
                                         recv_sem=recv_sems.at[k], device_id=(x, y, c), device_id_type=MESH).wait_recv()
        for cp in sends:
            cp.wait_send()
        mine.wait()

    return pl.pallas_call(
        body, name="all_gather_small",
        in_specs=[pl.BlockSpec(memory_space=pltpu.VMEM)], out_specs=pl.BlockSpec(memory_space=pltpu.VMEM),
        out_shape=jax.ShapeDtypeStruct((N_DEV, K, 128), F32),
        scratch_shapes=[pltpu.SemaphoreType.DMA((7,)), pltpu.SemaphoreType.DMA((7,)), pltpu.SemaphoreType.DMA],
        compiler_params=_params(VMEM_BIG),
    )(buf)


_CHIP_FLIPS = [(1, 0), (0, 1), (1, 1)]
_ANY = pl.BlockSpec(memory_space=pl.ANY)


def _gather_weights(w704, w256):
    n7, n2 = w704.shape[0], w256.shape[0]

    def body(a_ref, b_ref, oa_ref, ob_ref, send_sems, recv_sems, fsend_sems, frecv_sems, local_sems):
        x, y, c = _coords()
        s_me = 2 * x + y
        sib = (x, y, 1 - c)
        arrays = ((a_ref, oa_ref, n7), (b_ref, ob_ref, n2))

        def all_of(i):
            return arrays[i][1].at[:, 0, 0]

        for i, (src, dst, n) in enumerate(arrays):
            for t in range(n):
                pltpu.make_async_copy(src.at[t], dst.at[t, s_me], local_sems.at[i]).start()
        for j, (fx, fy) in enumerate(_CHIP_FLIPS):
            peer = (x ^ fx, y ^ fy, c)
            for i, (src, dst, n) in enumerate(arrays):
                for t in range(n):
                    pltpu.make_async_remote_copy(src_ref=src.at[t, c], dst_ref=dst.at[t, s_me, c],
                                                 send_sem=send_sems.at[2 * j + i], recv_sem=recv_sems.at[2 * j + i],
                                                 device_id=peer, device_id_type=MESH).start()
        for j, (fx, fy) in enumerate(_CHIP_FLIPS):
            s_src = 2 * (x ^ fx) + (y ^ fy)
            for i, (src, dst, n) in enumerate(arrays):
                pltpu.make_async_remote_copy(src_ref=all_of(i), dst_ref=all_of(i), send_sem=send_sems.at[2 * j + i],
                                             recv_sem=recv_sems.at[2 * j + i], device_id=sib, device_id_type=MESH).wait_recv()
                for t in range(n):
                    pltpu.make_async_remote_copy(src_ref=dst.at[t, s_src, c], dst_ref=dst.at[t, s_src, c],
                                                 send_sem=fsend_sems.at[2 * j + i], recv_sem=frecv_sems.at[2 * j + i],
                                                 device_id=sib, device_id_type=MESH).start()
        for k in range(6):
            i = k % 2
            for ssem, rsem in ((send_sems, recv_sems), (fsend_sems, frecv_sems)):
                pltpu.make_async_remote_copy(src_ref=all_of(i), dst_ref=all_of(i), send_sem=ssem.at[k], recv_sem=rsem.at[k],
                                             device_id=sib, device_id_type=MESH).wait_send()
            pltpu.make_async_remote_copy(src_ref=all_of(i), dst_ref=all_of(i), send_sem=fsend_sems.at[k],
                                         recv_sem=frecv_sems.at[k], device_id=sib, device_id_type=MESH).wait_recv()
        for i, (src, dst, n) in enumerate(arrays):
            pltpu.make_async_copy(src, dst.at[:, 0], local_sems.at[i]).wait()

    return pl.pallas_call(
        body, name="gather_weights", in_specs=[_ANY, _ANY], out_specs=[_ANY, _ANY],
        out_shape=[jax.ShapeDtypeStruct((n7, N_CHIP, 2, HSLAB, D), BF16), jax.ShapeDtypeStruct((n2, N_CHIP, 2, HOSLAB, D), BF16)],
        scratch_shapes=[pltpu.SemaphoreType.DMA((6,))] * 4 + [pltpu.SemaphoreType.DMA((2,))],
    )(w704, w256)


N_STREAMS = 12
CHUNK = ROWS // N_STREAMS


def _pair_exchange(g704, g256):
    n7, n2 = len(g704), len(g256)

    def body(*refs):
        srcs = refs[:n7 + n2]
        m_ref, p_ref = refs[n7 + n2:n7 + n2 + 2]
        send_sem, recv_sem, local_sem = refs[n7 + n2 + 2:]
        x, y, c = _coords()
        sib = (x, y, 1 - c)
        off = 0
        for i, ref in enumerate(srcs):
            n = HSLAB if i < n7 else HOSLAB
            for s in range(N_CHIP):
                pltpu.make_async_copy(ref.at[s, c], m_ref.at[s, pl.ds(off, n)], local_sem).start()
                pltpu.make_async_remote_copy(src_ref=ref.at[s, 1 - c], dst_ref=p_ref.at[s, pl.ds(off, n)], send_sem=send_sem,
                                             recv_sem=recv_sem, device_id=sib, device_id_type=MESH).start()
            off += n
        whole = pltpu.make_async_remote_copy(src_ref=m_ref, dst_ref=p_ref, send_sem=send_sem, recv_sem=recv_sem,
                                             device_id=sib, device_id_type=MESH)
        whole.wait_recv()
        whole.wait_send()
        pltpu.make_async_copy(p_ref, m_ref, local_sem).wait()

    return pl.pallas_call(
        body, name="pair_exchange", in_specs=[_ANY] * (n7 + n2), out_specs=[_ANY, _ANY],
        out_shape=[jax.ShapeDtypeStruct((N_CHIP, ROWS, D), BF16)] * 2,
        scratch_shapes=[pltpu.SemaphoreType.DMA] * 3,
    )(*g704, *g256)


def _chip_exchange(pair):
    def body(in_ref, q_ref, send_sems, recv_sems, local_sem):
        x, y, c = _coords()
        s_me = 2 * x + y
        for r in range(N_STREAMS):
            rows = pl.ds(r * CHUNK, CHUNK)
            pltpu.make_async_copy(in_ref.at[s_me, rows], q_ref.at[s_me, rows], local_sem).start()
        for j, (fx, fy) in enumerate(_CHIP_FLIPS):
            px, py = x ^ fx, y ^ fy
            for r in range(N_STREAMS):
                rows = pl.ds(r * CHUNK, CHUNK)
                pltpu.make_async_remote_copy(src_ref=in_ref.at[2 * px + py, rows], dst_ref=q_ref.at[s_me, rows],
                                             send_sem=send_sems.at[j], recv_sem=recv_sems.at[j], device_id=(px, py, c),
                                             device_id_type=MESH).start()
        for j in range(3):
            whole = pltpu.make_async_remote_copy(src_ref=in_ref.at[0], dst_ref=q_ref.at[0], send_sem=send_sems.at[j],
                                                 recv_sem=recv_sems.at[j], device_id=(x, y, c), device_id_type=MESH)
            whole.wait_recv()
            whole.wait_send()
        pltpu.make_async_copy(in_ref.at[0], q_ref.at[0], local_sem).wait()

    return pl.pallas_call(
        body, name="chip_exchange", in_specs=[_ANY], out_specs=_ANY,
        out_shape=jax.ShapeDtypeStruct((N_CHIP, ROWS, D), BF16),
        scratch_shapes=[pltpu.SemaphoreType.DMA((3,)), pltpu.SemaphoreType.DMA((3,)), pltpu.SemaphoreType.DMA],
    )(pair)


def _swap_halves(mine):
    def body(in_ref, out_ref, send_sem, recv_sem, local_sem):
        x, y, c = _coords()
        for r in range(N_STREAMS):
            rows = pl.ds(r * CHUNK, CHUNK)
            pltpu.make_async_copy(in_ref.at[rows], out_ref.at[c, rows], local_sem).start()
            pltpu.make_async_remote_copy(src_ref=in_ref.at[rows], dst_ref=out_ref.at[c, rows], send_sem=send_sem,
                                         recv_sem=recv_sem, device_id=(x, y, 1 - c), device_id_type=MESH).start()
        whole = pltpu.make_async_remote_copy(src_ref=in_ref, dst_ref=out_ref.at[0], send_sem=send_sem, recv_sem=recv_sem,
                                             device_id=(x, y, c), device_id_type=MESH)
        whole.wait_recv()
        whole.wait_send()
        pltpu.make_async_copy(in_ref, out_ref.at[0], local_sem).wait()

    return pl.pallas_call(
        body, name="swap_halves", in_specs=[_ANY], out_specs=_ANY,
        out_shape=jax.ShapeDtypeStruct((2, ROWS, D), F32),
        scratch_shapes=[pltpu.SemaphoreType.DMA, pltpu.SemaphoreType.DMA, pltpu.SemaphoreType.DMA],
    )(mine)


def _pack(arrays):
    flat = jnp.concatenate([a.reshape(-1).astype(F32) for a in arrays])
    pad = (-flat.shape[0]) % 1024
    return jnp.pad(flat, (0, pad)).reshape(-1, 128)


def _unpack(buf, shapes):
    flat = buf.reshape(-1)
    out, off = [], 0
    for s in shapes:
        n = 1
        for d in s:
            n *= d
        out.append(flat[off:off + n].reshape(s))
        off += n
    return out


def _block_diag(pw):
    bd = jnp.zeros((PW, PW), F32)
    g = PW // 4
    for i in range(4):
        bd = bd.at[i * g:(i + 1) * g, i * g:(i + 1) * g].set(pw[i])
    return bd


def _lanes(v):
    return jnp.broadcast_to(v.reshape(NH, 1, 1), (NH, 1, HD))


def _layer_fwd(z, mod, normg, w, small):
    S = z.shape[1]
    cos, sin = _rope_tables(S)
    fwd_sched, _ = _schedules(S)
    z1, f_a = _ffn_fwd(z, mod[:, :, 0], normg[0], w["w1t"][0], w["w3t"][0], w["w2"][0])
    pp, q, k, v, gg, pc = _mix_in_fwd(z1, mod[:, :, 1], normg[1], w["wint"], cos, sin)
    po, co = _pool_conv_fwd(pp, pc, small["bd"], small["pscale"], small["dw"], small["db"])
    ro = (small["gng"], small["lng"], small["lnb"])
    oa, ob = _retention(q, k, v, small["dec_f"], small["dec_b"], *fwd_sched)
    z2, out = _mix_out_fwd(z1, po, oa, ob, gg, co, ro, mod[:, :, 1], w["wout"])
    z3, f_b = _ffn_fwd(z2, mod[:, :, 2], normg[2], w["w1t"][1], w["w3t"][1], w["w2"][1])
    saved = dict(z=z, f_a=f_a, z1=z1, pp=pp, q=q, k=k, v=v, gg=gg, pc=pc, po=po, co=co, oa=oa, ob=ob, out=out, z2=z2, f_b=f_b)
    return z3, saved


def _layer_bwd(dz3, sv, mod, normg, w, small):
    S = dz3.shape[1]
    B = dz3.shape[0]
    T = B * S
    cos, sin = _rope_tables(S)
    fwd_sched, bwd_sched = _schedules(S)
    dz2, dmod_b, dg_b, gw1t_b, gw3t_b, gw2_b = _ffn_bwd(sv["z2"], dz3, sv["f_b"], mod[:, :, 2], normg[2],
                                                          w["w1t"][1], w["w3t"][1], w["w2"][1])
    ro = (small["gng"], small["lng"], small["lnb"])
    dpo, do, dgg, dco, cat, dout, dmod_gate, dgng, dlng, dlnb = _mix_out_bwd(
        dz2, sv["out"], sv["po"], sv["oa"], sv["ob"], sv["gg"], sv["co"], ro, mod[:, :, 1], w["wout"])
    gwout = _tn_matmul(cat.reshape(T, D), dout.reshape(T, D))
    dqa, dqb = _retention(do, sv["v"], sv["k"], small["dec_f"], small["dec_b"], *fwd_sched)
    dka, dkb = _retention(sv["v"], do, sv["q"], small["dec_f"], small["dec_b"], *bwd_sched)
    dva, dvb = _retention(sv["k"], sv["q"], do, small["dec_f"], small["dec_b"], *bwd_sched)
    ddec = _retention_ddecay(sv["q"], sv["k"], sv["v"], do, small["dec_f"], small["dec_b"], *fwd_sched)
    dq, dk = _rope_bwd(dqa, dqb, dka, dkb, cos, sin)
    dpp, dpc, dbd, dps, ddw, ddb = _pool_conv_bwd(sv["pp"], sv["pc"], dpo, dco, small["bd"], small["pscale"],
                                                   small["dw"], small["db"])
    dz1, h, dp, dmod_m, dg_m = _mix_in_bwd(sv["z1"], dz2, dpp, dq, dk, dva + dvb, dgg, dpc, mod[:, :, 1], normg[1], w["wint"])
    gwint = _tn_matmul(dp.reshape(T, F), h.reshape(T, D))
    dz, dmod_a, dg_a, gw1t_a, gw3t_a, gw2_a = _ffn_bwd(sv["z"], dz1, sv["f_a"], mod[:, :, 0], normg[0],
                                                        w["w1t"][0], w["w3t"][0], w["w2"][0])
    dmod = jnp.stack([dmod_a, dmod_m + dmod_gate, dmod_b], axis=2)
    dnormg = jnp.stack([dg_a, dg_m, dg_b], axis=0)
    big = dict(w1t=[gw1t_a, gw1t_b], w3t=[gw3t_a, gw3t_b], w2=[gw2_a, gw2_b], wint=gwint, wout=gwout)
    g = PW // 4
    dpool_w = jnp.stack([dbd[i * g:(i + 1) * g, i * g:(i + 1) * g] for i in range(4)], axis=0)
    sm = dict(pool_w=dpool_w, pool_scale=dps[0], dec_f=ddec[:, 0, 0], dec_b=ddec[:, 1, 0], gng=dgng[0],
              conv_dw=ddw[0:CONV_K], conv_b=ddb[0], conv_ln_g=dlng[0], conv_ln_b=dlnb[0])
    return dz, dmod, dnormg, big, sm


def _small_params(pool_w, pool_scale, dec_f, dec_b, gng, conv_dw, conv_b, lng, lnb):
    return dict(bd=_block_diag(pool_w), pscale=pool_scale.reshape(1, PW), dec_f=_lanes(dec_f), dec_b=_lanes(dec_b),
                gng=gng.reshape(1, RW), dw=jnp.pad(conv_dw, ((0, 1), (0, 0))), db=conv_b.reshape(1, PW),
                lng=lng.reshape(1, PW), lnb=lnb.reshape(1, PW))


_WEIGHTS = ["c_ctx", "w_mod", "b_mod", "norm_g", "ffn_w1", "ffn_w3", "ffn_w2", "w_in", "w_out", "pool_w", "pool_scale",
            "ret_decay_fwd", "ret_decay_bwd", "ret_gn_g", "conv_dw", "conv_b", "conv_ln_g", "conv_ln_b", "final_g"]
_BIG = ["w_mod", "ffn_w1", "ffn_w3", "ffn_w2", "w_in", "w_out"]
_SMALL = [n for n in _WEIGHTS if n not in _BIG]


def _adam_any(w, g, m, v):
    shape = w.shape
    cols = shape[-1] if w.ndim >= 2 else 128
    outs = _adam(w.reshape(-1, cols), g.reshape(-1, cols), m.reshape(-1, cols), v.reshape(-1, cols))
    return [o.reshape(shape) for o in outs]


def _step(a):
    x, c, ctx = a["x"], a["c"], a["ctx"]
    B = x.shape[0]
    nex = N_DEV * B
    assert nex + B <= MROWS and ctx.shape[1] == LC and x.shape[1] % TM == 0
    xi, yi, ci = _coords()
    me = 4 * xi + 2 * yi + ci
    chip = 2 * xi + yi
    ncol = a["w_mod"].shape[2]

    shapes1 = [(B, D), (2, 3, D // N_CHIP), (2, CONV_K, PW // N_CHIP)]
    g1 = _all_gather_small(_pack([c, a["norm_g"], a["conv_dw"]]))
    per = [_unpack(g1[d], shapes1) for d in range(N_DEV)]
    c_all = jnp.concatenate([per[d][0] for d in range(N_DEV)], axis=0)
    norm_g_full = jnp.concatenate([per[2 * s][1] for s in range(N_CHIP)], axis=-1)
    conv_dw_full = jnp.concatenate([per[2 * s][2] for s in range(N_CHIP)], axis=-1)
    cctx = a["c_ctx"].reshape(1, D)
    c24 = jnp.concatenate([c_all] + [cctx] * B + [jnp.zeros((MROWS - nex - B, D), F32)], axis=0)

    bsh = lax.dynamic_slice(a["b_mod"], (0, chip * ncol), (2, ncol)).reshape(2, 1, ncol)
    mod_raw = _mod_fwd(c24, a["w_mod"], bsh)
    g2 = _all_gather_small(_pack([mod_raw]))
    mod_full = jnp.concatenate([_unpack(g2[2 * s], [(2, MROWS, ncol)])[0] for s in range(N_CHIP)], axis=-1)
    mods = []
    for l in range(2):
        lat = lax.dynamic_slice(mod_full[l], (B * me, 0), (B, N_MOD * D))
        cx = jnp.broadcast_to(mod_full[l, nex][None], (B, N_MOD * D))
        mods.append(jnp.stack([cx, lat], axis=1).reshape(B, 2, 3, 3, D))

    def t_bf16(w):
        return jnp.swapaxes(w, -1, -2).astype(BF16)

    w1t, w3t, w2 = t_bf16(a["ffn_w1"]), t_bf16(a["ffn_w3"]), a["ffn_w2"].astype(BF16)
    wint, wout = t_bf16(a["w_in"]), a["w_out"].astype(BF16)
    l704 = []
    for l in range(2):
        for i in range(2):
            l704 += [w1t[l, i], w3t[l, i], w2[l, i]]
    l704 += [wint[0], wint[1]]
    wg, wo = _gather_weights(jnp.stack(l704, axis=0).reshape(N704, 2, HSLAB, D), wout.reshape(N256, 2, HOSLAB, D))
    wg = wg.reshape(N704, F, D)
    wo = wo.reshape(N256, D, D)
    ws = [dict(w1t=[(wg, 6 * l), (wg, 6 * l + 3)], w3t=[(wg, 6 * l + 1), (wg, 6 * l + 4)],
               w2=[(wg, 6 * l + 2), (wg, 6 * l + 5)], wint=(wg, 12 + l), wout=(wo, l)) for l in range(2)]
    smalls = [_small_params(a["pool_w"][l], a["pool_scale"][l], a["ret_decay_fwd"][l], a["ret_decay_bwd"][l],
                            a["ret_gn_g"][l], conv_dw_full[l], a["conv_b"][l], a["conv_ln_g"][l], a["conv_ln_b"][l])
              for l in range(2)]
    normgs = [norm_g_full[l].reshape(3, 1, D) for l in range(2)]

    z = jnp.concatenate([ctx, x], axis=1)
    saved = []
    for l in range(2):
        z, sv = _layer_fwd(z, mods[l], normgs[l], ws[l], smalls[l])
        saved.append(sv)
    dz, dfinal_g, loss_part = _head(z, a["loss_target"], a["final_g"].reshape(1, D))
    back = [None, None]
    for l in (1, 0):
        dz, dmod, dnormg, big, sm = _layer_bwd(dz, saved[l], mods[l], normgs[l], ws[l], smalls[l])
        back[l] = (dmod, dnormg, big, sm)
    grad_x = dz[:, LC:]

    g704 = []
    for l in range(2):
        big = back[l][2]
        for i in range(2):
            g704 += [big["w1t"][i], big["w3t"][i], big["w2"][i]]
    g704 += [back[0][2]["wint"], back[1][2]["wint"]]
    g704 = [g.reshape(N_CHIP, 2, HSLAB, D) for g in g704]
    g256 = [back[l][2]["wout"].reshape(N_CHIP, 2, HOSLAB, D) for l in range(2)]
    pair = _pair_sum(*_pair_exchange(g704, g256))
    both = _swap_halves(_sum_chips(_chip_exchange(pair)))

    def slab(i):
        return both[:, i * HSLAB:(i + 1) * HSLAB].reshape(SLAB, D)

    def oslab(j):
        off = N704 * HSLAB + j * HOSLAB
        return both[:, off:off + HOSLAB].reshape(OSLAB, D)

    grads = {}
    grads["ffn_w1"] = jnp.stack([jnp.stack([slab(6 * l + 3 * i).T for i in range(2)]) for l in range(2)])
    grads["ffn_w3"] = jnp.stack([jnp.stack([slab(6 * l + 3 * i + 1).T for i in range(2)]) for l in range(2)])
    grads["ffn_w2"] = jnp.stack([jnp.stack([slab(6 * l + 3 * i + 2) for i in range(2)]) for l in range(2)])
    grads["w_in"] = jnp.stack([slab(12 + l).T for l in range(2)])
    grads["w_out"] = jnp.stack([oslab(l) for l in range(2)])

    dmods = [back[l][0].reshape(B, 2, N_MOD * D) for l in range(2)]
    pack_a = _pack([jnp.stack([dm[:, 1] for dm in dmods])])
    ka = pack_a.shape[0]
    sm = [back[l][3] for l in range(2)]
    sum_list = [jnp.stack([dm[:, 0] for dm in dmods]), jnp.stack([back[l][1][:, 0] for l in range(2)])]
    sm_keys = ["pool_w", "pool_scale", "dec_f", "dec_b", "gng", "conv_dw", "conv_b", "conv_ln_g", "conv_ln_b"]
    sum_list += [jnp.stack([sm[l][k] for l in range(2)]) for k in sm_keys]
    sum_list += [dfinal_g[0], loss_part[0, 0:1]]
    sum_shapes = [s.shape for s in sum_list]
    g3 = _all_gather_small(jnp.concatenate([pack_a, _pack(sum_list)], axis=0))
    dmx_all = jnp.concatenate([_unpack(g3[d, :ka], [(2, B, N_MOD * D)])[0] for d in range(N_DEV)], axis=1)
    summed = _unpack(_sum_devices(g3[:, ka:]), sum_shapes)
    dmy, dnorm_full = summed[0], summed[1]
    sgrad = dict(zip(sm_keys, summed[2:2 + len(sm_keys)]))
    loss = summed[-1].reshape(())

    dmod24 = jnp.concatenate([dmx_all, dmy, jnp.zeros((2, MROWS - nex - B, N_MOD * D), F32)], axis=1)
    dmod_my = lax.dynamic_slice(dmod24, (0, 0, chip * ncol), (2, MROWS, ncol))
    grads["w_mod"], dsc = _mod_bwd(c24, dmod_my, a["w_mod"])
    g4 = _all_gather_small(_pack([dsc[nex:nex + 8]]))
    dsc_parts = jnp.stack([_unpack(g4[2 * s], [(8, D)])[0] for s in range(N_CHIP)])
    dbmod, dcctx = _bmod_cctx_grad(dmod24, dsc_parts, cctx)

    grads["c_ctx"] = dcctx[0]
    grads["b_mod"] = dbmod.reshape(2, N_MOD * D)
    grads["norm_g"] = lax.dynamic_slice(dnorm_full, (0, 0, chip * (D // N_CHIP)), (2, 3, D // N_CHIP))
    grads["pool_w"] = sgrad["pool_w"]
    grads["pool_scale"] = sgrad["pool_scale"]
    grads["ret_decay_fwd"] = sgrad["dec_f"]
    grads["ret_decay_bwd"] = sgrad["dec_b"]
    grads["ret_gn_g"] = sgrad["gng"]
    grads["conv_dw"] = lax.dynamic_slice(sgrad["conv_dw"], (0, 0, chip * (PW // N_CHIP)), (2, CONV_K, PW // N_CHIP))
    grads["conv_b"] = sgrad["conv_b"]
    grads["conv_ln_g"] = sgrad["conv_ln_g"]
    grads["conv_ln_b"] = sgrad["conv_ln_b"]
    grads["final_g"] = summed[-2]

    delta, new_m, new_v = {}, {}, {}
    for n in _BIG:
        delta[n], new_m[n], new_v[n] = _adam_any(a[n], grads[n], a["m_" + n], a["v_" + n])
    shapes_s = [a[n].shape for n in _SMALL]
    packed = _adam(_pack([a[n] for n in _SMALL]), _pack([grads[n] for n in _SMALL]),
                   _pack([a["m_" + n] for n in _SMALL]), _pack([a["v_" + n] for n in _SMALL]))
    for res, out in zip(packed, (delta, new_m, new_v)):
        for n, val in zip(_SMALL, _unpack(res, shapes_s)):
            out[n] = val
    return (loss, grad_x, *[grads[n] for n in _WEIGHTS], *[delta[n] for n in _WEIGHTS],
            *[new_m[n] for n in _WEIGHTS], *[new_v[n] for n in _WEIGHTS])


def kernel(x, c, ctx, c_ctx, w_mod, b_mod, norm_g, ffn_w1, ffn_w3, ffn_w2, w_in, w_out, pool_w, pool_scale, ret_decay_fwd, ret_decay_bwd, ret_gn_g, conv_dw, conv_b, conv_ln_g, conv_ln_b, final_g, loss_target, m_c_ctx, m_w_mod, m_b_mod, m_norm_g, m_ffn_w1, m_ffn_w3, m_ffn_w2, m_w_in, m_w_out, m_pool_w, m_pool_scale, m_ret_decay_fwd, m_ret_decay_bwd, m_ret_gn_g, m_conv_dw, m_conv_b, m_conv_ln_g, m_conv_ln_b, m_final_g, v_c_ctx, v_w_mod, v_b_mod, v_norm_g, v_ffn_w1, v_ffn_w3, v_ffn_w2, v_w_in, v_w_out, v_pool_w, v_pool_scale, v_ret_decay_fwd, v_ret_decay_bwd, v_ret_gn_g, v_conv_dw, v_conv_b, v_conv_ln_g, v_conv_ln_b, v_final_g):
    return _step(dict(locals()))
```

```python
import functools

import jax
import jax.numpy as jnp
from jax import lax
from jax.experimental import pallas as pl
from jax.experimental.pallas import tpu as pltpu

F32 = jnp.float32
BF16 = jnp.bfloat16

D = 1024
F = 2816
FH = 1408
N_MOD = 9
LC = 256
TM = 256
HD = 128
NH = 4
RW = 512
PW = 256
CONV_K = 31
GRID_W = 64
EPS = 1e-6
K_SCALE = HD ** -0.5
N_DEV = 8
N_CHIP = 4
SLAB = F // N_CHIP
HSLAB = SLAB // 2
OSLAB = D // N_CHIP
HOSLAB = OSLAB // 2
VMEM_BIG = 60 * 1024 * 1024
MESH = pl.DeviceIdType.MESH

ADAM_LR = 0.001
ADAM_B1 = 0.9
ADAM_B2 = 0.999
ADAM_EPS = 1e-08
ADAM_WD = 0.01
ADAM_STEP = 10


def _nt(a, b):
    return lax.dot_general(a, b, (((1,), (1,)), ((), ())), preferred_element_type=F32)


def _nn(a, b):
    return lax.dot_general(a, b, (((1,), (0,)), ((), ())), preferred_element_type=F32)


def _tn(a, b):
    return lax.dot_general(a, b, (((0,), (0,)), ((), ())), preferred_element_type=F32)


def _params(vmem=None, sem=None):
    return pltpu.CompilerParams(dimension_semantics=sem, vmem_limit_bytes=vmem)


def _rms_mod(z, g, shift, scale):
    y = z * lax.rsqrt(jnp.mean(z * z, axis=-1, keepdims=True) + EPS)
    return (y * g) * (1.0 + scale) + shift


def _acc(ref, val, first):
    @pl.when(first)
    def _():
        ref[...] = val

    @pl.when(jnp.logical_not(first))
    def _():
        ref[...] += val


def _tok(width):
    return pl.BlockSpec((None, TM, width), lambda b, t: (b, t, 0))


def _modspec():
    return pl.BlockSpec((None, None, 3, D), lambda b, t: (b, jnp.minimum(t, 1), 0, 0))


def _const(shape):
    nd = len(shape)
    return pl.BlockSpec(shape, lambda b, t: (0,) * nd)


def _wspec(w):
    stack, idx = w
    return pl.BlockSpec((None,) + stack.shape[1:], lambda b, t: (idx, 0, 0), pipeline_mode=pl.Buffered(1))


def _ffn_fwd(z, mod, g, w1t, w3t, w2):
    B, S, _ = z.shape

    def body(z_ref, mod_ref, g_ref, w1_ref, w3_ref, w2_ref, zo_ref, f_ref):
        zt = z_ref[...]
        h = _rms_mod(zt, g_ref[...], mod_ref[0:1, :], mod_ref[1:2, :]).astype(BF16)
        f = jnp.zeros((TM, D), F32)
        for c in range(F // FH):
            rows = slice(c * FH, (c + 1) * FH)
            u1 = _nt(h, w1_ref[rows, :])
            u3 = _nt(h, w3_ref[rows, :])
            a = (u1 * jax.nn.sigmoid(u1) * u3).astype(BF16)
            f = f + _nn(a, w2_ref[rows, :])
        f_ref[...] = f
        zo_ref[...] = zt + 0.5 * mod_ref[2:3, :] * f

    return pl.pallas_call(
        body, name="ffn_fwd", grid=(B, S // TM),
        in_specs=[_tok(D), _modspec(), _const((1, D)), _wspec(w1t), _wspec(w3t), _wspec(w2)],
        out_specs=[_tok(D), _tok(D)],
        out_shape=[jax.ShapeDtypeStruct((B, S, D), F32)] * 2,
        compiler_params=_params(VMEM_BIG, ("arbitrary", "arbitrary")),
    )(z, mod, g, w1t[0], w3t[0], w2[0])


def _ffn_bwd(z, dzo, f, mod, g, w1t, w3t, w2):
    B, S, _ = z.shape

    def body(z_ref, dzo_ref, f_ref, mod_ref, g_ref, w1_ref, w3_ref, w2_ref,
             dz_ref, h_ref, du1_ref, du3_ref, a_ref, do_ref, dmod_ref, dg_ref):
        b, t = pl.program_id(0), pl.program_id(1)
        zt = z_ref[...]
        dzo = dzo_ref[...]
        gate = mod_ref[2:3, :]
        h32, vjp_h = jax.vjp(_rms_mod, zt, g_ref[...], mod_ref[0:1, :], mod_ref[1:2, :])
        h = h32.astype(BF16)
        h_ref[...] = h
        do = (0.5 * gate * dzo).astype(BF16)
        do_ref[...] = do
        dgate = jnp.sum(0.5 * f_ref[...] * dzo, axis=0, keepdims=True)
        dh = jnp.zeros((TM, D), F32)
        for c in range(F // FH):
            rows = slice(c * FH, (c + 1) * FH)
            u1 = _nt(h, w1_ref[rows, :])
            u3 = _nt(h, w3_ref[rows, :])
            sg = jax.nn.sigmoid(u1)
            s = u1 * sg
            a_ref[:, rows] = (s * u3).astype(BF16)
            da = _nt(do, w2_ref[rows, :])
            du3 = (da * s).astype(BF16)
            du1 = (da * u3 * (sg * (1.0 + u1 * (1.0 - sg)))).astype(BF16)
            du1_ref[:, rows] = du1
            du3_ref[:, rows] = du3
            dh = dh + _nn(du1, w1_ref[rows, :]) + _nn(du3, w3_ref[rows, :])
        dz_h, dg, dshift, dscale = vjp_h(dh)
        dz_ref[...] = dzo + dz_h
        _acc(dmod_ref, jnp.concatenate([dshift, dscale, dgate], axis=0), t <= 1)
        _acc(dg_ref, dg, jnp.logical_and(b == 0, t == 0))

    T = B * S
    outs = pl.pallas_call(
        body, name="ffn_bwd", grid=(B, S // TM),
        in_specs=[_tok(D), _tok(D), _tok(D), _modspec(), _const((1, D)), _wspec(w1t), _wspec(w3t), _wspec(w2)],
        out_specs=[_tok(D), _tok(D), _tok(F), _tok(F), _tok(F), _tok(D), _modspec(), _const((1, D))],
        out_shape=[jax.ShapeDtypeStruct((B, S, D), F32), jax.ShapeDtypeStruct((B, S, D), BF16),
                   jax.ShapeDtypeStruct((B, S, F), BF16), jax.ShapeDtypeStruct((B, S, F), BF16),
                   jax.ShapeDtypeStruct((B, S, F), BF16), jax.ShapeDtypeStruct((B, S, D), BF16),
                   jax.ShapeDtypeStruct((B, 2, 3, D), F32), jax.ShapeDtypeStruct((1, D), F32)],
        compiler_params=_params(VMEM_BIG, ("arbitrary", "arbitrary")),
    )(z, dzo, f, mod, g, w1t[0], w3t[0], w2[0])
    dz, h, du1, du3, a, do, dmod, dg = outs
    gw1t = _tn_matmul(du1.reshape(T, F), h.reshape(T, D))
    gw3t = _tn_matmul(du3.reshape(T, F), h.reshape(T, D))
    gw2 = _tn_matmul(a.reshape(T, F), do.reshape(T, D))
    return dz, dmod, dg, gw1t, gw3t, gw2


def _tn_matmul(a, b):
    T, M = a.shape
    N = b.shape[1]
    MB = FH if M > FH else M
    TT = 512 if T % 512 == 0 else TM
    nt = T // TT

    def body(a_ref, b_ref, o_ref, acc_ref):
        t = pl.program_id(1)
        prod = _tn(a_ref[...], b_ref[...])
        _acc(acc_ref, prod, t == 0)

        @pl.when(t == nt - 1)
        def _():
            o_ref[...] = acc_ref[...].astype(BF16)

    return pl.pallas_call(
        body, name="tn_matmul", grid=(M // MB, nt),
        in_specs=[pl.BlockSpec((TT, MB), lambda i, t: (t, i)), pl.BlockSpec((TT, N), lambda i, t: (t, 0))],
        out_specs=pl.BlockSpec((MB, N), lambda i, t: (i, 0)),
        out_shape=jax.ShapeDtypeStruct((M, N), BF16),
        scratch_shapes=[pltpu.VMEM((MB, N), F32)],
        compiler_params=_params(VMEM_BIG, ("arbitrary", "arbitrary")),
    )(a, b)


def _swap32(x):
    n = x.shape[1]
    lane = lax.broadcasted_iota(jnp.int32, x.shape, 1)
    return jnp.where((lane % 64) < 32, pltpu.roll(x, n - 32, 1), pltpu.roll(x, 32, 1))


def _rope(x, cos, sin):
    return x * cos + _swap32(x) * sin


def _rope_t(dy, cos, sin):
    return dy * cos + _swap32(dy * sin)


def _rope_tables(S):
    L = S - LC
    n_freq = HD // 4
    inv = 10000.0 ** (-jnp.arange(n_freq, dtype=F32) / n_freq)
    i = jnp.arange(L)
    row = (i // GRID_W).astype(F32)
    col = (i % GRID_W).astype(F32)
    ang_r = row[:, None] * inv[None]
    ang_c = col[:, None] * inv[None]
    ang = jnp.concatenate([ang_r, ang_r, ang_c, ang_c], axis=1)
    ang = jnp.concatenate([jnp.zeros((LC, HD), F32), ang], axis=0)
    sign = jnp.where((jnp.arange(HD) % 64) < 32, -1.0, 1.0).astype(F32)
    return jnp.cos(ang), jnp.sin(ang) * sign[None]


def _tabspec():
    return pl.BlockSpec((TM, HD), lambda b, t: (t, 0))


def _mix_in_fwd(z, mod, g, wint, cos, sin):
    B, S, _ = z.shape

    def body(z_ref, mod_ref, g_ref, w_ref, cos_ref, sin_ref, pp_ref, q_ref, k_ref, v_ref, gg_ref, pc_ref):
        h = _rms_mod(z_ref[...], g_ref[...], mod_ref[0:1, :], mod_ref[1:2, :]).astype(BF16)
        p = _nt(h, w_ref[...])
        cos = jnp.tile(cos_ref[...], (1, NH))
        sin = jnp.tile(sin_ref[...], (1, NH))
        pp_ref[...] = p[:, 0:PW]
        q_ref[...] = _rope(p[:, PW:PW + RW], cos, sin)
        k_ref[...] = _rope(p[:, PW + RW:PW + 2 * RW], cos, sin) * K_SCALE
        v_ref[...] = p[:, PW + 2 * RW:PW + 3 * RW]
        gg_ref[...] = p[:, PW + 3 * RW:PW + 4 * RW]
        pc_ref[...] = p[:, PW + 4 * RW:]

    return pl.pallas_call(
        body, name="mix_in_fwd", grid=(B, S // TM),
        in_specs=[_tok(D), _modspec(), _const((1, D)), _wspec(wint), _tabspec(), _tabspec()],
        out_specs=[_tok(PW), _tok(RW), _tok(RW), _tok(RW), _tok(RW), _tok(2 * PW)],
        out_shape=[jax.ShapeDtypeStruct((B, S, PW), F32)] + [jax.ShapeDtypeStruct((B, S, RW), F32)] * 5,
        compiler_params=_params(VMEM_BIG, ("arbitrary", "arbitrary")),
    )(z, mod, g, wint[0], cos, sin)


def _mix_in_bwd(z, dzo, dpp, dq, dk, dv, dgg, dpc, mod, g, wint):
    B, S, _ = z.shape

    def body(z_ref, dzo_ref, dpp_ref, dq_ref, dk_ref, dv_ref, dgg_ref, dpc_ref, mod_ref, g_ref, w_ref,
             dz_ref, h_ref, dp_ref, dmod_ref, dg_ref):
        b, t = pl.program_id(0), pl.program_id(1)
        h32, vjp_h = jax.vjp(_rms_mod, z_ref[...], g_ref[...], mod_ref[0:1, :], mod_ref[1:2, :])
        h_ref[...] = h32.astype(BF16)
        dp = jnp.concatenate([dpp_ref[...], dq_ref[...], dk_ref[...], dv_ref[...], dgg_ref[...], dpc_ref[...]],
                             axis=1).astype(BF16)
        dp_ref[...] = dp
        dh = _nn(dp, w_ref[...])
        dz_h, dg, dshift, dscale = vjp_h(dh)
        dz_ref[...] = dzo_ref[...] + dz_h
        _acc(dmod_ref, jnp.concatenate([dshift, dscale, jnp.zeros_like(dshift)], axis=0), t <= 1)
        _acc(dg_ref, dg, jnp.logical_and(b == 0, t == 0))

    return pl.pallas_call(
        body, name="mix_in_bwd", grid=(B, S // TM),
        in_specs=[_tok(D), _tok(D), _tok(PW), _tok(RW), _tok(RW), _tok(RW), _tok(RW), _tok(2 * PW),
                  _modspec(), _const((1, D)), _wspec(wint)],
        out_specs=[_tok(D), _tok(D), _tok(F), _modspec(), _const((1, D))],
        out_shape=[jax.ShapeDtypeStruct((B, S, D), F32), jax.ShapeDtypeStruct((B, S, D), BF16),
                   jax.ShapeDtypeStruct((B, S, F), BF16), jax.ShapeDtypeStruct((B, 2, 3, D), F32),
                   jax.ShapeDtypeStruct((1, D), F32)],
        compiler_params=_params(VMEM_BIG, ("arbitrary", "arbitrary")),
    )(z, dzo, dpp, dq, dk, dv, dgg, dpc, mod, g, wint[0])


def _rope_bwd(dqa, dqb, dka, dkb, cos, sin):
    B, S, _ = dqa.shape

    def body(dqa_ref, dqb_ref, dka_ref, dkb_ref, cos_ref, sin_ref, dq_ref, dk_ref):
        cos = jnp.tile(cos_ref[...], (1, NH))
        sin = jnp.tile(sin_ref[...], (1, NH))
        dq_ref[...] = _rope_t(dqa_ref[...] + dqb_ref[...], cos, sin)
        dk_ref[...] = _rope_t(dka_ref[...] + dkb_ref[...], cos, sin) * K_SCALE

    return pl.pallas_call(
        body, name="rope_bwd", grid=(B, S // TM),
        in_specs=[_tok(RW)] * 4 + [_tabspec(), _tabspec()],
        out_specs=[_tok(RW), _tok(RW)],
        out_shape=[jax.ShapeDtypeStruct((B, S, RW), F32)] * 2,
        compiler_params=_params(None, ("arbitrary", "arbitrary")),
    )(dqa, dqb, dka, dkb, cos, sin)


def _log_sigmoid(x):
    return jnp.minimum(x, 0.0) - jnp.log(1.0 + jnp.exp(-jnp.abs(x)))


def _retention(a, b, c, dec_a, dec_b, sched_a, sched_b):
    B, S, _ = a.shape
    C = TM

    def body(a_ref, b_ref, c_ref, da_ref, db_ref, oa_ref, ob_ref):
        ii = lax.broadcasted_iota(jnp.int32, (C, C), 0)
        jj = lax.broadcasted_iota(jnp.int32, (C, C), 1)
        pos = lax.broadcasted_iota(jnp.int32, (C, 1), 0).astype(F32)
        for dec_ref, o_ref, (order, causal, strict) in ((da_ref, oa_ref, sched_a), (db_ref, ob_ref, sched_b)):
            lg = _log_sigmoid(dec_ref[...])
            lg1 = lg[:, 0:1]
            dist = ((ii - jj) if causal else (jj - ii)).astype(F32)
            mask = (dist > 0.0) if strict else (dist >= 0.0)
            decay = jnp.where(mask, jnp.exp(jnp.maximum(dist, 0.0) * lg1), 0.0)
            p = pos if causal else (C - 1.0 - pos)
            w_q = jnp.exp((p + 1.0) * lg1)
            w_k = jnp.exp((C - 1.0 - p) * lg1)
            chunk_decay = jnp.exp(C * lg)
            state = jnp.zeros((HD, HD), F32)
            for n in order:
                rows = pl.ds(n * C, C)
                at, bt, ct = a_ref[rows, :], b_ref[rows, :], c_ref[rows, :]
                cb = ct.astype(BF16)
                scores = _nt(at.astype(BF16), bt.astype(BF16)) * decay
                o = _nn(scores.astype(BF16), cb)
                o = o + _nn((at * w_q).astype(BF16), state.astype(BF16))
                o_ref[rows, :] = o
                state = chunk_decay * state + _tn((bt * w_k).astype(BF16), cb)

    seq = pl.BlockSpec((None, S, HD), lambda b, h: (b, 0, h))
    dspec = pl.BlockSpec((None, 1, HD), lambda b, h: (h, 0, 0))
    return pl.pallas_call(
        body, name="retention", grid=(B, NH),
        in_specs=[seq, seq, seq, dspec, dspec], out_specs=[seq, seq],
        out_shape=[jax.ShapeDtypeStruct((B, S, RW), F32)] * 2,
        compiler_params=_params(VMEM_BIG, ("arbitrary", "arbitrary")),
    )(a, b, c, dec_a, dec_b)


def _retention_ddecay(q, k, v, do, dec_a, dec_b, sched_a, sched_b):
    B, S, _ = q.shape
    C = TM

    def body(q_ref, k_ref, v_ref, do_ref, da_ref, db_ref, o_ref):
        ii = lax.broadcasted_iota(jnp.int32, (C, C), 0)
        jj = lax.broadcasted_iota(jnp.int32, (C, C), 1)
        pos = lax.broadcasted_iota(jnp.int32, (C, 1), 0).astype(F32)
        vals = []
        for dec_ref, (order, causal, strict) in ((da_ref, sched_a), (db_ref, sched_b)):
            x = dec_ref[...]
            lg = _log_sigmoid(x)
            lg1 = lg[:, 0:1]
            dist = ((ii - jj) if causal else (jj - ii)).astype(F32)
            mask = (dist > 0.0) if strict else (dist >= 0.0)
            ddecay = jnp.where(mask, dist * jnp.exp(jnp.maximum(dist, 0.0) * lg1), 0.0)
            p = pos if causal else (C - 1.0 - pos)
            w_q = jnp.exp((p + 1.0) * lg1)
            w_k = jnp.exp((C - 1.0 - p) * lg1)
            chunk_decay = jnp.exp(C * lg)
            state = jnp.zeros((HD, HD), F32)
            dstate = jnp.zeros((HD, HD), F32)
            tot = jnp.zeros((), F32)
            for n in order:
                rows = pl.ds(n * C, C)
                qt, kt, vt, dot = q_ref[rows, :], k_ref[rows, :], v_ref[rows, :], do_ref[rows, :]
                vb = vt.astype(BF16)
                scores = _nt(qt.astype(BF16), kt.astype(BF16))
                dscores = _nt(dot.astype(BF16), vb)
                qw = (qt * w_q).astype(BF16)
                cross = _nn(qw, state.astype(BF16))
                dcross = _nn(qw, dstate.astype(BF16))
                tot = tot + jnp.sum(scores * dscores * ddecay) + jnp.sum(((p + 1.0) * cross + dcross) * dot)
                kv = _tn((kt * w_k).astype(BF16), vb)
                dkv = _tn((kt * ((C - 1.0 - p) * w_k)).astype(BF16), vb)
                dstate = chunk_decay * (dstate + C * state) + dkv
                state = chunk_decay * state + kv
            vals.append(tot * jax.nn.sigmoid(-x))
        row = lax.broadcasted_iota(jnp.int32, (8, HD), 0)
        tile = jnp.where(row == 0, vals[0], 0.0) + jnp.where(row == 1, vals[1], 0.0)
        _acc(o_ref, tile, pl.program_id(1) == 0)

    seq = pl.BlockSpec((None, S, HD), lambda h, b: (b, 0, h))
    dspec = pl.BlockSpec((None, 1, HD), lambda h, b: (h, 0, 0))
    return pl.pallas_call(
        body, name="retention_ddecay", grid=(NH, B),
        in_specs=[seq, seq, seq, seq, dspec, dspec], out_specs=pl.BlockSpec((None, 8, HD), lambda h, b: (h, 0, 0)),
        out_shape=jax.ShapeDtypeStruct((NH, 8, HD), F32),
        compiler_params=_params(VMEM_BIG, ("arbitrary", "arbitrary")),
    )(q, k, v, do, dec_a, dec_b)


def _schedules(S):
    n = S // TM
    lat_up = tuple(range(1, n))
    lat_down = tuple(range(n - 1, 0, -1))
    fwd = (((0,) + lat_up, True, False), ((0,) + lat_down, False, True))
    bwd = ((lat_down + (0,), False, False), (lat_up + (0,), True, True))
    return fwd, bwd


def _shift_rows(x, d):
    if d == 0:
        return x
    S = x.shape[0]
    t = lax.broadcasted_iota(jnp.int32, x.shape, 0)
    tt = t + d
    lo = jnp.where(t < LC, 0, LC)
    hi = jnp.where(t < LC, LC, S)
    return jnp.where((tt >= lo) & (tt < hi), pltpu.roll(x, (-d) % S, 0), 0.0)


@functools.partial(jax.custom_vjp, nondiff_argnums=(1,))
def _shift(x, d):
    return _shift_rows(x, d)


_shift.defvjp(lambda x, d: (_shift_rows(x, d), None), lambda d, _, g: (_shift_rows(g, -d),))


def _pool_fn(p, bd, pscale):
    lane = lax.broadcasted_iota(jnp.int32, p.shape, 1)
    grp = lane // (PW // 4)
    half = jnp.where(grp == 0, 1, jnp.where(grp == 1, 2, jnp.where(grp == 2, 4, 8)))
    ones = jnp.ones(p.shape, F32)
    acc = jnp.zeros(p.shape, F32)
    cnt = jnp.zeros(p.shape, F32)
    for d in range(-8, 8):
        inwin = ((d >= -half) & (d < half)).astype(F32)
        acc = acc + _shift(p, d) * inwin
        cnt = cnt + _shift_rows(ones, d) * inwin
    pooled = acc / cnt - p
    mixed = _nn(pooled.astype(BF16), bd.astype(BF16))
    return mixed * pscale


def _dwconv_raw(zc, dw):
    y = jnp.zeros(zc.shape, F32)
    for k in range(CONV_K):
        y = y + _shift_rows(zc, k - CONV_K // 2) * dw[k:k + 1, :]
    return y


@jax.custom_vjp
def _dwconv(zc, dw):
    return _dwconv_raw(zc, dw)


def _dwconv_fwd(zc, dw):
    return _dwconv_raw(zc, dw), (zc, dw)


def _dwconv_bwd(res, g):
    zc, dw = res
    dz = jnp.zeros(zc.shape, F32)
    ddw = jnp.zeros(dw.shape, F32)
    row = lax.broadcasted_iota(jnp.int32, dw.shape, 0)
    for k in range(CONV_K):
        dz = dz + _shift_rows(g, CONV_K // 2 - k) * dw[k:k + 1, :]
        r = jnp.sum(g * _shift_rows(zc, k - CONV_K // 2), axis=0, keepdims=True)
        ddw = ddw + jnp.where(row == k, r, 0.0)
    return dz, ddw


_dwconv.defvjp(_dwconv_fwd, _dwconv_bwd)


def _conv_fn(u, dw, db):
    zc = u[:, :PW] * jax.nn.sigmoid(u[:, PW:])
    return _dwconv(zc, dw) + db


def _ln_swish(y, lng, lnb):
    mu = jnp.mean(y, axis=-1, keepdims=True)
    yc = y - mu
    var = jnp.mean(yc * yc, axis=-1, keepdims=True)
    yn = yc * lax.rsqrt(var + EPS) * lng + lnb
    return yn * jax.nn.sigmoid(yn)


def _seq(shape, single=False):
    return pl.BlockSpec((None,) + shape, lambda b: (b, 0, 0), pipeline_mode=pl.Buffered(1) if single else None)


def _c1(shape):
    nd = len(shape)
    return pl.BlockSpec(shape, lambda b: (0,) * nd)


def _seq_apply(fn, name, xs, consts, width):
    B, S, w = xs.shape

    def body(x_ref, *refs):
        refs[-1][...] = fn(x_ref[...], *[r[...] for r in refs[:-1]])

    return pl.pallas_call(
        body, name=name, grid=(B,),
        in_specs=[_seq((S, w))] + [_c1(c.shape) for c in consts], out_specs=_seq((S, width)),
        out_shape=jax.ShapeDtypeStruct((B, S, width), F32),
        compiler_params=_params(VMEM_BIG, ("arbitrary",)),
    )(xs, *consts)


def _seq_vjp(fn, name, xs, consts, dout):
    B, S, w = xs.shape
    n = len(consts)

    def body(x_ref, d_ref, *refs):
        first = pl.program_id(0) == 0
        _, vjp = jax.vjp(fn, x_ref[...], *[r[...] for r in refs[:n]])
        grads = vjp(d_ref[...])
        refs[n][...] = grads[0]
        for ref, val in zip(refs[n + 1:], grads[1:]):
            _acc(ref, val, first)

    return pl.pallas_call(
        body, name=name, grid=(B,),
        in_specs=[_seq((S, w), True), _seq((S, dout.shape[2]), True)] + [_c1(c.shape) for c in consts],
        out_specs=[_seq((S, w))] + [_c1(c.shape) for c in consts],
        out_shape=[jax.ShapeDtypeStruct((B, S, w), F32)] + [jax.ShapeDtypeStruct(c.shape, F32) for c in consts],
        compiler_params=_params(VMEM_BIG, ("arbitrary",)),
    )(xs, dout, *consts)


def _pool_conv_fwd(pp, pc, bd, pscale, dw, db):
    return (_seq_apply(_pool_fn, "pool_fwd", pp, (bd, pscale), PW),
            _seq_apply(_conv_fn, "conv_fwd", pc, (dw, db), PW))


def _pool_conv_bwd(pp, pc, dpo, dco, bd, pscale, dw, db):
    dpp, dbd, dps = _seq_vjp(_pool_fn, "pool_bwd", pp, (bd, pscale), dpo)
    dpc, ddw, ddb = _seq_vjp(_conv_fn, "conv_bwd", pc, (dw, db), dco)
    return dpp, dpc, dbd, dps, ddw, ddb


def _cat_fn(po, oa, ob, gg, co, gng, lng, lnb):
    o = oa + ob
    outs = []
    for h in range(NH):
        oh = o[:, h * HD:(h + 1) * HD]
        mu = jnp.mean(oh, axis=-1, keepdims=True)
        oc = oh - mu
        var = jnp.mean(oc * oc, axis=-1, keepdims=True)
        outs.append(oc * lax.rsqrt(var + EPS))
    ret = jnp.concatenate(outs, axis=1) * gng * (gg * jax.nn.sigmoid(gg))
    return jnp.concatenate([po, ret, _ln_swish(co, lng, lnb)], axis=1)


def _mix_out_fwd(z, po, oa, ob, gg, co, ro, mod, wout):
    B, S, _ = z.shape

    def body(z_ref, po_ref, oa_ref, ob_ref, gg_ref, co_ref, gn_ref, lg_ref, lb_ref, mod_ref, w_ref, zo_ref, out_ref):
        cat = _cat_fn(po_ref[...], oa_ref[...], ob_ref[...], gg_ref[...], co_ref[...], gn_ref[...], lg_ref[...], lb_ref[...])
        out = _nn(cat.astype(BF16), w_ref[...])
        out_ref[...] = out
        zo_ref[...] = z_ref[...] + mod_ref[2:3, :] * out

    return pl.pallas_call(
        body, name="mix_out_fwd", grid=(B, S // TM),
        in_specs=[_tok(D), _tok(PW), _tok(RW), _tok(RW), _tok(RW), _tok(PW), _const((1, RW)), _const((1, PW)),
                  _const((1, PW)), _modspec(), _wspec(wout)],
        out_specs=[_tok(D), _tok(D)],
        out_shape=[jax.ShapeDtypeStruct((B, S, D), F32)] * 2,
        compiler_params=_params(None, ("arbitrary", "arbitrary")),
    )(z, po, oa, ob, gg, co, *ro, mod, wout[0])


def _mix_out_bwd(dzo, out, po, oa, ob, gg, co, ro, mod, wout):
    B, S, _ = dzo.shape

    def body(dzo_ref, out_ref, po_ref, oa_ref, ob_ref, gg_ref, co_ref, gn_ref, lg_ref, lb_ref, mod_ref, w_ref,
             dpo_ref, do_ref, dgg_ref, dco_ref, cat_ref, dout_ref, dmod_ref, dgn_ref, dlg_ref, dlb_ref):
        b, t = pl.program_id(0), pl.program_id(1)
        dzo = dzo_ref[...]
        cat, vjp = jax.vjp(_cat_fn, po_ref[...], oa_ref[...], ob_ref[...], gg_ref[...], co_ref[...], gn_ref[...],
                           lg_ref[...], lb_ref[...])
        cat_ref[...] = cat.astype(BF16)
        dout = (mod_ref[2:3, :] * dzo).astype(BF16)
        dout_ref[...] = dout
        dgate = jnp.sum(out_ref[...] * dzo, axis=0, keepdims=True)
        dcat = _nt(dout, w_ref[...])
        dpo, doa, _, dgg, dco, dgn, dlg, dlb = vjp(dcat)
        dpo_ref[...] = dpo
        do_ref[...] = doa
        dgg_ref[...] = dgg
        dco_ref[...] = dco
        zero = jnp.zeros_like(dgate)
        _acc(dmod_ref, jnp.concatenate([zero, zero, dgate], axis=0), t <= 1)
        first = jnp.logical_and(b == 0, t == 0)
        _acc(dgn_ref, dgn, first)
        _acc(dlg_ref, dlg, first)
        _acc(dlb_ref, dlb, first)

    return pl.pallas_call(
        body, name="mix_out_bwd", grid=(B, S // TM),
        in_specs=[_tok(D), _tok(D), _tok(PW), _tok(RW), _tok(RW), _tok(RW), _tok(PW), _const((1, RW)), _const((1, PW)),
                  _const((1, PW)), _modspec(), _wspec(wout)],
        out_specs=[_tok(PW), _tok(RW), _tok(RW), _tok(PW), _tok(D), _tok(D), _modspec(), _const((1, RW)),
                   _const((1, PW)), _const((1, PW))],
        out_shape=[jax.ShapeDtypeStruct((B, S, PW), F32), jax.ShapeDtypeStruct((B, S, RW), F32),
                   jax.ShapeDtypeStruct((B, S, RW), F32), jax.ShapeDtypeStruct((B, S, PW), F32),
                   jax.ShapeDtypeStruct((B, S, D), BF16), jax.ShapeDtypeStruct((B, S, D), BF16),
                   jax.ShapeDtypeStruct((B, 2, 3, D), F32), jax.ShapeDtypeStruct((1, RW), F32),
                   jax.ShapeDtypeStruct((1, PW), F32), jax.ShapeDtypeStruct((1, PW), F32)],
        compiler_params=_params(None, ("arbitrary", "arbitrary")),
    )(dzo, out, po, oa, ob, gg, co, *ro, mod, wout[0])


def _rms(z, g):
    return z * lax.rsqrt(jnp.mean(z * z, axis=-1, keepdims=True) + EPS) * g


def _head(z, target, fg):
    B, S, _ = z.shape

    def body(z_ref, t_ref, g_ref, dz_ref, dg_ref, loss_ref):
        b, t = pl.program_id(0), pl.program_id(1)
        first = jnp.logical_and(b == 0, t == 0)

        @pl.when(t == 0)
        def _():
            dz_ref[...] = jnp.zeros((TM, D), F32)

        @pl.when(first)
        def _():
            dg_ref[...] = jnp.zeros((1, D), F32)
            loss_ref[...] = jnp.zeros((8, 128), F32)

        @pl.when(t > 0)
        def _():
            y, vjp = jax.vjp(_rms, z_ref[...], g_ref[...])
            err = y - t_ref[...]
            dz, dg = vjp(err * (1.0 / D))
            dz_ref[...] = dz
            dg_ref[...] += dg
            loss_ref[...] += 0.5 * jnp.sum(err * err) * (1.0 / D)

    return pl.pallas_call(
        body, name="head", grid=(B, S // TM),
        in_specs=[_tok(D), pl.BlockSpec((None, TM, D), lambda b, t: (b, jnp.maximum(t - 1, 0), 0)), _const((1, D))],
        out_specs=[_tok(D), _const((1, D)), _const((8, 128))],
        out_shape=[jax.ShapeDtypeStruct((B, S, D), F32), jax.ShapeDtypeStruct((1, D), F32),
                   jax.ShapeDtypeStruct((8, 128), F32)],
        compiler_params=_params(None, ("arbitrary", "arbitrary")),
    )(z, target, fg)


MROWS = 24
MCOL = 768


def _silu(x):
    return x * jax.nn.sigmoid(x)


def _mod_fwd(c24, wmod, bmod):
    ncol = wmod.shape[2]

    def body(c_ref, w_ref, b_ref, o_ref):
        sc = _silu(c_ref[...]).astype(BF16)
        o_ref[...] = _nn(sc, w_ref[...].astype(BF16)) + b_ref[...]

    return pl.pallas_call(
        body, name="mod_fwd", grid=(2, ncol // MCOL),
        in_specs=[pl.BlockSpec((MROWS, D), lambda l, j: (0, 0)), pl.BlockSpec((None, D, MCOL), lambda l, j: (l, 0, j)),
                  pl.BlockSpec((None, 1, MCOL), lambda l, j: (l, 0, j))],
        out_specs=pl.BlockSpec((None, MROWS, MCOL), lambda l, j: (l, 0, j)),
        out_shape=jax.ShapeDtypeStruct((2, MROWS, ncol), F32),
        compiler_params=_params(None, ("arbitrary", "arbitrary")),
    )(c24, wmod, bmod)


def _mod_bwd(c24, dmod, wmod):
    ncol = wmod.shape[2]

    def body(c_ref, d_ref, w_ref, dw_ref, dsc_ref):
        l, j = pl.program_id(0), pl.program_id(1)
        sc = _silu(c_ref[...]).astype(BF16)
        dm = d_ref[...].astype(BF16)
        dw_ref[...] = _tn(sc, dm)
        _acc(dsc_ref, _nt(dm, w_ref[...].astype(BF16)), jnp.logical_and(l == 0, j == 0))

    return pl.pallas_call(
        body, name="mod_bwd", grid=(2, ncol // MCOL),
        in_specs=[pl.BlockSpec((MROWS, D), lambda l, j: (0, 0)), pl.BlockSpec((None, MROWS, MCOL), lambda l, j: (l, 0, j)),
                  pl.BlockSpec((None, D, MCOL), lambda l, j: (l, 0, j))],
        out_specs=[pl.BlockSpec((None, D, MCOL), lambda l, j: (l, 0, j)), pl.BlockSpec((MROWS, D), lambda l, j: (0, 0))],
        out_shape=[jax.ShapeDtypeStruct((2, D, ncol), F32), jax.ShapeDtypeStruct((MROWS, D), F32)],
        compiler_params=_params(None, ("arbitrary", "arbitrary")),
    )(c24, dmod, wmod)


def _bmod_cctx_grad(dmod_full, dsc_parts, cctx):
    def body(d_ref, p_ref, c_ref, db_ref, dc_ref):
        db_ref[...] = jnp.sum(d_ref[...], axis=1, keepdims=True)
        tot = jnp.zeros((8, D), F32)
        for s in range(N_CHIP):
            tot = tot + p_ref[s]
        x = c_ref[...]
        sg = jax.nn.sigmoid(x)
        dc_ref[...] = jnp.sum(tot, axis=0, keepdims=True) * (sg * (1.0 + x * (1.0 - sg)))

    return pl.pallas_call(
        body, name="bmod_cctx_grad",
        out_shape=[jax.ShapeDtypeStruct((2, 1, N_MOD * D), F32), jax.ShapeDtypeStruct((1, D), F32)],
    )(dmod_full, dsc_parts, cctx)


def _adam_math(w, g, m, v):
    m = ADAM_B1 * m + (1.0 - ADAM_B1) * g
    v = ADAM_B2 * v + (1.0 - ADAM_B2) * (g * g)
    m_hat = m / (1.0 - ADAM_B1 ** ADAM_STEP)
    v_hat = v / (1.0 - ADAM_B2 ** ADAM_STEP)
    delta = -ADAM_LR * (m_hat / (jnp.sqrt(v_hat) + ADAM_EPS) + ADAM_WD * w)
    return delta, m, v


def _adam(w, g, m, v):
    R, Cc = w.shape
    if R * Cc * 4 <= (1 << 20):
        RB = R
    else:
        RB = 1 << (((1 << 18) // Cc).bit_length() - 1)
        assert R % RB == 0

    def body(w_ref, g_ref, m_ref, v_ref, d_ref, mo_ref, vo_ref):
        d, mn, vn = _adam_math(w_ref[...], g_ref[...], m_ref[...], v_ref[...])
        d_ref[...] = d
        mo_ref[...] = mn
        vo_ref[...] = vn

    spec = pl.BlockSpec((RB, Cc), lambda i: (i, 0))
    return pl.pallas_call(
        body, name="adam", grid=(R // RB,), in_specs=[spec] * 4, out_specs=[spec] * 3,
        out_shape=[jax.ShapeDtypeStruct((R, Cc), F32)] * 3,
        compiler_params=_params(None, ("arbitrary",)),
    )(w, g, m, v)


def _sum_devices(parts):
    K = parts.shape[1]

    def body(p_ref, o_ref):
        tot = p_ref[0]
        for i in range(1, N_DEV):
            tot = tot + p_ref[i]
        o_ref[...] = tot

    return pl.pallas_call(body, name="sum_devices", out_shape=jax.ShapeDtypeStruct((K, 128), F32))(parts)


def _sum_pieces(own, others):
    R = own.shape[0]
    RB = 96 if R % 96 == 0 else 32

    def body(a_ref, r_ref, o_ref):
        tot = a_ref[...].astype(F32)
        for i in range(N_DEV - 1):
            tot = tot + r_ref[i].astype(F32)
        o_ref[...] = tot

    return pl.pallas_call(
        body, name="sum_pieces", grid=(R // RB,),
        in_specs=[pl.BlockSpec((RB, D), lambda i: (i, 0)), pl.BlockSpec((N_DEV - 1, RB, D), lambda i: (0, i, 0))],
        out_specs=pl.BlockSpec((RB, D), lambda i: (i, 0)),
        out_shape=jax.ShapeDtypeStruct((R, D), F32),
        compiler_params=_params(None, ("arbitrary",)),
    )(own, others)


def _coords():
    return lax.axis_index("x"), lax.axis_index("y"), lax.axis_index("c")


_FLIPS = [(fx, fy, fc) for fx in (0, 1) for fy in (0, 1) for fc in (0, 1)][1:]


def _all_gather_small(buf):
    K = buf.shape[0]

    def body(in_ref, out_ref, send_sems, recv_sems, local_sem):
        x, y, c = _coords()
        me = 4 * x + 2 * y + c
        mine = pltpu.make_async_copy(in_ref, out_ref.at[me], local_sem)
        mine.start()
        sends = []
        for k, (fx, fy, fc) in enumerate(_FLIPS):
            peer = (x ^ fx, y ^ fy, c ^ fc)
            cp = pltpu.make_async_remote_copy(src_ref=in_ref, dst_ref=out_ref.at[me], send_sem=send_sems.at[k],
                                              recv_sem=recv_sems.at[k], device_id=peer, device_id_type=MESH)
            cp.start()
            sends.append(cp)
        for k, (fx, fy, fc) in enumerate(_FLIPS):
            src = 4 * (x ^ fx) + 2 * (y ^ fy) + (c ^ fc)
            pltpu.make_async_remote_copy(src_ref=in_ref, dst_ref=out_ref.at[src], send_sem=send_sems.at[k],
                                         recv_sem=recv_sems.at[k], device_id=(x, y, c), device_id_type=MESH).wait_recv()
        for cp in sends:
            cp.wait_send()
        mine.wait()

    return pl.pallas_call(
        body, name="all_gather_small",
        in_specs=[pl.BlockSpec(memory_space=pltpu.VMEM)], out_specs=pl.BlockSpec(memory_space=pltpu.VMEM),
        out_shape=jax.ShapeDtypeStruct((N_DEV, K, 128), F32),
        scratch_shapes=[pltpu.SemaphoreType.DMA((7,)), pltpu.SemaphoreType.DMA((7,)), pltpu.SemaphoreType.DMA],
        compiler_params=_params(VMEM_BIG),
    )(buf)


_CHIP_FLIPS = [(1, 0), (0, 1), (1, 1)]


_HBM = pl.BlockSpec(memory_space=pltpu.HBM)
_SEMS = pl.BlockSpec(memory_space=pltpu.SEMAPHORE)
_EFFECT = pltpu.SideEffectType.DATAFLOW_SIDE_EFFECTING


def _in_hbm(v):
    return pltpu.with_memory_space_constraint(v, pltpu.HBM)


def _copies_start(name, srcs, lands, n_sems, issue, after):
    ns, nl = len(srcs), len(lands)

    def body(*refs):
        src_refs, land_refs = refs[:ns], refs[ns:ns + nl]
        out = refs[ns + nl + 1:]
        issue(src_refs, land_refs, out[:nl], out[nl:2 * nl])
        out[-1][...] = jnp.zeros((8, 128), F32)

    outs = pl.pallas_call(
        body, name=name, in_specs=[_HBM] * (ns + nl) + [pl.BlockSpec(memory_space=pl.ANY)],
        out_specs=[_SEMS] * (2 * nl) + [_HBM] * (ns + nl) + [pl.BlockSpec(memory_space=pltpu.VMEM)],
        out_shape=[pltpu.SemaphoreType.DMA((n_sems,))] * (2 * nl) + [pltpu.HBM(v.shape, v.dtype) for v in (*srcs, *lands)]
        + [jax.ShapeDtypeStruct((8, 128), F32)],
        input_output_aliases={i: 2 * nl + i for i in range(ns + nl)},
        compiler_params=pltpu.CompilerParams(has_side_effects=_EFFECT),
    )(*[_in_hbm(v) for v in (*srcs, *lands)], after)
    return outs[:nl], outs[nl:2 * nl], outs[2 * nl:2 * nl + ns], outs[2 * nl + ns:2 * nl + ns + nl], outs[-1]


def _copies_wait(name, srcs, lands, send_sems, recv_sems, finish, after):
    ns, nl = len(srcs), len(lands)

    def body(*refs):
        src_refs, land_refs = refs[:ns], refs[ns:ns + nl]
        finish(src_refs, land_refs, refs[ns + nl:ns + 2 * nl], refs[ns + 2 * nl:ns + 3 * nl])

    outs = pl.pallas_call(
        body, name=name, in_specs=[_HBM] * (ns + nl) + [_SEMS] * (2 * nl) + [pl.BlockSpec(memory_space=pl.ANY)],
        out_specs=[_HBM] * (ns + nl), out_shape=[pltpu.HBM(v.shape, v.dtype) for v in (*srcs, *lands)],
        input_output_aliases={i: i for i in range(ns + nl)},
        compiler_params=pltpu.CompilerParams(has_side_effects=_EFFECT),
    )(*srcs, *lands, *send_sems, *recv_sems, after)
    return outs[ns:]


def _own_slab(land, mine, index):
    return lax.dynamic_update_slice_in_dim(land, mine[:, None], index, axis=1)


def _gather_start(units, after):
    x, y, c = _coords()
    chip = 2 * x + y
    lands = [_own_slab(lax.empty((u.shape[0], N_CHIP) + u.shape[1:], u.dtype), u, chip) for u in units]

    def issue(src_refs, land_refs, send_sems, recv_sems):
        x, y, c = _coords()
        s_me = 2 * x + y
        for i, (src, land) in enumerate(zip(src_refs, land_refs)):
            for j, (fx, fy) in enumerate(_CHIP_FLIPS):
                pltpu.make_async_remote_copy(src_ref=src, dst_ref=land.at[:, s_me], send_sem=send_sems[i].at[j],
                                             recv_sem=recv_sems[i].at[j], device_id=(x ^ fx, y ^ fy, c),
                                             device_id_type=MESH).start()

    return _copies_start("gather_start", units, lands, 3, issue, after)


def _gather_wait(tag, started, which, after):
    send_sems, recv_sems, srcs, lands, _ = started

    def finish(src_refs, land_refs, ssems, rsems):
        x, y, c = _coords()
        for src, land, ss, rs in zip(src_refs, land_refs, ssems, rsems):
            for j in range(3):
                cp = pltpu.make_async_remote_copy(src_ref=src, dst_ref=land.at[:, 0], send_sem=ss.at[j], recv_sem=rs.at[j],
                                                  device_id=(x, y, c), device_id_type=MESH)
                cp.wait_send()
                cp.wait_recv()

    done = _copies_wait("gather_wait_" + tag, [srcs[i] for i in which], [lands[i] for i in which], [send_sems[i] for i in which],
                        [recv_sems[i] for i in which], finish, after)
    return [d.reshape(d.shape[0], N_CHIP * d.shape[2], D) for d in done]


def _piece_rows(grads):
    return [g.shape[2] for g in grads]


def _scatter_start(tag, grads, after):
    rows = _piece_rows(grads)
    land = lax.empty((N_DEV - 1, sum(rows), D), grads[0].dtype)

    def issue(src_refs, land_refs, send_sems, recv_sems):
        x, y, c = _coords()
        for k, (fx, fy, fc) in enumerate(_FLIPS):
            px, py, pc = x ^ fx, y ^ fy, c ^ fc
            off = 0
            for src, n in zip(src_refs, rows):
                pltpu.make_async_remote_copy(src_ref=src.at[2 * px + py, pc], dst_ref=land_refs[0].at[k, pl.ds(off, n)],
                                             send_sem=send_sems[0].at[k], recv_sem=recv_sems[0].at[k],
                                             device_id=(px, py, pc), device_id_type=MESH).start()
                off += n

    return _copies_start("scatter_start_" + tag, grads, [land], N_DEV - 1, issue, after)


def _scatter_wait(tag, started, after):
    send_sems, recv_sems, srcs, lands, _ = started

    def finish(src_refs, land_refs, ssems, rsems):
        x, y, c = _coords()
        for k in range(N_DEV - 1):
            cp = pltpu.make_async_remote_copy(src_ref=land_refs[0].at[0], dst_ref=land_refs[0].at[0], send_sem=ssems[0].at[k],
                                              recv_sem=rsems[0].at[k], device_id=(x, y, c), device_id_type=MESH)
            cp.wait_send()
            cp.wait_recv()

    return _copies_wait("scatter_wait_" + tag, srcs, lands, send_sems, recv_sems, finish, after)[0]


def _swap_start(tag, mine, after):
    x, y, c = _coords()
    land = lax.dynamic_update_slice_in_dim(lax.empty((2,) + mine.shape, mine.dtype), mine[None], c, axis=0)

    def issue(src_refs, land_refs, send_sems, recv_sems):
        x, y, c = _coords()
        pltpu.make_async_remote_copy(src_ref=src_refs[0], dst_ref=land_refs[0].at[c], send_sem=send_sems[0].at[0],
                                     recv_sem=recv_sems[0].at[0], device_id=(x, y, 1 - c), device_id_type=MESH).start()

    return _copies_start("swap_start_" + tag, [mine], [land], 1, issue, after)


def _swap_wait(tag, started, after):
    send_sems, recv_sems, srcs, lands, _ = started

    def finish(src_refs, land_refs, ssems, rsems):
        x, y, c = _coords()
        cp = pltpu.make_async_remote_copy(src_ref=src_refs[0], dst_ref=land_refs[0].at[0], send_sem=ssems[0].at[0],
                                          recv_sem=rsems[0].at[0], device_id=(x, y, c), device_id_type=MESH)
        cp.wait_send()
        cp.wait_recv()

    return _copies_wait("swap_wait_" + tag, srcs, lands, send_sems, recv_sems, finish, after)[0]


def _pack(arrays):
    flat = jnp.concatenate([a.reshape(-1).astype(F32) for a in arrays])
    pad = (-flat.shape[0]) % 1024
    return jnp.pad(flat, (0, pad)).reshape(-1, 128)


def _unpack(buf, shapes):
    flat = buf.reshape(-1)
    out, off = [], 0
    for s in shapes:
        n = 1
        for d in s:
            n *= d
        out.append(flat[off:off + n].reshape(s))
        off += n
    return out


def _block_diag(pw):
    bd = jnp.zeros((PW, PW), F32)
    g = PW // 4
    for i in range(4):
        bd = bd.at[i * g:(i + 1) * g, i * g:(i + 1) * g].set(pw[i])
    return bd


def _lanes(v):
    return jnp.broadcast_to(v.reshape(NH, 1, 1), (NH, 1, HD))


def _layer_fwd(z, mod, normg, wget, small):
    S = z.shape[1]
    cos, sin = _rope_tables(S)
    fwd_sched, _ = _schedules(S)
    wa = wget("a", z)
    z1, f_a = _ffn_fwd(z, mod[:, :, 0], normg[0], *wa)
    wm = wget("m", z1)
    pp, q, k, v, gg, pc = _mix_in_fwd(z1, mod[:, :, 1], normg[1], wm[0], cos, sin)
    po, co = _pool_conv_fwd(pp, pc, small["bd"], small["pscale"], small["dw"], small["db"])
    ro = (small["gng"], small["lng"], small["lnb"])
    oa, ob = _retention(q, k, v, small["dec_f"], small["dec_b"], *fwd_sched)
    z2, out = _mix_out_fwd(z1, po, oa, ob, gg, co, ro, mod[:, :, 1], wm[1])
    wb = wget("b", z2)
    z3, f_b = _ffn_fwd(z2, mod[:, :, 2], normg[2], *wb)
    saved = dict(z=z, f_a=f_a, z1=z1, pp=pp, q=q, k=k, v=v, gg=gg, pc=pc, po=po, co=co, oa=oa, ob=ob, out=out, z2=z2, f_b=f_b,
                 wa=wa, wm=wm, wb=wb)
    return z3, saved


def _layer_bwd(dz3, sv, mod, normg, small, emit, tok):
    S = dz3.shape[1]
    B = dz3.shape[0]
    T = B * S
    cos, sin = _rope_tables(S)
    fwd_sched, bwd_sched = _schedules(S)
    wa, wm, wb = sv["wa"], sv["wm"], sv["wb"]
    dz2, dmod_b, dg_b, gw1t_b, gw3t_b, gw2_b = _ffn_bwd(sv["z2"], dz3, sv["f_b"], mod[:, :, 2] + tok, normg[2], *wb)
    tok = emit("b", [gw1t_b, gw3t_b, gw2_b])
    mod_m = mod[:, :, 1] + tok
    ro = (small["gng"], small["lng"], small["lnb"])
    dpo, do, dgg, dco, cat, dout, dmod_gate, dgng, dlng, dlnb = _mix_out_bwd(
        dz2, sv["out"], sv["po"], sv["oa"], sv["ob"], sv["gg"], sv["co"], ro, mod_m, wm[1])
    gwout = _tn_matmul(cat.reshape(T, D), dout.reshape(T, D))
    dqa, dqb = _retention(do, sv["v"], sv["k"], small["dec_f"], small["dec_b"], *fwd_sched)
    dka, dkb = _retention(sv["v"], do, sv["q"], small["dec_f"], small["dec_b"], *bwd_sched)
    dva, dvb = _retention(sv["k"], sv["q"], do, small["dec_f"], small["dec_b"], *bwd_sched)
    ddec = _retention_ddecay(sv["q"], sv["k"], sv["v"], do, small["dec_f"], small["dec_b"], *fwd_sched)
    dq, dk = _rope_bwd(dqa, dqb, dka, dkb, cos, sin)
    dpp, dpc, dbd, dps, ddw, ddb = _pool_conv_bwd(sv["pp"], sv["pc"], dpo, dco, small["bd"], small["pscale"],
                                                   small["dw"], small["db"])
    dz1, h, dp, dmod_m, dg_m = _mix_in_bwd(sv["z1"], dz2, dpp, dq, dk, dva + dvb, dgg, dpc, mod_m, normg[1], wm[0])
    gwint = _tn_matmul(dp.reshape(T, F), h.reshape(T, D))
    tok = emit("m", [gwint, gwout])
    dz, dmod_a, dg_a, gw1t_a, gw3t_a, gw2_a = _ffn_bwd(sv["z"], dz1, sv["f_a"], mod[:, :, 0] + tok, normg[0], *wa)
    tok = emit("a", [gw1t_a, gw3t_a, gw2_a])
    dmod = jnp.stack([dmod_a, dmod_m + dmod_gate, dmod_b], axis=2)
    dnormg = jnp.stack([dg_a, dg_m, dg_b], axis=0)
    g = PW // 4
    dpool_w = jnp.stack([dbd[i * g:(i + 1) * g, i * g:(i + 1) * g] for i in range(4)], axis=0)
    sm = dict(pool_w=dpool_w, pool_scale=dps[0], dec_f=ddec[:, 0, 0], dec_b=ddec[:, 1, 0], gng=dgng[0],
              conv_dw=ddw[0:CONV_K], conv_b=ddb[0], conv_ln_g=dlng[0], conv_ln_b=dlnb[0])
    return dz, dmod, dnormg, sm, tok


def _small_params(pool_w, pool_scale, dec_f, dec_b, gng, conv_dw, conv_b, lng, lnb):
    return dict(bd=_block_diag(pool_w), pscale=pool_scale.reshape(1, PW), dec_f=_lanes(dec_f), dec_b=_lanes(dec_b),
                gng=gng.reshape(1, RW), dw=jnp.pad(conv_dw, ((0, 1), (0, 0))), db=conv_b.reshape(1, PW),
                lng=lng.reshape(1, PW), lnb=lnb.reshape(1, PW))


_WEIGHTS = ["c_ctx", "w_mod", "b_mod", "norm_g", "ffn_w1", "ffn_w3", "ffn_w2", "w_in", "w_out", "pool_w", "pool_scale",
            "ret_decay_fwd", "ret_decay_bwd", "ret_gn_g", "conv_dw", "conv_b", "conv_ln_g", "conv_ln_b", "final_g"]
_BIG = ["w_mod", "ffn_w1", "ffn_w3", "ffn_w2", "w_in", "w_out"]
_SMALL = [n for n in _WEIGHTS if n not in _BIG]


def _adam_any(w, g, m, v):
    shape = w.shape
    cols = shape[-1] if w.ndim >= 2 else 128
    outs = _adam(w.reshape(-1, cols), g.reshape(-1, cols), m.reshape(-1, cols), v.reshape(-1, cols))
    return [o.reshape(shape) for o in outs]


def _step(a):
    x, c, ctx = a["x"], a["c"], a["ctx"]
    B = x.shape[0]
    nex = N_DEV * B
    assert nex + B <= MROWS and ctx.shape[1] == LC and x.shape[1] % TM == 0
    xi, yi, ci = _coords()
    me = 4 * xi + 2 * yi + ci
    chip = 2 * xi + yi
    ncol = a["w_mod"].shape[2]

    def t_bf16(w):
        return jnp.swapaxes(w, -1, -2).astype(BF16)

    w1t, w3t, w2 = t_bf16(a["ffn_w1"]), t_bf16(a["ffn_w3"]), a["ffn_w2"].astype(BF16)
    wint, wout = t_bf16(a["w_in"]), a["w_out"].astype(BF16)
    units = []
    for l in range(2):
        units += [jnp.stack([w1t[l, 0], w3t[l, 0], w2[l, 0]]), wint[l][None], wout[l][None],
                  jnp.stack([w1t[l, 1], w3t[l, 1], w2[l, 1]])]

    shapes1 = [(B, D), (2, 3, D // N_CHIP), (2, CONV_K, PW // N_CHIP)]
    g1 = _all_gather_small(_pack([c, a["norm_g"], a["conv_dw"]]))
    per = [_unpack(g1[d], shapes1) for d in range(N_DEV)]
    c_all = jnp.concatenate([per[d][0] for d in range(N_DEV)], axis=0)
    norm_g_full = jnp.concatenate([per[2 * s][1] for s in range(N_CHIP)], axis=-1)
    conv_dw_full = jnp.concatenate([per[2 * s][2] for s in range(N_CHIP)], axis=-1)
    cctx = a["c_ctx"].reshape(1, D)
    c24 = jnp.concatenate([c_all] + [cctx] * B + [jnp.zeros((MROWS - nex - B, D), F32)], axis=0)

    bsh = lax.dynamic_slice(a["b_mod"], (0, chip * ncol), (2, ncol)).reshape(2, 1, ncol)
    mod_raw = _mod_fwd(c24, a["w_mod"], bsh)
    g2 = _all_gather_small(_pack([mod_raw]))
    mod_full = jnp.concatenate([_unpack(g2[2 * s], [(2, MROWS, ncol)])[0] for s in range(N_CHIP)], axis=-1)
    mods = []
    for l in range(2):
        lat = lax.dynamic_slice(mod_full[l], (B * me, 0), (B, N_MOD * D))
        cx = jnp.broadcast_to(mod_full[l, nex][None], (B, N_MOD * D))
        mods.append(jnp.stack([cx, lat], axis=1).reshape(B, 2, 3, 3, D))

    started = _gather_start(units, mod_full)

    def wget_of(l):
        def wget(stage, after):
            if stage == "m":
                win, wo = _gather_wait(f"m{l}", started, [4 * l + 1, 4 * l + 2], after)
                return (win, 0), (wo, 0)
            (g,) = _gather_wait(f"{stage}{l}", started, [4 * l + (0 if stage == "a" else 3)], after)
            return (g, 0), (g, 1), (g, 2)
        return wget

    smalls = [_small_params(a["pool_w"][l], a["pool_scale"][l], a["ret_decay_fwd"][l], a["ret_decay_bwd"][l],
                            a["ret_gn_g"][l], conv_dw_full[l], a["conv_b"][l], a["conv_ln_g"][l], a["conv_ln_b"][l])
              for l in range(2)]
    normgs = [norm_g_full[l].reshape(3, 1, D) for l in range(2)]
    z = jnp.concatenate([ctx, x], axis=1)
    saved = []
    for l in range(2):
        z, sv = _layer_fwd(z, mods[l], normgs[l], wget_of(l), smalls[l])
        saved.append(sv)
    dz, dfinal_g, loss_part = _head(z, a["loss_target"], a["final_g"].reshape(1, D))

    scattering, swapping, reduced = [], [], {}

    def reduce_previous(after):
        tag, st, own = scattering.pop()
        mine = _sum_pieces(own, _scatter_wait(tag, st, after))
        sw = _swap_start(tag, mine, mine)
        if swapping:
            ptag, psw = swapping.pop()
            reduced[ptag] = _swap_wait(ptag, psw, sw[4])
        swapping.append((tag, sw))
        return sw[4]

    def emit_of(l):
        def emit(stage, grads):
            grads = [g.reshape(N_CHIP, 2, g.shape[0] // (2 * N_CHIP), D) for g in grads]
            st = _scatter_start(f"{stage}{l}", grads, grads[0])
            tok = st[4][0, 0]
            own = jnp.concatenate([lax.dynamic_slice(g, (chip, ci, 0, 0), (1, 1) + g.shape[2:]).reshape(g.shape[2:])
                                   for g in grads], axis=0)
            if scattering:
                tok = tok + reduce_previous(st[4])[0, 0]
            scattering.append((f"{stage}{l}", st, own))
            return tok
        return emit

    back = [None, None]
    tok = jnp.zeros((), F32)
    for l in (1, 0):
        dz, dmod, dnormg, sm, tok = _layer_bwd(dz, saved[l], mods[l], normgs[l], smalls[l], emit_of(l), tok)
        back[l] = (dmod, dnormg, None, sm)
    grad_x = dz[:, LC:]
    grads = {}

    dmods = [back[l][0].reshape(B, 2, N_MOD * D) for l in range(2)]
    pack_a = _pack([jnp.stack([dm[:, 1] for dm in dmods])])
    ka = pack_a.shape[0]
    sm = [back[l][3] for l in range(2)]
    sum_list = [jnp.stack([dm[:, 0] for dm in dmods]), jnp.stack([back[l][1][:, 0] for l in range(2)])]
    sm_keys = ["pool_w", "pool_scale", "dec_f", "dec_b", "gng", "conv_dw", "conv_b", "conv_ln_g", "conv_ln_b"]
    sum_list += [jnp.stack([sm[l][k] for l in range(2)]) for k in sm_keys]
    sum_list += [dfinal_g[0], loss_part[0, 0:1]]
    sum_shapes = [s.shape for s in sum_list]
    g3 = _all_gather_small(jnp.concatenate([pack_a, _pack(sum_list)], axis=0))
    dmx_all = jnp.concatenate([_unpack(g3[d, :ka], [(2, B, N_MOD * D)])[0] for d in range(N_DEV)], axis=1)
    summed = _unpack(_sum_devices(g3[:, ka:]), sum_shapes)
    dmy, dnorm_full = summed[0], summed[1]
    sgrad = dict(zip(sm_keys, summed[2:2 + len(sm_keys)]))
    loss = summed[-1].reshape(())

    dmod24 = jnp.concatenate([dmx_all, dmy, jnp.zeros((2, MROWS - nex - B, N_MOD * D), F32)], axis=1)
    dmod_my = lax.dynamic_slice(dmod24, (0, 0, chip * ncol), (2, MROWS, ncol))
    grads["w_mod"], dsc = _mod_bwd(c24, dmod_my, a["w_mod"])
    g4 = _all_gather_small(_pack([dsc[nex:nex + 8]]))
    dsc_parts = jnp.stack([_unpack(g4[2 * s], [(8, D)])[0] for s in range(N_CHIP)])
    dbmod, dcctx = _bmod_cctx_grad(dmod24, dsc_parts, cctx)

    grads["c_ctx"] = dcctx[0]
    grads["b_mod"] = dbmod.reshape(2, N_MOD * D)
    grads["norm_g"] = lax.dynamic_slice(dnorm_full, (0, 0, chip * (D // N_CHIP)), (2, 3, D // N_CHIP))
    grads["pool_w"] = sgrad["pool_w"]
    grads["pool_scale"] = sgrad["pool_scale"]
    grads["ret_decay_fwd"] = sgrad["dec_f"]
    grads["ret_decay_bwd"] = sgrad["dec_b"]
    grads["ret_gn_g"] = sgrad["gng"]
    grads["conv_dw"] = lax.dynamic_slice(sgrad["conv_dw"], (0, 0, chip * (PW // N_CHIP)), (2, CONV_K, PW // N_CHIP))
    grads["conv_b"] = sgrad["conv_b"]
    grads["conv_ln_g"] = sgrad["conv_ln_g"]
    grads["conv_ln_b"] = sgrad["conv_ln_b"]
    grads["final_g"] = summed[-2]

    delta, new_m, new_v = {}, {}, {}
    delta["w_mod"], new_m["w_mod"], new_v["w_mod"] = _adam_any(a["w_mod"], grads["w_mod"], a["m_w_mod"], a["v_w_mod"])
    shapes_s = [a[n].shape for n in _SMALL]
    packed = _adam(_pack([a[n] for n in _SMALL]), _pack([grads[n] for n in _SMALL]),
                   _pack([a["m_" + n] for n in _SMALL]), _pack([a["v_" + n] for n in _SMALL]))
    for res, out in zip(packed, (delta, new_m, new_v)):
        for n, val in zip(_SMALL, _unpack(res, shapes_s)):
            out[n] = val

    last = reduce_previous(packed[0])
    tag, sw = swapping.pop()
    reduced[tag] = _swap_wait(tag, sw, last)

    def rows_of(both, lo, n):
        return both[:, lo:lo + n].reshape(2 * n, D)

    ffn = [[reduced[f"a{l}"], reduced[f"b{l}"]] for l in range(2)]
    grads["ffn_w1"] = jnp.stack([jnp.stack([rows_of(ffn[l][i], 0, HSLAB).T for i in range(2)]) for l in range(2)])
    grads["ffn_w3"] = jnp.stack([jnp.stack([rows_of(ffn[l][i], HSLAB, HSLAB).T for i in range(2)]) for l in range(2)])
    grads["ffn_w2"] = jnp.stack([jnp.stack([rows_of(ffn[l][i], 2 * HSLAB, HSLAB) for i in range(2)]) for l in range(2)])
    grads["w_in"] = jnp.stack([rows_of(reduced[f"m{l}"], 0, HSLAB).T for l in range(2)])
    grads["w_out"] = jnp.stack([rows_of(reduced[f"m{l}"], HSLAB, HOSLAB) for l in range(2)])
    for n in _BIG[1:]:
        delta[n], new_m[n], new_v[n] = _adam_any(a[n], grads[n], a["m_" + n], a["v_" + n])
    return (loss, grad_x, *[grads[n] for n in _WEIGHTS], *[delta[n] for n in _WEIGHTS],
            *[new_m[n] for n in _WEIGHTS], *[new_v[n] for n in _WEIGHTS])


def kernel(x, c, ctx, c_ctx, w_mod, b_mod, norm_g, ffn_w1, ffn_w3, ffn_w2, w_in, w_out, pool_w, pool_scale, ret_decay_fwd, ret_decay_bwd, ret_gn_g, conv_dw, conv_b, conv_ln_g, conv_ln_b, final_g, loss_target, m_c_ctx, m_w_mod, m_b_mod, m_norm_g, m_ffn_w1, m_ffn_w3, m_ffn_w2, m_w_in, m_w_out, m_pool_w, m_pool_scale, m_ret_decay_fwd, m_ret_decay_bwd, m_ret_gn_g, m_conv_dw, m_conv_b, m_conv_ln_g, m_conv_ln_b, m_final_g, v_c_ctx, v_w_mod, v_b_mod, v_norm_g, v_ffn_w1, v_ffn_w3, v_ffn_w2, v_w_in, v_w_out, v_pool_w, v_pool_scale, v_ret_decay_fwd, v_ret_decay_bwd, v_ret_gn_g, v_conv_dw, v_conv_b, v_conv_ln_g, v_conv_ln_b, v_final_g):
    return _step(dict(locals()))
```

```python
import functools

import jax
import jax.numpy as jnp
from jax import lax
from jax.experimental import pallas as pl
from jax.experimental.pallas import tpu as pltpu

F32 = jnp.float32
BF16 = jnp.bfloat16

D = 1024
F = 2816
FH = 1408
N_MOD = 9
LC = 256
TM = 256
HD = 128
NH = 4
RW = 512
PW = 256
CONV_K = 31
GRID_W = 64
EPS = 1e-6
K_SCALE = HD ** -0.5
N_DEV = 8
N_CHIP = 4
SLAB = F // N_CHIP
HSLAB = SLAB // 2
OSLAB = D // N_CHIP
HOSLAB = OSLAB // 2
VMEM_BIG = 60 * 1024 * 1024
MESH = pl.DeviceIdType.MESH

ADAM_LR = 0.001
ADAM_B1 = 0.9
ADAM_B2 = 0.999
ADAM_EPS = 1e-08
ADAM_WD = 0.01
ADAM_STEP = 10


def _nt(a, b):
    return lax.dot_general(a, b, (((1,), (1,)), ((), ())), preferred_element_type=F32)


def _nn(a, b):
    return lax.dot_general(a, b, (((1,), (0,)), ((), ())), preferred_element_type=F32)


def _tn(a, b):
    return lax.dot_general(a, b, (((0,), (0,)), ((), ())), preferred_element_type=F32)


def _params(vmem=None, sem=None):
    return pltpu.CompilerParams(dimension_semantics=sem, vmem_limit_bytes=vmem)


def _rms_mod(z, g, shift, scale):
    y = z * lax.rsqrt(jnp.mean(z * z, axis=-1, keepdims=True) + EPS)
    return (y * g) * (1.0 + scale) + shift


def _acc(ref, val, first):
    @pl.when(first)
    def _():
        ref[...] = val

    @pl.when(jnp.logical_not(first))
    def _():
        ref[...] += val


def _tok(width):
    return pl.BlockSpec((None, TM, width), lambda b, t: (b, t, 0))


def _modspec():
    return pl.BlockSpec((None, None, 3, D), lambda b, t: (b, jnp.minimum(t, 1), 0, 0))


def _const(shape):
    nd = len(shape)
    return pl.BlockSpec(shape, lambda b, t: (0,) * nd)


def _wspec(w):
    stack, idx = w
    return pl.BlockSpec((None,) + stack.shape[1:], lambda b, t: (idx, 0, 0), pipeline_mode=pl.Buffered(1))


def _ffn_fwd(z, mod, g, w1t, w3t, w2):
    B, S, _ = z.shape

    def body(z_ref, mod_ref, g_ref, w1_ref, w3_ref, w2_ref, zo_ref, f_ref):
        zt = z_ref[...]
        h = _rms_mod(zt, g_ref[...], mod_ref[0:1, :], mod_ref[1:2, :]).astype(BF16)
        f = jnp.zeros((TM, D), F32)
        for c in range(F // FH):
            rows = slice(c * FH, (c + 1) * FH)
            u1 = _nt(h, w1_ref[rows, :])
            u3 = _nt(h, w3_ref[rows, :])
            a = (u1 * jax.nn.sigmoid(u1) * u3).astype(BF16)
            f = f + _nn(a, w2_ref[rows, :])
        f_ref[...] = f
        zo_ref[...] = zt + 0.5 * mod_ref[2:3, :] * f

    return pl.pallas_call(
        body, name="ffn_fwd", grid=(B, S // TM),
        in_specs=[_tok(D), _modspec(), _const((1, D)), _wspec(w1t), _wspec(w3t), _wspec(w2)],
        out_specs=[_tok(D), _tok(D)],
        out_shape=[jax.ShapeDtypeStruct((B, S, D), F32)] * 2,
        compiler_params=_params(VMEM_BIG, ("arbitrary", "arbitrary")),
    )(z, mod, g, w1t[0], w3t[0], w2[0])


def _ffn_bwd(z, dzo, f, mod, g, w1t, w3t, w2):
    B, S, _ = z.shape

    def body(z_ref, dzo_ref, f_ref, mod_ref, g_ref, w1_ref, w3_ref, w2_ref,
             dz_ref, h_ref, du1_ref, du3_ref, a_ref, do_ref, dmod_ref, dg_ref):
        b, t = pl.program_id(0), pl.program_id(1)
        zt = z_ref[...]
        dzo = dzo_ref[...]
        gate = mod_ref[2:3, :]
        h32, vjp_h = jax.vjp(_rms_mod, zt, g_ref[...], mod_ref[0:1, :], mod_ref[1:2, :])
        h = h32.astype(BF16)
        h_ref[...] = h
        do = (0.5 * gate * dzo).astype(BF16)
        do_ref[...] = do
        dgate = jnp.sum(0.5 * f_ref[...] * dzo, axis=0, keepdims=True)
        dh = jnp.zeros((TM, D), F32)
        for c in range(F // FH):
            rows = slice(c * FH, (c + 1) * FH)
            u1 = _nt(h, w1_ref[rows, :])
            u3 = _nt(h, w3_ref[rows, :])
            sg = jax.nn.sigmoid(u1)
            s = u1 * sg
            a_ref[:, rows] = (s * u3).astype(BF16)
            da = _nt(do, w2_ref[rows, :])
            du3 = (da * s).astype(BF16)
            du1 = (da * u3 * (sg * (1.0 + u1 * (1.0 - sg)))).astype(BF16)
            du1_ref[:, rows] = du1
            du3_ref[:, rows] = du3
            dh = dh + _nn(du1, w1_ref[rows, :]) + _nn(du3, w3_ref[rows, :])
        dz_h, dg, dshift, dscale = vjp_h(dh)
        dz_ref[...] = dzo + dz_h
        _acc(dmod_ref, jnp.concatenate([dshift, dscale, dgate], axis=0), t <= 1)
        _acc(dg_ref, dg, jnp.logical_and(b == 0, t == 0))

    T = B * S
    outs = pl.pallas_call(
        body, name="ffn_bwd", grid=(B, S // TM),
        in_specs=[_tok(D), _tok(D), _tok(D), _modspec(), _const((1, D)), _wspec(w1t), _wspec(w3t), _wspec(w2)],
        out_specs=[_tok(D), _tok(D), _tok(F), _tok(F), _tok(F), _tok(D), _modspec(), _const((1, D))],
        out_shape=[jax.ShapeDtypeStruct((B, S, D), F32), jax.ShapeDtypeStruct((B, S, D), BF16),
                   jax.ShapeDtypeStruct((B, S, F), BF16), jax.ShapeDtypeStruct((B, S, F), BF16),
                   jax.ShapeDtypeStruct((B, S, F), BF16), jax.ShapeDtypeStruct((B, S, D), BF16),
                   jax.ShapeDtypeStruct((B, 2, 3, D), F32), jax.ShapeDtypeStruct((1, D), F32)],
        compiler_params=_params(VMEM_BIG, ("arbitrary", "arbitrary")),
    )(z, dzo, f, mod, g, w1t[0], w3t[0], w2[0])
    dz, h, du1, du3, a, do, dmod, dg = outs
    gw1t = _tn_matmul(du1.reshape(T, F), h.reshape(T, D))
    gw3t = _tn_matmul(du3.reshape(T, F), h.reshape(T, D))
    gw2 = _tn_matmul(a.reshape(T, F), do.reshape(T, D))
    return dz, dmod, dg, gw1t, gw3t, gw2


def _tn_matmul(a, b):
    T, M = a.shape
    N = b.shape[1]
    MB = FH if M > FH else M
    TT = next(t for t in (1152, 1024, 768, 512, TM) if T % t == 0)
    nt = T // TT

    def body(a_ref, b_ref, o_ref, acc_ref):
        t = pl.program_id(1)
        prod = _tn(a_ref[...], b_ref[...])
        _acc(acc_ref, prod, t == 0)

        @pl.when(t == nt - 1)
        def _():
            o_ref[...] = acc_ref[...].astype(BF16)

    return pl.pallas_call(
        body, name="tn_matmul", grid=(M // MB, nt),
        in_specs=[pl.BlockSpec((TT, MB), lambda i, t: (t, i)), pl.BlockSpec((TT, N), lambda i, t: (t, 0))],
        out_specs=pl.BlockSpec((MB, N), lambda i, t: (i, 0)),
        out_shape=jax.ShapeDtypeStruct((M, N), BF16),
        scratch_shapes=[pltpu.VMEM((MB, N), F32)],
        compiler_params=_params(VMEM_BIG, ("arbitrary", "arbitrary")),
    )(a, b)


def _swap32(x):
    n = x.shape[1]
    lane = lax.broadcasted_iota(jnp.int32, x.shape, 1)
    return jnp.where((lane % 64) < 32, pltpu.roll(x, n - 32, 1), pltpu.roll(x, 32, 1))


def _rope(x, cos, sin):
    return x * cos + _swap32(x) * sin


def _rope_t(dy, cos, sin):
    return dy * cos + _swap32(dy * sin)


def _rope_tables(S):
    L = S - LC
    n_freq = HD // 4
    inv = 10000.0 ** (-jnp.arange(n_freq, dtype=F32) / n_freq)
    i = jnp.arange(L)
    row = (i // GRID_W).astype(F32)
    col = (i % GRID_W).astype(F32)
    ang_r = row[:, None] * inv[None]
    ang_c = col[:, None] * inv[None]
    ang = jnp.concatenate([ang_r, ang_r, ang_c, ang_c], axis=1)
    ang = jnp.concatenate([jnp.zeros((LC, HD), F32), ang], axis=0)
    sign = jnp.where((jnp.arange(HD) % 64) < 32, -1.0, 1.0).astype(F32)
    return jnp.cos(ang), jnp.sin(ang) * sign[None]


def _tabspec():
    return pl.BlockSpec((TM, HD), lambda b, t: (t, 0))


def _mix_in_fwd(z, mod, g, wint, cos, sin):
    B, S, _ = z.shape

    def body(z_ref, mod_ref, g_ref, w_ref, cos_ref, sin_ref, pp_ref, q_ref, k_ref, v_ref, gg_ref, pc_ref):
        h = _rms_mod(z_ref[...], g_ref[...], mod_ref[0:1, :], mod_ref[1:2, :]).astype(BF16)
        p = _nt(h, w_ref[...])
        cos = jnp.tile(cos_ref[...], (1, NH))
        sin = jnp.tile(sin_ref[...], (1, NH))
        pp_ref[...] = p[:, 0:PW]
        q_ref[...] = _rope(p[:, PW:PW + RW], cos, sin)
        k_ref[...] = _rope(p[:, PW + RW:PW + 2 * RW], cos, sin) * K_SCALE
        v_ref[...] = p[:, PW + 2 * RW:PW + 3 * RW]
        gg_ref[...] = p[:, PW + 3 * RW:PW + 4 * RW]
        pc_ref[...] = p[:, PW + 4 * RW:]

    return pl.pallas_call(
        body, name="mix_in_fwd", grid=(B, S // TM),
        in_specs=[_tok(D), _modspec(), _const((1, D)), _wspec(wint), _tabspec(), _tabspec()],
        out_specs=[_tok(PW), _tok(RW), _tok(RW), _tok(RW), _tok(RW), _tok(2 * PW)],
        out_shape=[jax.ShapeDtypeStruct((B, S, PW), F32)] + [jax.ShapeDtypeStruct((B, S, RW), F32)] * 5,
        compiler_params=_params(VMEM_BIG, ("arbitrary", "arbitrary")),
    )(z, mod, g, wint[0], cos, sin)


def _mix_in_bwd(z, dzo, dpp, dq, dk, dva, dvb, dgg, dpc, mod, g, wint):
    B, S, _ = z.shape

    def body(z_ref, dzo_ref, dpp_ref, dq_ref, dk_ref, dva_ref, dvb_ref, dgg_ref, dpc_ref, mod_ref, g_ref, w_ref,
             dz_ref, h_ref, dp_ref, dmod_ref, dg_ref):
        b, t = pl.program_id(0), pl.program_id(1)
        h32, vjp_h = jax.vjp(_rms_mod, z_ref[...], g_ref[...], mod_ref[0:1, :], mod_ref[1:2, :])
        h_ref[...] = h32.astype(BF16)
        dp = jnp.concatenate([dpp_ref[...], dq_ref[...], dk_ref[...], dva_ref[...] + dvb_ref[...], dgg_ref[...],
                              dpc_ref[...]], axis=1).astype(BF16)
        dp_ref[...] = dp
        dh = _nn(dp, w_ref[...])
        dz_h, dg, dshift, dscale = vjp_h(dh)
        dz_ref[...] = dzo_ref[...] + dz_h
        _acc(dmod_ref, jnp.concatenate([dshift, dscale, jnp.zeros_like(dshift)], axis=0), t <= 1)
        _acc(dg_ref, dg, jnp.logical_and(b == 0, t == 0))

    return pl.pallas_call(
        body, name="mix_in_bwd", grid=(B, S // TM),
        in_specs=[_tok(D), _tok(D), _tok(PW), _tok(RW), _tok(RW), _tok(RW), _tok(RW), _tok(RW), _tok(2 * PW),
                  _modspec(), _const((1, D)), _wspec(wint)],
        out_specs=[_tok(D), _tok(D), _tok(F), _modspec(), _const((1, D))],
        out_shape=[jax.ShapeDtypeStruct((B, S, D), F32), jax.ShapeDtypeStruct((B, S, D), BF16),
                   jax.ShapeDtypeStruct((B, S, F), BF16), jax.ShapeDtypeStruct((B, 2, 3, D), F32),
                   jax.ShapeDtypeStruct((1, D), F32)],
        compiler_params=_params(VMEM_BIG, ("arbitrary", "arbitrary")),
    )(z, dzo, dpp, dq, dk, dva, dvb, dgg, dpc, mod, g, wint[0])


def _rope_bwd(dqa, dqb, dka, dkb, cos, sin):
    B, S, _ = dqa.shape

    def body(dqa_ref, dqb_ref, dka_ref, dkb_ref, cos_ref, sin_ref, dq_ref, dk_ref):
        cos = jnp.tile(cos_ref[...], (1, NH))
        sin = jnp.tile(sin_ref[...], (1, NH))
        dq_ref[...] = _rope_t(dqa_ref[...] + dqb_ref[...], cos, sin)
        dk_ref[...] = _rope_t(dka_ref[...] + dkb_ref[...], cos, sin) * K_SCALE

    return pl.pallas_call(
        body, name="rope_bwd", grid=(B, S // TM),
        in_specs=[_tok(RW)] * 4 + [_tabspec(), _tabspec()],
        out_specs=[_tok(RW), _tok(RW)],
        out_shape=[jax.ShapeDtypeStruct((B, S, RW), F32)] * 2,
        compiler_params=_params(None, ("arbitrary", "arbitrary")),
    )(dqa, dqb, dka, dkb, cos, sin)


def _log_sigmoid(x):
    return jnp.minimum(x, 0.0) - jnp.log(1.0 + jnp.exp(-jnp.abs(x)))


def _retention(a, b, c, dec_a, dec_b, sched_a, sched_b):
    B, S, _ = a.shape
    C = TM

    def body(a_ref, b_ref, c_ref, da_ref, db_ref, oa_ref, ob_ref):
        ii = lax.broadcasted_iota(jnp.int32, (C, C), 0)
        jj = lax.broadcasted_iota(jnp.int32, (C, C), 1)
        pos = lax.broadcasted_iota(jnp.int32, (C, 1), 0).astype(F32)
        for dec_ref, o_ref, (order, causal, strict) in ((da_ref, oa_ref, sched_a), (db_ref, ob_ref, sched_b)):
            lg = _log_sigmoid(dec_ref[...])
            lg1 = lg[:, 0:1]
            dist = ((ii - jj) if causal else (jj - ii)).astype(F32)
            mask = (dist > 0.0) if strict else (dist >= 0.0)
            decay = jnp.where(mask, jnp.exp(jnp.maximum(dist, 0.0) * lg1), 0.0)
            p = pos if causal else (C - 1.0 - pos)
            w_q = jnp.exp((p + 1.0) * lg1)
            w_k = jnp.exp((C - 1.0 - p) * lg1)
            chunk_decay = jnp.exp(C * lg)
            state = jnp.zeros((HD, HD), F32)
            for n in order:
                rows = pl.ds(n * C, C)
                at, bt, ct = a_ref[rows, :], b_ref[rows, :], c_ref[rows, :]
                cb = ct.astype(BF16)
                scores = _nt(at.astype(BF16), bt.astype(BF16)) * decay
                o = _nn(scores.astype(BF16), cb)
                o = o + _nn((at * w_q).astype(BF16), state.astype(BF16))
                o_ref[rows, :] = o
                state = chunk_decay * state + _tn((bt * w_k).astype(BF16), cb)

    seq = pl.BlockSpec((None, S, HD), lambda b, h: (b, 0, h))
    dspec = pl.BlockSpec((None, 1, HD), lambda b, h: (h, 0, 0))
    return pl.pallas_call(
        body, name="retention", grid=(B, NH),
        in_specs=[seq, seq, seq, dspec, dspec], out_specs=[seq, seq],
        out_shape=[jax.ShapeDtypeStruct((B, S, RW), F32)] * 2,
        compiler_params=_params(VMEM_BIG, ("arbitrary", "arbitrary")),
    )(a, b, c, dec_a, dec_b)


def _retention_ddecay(q, k, v, do, dec_a, dec_b, sched_a, sched_b):
    B, S, _ = q.shape
    C = TM

    def body(q_ref, k_ref, v_ref, do_ref, da_ref, db_ref, o_ref):
        ii = lax.broadcasted_iota(jnp.int32, (C, C), 0)
        jj = lax.broadcasted_iota(jnp.int32, (C, C), 1)
        pos = lax.broadcasted_iota(jnp.int32, (C, 1), 0).astype(F32)
        vals = []
        for dec_ref, (order, causal, strict) in ((da_ref, sched_a), (db_ref, sched_b)):
            x = dec_ref[...]
            lg = _log_sigmoid(x)
            lg1 = lg[:, 0:1]
            dist = ((ii - jj) if causal else (jj - ii)).astype(F32)
            mask = (dist > 0.0) if strict else (dist >= 0.0)
            ddecay = jnp.where(mask, dist * jnp.exp(jnp.maximum(dist, 0.0) * lg1), 0.0)
            p = pos if causal else (C - 1.0 - pos)
            w_q = jnp.exp((p + 1.0) * lg1)
            w_k = jnp.exp((C - 1.0 - p) * lg1)
            chunk_decay = jnp.exp(C * lg)
            state = jnp.zeros((HD, HD), F32)
            dstate = jnp.zeros((HD, HD), F32)
            tot = jnp.zeros((), F32)
            for n in order:
                rows = pl.ds(n * C, C)
                qt, kt, vt, dot = q_ref[rows, :], k_ref[rows, :], v_ref[rows, :], do_ref[rows, :]
                vb = vt.astype(BF16)
                scores = _nt(qt.astype(BF16), kt.astype(BF16))
                dscores = _nt(dot.astype(BF16), vb)
                qw = (qt * w_q).astype(BF16)
                cross = _nn(qw, state.astype(BF16))
                dcross = _nn(qw, dstate.astype(BF16))
                tot = tot + jnp.sum(scores * dscores * ddecay) + jnp.sum(((p + 1.0) * cross + dcross) * dot)
                kv = _tn((kt * w_k).astype(BF16), vb)
                dkv = _tn((kt * ((C - 1.0 - p) * w_k)).astype(BF16), vb)
                dstate = chunk_decay * (dstate + C * state) + dkv
                state = chunk_decay * state + kv
            vals.append(tot * jax.nn.sigmoid(-x))
        row = lax.broadcasted_iota(jnp.int32, (8, HD), 0)
        tile = jnp.where(row == 0, vals[0], 0.0) + jnp.where(row == 1, vals[1], 0.0)
        _acc(o_ref, tile, pl.program_id(1) == 0)

    seq = pl.BlockSpec((None, S, HD), lambda h, b: (b, 0, h))
    dspec = pl.BlockSpec((None, 1, HD), lambda h, b: (h, 0, 0))
    return pl.pallas_call(
        body, name="retention_ddecay", grid=(NH, B),
        in_specs=[seq, seq, seq, seq, dspec, dspec], out_specs=pl.BlockSpec((None, 8, HD), lambda h, b: (h, 0, 0)),
        out_shape=jax.ShapeDtypeStruct((NH, 8, HD), F32),
        compiler_params=_params(VMEM_BIG, ("arbitrary", "arbitrary")),
    )(q, k, v, do, dec_a, dec_b)


def _schedules(S):
    n = S // TM
    lat_up = tuple(range(1, n))
    lat_down = tuple(range(n - 1, 0, -1))
    fwd = (((0,) + lat_up, True, False), ((0,) + lat_down, False, True))
    bwd = ((lat_down + (0,), False, False), (lat_up + (0,), True, True))
    return fwd, bwd


def _shift_rows(x, d):
    if d == 0:
        return x
    S = x.shape[0]
    t = lax.broadcasted_iota(jnp.int32, x.shape, 0)
    tt = t + d
    lo = jnp.where(t < LC, 0, LC)
    hi = jnp.where(t < LC, LC, S)
    return jnp.where((tt >= lo) & (tt < hi), pltpu.roll(x, (-d) % S, 0), 0.0)


@functools.partial(jax.custom_vjp, nondiff_argnums=(1,))
def _shift(x, d):
    return _shift_rows(x, d)


_shift.defvjp(lambda x, d: (_shift_rows(x, d), None), lambda d, _, g: (_shift_rows(g, -d),))


def _pool_fn(p, bd, pscale):
    lane = lax.broadcasted_iota(jnp.int32, p.shape, 1)
    grp = lane // (PW // 4)
    half = jnp.where(grp == 0, 1, jnp.where(grp == 1, 2, jnp.where(grp == 2, 4, 8)))
    ones = jnp.ones(p.shape, F32)
    acc = jnp.zeros(p.shape, F32)
    cnt = jnp.zeros(p.shape, F32)
    for d in range(-8, 8):
        inwin = ((d >= -half) & (d < half)).astype(F32)
        acc = acc + _shift(p, d) * inwin
        cnt = cnt + _shift_rows(ones, d) * inwin
    pooled = acc / cnt - p
    mixed = _nn(pooled.astype(BF16), bd.astype(BF16))
    return mixed * pscale


def _dwconv_raw(zc, dw):
    y = jnp.zeros(zc.shape, F32)
    for k in range(CONV_K):
        y = y + _shift_rows(zc, k - CONV_K // 2) * dw[k:k + 1, :]
    return y


@jax.custom_vjp
def _dwconv(zc, dw):
    return _dwconv_raw(zc, dw)


def _dwconv_fwd(zc, dw):
    return _dwconv_raw(zc, dw), (zc, dw)


def _dwconv_bwd(res, g):
    zc, dw = res
    dz = jnp.zeros(zc.shape, F32)
    ddw = jnp.zeros(dw.shape, F32)
    row = lax.broadcasted_iota(jnp.int32, dw.shape, 0)
    for k in range(CONV_K):
        dz = dz + _shift_rows(g, CONV_K // 2 - k) * dw[k:k + 1, :]
        r = jnp.sum(g * _shift_rows(zc, k - CONV_K // 2), axis=0, keepdims=True)
        ddw = ddw + jnp.where(row == k, r, 0.0)
    return dz, ddw


_dwconv.defvjp(_dwconv_fwd, _dwconv_bwd)


def _conv_fn(u, dw, db):
    zc = u[:, :PW] * jax.nn.sigmoid(u[:, PW:])
    return _dwconv(zc, dw) + db


def _ln_swish(y, lng, lnb):
    mu = jnp.mean(y, axis=-1, keepdims=True)
    yc = y - mu
    var = jnp.mean(yc * yc, axis=-1, keepdims=True)
    yn = yc * lax.rsqrt(var + EPS) * lng + lnb
    return yn * jax.nn.sigmoid(yn)


def _seq(shape, single=False):
    return pl.BlockSpec((None,) + shape, lambda b: (b, 0, 0), pipeline_mode=pl.Buffered(1) if single else None)


def _c1(shape):
    nd = len(shape)
    return pl.BlockSpec(shape, lambda b: (0,) * nd)


def _seq_apply(fn, name, xs, consts, width):
    B, S, w = xs.shape

    def body(x_ref, *refs):
        refs[-1][...] = fn(x_ref[...], *[r[...] for r in refs[:-1]])

    return pl.pallas_call(
        body, name=name, grid=(B,),
        in_specs=[_seq((S, w))] + [_c1(c.shape) for c in consts], out_specs=_seq((S, width)),
        out_shape=jax.ShapeDtypeStruct((B, S, width), F32),
        compiler_params=_params(VMEM_BIG, ("arbitrary",)),
    )(xs, *consts)


def _seq_vjp(fn, name, xs, consts, dout):
    B, S, w = xs.shape
    n = len(consts)

    def body(x_ref, d_ref, *refs):
        first = pl.program_id(0) == 0
        _, vjp = jax.vjp(fn, x_ref[...], *[r[...] for r in refs[:n]])
        grads = vjp(d_ref[...])
        refs[n][...] = grads[0]
        for ref, val in zip(refs[n + 1:], grads[1:]):
            _acc(ref, val, first)

    return pl.pallas_call(
        body, name=name, grid=(B,),
        in_specs=[_seq((S, w), True), _seq((S, dout.shape[2]), True)] + [_c1(c.shape) for c in consts],
        out_specs=[_seq((S, w))] + [_c1(c.shape) for c in consts],
        out_shape=[jax.ShapeDtypeStruct((B, S, w), F32)] + [jax.ShapeDtypeStruct(c.shape, F32) for c in consts],
        compiler_params=_params(VMEM_BIG, ("arbitrary",)),
    )(xs, dout, *consts)


def _pool_conv_fwd(pp, pc, bd, pscale, dw, db):
    return (_seq_apply(_pool_fn, "pool_fwd", pp, (bd, pscale), PW),
            _seq_apply(_conv_fn, "conv_fwd", pc, (dw, db), PW))


def _pool_conv_bwd(pp, pc, dpo, dco, bd, pscale, dw, db):
    dpp, dbd, dps = _seq_vjp(_pool_fn, "pool_bwd", pp, (bd, pscale), dpo)
    dpc, ddw, ddb = _seq_vjp(_conv_fn, "conv_bwd", pc, (dw, db), dco)
    return dpp, dpc, dbd, dps, ddw, ddb


def _cat_fn(po, oa, ob, gg, co, gng, lng, lnb):
    o = oa + ob
    outs = []
    for h in range(NH):
        oh = o[:, h * HD:(h + 1) * HD]
        mu = jnp.mean(oh, axis=-1, keepdims=True)
        oc = oh - mu
        var = jnp.mean(oc * oc, axis=-1, keepdims=True)
        outs.append(oc * lax.rsqrt(var + EPS))
    ret = jnp.concatenate(outs, axis=1) * gng * (gg * jax.nn.sigmoid(gg))
    return jnp.concatenate([po, ret, _ln_swish(co, lng, lnb)], axis=1)


def _mix_out_fwd(z, po, oa, ob, gg, co, ro, mod, wout):
    B, S, _ = z.shape

    def body(z_ref, po_ref, oa_ref, ob_ref, gg_ref, co_ref, gn_ref, lg_ref, lb_ref, mod_ref, w_ref, zo_ref, out_ref):
        cat = _cat_fn(po_ref[...], oa_ref[...], ob_ref[...], gg_ref[...], co_ref[...], gn_ref[...], lg_ref[...], lb_ref[...])
        out = _nn(cat.astype(BF16), w_ref[...])
        out_ref[...] = out
        zo_ref[...] = z_ref[...] + mod_ref[2:3, :] * out

    return pl.pallas_call(
        body, name="mix_out_fwd", grid=(B, S // TM),
        in_specs=[_tok(D), _tok(PW), _tok(RW), _tok(RW), _tok(RW), _tok(PW), _const((1, RW)), _const((1, PW)),
                  _const((1, PW)), _modspec(), _wspec(wout)],
        out_specs=[_tok(D), _tok(D)],
        out_shape=[jax.ShapeDtypeStruct((B, S, D), F32)] * 2,
        compiler_params=_params(None, ("arbitrary", "arbitrary")),
    )(z, po, oa, ob, gg, co, *ro, mod, wout[0])


def _mix_out_bwd(dzo, out, po, oa, ob, gg, co, ro, mod, wout):
    B, S, _ = dzo.shape

    def body(dzo_ref, out_ref, po_ref, oa_ref, ob_ref, gg_ref, co_ref, gn_ref, lg_ref, lb_ref, mod_ref, w_ref,
             dpo_ref, do_ref, dgg_ref, dco_ref, cat_ref, dout_ref, dmod_ref, dgn_ref, dlg_ref, dlb_ref):
        b, t = pl.program_id(0), pl.program_id(1)
        dzo = dzo_ref[...]
        cat, vjp = jax.vjp(_cat_fn, po_ref[...], oa_ref[...], ob_ref[...], gg_ref[...], co_ref[...], gn_ref[...],
                           lg_ref[...], lb_ref[...])
        cat_ref[...] = cat.astype(BF16)
        dout = (mod_ref[2:3, :] * dzo).astype(BF16)
        dout_ref[...] = dout
        dgate = jnp.sum(out_ref[...] * dzo, axis=0, keepdims=True)
        dcat = _nt(dout, w_ref[...])
        dpo, doa, _, dgg, dco, dgn, dlg, dlb = vjp(dcat)
        dpo_ref[...] = dpo
        do_ref[...] = doa
        dgg_ref[...] = dgg
        dco_ref[...] = dco
        zero = jnp.zeros_like(dgate)
        _acc(dmod_ref, jnp.concatenate([zero, zero, dgate], axis=0), t <= 1)
        first = jnp.logical_and(b == 0, t == 0)
        _acc(dgn_ref, dgn, first)
        _acc(dlg_ref, dlg, first)
        _acc(dlb_ref, dlb, first)

    return pl.pallas_call(
        body, name="mix_out_bwd", grid=(B, S // TM),
        in_specs=[_tok(D), _tok(D), _tok(PW), _tok(RW), _tok(RW), _tok(RW), _tok(PW), _const((1, RW)), _const((1, PW)),
                  _const((1, PW)), _modspec(), _wspec(wout)],
        out_specs=[_tok(PW), _tok(RW), _tok(RW), _tok(PW), _tok(D), _tok(D), _modspec(), _const((1, RW)),
                   _const((1, PW)), _const((1, PW))],
        out_shape=[jax.ShapeDtypeStruct((B, S, PW), F32), jax.ShapeDtypeStruct((B, S, RW), F32),
                   jax.ShapeDtypeStruct((B, S, RW), F32), jax.ShapeDtypeStruct((B, S, PW), F32),
                   jax.ShapeDtypeStruct((B, S, D), BF16), jax.ShapeDtypeStruct((B, S, D), BF16),
                   jax.ShapeDtypeStruct((B, 2, 3, D), F32), jax.ShapeDtypeStruct((1, RW), F32),
                   jax.ShapeDtypeStruct((1, PW), F32), jax.ShapeDtypeStruct((1, PW), F32)],
        compiler_params=_params(None, ("arbitrary", "arbitrary")),
    )(dzo, out, po, oa, ob, gg, co, *ro, mod, wout[0])


def _rms(z, g):
    return z * lax.rsqrt(jnp.mean(z * z, axis=-1, keepdims=True) + EPS) * g


def _head(z, target, fg):
    B, S, _ = z.shape

    def body(z_ref, t_ref, g_ref, dz_ref, dg_ref, loss_ref):
        b, t = pl.program_id(0), pl.program_id(1)
        first = jnp.logical_and(b == 0, t == 0)

        @pl.when(t == 0)
        def _():
            dz_ref[...] = jnp.zeros((TM, D), F32)

        @pl.when(first)
        def _():
            dg_ref[...] = jnp.zeros((1, D), F32)
            loss_ref[...] = jnp.zeros((8, 128), F32)

        @pl.when(t > 0)
        def _():
            y, vjp = jax.vjp(_rms, z_ref[...], g_ref[...])
            err = y - t_ref[...]
            dz, dg = vjp(err * (1.0 / D))
            dz_ref[...] = dz
            dg_ref[...] += dg
            loss_ref[...] += 0.5 * jnp.sum(err * err) * (1.0 / D)

    return pl.pallas_call(
        body, name="head", grid=(B, S // TM),
        in_specs=[_tok(D), pl.BlockSpec((None, TM, D), lambda b, t: (b, jnp.maximum(t - 1, 0), 0)), _const((1, D))],
        out_specs=[_tok(D), _const((1, D)), _const((8, 128))],
        out_shape=[jax.ShapeDtypeStruct((B, S, D), F32), jax.ShapeDtypeStruct((1, D), F32),
                   jax.ShapeDtypeStruct((8, 128), F32)],
        compiler_params=_params(None, ("arbitrary", "arbitrary")),
    )(z, target, fg)


MROWS = 24
MCOL = 768


def _silu(x):
    return x * jax.nn.sigmoid(x)


def _mod_fwd(c24, wmod, bmod):
    ncol = wmod.shape[2]

    def body(c_ref, w_ref, b_ref, o_ref):
        sc = _silu(c_ref[...]).astype(BF16)
        o_ref[...] = _nn(sc, w_ref[...].astype(BF16)) + b_ref[...]

    return pl.pallas_call(
        body, name="mod_fwd", grid=(2, ncol // MCOL),
        in_specs=[pl.BlockSpec((MROWS, D), lambda l, j: (0, 0)), pl.BlockSpec((None, D, MCOL), lambda l, j: (l, 0, j)),
                  pl.BlockSpec((None, 1, MCOL), lambda l, j: (l, 0, j))],
        out_specs=pl.BlockSpec((None, MROWS, MCOL), lambda l, j: (l, 0, j)),
        out_shape=jax.ShapeDtypeStruct((2, MROWS, ncol), F32),
        compiler_params=_params(None, ("arbitrary", "arbitrary")),
    )(c24, wmod, bmod)


def _mod_bwd(c24, dmod, wmod):
    ncol = wmod.shape[2]

    def body(c_ref, d_ref, w_ref, dw_ref, dsc_ref):
        l, j = pl.program_id(0), pl.program_id(1)
        sc = _silu(c_ref[...]).astype(BF16)
        dm = d_ref[...].astype(BF16)
        dw_ref[...] = _tn(sc, dm)
        _acc(dsc_ref, _nt(dm, w_ref[...].astype(BF16)), jnp.logical_and(l == 0, j == 0))

    return pl.pallas_call(
        body, name="mod_bwd", grid=(2, ncol // MCOL),
        in_specs=[pl.BlockSpec((MROWS, D), lambda l, j: (0, 0)), pl.BlockSpec((None, MROWS, MCOL), lambda l, j: (l, 0, j)),
                  pl.BlockSpec((None, D, MCOL), lambda l, j: (l, 0, j))],
        out_specs=[pl.BlockSpec((None, D, MCOL), lambda l, j: (l, 0, j)), pl.BlockSpec((MROWS, D), lambda l, j: (0, 0))],
        out_shape=[jax.ShapeDtypeStruct((2, D, ncol), F32), jax.ShapeDtypeStruct((MROWS, D), F32)],
        compiler_params=_params(None, ("arbitrary", "arbitrary")),
    )(c24, dmod, wmod)


def _bmod_cctx_grad(dmod_full, dsc_parts, cctx):
    def body(d_ref, p_ref, c_ref, db_ref, dc_ref):
        db_ref[...] = jnp.sum(d_ref[...], axis=1, keepdims=True)
        tot = jnp.zeros((8, D), F32)
        for s in range(N_CHIP):
            tot = tot + p_ref[s]
        x = c_ref[...]
        sg = jax.nn.sigmoid(x)
        dc_ref[...] = jnp.sum(tot, axis=0, keepdims=True) * (sg * (1.0 + x * (1.0 - sg)))

    return pl.pallas_call(
        body, name="bmod_cctx_grad",
        out_shape=[jax.ShapeDtypeStruct((2, 1, N_MOD * D), F32), jax.ShapeDtypeStruct((1, D), F32)],
    )(dmod_full, dsc_parts, cctx)


def _adam_math(w, g, m, v):
    m = ADAM_B1 * m + (1.0 - ADAM_B1) * g
    v = ADAM_B2 * v + (1.0 - ADAM_B2) * (g * g)
    m_hat = m / (1.0 - ADAM_B1 ** ADAM_STEP)
    v_hat = v / (1.0 - ADAM_B2 ** ADAM_STEP)
    delta = -ADAM_LR * (m_hat / (jnp.sqrt(v_hat) + ADAM_EPS) + ADAM_WD * w)
    return delta, m, v


def _adam(w, g, m, v):
    R, Cc = w.shape
    if R * Cc * 4 <= (1 << 20):
        RB = R
    else:
        RB = 1 << (((1 << 18) // Cc).bit_length() - 1)
        assert R % RB == 0

    def body(w_ref, g_ref, m_ref, v_ref, d_ref, mo_ref, vo_ref):
        d, mn, vn = _adam_math(w_ref[...], g_ref[...], m_ref[...], v_ref[...])
        d_ref[...] = d
        mo_ref[...] = mn
        vo_ref[...] = vn

    spec = pl.BlockSpec((RB, Cc), lambda i: (i, 0))
    return pl.pallas_call(
        body, name="adam", grid=(R // RB,), in_specs=[spec] * 4, out_specs=[spec] * 3,
        out_shape=[jax.ShapeDtypeStruct((R, Cc), F32)] * 3,
        compiler_params=_params(None, ("arbitrary",)),
    )(w, g, m, v)


def _adam_layer(w, g, m, v, layer, prev):
    shape = w.shape
    n, cols = shape[0], shape[-1]
    w3, m3, v3 = (t.reshape(n, -1, cols) for t in (w, m, v))
    g2 = g.reshape(-1, cols)
    R = g2.shape[0]
    RB = max(r for r in range(8, (1 << 18) // cols + 1, 8) if R % r == 0)

    def body(w_ref, g_ref, m_ref, v_ref, *refs):
        go_ref, d_ref, mo_ref, vo_ref = refs[-4:]
        gt = g_ref[...]
        d, mn, vn = _adam_math(w_ref[...], gt, m_ref[...], v_ref[...])
        go_ref[...] = gt
        d_ref[...] = d
        mo_ref[...] = mn
        vo_ref[...] = vn

    lay = pl.BlockSpec((None, RB, cols), lambda i: (layer, i, 0))
    flat = pl.BlockSpec((RB, cols), lambda i: (i, 0))
    hold = [] if prev is None else [t.reshape(n, -1, cols) for t in prev]
    outs = pl.pallas_call(
        body, name="adam_layer", grid=(R // RB,),
        in_specs=[lay, flat, lay, lay] + [pl.BlockSpec(memory_space=pl.ANY)] * len(hold), out_specs=[lay] * 4,
        out_shape=[jax.ShapeDtypeStruct(w3.shape, F32)] * 4,
        input_output_aliases={4 + k: k for k in range(len(hold))},
        compiler_params=_params(None, ("arbitrary",)),
    )(w3, g2, m3, v3, *hold)
    return [o.reshape(shape) for o in outs]


def _sum_devices(parts):
    K = parts.shape[1]

    def body(p_ref, o_ref):
        tot = p_ref[0]
        for i in range(1, N_DEV):
            tot = tot + p_ref[i]
        o_ref[...] = tot

    return pl.pallas_call(body, name="sum_devices", out_shape=jax.ShapeDtypeStruct((K, 128), F32))(parts)


def _sum_pieces(own, others):
    R = own.shape[0]
    RB = 96 if R % 96 == 0 else 32

    def body(a_ref, r_ref, o_ref):
        tot = a_ref[...].astype(F32)
        for i in range(N_DEV - 1):
            tot = tot + r_ref[i].astype(F32)
        o_ref[...] = tot

    return pl.pallas_call(
        body, name="sum_pieces", grid=(R // RB,),
        in_specs=[pl.BlockSpec((RB, D), lambda i: (i, 0)), pl.BlockSpec((N_DEV - 1, RB, D), lambda i: (0, i, 0))],
        out_specs=pl.BlockSpec((RB, D), lambda i: (i, 0)),
        out_shape=jax.ShapeDtypeStruct((R, D), F32),
        compiler_params=_params(None, ("arbitrary",)),
    )(own, others)


def _coords():
    return lax.axis_index("x"), lax.axis_index("y"), lax.axis_index("c")


_FLIPS = [(fx, fy, fc) for fx in (0, 1) for fy in (0, 1) for fc in (0, 1)][1:]


def _all_gather_small(buf):
    K = buf.shape[0]

    def body(in_ref, out_ref, send_sems, recv_sems, local_sem):
        x, y, c = _coords()
        me = 4 * x + 2 * y + c
        mine = pltpu.make_async_copy(in_ref, out_ref.at[me], local_sem)
        mine.start()
        sends = []
        for k, (fx, fy, fc) in enumerate(_FLIPS):
            peer = (x ^ fx, y ^ fy, c ^ fc)
            cp = pltpu.make_async_remote_copy(src_ref=in_ref, dst_ref=out_ref.at[me], send_sem=send_sems.at[k],
                                              recv_sem=recv_sems.at[k], device_id=peer, device_id_type=MESH)
            cp.start()
            sends.append(cp)
        for k, (fx, fy, fc) in enumerate(_FLIPS):
            src = 4 * (x ^ fx) + 2 * (y ^ fy) + (c ^ fc)
            pltpu.make_async_remote_copy(src_ref=in_ref, dst_ref=out_ref.at[src], send_sem=send_sems.at[k],
                                         recv_sem=recv_sems.at[k], device_id=(x, y, c), device_id_type=MESH).wait_recv()
        for cp in sends:
            cp.wait_send()
        mine.wait()

    return pl.pallas_call(
        body, name="all_gather_small",
        in_specs=[pl.BlockSpec(memory_space=pltpu.VMEM)], out_specs=pl.BlockSpec(memory_space=pltpu.VMEM),
        out_shape=jax.ShapeDtypeStruct((N_DEV, K, 128), F32),
        scratch_shapes=[pltpu.SemaphoreType.DMA((7,)), pltpu.SemaphoreType.DMA((7,)), pltpu.SemaphoreType.DMA],
        compiler_params=_params(VMEM_BIG),
    )(buf)


_CHIP_FLIPS = [(1, 0), (0, 1), (1, 1)]


_HBM = pl.BlockSpec(memory_space=pltpu.HBM)
_SEMS = pl.BlockSpec(memory_space=pltpu.SEMAPHORE)
_EFFECT = pltpu.SideEffectType.DATAFLOW_SIDE_EFFECTING


def _in_hbm(v):
    return pltpu.with_memory_space_constraint(v, pltpu.HBM)


def _copies_start(name, srcs, lands, n_sems, issue, after):
    ns, nl = len(srcs), len(lands)

    def body(*refs):
        src_refs, land_refs = refs[:ns], refs[ns:ns + nl]
        out = refs[ns + nl + 1:]
        issue(src_refs, land_refs, out[:nl], out[nl:2 * nl])
        out[-1][...] = jnp.zeros((8, 128), F32)

    outs = pl.pallas_call(
        body, name=name, in_specs=[_HBM] * (ns + nl) + [pl.BlockSpec(memory_space=pl.ANY)],
        out_specs=[_SEMS] * (2 * nl) + [_HBM] * (ns + nl) + [pl.BlockSpec(memory_space=pltpu.VMEM)],
        out_shape=[pltpu.SemaphoreType.DMA((n_sems,))] * (2 * nl) + [pltpu.HBM(v.shape, v.dtype) for v in (*srcs, *lands)]
        + [jax.ShapeDtypeStruct((8, 128), F32)],
        input_output_aliases={i: 2 * nl + i for i in range(ns + nl)},
        compiler_params=pltpu.CompilerParams(has_side_effects=_EFFECT),
    )(*[_in_hbm(v) for v in (*srcs, *lands)], after)
    return outs[:nl], outs[nl:2 * nl], outs[2 * nl:2 * nl + ns], outs[2 * nl + ns:2 * nl + ns + nl], outs[-1]


def _copies_wait(name, srcs, lands, send_sems, recv_sems, finish, after):
    after = list(after) if isinstance(after, (list, tuple)) else [after]
    ns, nl = len(srcs), len(lands)

    def body(*refs):
        src_refs, land_refs = refs[:ns], refs[ns:ns + nl]
        finish(src_refs, land_refs, refs[ns + nl:ns + 2 * nl], refs[ns + 2 * nl:ns + 3 * nl])

    outs = pl.pallas_call(
        body, name=name, in_specs=[_HBM] * (ns + nl) + [_SEMS] * (2 * nl) + [pl.BlockSpec(memory_space=pl.ANY)] * len(after),
        out_specs=[_HBM] * (ns + nl), out_shape=[pltpu.HBM(v.shape, v.dtype) for v in (*srcs, *lands)],
        input_output_aliases={i: i for i in range(ns + nl)},
        compiler_params=pltpu.CompilerParams(has_side_effects=_EFFECT),
    )(*srcs, *lands, *send_sems, *recv_sems, *after)
    return outs[ns:]


def _own_slab(land, mine, index):
    return lax.dynamic_update_slice_in_dim(land, mine[:, None], index, axis=1)


def _gather_start(units, after):
    x, y, c = _coords()
    chip = 2 * x + y
    lands = [_own_slab(lax.empty((u.shape[0], N_CHIP) + u.shape[1:], u.dtype), u, chip) for u in units]

    def issue(src_refs, land_refs, send_sems, recv_sems):
        x, y, c = _coords()
        s_me = 2 * x + y
        for i, (src, land) in enumerate(zip(src_refs, land_refs)):
            for j, (fx, fy) in enumerate(_CHIP_FLIPS):
                pltpu.make_async_remote_copy(src_ref=src, dst_ref=land.at[:, s_me], send_sem=send_sems[i].at[j],
                                             recv_sem=recv_sems[i].at[j], device_id=(x ^ fx, y ^ fy, c),
                                             device_id_type=MESH).start()

    return _copies_start("gather_start", units, lands, 3, issue, after)


def _gather_wait(tag, started, which, after):
    send_sems, recv_sems, srcs, lands, _ = started

    def finish(src_refs, land_refs, ssems, rsems):
        x, y, c = _coords()
        for src, land, ss, rs in zip(src_refs, land_refs, ssems, rsems):
            for j in range(3):
                cp = pltpu.make_async_remote_copy(src_ref=src, dst_ref=land.at[:, 0], send_sem=ss.at[j], recv_sem=rs.at[j],
                                                  device_id=(x, y, c), device_id_type=MESH)
                cp.wait_send()
                cp.wait_recv()

    done = _copies_wait("gather_wait_" + tag, [srcs[i] for i in which], [lands[i] for i in which], [send_sems[i] for i in which],
                        [recv_sems[i] for i in which], finish, after)
    return [d.reshape(d.shape[0], N_CHIP * d.shape[2], D) for d in done]


def _piece_rows(grads):
    return [g.shape[2] for g in grads]


def _scatter_start(tag, grads, after):
    rows = _piece_rows(grads)
    land = lax.empty((N_DEV - 1, sum(rows), D), grads[0].dtype)

    def issue(src_refs, land_refs, send_sems, recv_sems):
        x, y, c = _coords()
        for k, (fx, fy, fc) in enumerate(_FLIPS):
            px, py, pc = x ^ fx, y ^ fy, c ^ fc
            off = 0
            for src, n in zip(src_refs, rows):
                pltpu.make_async_remote_copy(src_ref=src.at[2 * px + py, pc], dst_ref=land_refs[0].at[k, pl.ds(off, n)],
                                             send_sem=send_sems[0].at[k], recv_sem=recv_sems[0].at[k],
                                             device_id=(px, py, pc), device_id_type=MESH).start()
                off += n

    return _copies_start("scatter_start_" + tag, grads, [land], N_DEV - 1, issue, after)


def _scatter_wait(tag, started, after):
    send_sems, recv_sems, srcs, lands, _ = started

    def finish(src_refs, land_refs, ssems, rsems):
        x, y, c = _coords()
        for k in range(N_DEV - 1):
            cp = pltpu.make_async_remote_copy(src_ref=land_refs[0].at[0], dst_ref=land_refs[0].at[0], send_sem=ssems[0].at[k],
                                              recv_sem=rsems[0].at[k], device_id=(x, y, c), device_id_type=MESH)
            cp.wait_send()
            cp.wait_recv()

    return _copies_wait("scatter_wait_" + tag, srcs, lands, send_sems, recv_sems, finish, after)[0]


def _swap_start(tag, mine, rows, after):
    x, y, c = _coords()
    offs = [sum(rows[:t]) for t in range(len(rows))]
    lands = [lax.dynamic_update_slice_in_dim(lax.empty((2, n, D), mine.dtype), mine[o:o + n][None], c, axis=0)
             for o, n in zip(offs, rows)]

    def issue(src_refs, land_refs, send_sems, recv_sems):
        x, y, c = _coords()
        for t, (o, n) in enumerate(zip(offs, rows)):
            pltpu.make_async_remote_copy(src_ref=src_refs[0].at[pl.ds(o, n)], dst_ref=land_refs[t].at[c],
                                         send_sem=send_sems[t].at[0], recv_sem=recv_sems[t].at[0],
                                         device_id=(x, y, 1 - c), device_id_type=MESH).start()

    return _copies_start("swap_start_" + tag, [mine], lands, 1, issue, after)


def _swap_wait(tag, started, after):
    send_sems, recv_sems, srcs, lands, _ = started

    def finish(src_refs, land_refs, ssems, rsems):
        x, y, c = _coords()
        for land, ss, rs in zip(land_refs, ssems, rsems):
            cp = pltpu.make_async_remote_copy(src_ref=land.at[0], dst_ref=land.at[0], send_sem=ss.at[0], recv_sem=rs.at[0],
                                              device_id=(x, y, c), device_id_type=MESH)
            cp.wait_send()
            cp.wait_recv()

    return _copies_wait("swap_wait_" + tag, srcs, lands, send_sems, recv_sems, finish, after)


def _pack(arrays):
    flat = jnp.concatenate([a.reshape(-1).astype(F32) for a in arrays])
    pad = (-flat.shape[0]) % 1024
    return jnp.pad(flat, (0, pad)).reshape(-1, 128)


def _unpack(buf, shapes):
    flat = buf.reshape(-1)
    out, off = [], 0
    for s in shapes:
        n = 1
        for d in s:
            n *= d
        out.append(flat[off:off + n].reshape(s))
        off += n
    return out


def _block_diag(pw):
    bd = jnp.zeros((PW, PW), F32)
    g = PW // 4
    for i in range(4):
        bd = bd.at[i * g:(i + 1) * g, i * g:(i + 1) * g].set(pw[i])
    return bd


def _lanes(v):
    return jnp.broadcast_to(v.reshape(NH, 1, 1), (NH, 1, HD))


def _layer_fwd(z, mod, normg, wget, small):
    S = z.shape[1]
    cos, sin = _rope_tables(S)
    fwd_sched, _ = _schedules(S)
    wa = wget("a", z)
    z1, f_a = _ffn_fwd(z, mod[:, :, 0], normg[0], *wa)
    wm = wget("m", z1)
    pp, q, k, v, gg, pc = _mix_in_fwd(z1, mod[:, :, 1], normg[1], wm[0], cos, sin)
    po, co = _pool_conv_fwd(pp, pc, small["bd"], small["pscale"], small["dw"], small["db"])
    ro = (small["gng"], small["lng"], small["lnb"])
    oa, ob = _retention(q, k, v, small["dec_f"], small["dec_b"], *fwd_sched)
    z2, out = _mix_out_fwd(z1, po, oa, ob, gg, co, ro, mod[:, :, 1], wm[1])
    wb = wget("b", z2)
    z3, f_b = _ffn_fwd(z2, mod[:, :, 2], normg[2], *wb)
    saved = dict(z=z, f_a=f_a, z1=z1, pp=pp, q=q, k=k, v=v, gg=gg, pc=pc, po=po, co=co, oa=oa, ob=ob, out=out, z2=z2, f_b=f_b,
                 wa=wa, wm=wm, wb=wb)
    return z3, saved


def _layer_bwd(dz3, sv, mod, normg, small, emit, tok):
    S = dz3.shape[1]
    B = dz3.shape[0]
    T = B * S
    cos, sin = _rope_tables(S)
    fwd_sched, bwd_sched = _schedules(S)
    wa, wm, wb = sv["wa"], sv["wm"], sv["wb"]
    dz2, dmod_b, dg_b, gw1t_b, gw3t_b, gw2_b = _ffn_bwd(sv["z2"], dz3, sv["f_b"], mod[:, :, 2] + tok, normg[2], *wb)
    tok = emit("b", [gw1t_b, gw3t_b, gw2_b])
    mod_m = mod[:, :, 1] + tok
    ro = (small["gng"], small["lng"], small["lnb"])
    dpo, do, dgg, dco, cat, dout, dmod_gate, dgng, dlng, dlnb = _mix_out_bwd(
        dz2, sv["out"], sv["po"], sv["oa"], sv["ob"], sv["gg"], sv["co"], ro, mod_m, wm[1])
    gwout = _tn_matmul(cat.reshape(T, D), dout.reshape(T, D))
    dqa, dqb = _retention(do, sv["v"], sv["k"], small["dec_f"], small["dec_b"], *fwd_sched)
    dka, dkb = _retention(sv["v"], do, sv["q"], small["dec_f"], small["dec_b"], *bwd_sched)
    dva, dvb = _retention(sv["k"], sv["q"], do, small["dec_f"], small["dec_b"], *bwd_sched)
    ddec = _retention_ddecay(sv["q"], sv["k"], sv["v"], do, small["dec_f"], small["dec_b"], *fwd_sched)
    dq, dk = _rope_bwd(dqa, dqb, dka, dkb, cos, sin)
    dpp, dpc, dbd, dps, ddw, ddb = _pool_conv_bwd(sv["pp"], sv["pc"], dpo, dco, small["bd"], small["pscale"],
                                                   small["dw"], small["db"])
    dz1, h, dp, dmod_m, dg_m = _mix_in_bwd(sv["z1"], dz2, dpp, dq, dk, dva, dvb, dgg, dpc, mod_m, normg[1], wm[0])
    gwint = _tn_matmul(dp.reshape(T, F), h.reshape(T, D))
    tok = emit("m", [gwint, gwout])
    dz, dmod_a, dg_a, gw1t_a, gw3t_a, gw2_a = _ffn_bwd(sv["z"], dz1, sv["f_a"], mod[:, :, 0] + tok, normg[0], *wa)
    tok = emit("a", [gw1t_a, gw3t_a, gw2_a])
    dmod = jnp.stack([dmod_a, dmod_m + dmod_gate, dmod_b], axis=2)
    dnormg = jnp.stack([dg_a, dg_m, dg_b], axis=0)
    g = PW // 4
    dpool_w = jnp.stack([dbd[i * g:(i + 1) * g, i * g:(i + 1) * g] for i in range(4)], axis=0)
    sm = dict(pool_w=dpool_w, pool_scale=dps[0], dec_f=ddec[:, 0, 0], dec_b=ddec[:, 1, 0], gng=dgng[0],
              conv_dw=ddw[0:CONV_K], conv_b=ddb[0], conv_ln_g=dlng[0], conv_ln_b=dlnb[0])
    return dz, dmod, dnormg, sm, tok


def _small_params(pool_w, pool_scale, dec_f, dec_b, gng, conv_dw, conv_b, lng, lnb):
    return dict(bd=_block_diag(pool_w), pscale=pool_scale.reshape(1, PW), dec_f=_lanes(dec_f), dec_b=_lanes(dec_b),
                gng=gng.reshape(1, RW), dw=jnp.pad(conv_dw, ((0, 1), (0, 0))), db=conv_b.reshape(1, PW),
                lng=lng.reshape(1, PW), lnb=lnb.reshape(1, PW))


_WEIGHTS = ["c_ctx", "w_mod", "b_mod", "norm_g", "ffn_w1", "ffn_w3", "ffn_w2", "w_in", "w_out", "pool_w", "pool_scale",
            "ret_decay_fwd", "ret_decay_bwd", "ret_gn_g", "conv_dw", "conv_b", "conv_ln_g", "conv_ln_b", "final_g"]
_BIG = ["w_mod", "ffn_w1", "ffn_w3", "ffn_w2", "w_in", "w_out"]
_SMALL = [n for n in _WEIGHTS if n not in _BIG]


def _adam_any(w, g, m, v):
    shape = w.shape
    cols = shape[-1] if w.ndim >= 2 else 128
    outs = _adam(w.reshape(-1, cols), g.reshape(-1, cols), m.reshape(-1, cols), v.reshape(-1, cols))
    return [o.reshape(shape) for o in outs]


def _step(a):
    x, c, ctx = a["x"], a["c"], a["ctx"]
    B = x.shape[0]
    nex = N_DEV * B
    assert nex + B <= MROWS and ctx.shape[1] == LC and x.shape[1] % TM == 0
    xi, yi, ci = _coords()
    me = 4 * xi + 2 * yi + ci
    chip = 2 * xi + yi
    ncol = a["w_mod"].shape[2]

    def t_bf16(w):
        return jnp.swapaxes(w, -1, -2).astype(BF16)

    w1t, w3t, w2 = t_bf16(a["ffn_w1"]), t_bf16(a["ffn_w3"]), a["ffn_w2"].astype(BF16)
    wint, wout = t_bf16(a["w_in"]), a["w_out"].astype(BF16)
    units = []
    for l in range(2):
        units += [jnp.stack([w1t[l, 0], w3t[l, 0], w2[l, 0]]), wint[l][None], wout[l][None],
                  jnp.stack([w1t[l, 1], w3t[l, 1], w2[l, 1]])]

    shapes1 = [(B, D), (2, 3, D // N_CHIP), (2, CONV_K, PW // N_CHIP)]
    g1 = _all_gather_small(_pack([c, a["norm_g"], a["conv_dw"]]))
    per = [_unpack(g1[d], shapes1) for d in range(N_DEV)]
    c_all = jnp.concatenate([per[d][0] for d in range(N_DEV)], axis=0)
    norm_g_full = jnp.concatenate([per[2 * s][1] for s in range(N_CHIP)], axis=-1)
    conv_dw_full = jnp.concatenate([per[2 * s][2] for s in range(N_CHIP)], axis=-1)
    cctx = a["c_ctx"].reshape(1, D)
    c24 = jnp.concatenate([c_all] + [cctx] * B + [jnp.zeros((MROWS - nex - B, D), F32)], axis=0)

    bsh = lax.dynamic_slice(a["b_mod"], (0, chip * ncol), (2, ncol)).reshape(2, 1, ncol)
    mod_raw = _mod_fwd(c24, a["w_mod"], bsh)
    g2 = _all_gather_small(_pack([mod_raw]))
    mod_full = jnp.concatenate([_unpack(g2[2 * s], [(2, MROWS, ncol)])[0] for s in range(N_CHIP)], axis=-1)
    mods = []
    for l in range(2):
        lat = lax.dynamic_slice(mod_full[l], (B * me, 0), (B, N_MOD * D))
        cx = jnp.broadcast_to(mod_full[l, nex][None], (B, N_MOD * D))
        mods.append(jnp.stack([cx, lat], axis=1).reshape(B, 2, 3, 3, D))

    started = _gather_start(units, mod_full)

    def wget_of(l):
        def wget(stage, after):
            if stage == "m":
                win, wo = _gather_wait(f"m{l}", started, [4 * l + 1, 4 * l + 2], after)
                return (win, 0), (wo, 0)
            (g,) = _gather_wait(f"{stage}{l}", started, [4 * l + (0 if stage == "a" else 3)], after)
            return (g, 0), (g, 1), (g, 2)
        return wget

    smalls = [_small_params(a["pool_w"][l], a["pool_scale"][l], a["ret_decay_fwd"][l], a["ret_decay_bwd"][l],
                            a["ret_gn_g"][l], conv_dw_full[l], a["conv_b"][l], a["conv_ln_g"][l], a["conv_ln_b"][l])
              for l in range(2)]
    normgs = [norm_g_full[l].reshape(3, 1, D) for l in range(2)]
    z = jnp.concatenate([ctx, x], axis=1)
    saved = []
    for l in range(2):
        z, sv = _layer_fwd(z, mods[l], normgs[l], wget_of(l), smalls[l])
        saved.append(sv)
    dz, dfinal_g, loss_part = _head(z, a["loss_target"], a["final_g"].reshape(1, D))

    scattering, swapping, reduced = [], [], {}

    def reduce_previous(after):
        tag, st, own, rows = scattering.pop()
        mine = _sum_pieces(own, _scatter_wait(tag, st, after))
        sw = _swap_start(tag, mine, rows, mine)
        if swapping:
            ptag, psw = swapping.pop()
            reduced[ptag] = _swap_wait(ptag, psw, sw[4])
        swapping.append((tag, sw))
        return sw[4]

    def emit_of(l):
        def emit(stage, grads):
            grads = [g.reshape(N_CHIP, 2, g.shape[0] // (2 * N_CHIP), D) for g in grads]
            st = _scatter_start(f"{stage}{l}", grads, grads[0])
            tok = st[4][0, 0]
            own = jnp.concatenate([lax.dynamic_slice(g, (chip, ci, 0, 0), (1, 1) + g.shape[2:]).reshape(g.shape[2:])
                                   for g in grads], axis=0)
            if scattering:
                tok = tok + reduce_previous(st[4])[0, 0]
            scattering.append((f"{stage}{l}", st, own, _piece_rows(grads)))
            return tok
        return emit

    back = [None, None]
    tok = jnp.zeros((), F32)
    for l in (1, 0):
        dz, dmod, dnormg, sm, tok = _layer_bwd(dz, saved[l], mods[l], normgs[l], smalls[l], emit_of(l), tok)
        back[l] = (dmod, dnormg, None, sm)
    grad_x = dz[:, LC:]
    grads = {}

    dmods = [back[l][0].reshape(B, 2, N_MOD * D) for l in range(2)]
    pack_a = _pack([jnp.stack([dm[:, 1] for dm in dmods])])
    ka = pack_a.shape[0]
    sm = [back[l][3] for l in range(2)]
    sum_list = [jnp.stack([dm[:, 0] for dm in dmods]), jnp.stack([back[l][1][:, 0] for l in range(2)])]
    sm_keys = ["pool_w", "pool_scale", "dec_f", "dec_b", "gng", "conv_dw", "conv_b", "conv_ln_g", "conv_ln_b"]
    sum_list += [jnp.stack([sm[l][k] for l in range(2)]) for k in sm_keys]
    sum_list += [dfinal_g[0], loss_part[0, 0:1]]
    sum_shapes = [s.shape for s in sum_list]
    g3 = _all_gather_small(jnp.concatenate([pack_a, _pack(sum_list)], axis=0))
    dmx_all = jnp.concatenate([_unpack(g3[d, :ka], [(2, B, N_MOD * D)])[0] for d in range(N_DEV)], axis=1)
    summed = _unpack(_sum_devices(g3[:, ka:]), sum_shapes)
    dmy, dnorm_full = summed[0], summed[1]
    sgrad = dict(zip(sm_keys, summed[2:2 + len(sm_keys)]))
    loss = summed[-1].reshape(())

    dmod24 = jnp.concatenate([dmx_all, dmy, jnp.zeros((2, MROWS - nex - B, N_MOD * D), F32)], axis=1)
    dmod_my = lax.dynamic_slice(dmod24, (0, 0, chip * ncol), (2, MROWS, ncol))
    grads["w_mod"], dsc = _mod_bwd(c24, dmod_my, a["w_mod"])
    g4 = _all_gather_small(_pack([dsc[nex:nex + 8]]))
    dsc_parts = jnp.stack([_unpack(g4[2 * s], [(8, D)])[0] for s in range(N_CHIP)])
    dbmod, dcctx = _bmod_cctx_grad(dmod24, dsc_parts, cctx)

    grads["c_ctx"] = dcctx[0]
    grads["b_mod"] = dbmod.reshape(2, N_MOD * D)
    grads["norm_g"] = lax.dynamic_slice(dnorm_full, (0, 0, chip * (D // N_CHIP)), (2, 3, D // N_CHIP))
    grads["pool_w"] = sgrad["pool_w"]
    grads["pool_scale"] = sgrad["pool_scale"]
    grads["ret_decay_fwd"] = sgrad["dec_f"]
    grads["ret_decay_bwd"] = sgrad["dec_b"]
    grads["ret_gn_g"] = sgrad["gng"]
    grads["conv_dw"] = lax.dynamic_slice(sgrad["conv_dw"], (0, 0, chip * (PW // N_CHIP)), (2, CONV_K, PW // N_CHIP))
    grads["conv_b"] = sgrad["conv_b"]
    grads["conv_ln_g"] = sgrad["conv_ln_g"]
    grads["conv_ln_b"] = sgrad["conv_ln_b"]
    grads["final_g"] = summed[-2]

    delta, new_m, new_v = {}, {}, {}
    delta["w_mod"], new_m["w_mod"], new_v["w_mod"] = _adam_any(a["w_mod"], grads["w_mod"], a["m_w_mod"], a["v_w_mod"])
    shapes_s = [a[n].shape for n in _SMALL]
    packed = _adam(_pack([a[n] for n in _SMALL]), _pack([grads[n] for n in _SMALL]),
                   _pack([a["m_" + n] for n in _SMALL]), _pack([a["v_" + n] for n in _SMALL]))
    for res, out in zip(packed, (delta, new_m, new_v)):
        for n, val in zip(_SMALL, _unpack(res, shapes_s)):
            out[n] = val

    def layer_grads(l):
        ffn = [[h.reshape(-1, D) for h in reduced[f"{stage}{l}"]] for stage in "ab"]
        win, wo = [h.reshape(-1, D) for h in reduced[f"m{l}"]]
        return dict(ffn_w1=jnp.stack([ffn[i][0].T for i in range(2)]), ffn_w3=jnp.stack([ffn[i][1].T for i in range(2)]),
                    ffn_w2=jnp.stack([ffn[i][2] for i in range(2)]), w_in=win.T, w_out=wo)

    g1 = layer_grads(1)
    half = {n: _adam_layer(a[n], g1[n], a["m_" + n], a["v_" + n], 1, None) for n in _BIG[1:]}

    last = reduce_previous([delta["w_mod"], packed[0]] + [half[n][1] for n in _BIG[1:]])
    tag, sw = swapping.pop()
    reduced[tag] = _swap_wait(tag, sw, last)
    g0 = layer_grads(0)
    for n in _BIG[1:]:
        grads[n], delta[n], new_m[n], new_v[n] = _adam_layer(a[n], g0[n], a["m_" + n], a["v_" + n], 0, half[n])
    return (loss, grad_x, *[grads[n] for n in _WEIGHTS], *[delta[n] for n in _WEIGHTS],
            *[new_m[n] for n in _WEIGHTS], *[new_v[n] for n in _WEIGHTS])


def kernel(x, c, ctx, c_ctx, w_mod, b_mod, norm_g, ffn_w1, ffn_w3, ffn_w2, w_in, w_out, pool_w, pool_scale, ret_decay_fwd, ret_decay_bwd, ret_gn_g, conv_dw, conv_b, conv_ln_g, conv_ln_b, final_g, loss_target, m_c_ctx, m_w_mod, m_b_mod, m_norm_g, m_ffn_w1, m_ffn_w3, m_ffn_w2, m_w_in, m_w_out, m_pool_w, m_pool_scale, m_ret_decay_fwd, m_ret_decay_bwd, m_ret_gn_g, m_conv_dw, m_conv_b, m_conv_ln_g, m_conv_ln_b, m_final_g, v_c_ctx, v_w_mod, v_b_mod, v_norm_g, v_ffn_w1, v_ffn_w3, v_ffn_w2, v_w_in, v_w_out, v_pool_w, v_pool_scale, v_ret_decay_fwd, v_ret_decay_bwd, v_ret_gn_g, v_conv_dw, v_conv_b, v_conv_ln_g, v_conv_ln_b, v_final_g):
    return _step(dict(locals()))
```

```python
import functools

import jax
import jax.numpy as jnp
from jax import lax
from jax.experimental import pallas as pl
from jax.experimental.pallas import tpu as pltpu

F32 = jnp.float32
BF16 = jnp.bfloat16

D = 1024
F = 2816
FH = 1408
N_MOD = 9
LC = 256
TM = 256
HD = 128
NH = 4
RW = 512
PW = 256
CONV_K = 31
GRID_W = 64
EPS = 1e-6
K_SCALE = HD ** -0.5
N_DEV = 8
N_CHIP = 4
SLAB = F // N_CHIP
HSLAB = SLAB // 2
OSLAB = D // N_CHIP
HOSLAB = OSLAB // 2
VMEM_BIG = 60 * 1024 * 1024
MESH = pl.DeviceIdType.MESH

ADAM_LR = 0.001
ADAM_B1 = 0.9
ADAM_B2 = 0.999
ADAM_EPS = 1e-08
ADAM_WD = 0.01
ADAM_STEP = 10


def _nt(a, b):
    return lax.dot_general(a, b, (((1,), (1,)), ((), ())), preferred_element_type=F32)


def _nn(a, b):
    return lax.dot_general(a, b, (((1,), (0,)), ((), ())), preferred_element_type=F32)


def _tn(a, b):
    return lax.dot_general(a, b, (((0,), (0,)), ((), ())), preferred_element_type=F32)


def _params(vmem=None, sem=None):
    return pltpu.CompilerParams(dimension_semantics=sem, vmem_limit_bytes=vmem)


def _rms_mod(z, g, shift, scale):
    y = z * lax.rsqrt(jnp.mean(z * z, axis=-1, keepdims=True) + EPS)
    return (y * g) * (1.0 + scale) + shift


def _acc(ref, val, first):
    @pl.when(first)
    def _():
        ref[...] = val

    @pl.when(jnp.logical_not(first))
    def _():
        ref[...] += val


def _tok(width):
    return pl.BlockSpec((None, TM, width), lambda b, t: (b, t, 0))


def _modspec():
    return pl.BlockSpec((None, None, 3, D), lambda b, t: (b, jnp.minimum(t, 1), 0, 0))


def _const(shape):
    nd = len(shape)
    return pl.BlockSpec(shape, lambda b, t: (0,) * nd)


def _wspec(w):
    stack, idx = w
    return pl.BlockSpec((None,) + stack.shape[1:], lambda b, t: (idx, 0, 0), pipeline_mode=pl.Buffered(1))


def _ffn_fwd(z, mod, g, w1t, w3t, w2):
    B, S, _ = z.shape

    def body(z_ref, mod_ref, g_ref, w1_ref, w3_ref, w2_ref, zo_ref, f_ref):
        zt = z_ref[...]
        h = _rms_mod(zt, g_ref[...], mod_ref[0:1, :], mod_ref[1:2, :]).astype(BF16)
        f = jnp.zeros((TM, D), F32)
        for c in range(F // FH):
            rows = slice(c * FH, (c + 1) * FH)
            u1 = _nt(h, w1_ref[rows, :])
            u3 = _nt(h, w3_ref[rows, :])
            a = (u1 * jax.nn.sigmoid(u1) * u3).astype(BF16)
            f = f + _nn(a, w2_ref[rows, :])
        f_ref[...] = f
        zo_ref[...] = zt + 0.5 * mod_ref[2:3, :] * f

    return pl.pallas_call(
        body, name="ffn_fwd", grid=(B, S // TM),
        in_specs=[_tok(D), _modspec(), _const((1, D)), _wspec(w1t), _wspec(w3t), _wspec(w2)],
        out_specs=[_tok(D), _tok(D)],
        out_shape=[jax.ShapeDtypeStruct((B, S, D), F32)] * 2,
        compiler_params=_params(VMEM_BIG, ("arbitrary", "arbitrary")),
    )(z, mod, g, w1t[0], w3t[0], w2[0])


def _ffn_bwd(z, dzo, f, mod, g, w1t, w3t, w2):
    B, S, _ = z.shape

    def body(z_ref, dzo_ref, f_ref, mod_ref, g_ref, w1_ref, w3_ref, w2_ref,
             dz_ref, h_ref, du1_ref, du3_ref, a_ref, do_ref, dmod_ref, dg_ref):
        b, t = pl.program_id(0), pl.program_id(1)
        zt = z_ref[...]
        dzo = dzo_ref[...]
        gate = mod_ref[2:3, :]
        h32, vjp_h = jax.vjp(_rms_mod, zt, g_ref[...], mod_ref[0:1, :], mod_ref[1:2, :])
        h = h32.astype(BF16)
        h_ref[...] = h
        do = (0.5 * gate * dzo).astype(BF16)
        do_ref[...] = do
        dgate = jnp.sum(0.5 * f_ref[...] * dzo, axis=0, keepdims=True)
        dh = jnp.zeros((TM, D), F32)
        for c in range(F // FH):
            rows = slice(c * FH, (c + 1) * FH)
            u1 = _nt(h, w1_ref[rows, :])
            u3 = _nt(h, w3_ref[rows, :])
            sg = jax.nn.sigmoid(u1)
            s = u1 * sg
            a_ref[:, rows] = (s * u3).astype(BF16)
            da = _nt(do, w2_ref[rows, :])
            du3 = (da * s).astype(BF16)
            du1 = (da * u3 * (sg * (1.0 + u1 * (1.0 - sg)))).astype(BF16)
            du1_ref[:, rows] = du1
            du3_ref[:, rows] = du3
            dh = dh + _nn(du1, w1_ref[rows, :]) + _nn(du3, w3_ref[rows, :])
        dz_h, dg, dshift, dscale = vjp_h(dh)
        dz_ref[...] = dzo + dz_h
        _acc(dmod_ref, jnp.concatenate([dshift, dscale, dgate], axis=0), t <= 1)
        _acc(dg_ref, dg, jnp.logical_and(b == 0, t == 0))

    T = B * S
    outs = pl.pallas_call(
        body, name="ffn_bwd", grid=(B, S // TM),
        in_specs=[_tok(D), _tok(D), _tok(D), _modspec(), _const((1, D)), _wspec(w1t), _wspec(w3t), _wspec(w2)],
        out_specs=[_tok(D), _tok(D), _tok(F), _tok(F), _tok(F), _tok(D), _modspec(), _const((1, D))],
        out_shape=[jax.ShapeDtypeStruct((B, S, D), F32), jax.ShapeDtypeStruct((B, S, D), BF16),
                   jax.ShapeDtypeStruct((B, S, F), BF16), jax.ShapeDtypeStruct((B, S, F), BF16),
                   jax.ShapeDtypeStruct((B, S, F), BF16), jax.ShapeDtypeStruct((B, S, D), BF16),
                   jax.ShapeDtypeStruct((B, 2, 3, D), F32), jax.ShapeDtypeStruct((1, D), F32)],
        compiler_params=_params(VMEM_BIG, ("arbitrary", "arbitrary")),
    )(z, dzo, f, mod, g, w1t[0], w3t[0], w2[0])
    dz, h, du1, du3, a, do, dmod, dg = outs
    gw1t = _tn_matmul(du1.reshape(T, F), h.reshape(T, D))
    gw3t = _tn_matmul(du3.reshape(T, F), h.reshape(T, D))
    gw2 = _tn_matmul(a.reshape(T, F), do.reshape(T, D))
    return dz, dmod, dg, gw1t, gw3t, gw2


def _tn_matmul(a, b):
    T, M = a.shape
    N = b.shape[1]
    MB = FH if M > FH else M
    TT = next(t for t in (1152, 1024, 768, 512, TM) if T % t == 0)
    nt = T // TT

    def body(a_ref, b_ref, o_ref, acc_ref):
        t = pl.program_id(1)
        prod = _tn(a_ref[...], b_ref[...])
        _acc(acc_ref, prod, t == 0)

        @pl.when(t == nt - 1)
        def _():
            o_ref[...] = acc_ref[...].astype(BF16)

    return pl.pallas_call(
        body, name="tn_matmul", grid=(M // MB, nt),
        in_specs=[pl.BlockSpec((TT, MB), lambda i, t: (t, i)), pl.BlockSpec((TT, N), lambda i, t: (t, 0))],
        out_specs=pl.BlockSpec((MB, N), lambda i, t: (i, 0)),
        out_shape=jax.ShapeDtypeStruct((M, N), BF16),
        scratch_shapes=[pltpu.VMEM((MB, N), F32)],
        compiler_params=_params(VMEM_BIG, ("arbitrary", "arbitrary")),
    )(a, b)


def _swap32(x):
    n = x.shape[1]
    lane = lax.broadcasted_iota(jnp.int32, x.shape, 1)
    return jnp.where((lane % 64) < 32, pltpu.roll(x, n - 32, 1), pltpu.roll(x, 32, 1))


def _rope(x, cos, sin):
    return x * cos + _swap32(x) * sin


def _rope_t(dy, cos, sin):
    return dy * cos + _swap32(dy * sin)


def _rope_tables(S):
    L = S - LC
    n_freq = HD // 4
    inv = 10000.0 ** (-jnp.arange(n_freq, dtype=F32) / n_freq)
    i = jnp.arange(L)
    row = (i // GRID_W).astype(F32)
    col = (i % GRID_W).astype(F32)
    ang_r = row[:, None] * inv[None]
    ang_c = col[:, None] * inv[None]
    ang = jnp.concatenate([ang_r, ang_r, ang_c, ang_c], axis=1)
    ang = jnp.concatenate([jnp.zeros((LC, HD), F32), ang], axis=0)
    sign = jnp.where((jnp.arange(HD) % 64) < 32, -1.0, 1.0).astype(F32)
    return jnp.cos(ang), jnp.sin(ang) * sign[None]


def _tabspec():
    return pl.BlockSpec((TM, HD), lambda b, t: (t, 0))


def _mix_in_fwd(z, mod, g, wint, cos, sin):
    B, S, _ = z.shape

    def body(z_ref, mod_ref, g_ref, w_ref, cos_ref, sin_ref, pp_ref, q_ref, k_ref, v_ref, gg_ref, pc_ref):
        h = _rms_mod(z_ref[...], g_ref[...], mod_ref[0:1, :], mod_ref[1:2, :]).astype(BF16)
        p = _nt(h, w_ref[...])
        cos = jnp.tile(cos_ref[...], (1, NH))
        sin = jnp.tile(sin_ref[...], (1, NH))
        pp_ref[...] = p[:, 0:PW]
        q_ref[...] = _rope(p[:, PW:PW + RW], cos, sin)
        k_ref[...] = _rope(p[:, PW + RW:PW + 2 * RW], cos, sin) * K_SCALE
        v_ref[...] = p[:, PW + 2 * RW:PW + 3 * RW]
        gg_ref[...] = p[:, PW + 3 * RW:PW + 4 * RW]
        pc_ref[...] = p[:, PW + 4 * RW:]

    return pl.pallas_call(
        body, name="mix_in_fwd", grid=(B, S // TM),
        in_specs=[_tok(D), _modspec(), _const((1, D)), _wspec(wint), _tabspec(), _tabspec()],
        out_specs=[_tok(PW), _tok(RW), _tok(RW), _tok(RW), _tok(RW), _tok(2 * PW)],
        out_shape=[jax.ShapeDtypeStruct((B, S, PW), F32)] + [jax.ShapeDtypeStruct((B, S, RW), F32)] * 5,
        compiler_params=_params(VMEM_BIG, ("arbitrary", "arbitrary")),
    )(z, mod, g, wint[0], cos, sin)


def _mix_in_bwd(z, dzo, dpp, dq, dk, dva, dvb, dgg, dpc, mod, g, wint):
    B, S, _ = z.shape

    def body(z_ref, dzo_ref, dpp_ref, dq_ref, dk_ref, dva_ref, dvb_ref, dgg_ref, dpc_ref, mod_ref, g_ref, w_ref,
             dz_ref, h_ref, dp_ref, dmod_ref, dg_ref):
        b, t = pl.program_id(0), pl.program_id(1)
        h32, vjp_h = jax.vjp(_rms_mod, z_ref[...], g_ref[...], mod_ref[0:1, :], mod_ref[1:2, :])
        h_ref[...] = h32.astype(BF16)
        dp = jnp.concatenate([dpp_ref[...], dq_ref[...], dk_ref[...], dva_ref[...] + dvb_ref[...], dgg_ref[...],
                              dpc_ref[...]], axis=1).astype(BF16)
        dp_ref[...] = dp
        dh = _nn(dp, w_ref[...])
        dz_h, dg, dshift, dscale = vjp_h(dh)
        dz_ref[...] = dzo_ref[...] + dz_h
        _acc(dmod_ref, jnp.concatenate([dshift, dscale, jnp.zeros_like(dshift)], axis=0), t <= 1)
        _acc(dg_ref, dg, jnp.logical_and(b == 0, t == 0))

    return pl.pallas_call(
        body, name="mix_in_bwd", grid=(B, S // TM),
        in_specs=[_tok(D), _tok(D), _tok(PW), _tok(RW), _tok(RW), _tok(RW), _tok(RW), _tok(RW), _tok(2 * PW),
                  _modspec(), _const((1, D)), _wspec(wint)],
        out_specs=[_tok(D), _tok(D), _tok(F), _modspec(), _const((1, D))],
        out_shape=[jax.ShapeDtypeStruct((B, S, D), F32), jax.ShapeDtypeStruct((B, S, D), BF16),
                   jax.ShapeDtypeStruct((B, S, F), BF16), jax.ShapeDtypeStruct((B, 2, 3, D), F32),
                   jax.ShapeDtypeStruct((1, D), F32)],
        compiler_params=_params(VMEM_BIG, ("arbitrary", "arbitrary")),
    )(z, dzo, dpp, dq, dk, dva, dvb, dgg, dpc, mod, g, wint[0])


def _rope_bwd(dqa, dqb, dka, dkb, cos, sin):
    B, S, _ = dqa.shape

    def body(dqa_ref, dqb_ref, dka_ref, dkb_ref, cos_ref, sin_ref, dq_ref, dk_ref):
        cos = jnp.tile(cos_ref[...], (1, NH))
        sin = jnp.tile(sin_ref[...], (1, NH))
        dq_ref[...] = _rope_t(dqa_ref[...] + dqb_ref[...], cos, sin)
        dk_ref[...] = _rope_t(dka_ref[...] + dkb_ref[...], cos, sin) * K_SCALE

    return pl.pallas_call(
        body, name="rope_bwd", grid=(B, S // TM),
        in_specs=[_tok(RW)] * 4 + [_tabspec(), _tabspec()],
        out_specs=[_tok(RW), _tok(RW)],
        out_shape=[jax.ShapeDtypeStruct((B, S, RW), F32)] * 2,
        compiler_params=_params(None, ("arbitrary", "arbitrary")),
    )(dqa, dqb, dka, dkb, cos, sin)


def _log_sigmoid(x):
    return jnp.minimum(x, 0.0) - jnp.log(1.0 + jnp.exp(-jnp.abs(x)))


def _retention(a, b, c, dec_a, dec_b, sched_a, sched_b):
    B, S, _ = a.shape
    C = TM

    def body(a_ref, b_ref, c_ref, da_ref, db_ref, oa_ref, ob_ref):
        ii = lax.broadcasted_iota(jnp.int32, (C, C), 0)
        jj = lax.broadcasted_iota(jnp.int32, (C, C), 1)
        pos = lax.broadcasted_iota(jnp.int32, (C, 1), 0).astype(F32)
        for dec_ref, o_ref, (order, causal, strict) in ((da_ref, oa_ref, sched_a), (db_ref, ob_ref, sched_b)):
            lg = _log_sigmoid(dec_ref[...])
            lg1 = lg[:, 0:1]
            dist = ((ii - jj) if causal else (jj - ii)).astype(F32)
            mask = (dist > 0.0) if strict else (dist >= 0.0)
            decay = jnp.where(mask, jnp.exp(jnp.maximum(dist, 0.0) * lg1), 0.0)
            p = pos if causal else (C - 1.0 - pos)
            w_q = jnp.exp((p + 1.0) * lg1)
            w_k = jnp.exp((C - 1.0 - p) * lg1)
            chunk_decay = jnp.exp(C * lg)
            state = jnp.zeros((HD, HD), F32)
            for n in order:
                rows = pl.ds(n * C, C)
                at, bt, ct = a_ref[rows, :], b_ref[rows, :], c_ref[rows, :]
                cb = ct.astype(BF16)
                scores = _nt(at.astype(BF16), bt.astype(BF16)) * decay
                o = _nn(scores.astype(BF16), cb)
                o = o + _nn((at * w_q).astype(BF16), state.astype(BF16))
                o_ref[rows, :] = o
                state = chunk_decay * state + _tn((bt * w_k).astype(BF16), cb)

    seq = pl.BlockSpec((None, S, HD), lambda b, h: (b, 0, h))
    dspec = pl.BlockSpec((None, 1, HD), lambda b, h: (h, 0, 0))
    return pl.pallas_call(
        body, name="retention", grid=(B, NH),
        in_specs=[seq, seq, seq, dspec, dspec], out_specs=[seq, seq],
        out_shape=[jax.ShapeDtypeStruct((B, S, RW), F32)] * 2,
        compiler_params=_params(VMEM_BIG, ("arbitrary", "arbitrary")),
    )(a, b, c, dec_a, dec_b)


def _retention_ddecay(q, k, v, do, dec_a, dec_b, sched_a, sched_b):
    B, S, _ = q.shape
    C = TM

    def body(q_ref, k_ref, v_ref, do_ref, da_ref, db_ref, o_ref):
        ii = lax.broadcasted_iota(jnp.int32, (C, C), 0)
        jj = lax.broadcasted_iota(jnp.int32, (C, C), 1)
        pos = lax.broadcasted_iota(jnp.int32, (C, 1), 0).astype(F32)
        vals = []
        for dec_ref, (order, causal, strict) in ((da_ref, sched_a), (db_ref, sched_b)):
            x = dec_ref[...]
            lg = _log_sigmoid(x)
            lg1 = lg[:, 0:1]
            dist = ((ii - jj) if causal else (jj - ii)).astype(F32)
            mask = (dist > 0.0) if strict else (dist >= 0.0)
            ddecay = jnp.where(mask, dist * jnp.exp(jnp.maximum(dist, 0.0) * lg1), 0.0)
            p = pos if causal else (C - 1.0 - pos)
            w_q = jnp.exp((p + 1.0) * lg1)
            w_k = jnp.exp((C - 1.0 - p) * lg1)
            chunk_decay = jnp.exp(C * lg)
            state = jnp.zeros((HD, HD), F32)
            dstate = jnp.zeros((HD, HD), F32)
            tot = jnp.zeros((), F32)
            for n in order:
                rows = pl.ds(n * C, C)
                qt, kt, vt, dot = q_ref[rows, :], k_ref[rows, :], v_ref[rows, :], do_ref[rows, :]
                vb = vt.astype(BF16)
                scores = _nt(qt.astype(BF16), kt.astype(BF16))
                dscores = _nt(dot.astype(BF16), vb)
                qw = (qt * w_q).astype(BF16)
                cross = _nn(qw, state.astype(BF16))
                dcross = _nn(qw, dstate.astype(BF16))
                tot = tot + jnp.sum(scores * dscores * ddecay) + jnp.sum(((p + 1.0) * cross + dcross) * dot)
                kv = _tn((kt * w_k).astype(BF16), vb)
                dkv = _tn((kt * ((C - 1.0 - p) * w_k)).astype(BF16), vb)
                dstate = chunk_decay * (dstate + C * state) + dkv
                state = chunk_decay * state + kv
            vals.append(tot * jax.nn.sigmoid(-x))
        row = lax.broadcasted_iota(jnp.int32, (8, HD), 0)
        tile = jnp.where(row == 0, vals[0], 0.0) + jnp.where(row == 1, vals[1], 0.0)
        _acc(o_ref, tile, pl.program_id(1) == 0)

    seq = pl.BlockSpec((None, S, HD), lambda h, b: (b, 0, h))
    dspec = pl.BlockSpec((None, 1, HD), lambda h, b: (h, 0, 0))
    return pl.pallas_call(
        body, name="retention_ddecay", grid=(NH, B),
        in_specs=[seq, seq, seq, seq, dspec, dspec], out_specs=pl.BlockSpec((None, 8, HD), lambda h, b: (h, 0, 0)),
        out_shape=jax.ShapeDtypeStruct((NH, 8, HD), F32),
        compiler_params=_params(VMEM_BIG, ("arbitrary", "arbitrary")),
    )(q, k, v, do, dec_a, dec_b)


def _schedules(S):
    n = S // TM
    lat_up = tuple(range(1, n))
    lat_down = tuple(range(n - 1, 0, -1))
    fwd = (((0,) + lat_up, True, False), ((0,) + lat_down, False, True))
    bwd = ((lat_down + (0,), False, False), (lat_up + (0,), True, True))
    return fwd, bwd


def _shift_rows(x, d):
    if d == 0:
        return x
    S = x.shape[0]
    t = lax.broadcasted_iota(jnp.int32, x.shape, 0)
    tt = t + d
    lo = jnp.where(t < LC, 0, LC)
    hi = jnp.where(t < LC, LC, S)
    return jnp.where((tt >= lo) & (tt < hi), pltpu.roll(x, (-d) % S, 0), 0.0)


@functools.partial(jax.custom_vjp, nondiff_argnums=(1,))
def _shift(x, d):
    return _shift_rows(x, d)


_shift.defvjp(lambda x, d: (_shift_rows(x, d), None), lambda d, _, g: (_shift_rows(g, -d),))


def _pool_fn(p, bd, pscale):
    lane = lax.broadcasted_iota(jnp.int32, p.shape, 1)
    grp = lane // (PW // 4)
    half = jnp.where(grp == 0, 1, jnp.where(grp == 1, 2, jnp.where(grp == 2, 4, 8)))
    ones = jnp.ones(p.shape, F32)
    acc = jnp.zeros(p.shape, F32)
    cnt = jnp.zeros(p.shape, F32)
    for d in range(-8, 8):
        inwin = ((d >= -half) & (d < half)).astype(F32)
        acc = acc + _shift(p, d) * inwin
        cnt = cnt + _shift_rows(ones, d) * inwin
    pooled = acc / cnt - p
    mixed = _nn(pooled.astype(BF16), bd.astype(BF16))
    return mixed * pscale


def _dwconv_raw(zc, dw):
    y = jnp.zeros(zc.shape, F32)
    for k in range(CONV_K):
        y = y + _shift_rows(zc, k - CONV_K // 2) * dw[k:k + 1, :]
    return y


@jax.custom_vjp
def _dwconv(zc, dw):
    return _dwconv_raw(zc, dw)


def _dwconv_fwd(zc, dw):
    return _dwconv_raw(zc, dw), (zc, dw)


def _dwconv_bwd(res, g):
    zc, dw = res
    dz = jnp.zeros(zc.shape, F32)
    ddw = jnp.zeros(dw.shape, F32)
    row = lax.broadcasted_iota(jnp.int32, dw.shape, 0)
    for k in range(CONV_K):
        dz = dz + _shift_rows(g, CONV_K // 2 - k) * dw[k:k + 1, :]
        r = jnp.sum(g * _shift_rows(zc, k - CONV_K // 2), axis=0, keepdims=True)
        ddw = ddw + jnp.where(row == k, r, 0.0)
    return dz, ddw


_dwconv.defvjp(_dwconv_fwd, _dwconv_bwd)


def _conv_fn(u, dw, db):
    zc = u[:, :PW] * jax.nn.sigmoid(u[:, PW:])
    return _dwconv(zc, dw) + db


def _ln_swish(y, lng, lnb):
    mu = jnp.mean(y, axis=-1, keepdims=True)
    yc = y - mu
    var = jnp.mean(yc * yc, axis=-1, keepdims=True)
    yn = yc * lax.rsqrt(var + EPS) * lng + lnb
    return yn * jax.nn.sigmoid(yn)


def _seq(shape, single=False):
    return pl.BlockSpec((None,) + shape, lambda b: (b, 0, 0), pipeline_mode=pl.Buffered(1) if single else None)


def _c1(shape):
    nd = len(shape)
    return pl.BlockSpec(shape, lambda b: (0,) * nd)


def _seq_apply(fn, name, xs, consts, width):
    B, S, w = xs.shape

    def body(x_ref, *refs):
        refs[-1][...] = fn(x_ref[...], *[r[...] for r in refs[:-1]])

    return pl.pallas_call(
        body, name=name, grid=(B,),
        in_specs=[_seq((S, w))] + [_c1(c.shape) for c in consts], out_specs=_seq((S, width)),
        out_shape=jax.ShapeDtypeStruct((B, S, width), F32),
        compiler_params=_params(VMEM_BIG, ("arbitrary",)),
    )(xs, *consts)


def _seq_vjp(fn, name, xs, consts, dout):
    B, S, w = xs.shape
    n = len(consts)

    def body(x_ref, d_ref, *refs):
        first = pl.program_id(0) == 0
        _, vjp = jax.vjp(fn, x_ref[...], *[r[...] for r in refs[:n]])
        grads = vjp(d_ref[...])
        refs[n][...] = grads[0]
        for ref, val in zip(refs[n + 1:], grads[1:]):
            _acc(ref, val, first)

    return pl.pallas_call(
        body, name=name, grid=(B,),
        in_specs=[_seq((S, w), True), _seq((S, dout.shape[2]), True)] + [_c1(c.shape) for c in consts],
        out_specs=[_seq((S, w))] + [_c1(c.shape) for c in consts],
        out_shape=[jax.ShapeDtypeStruct((B, S, w), F32)] + [jax.ShapeDtypeStruct(c.shape, F32) for c in consts],
        compiler_params=_params(VMEM_BIG, ("arbitrary",)),
    )(xs, dout, *consts)


def _pool_conv_fwd(pp, pc, bd, pscale, dw, db):
    return (_seq_apply(_pool_fn, "pool_fwd", pp, (bd, pscale), PW),
            _seq_apply(_conv_fn, "conv_fwd", pc, (dw, db), PW))


def _pool_conv_bwd(pp, pc, dpo, dco, bd, pscale, dw, db):
    dpp, dbd, dps = _seq_vjp(_pool_fn, "pool_bwd", pp, (bd, pscale), dpo)
    dpc, ddw, ddb = _seq_vjp(_conv_fn, "conv_bwd", pc, (dw, db), dco)
    return dpp, dpc, dbd, dps, ddw, ddb


def _cat_fn(po, oa, ob, gg, co, gng, lng, lnb):
    o = oa + ob
    outs = []
    for h in range(NH):
        oh = o[:, h * HD:(h + 1) * HD]
        mu = jnp.mean(oh, axis=-1, keepdims=True)
        oc = oh - mu
        var = jnp.mean(oc * oc, axis=-1, keepdims=True)
        outs.append(oc * lax.rsqrt(var + EPS))
    ret = jnp.concatenate(outs, axis=1) * gng * (gg * jax.nn.sigmoid(gg))
    return jnp.concatenate([po, ret, _ln_swish(co, lng, lnb)], axis=1)


def _mix_out_fwd(z, po, oa, ob, gg, co, ro, mod, wout):
    B, S, _ = z.shape

    def body(z_ref, po_ref, oa_ref, ob_ref, gg_ref, co_ref, gn_ref, lg_ref, lb_ref, mod_ref, w_ref, zo_ref, out_ref):
        cat = _cat_fn(po_ref[...], oa_ref[...], ob_ref[...], gg_ref[...], co_ref[...], gn_ref[...], lg_ref[...], lb_ref[...])
        out = _nn(cat.astype(BF16), w_ref[...])
        out_ref[...] = out
        zo_ref[...] = z_ref[...] + mod_ref[2:3, :] * out

    return pl.pallas_call(
        body, name="mix_out_fwd", grid=(B, S // TM),
        in_specs=[_tok(D), _tok(PW), _tok(RW), _tok(RW), _tok(RW), _tok(PW), _const((1, RW)), _const((1, PW)),
                  _const((1, PW)), _modspec(), _wspec(wout)],
        out_specs=[_tok(D), _tok(D)],
        out_shape=[jax.ShapeDtypeStruct((B, S, D), F32)] * 2,
        compiler_params=_params(None, ("arbitrary", "arbitrary")),
    )(z, po, oa, ob, gg, co, *ro, mod, wout[0])


def _mix_out_bwd(dzo, out, po, oa, ob, gg, co, ro, mod, wout):
    B, S, _ = dzo.shape

    def body(dzo_ref, out_ref, po_ref, oa_ref, ob_ref, gg_ref, co_ref, gn_ref, lg_ref, lb_ref, mod_ref, w_ref,
             dpo_ref, do_ref, dgg_ref, dco_ref, cat_ref, dout_ref, dmod_ref, dgn_ref, dlg_ref, dlb_ref):
        b, t = pl.program_id(0), pl.program_id(1)
        dzo = dzo_ref[...]
        cat, vjp = jax.vjp(_cat_fn, po_ref[...], oa_ref[...], ob_ref[...], gg_ref[...], co_ref[...], gn_ref[...],
                           lg_ref[...], lb_ref[...])
        cat_ref[...] = cat.astype(BF16)
        dout = (mod_ref[2:3, :] * dzo).astype(BF16)
        dout_ref[...] = dout
        dgate = jnp.sum(out_ref[...] * dzo, axis=0, keepdims=True)
        dcat = _nt(dout, w_ref[...])
        dpo, doa, _, dgg, dco, dgn, dlg, dlb = vjp(dcat)
        dpo_ref[...] = dpo
        do_ref[...] = doa
        dgg_ref[...] = dgg
        dco_ref[...] = dco
        zero = jnp.zeros_like(dgate)
        _acc(dmod_ref, jnp.concatenate([zero, zero, dgate], axis=0), t <= 1)
        first = jnp.logical_and(b == 0, t == 0)
        _acc(dgn_ref, dgn, first)
        _acc(dlg_ref, dlg, first)
        _acc(dlb_ref, dlb, first)

    return pl.pallas_call(
        body, name="mix_out_bwd", grid=(B, S // TM),
        in_specs=[_tok(D), _tok(D), _tok(PW), _tok(RW), _tok(RW), _tok(RW), _tok(PW), _const((1, RW)), _const((1, PW)),
                  _const((1, PW)), _modspec(), _wspec(wout)],
        out_specs=[_tok(PW), _tok(RW), _tok(RW), _tok(PW), _tok(D), _tok(D), _modspec(), _const((1, RW)),
                   _const((1, PW)), _const((1, PW))],
        out_shape=[jax.ShapeDtypeStruct((B, S, PW), F32), jax.ShapeDtypeStruct((B, S, RW), F32),
                   jax.ShapeDtypeStruct((B, S, RW), F32), jax.ShapeDtypeStruct((B, S, PW), F32),
                   jax.ShapeDtypeStruct((B, S, D), BF16), jax.ShapeDtypeStruct((B, S, D), BF16),
                   jax.ShapeDtypeStruct((B, 2, 3, D), F32), jax.ShapeDtypeStruct((1, RW), F32),
                   jax.ShapeDtypeStruct((1, PW), F32), jax.ShapeDtypeStruct((1, PW), F32)],
        compiler_params=_params(None, ("arbitrary", "arbitrary")),
    )(dzo, out, po, oa, ob, gg, co, *ro, mod, wout[0])


def _rms(z, g):
    return z * lax.rsqrt(jnp.mean(z * z, axis=-1, keepdims=True) + EPS) * g


def _head(z, target, fg):
    B, S, _ = z.shape

    def body(z_ref, t_ref, g_ref, dz_ref, dg_ref, loss_ref):
        b, t = pl.program_id(0), pl.program_id(1)
        first = jnp.logical_and(b == 0, t == 0)

        @pl.when(t == 0)
        def _():
            dz_ref[...] = jnp.zeros((TM, D), F32)

        @pl.when(first)
        def _():
            dg_ref[...] = jnp.zeros((1, D), F32)
            loss_ref[...] = jnp.zeros((8, 128), F32)

        @pl.when(t > 0)
        def _():
            y, vjp = jax.vjp(_rms, z_ref[...], g_ref[...])
            err = y - t_ref[...]
            dz, dg = vjp(err * (1.0 / D))
            dz_ref[...] = dz
            dg_ref[...] += dg
            loss_ref[...] += 0.5 * jnp.sum(err * err) * (1.0 / D)

    return pl.pallas_call(
        body, name="head", grid=(B, S // TM),
        in_specs=[_tok(D), pl.BlockSpec((None, TM, D), lambda b, t: (b, jnp.maximum(t - 1, 0), 0)), _const((1, D))],
        out_specs=[_tok(D), _const((1, D)), _const((8, 128))],
        out_shape=[jax.ShapeDtypeStruct((B, S, D), F32), jax.ShapeDtypeStruct((1, D), F32),
                   jax.ShapeDtypeStruct((8, 128), F32)],
        compiler_params=_params(None, ("arbitrary", "arbitrary")),
    )(z, target, fg)


MROWS = 24
MCOL = 768


def _silu(x):
    return x * jax.nn.sigmoid(x)


def _mod_fwd(c24, wmod, bmod):
    ncol = wmod.shape[2]

    def body(c_ref, w_ref, b_ref, o_ref):
        sc = _silu(c_ref[...]).astype(BF16)
        o_ref[...] = _nn(sc, w_ref[...].astype(BF16)) + b_ref[...]

    return pl.pallas_call(
        body, name="mod_fwd", grid=(2, ncol // MCOL),
        in_specs=[pl.BlockSpec((MROWS, D), lambda l, j: (0, 0)), pl.BlockSpec((None, D, MCOL), lambda l, j: (l, 0, j)),
                  pl.BlockSpec((None, 1, MCOL), lambda l, j: (l, 0, j))],
        out_specs=pl.BlockSpec((None, MROWS, MCOL), lambda l, j: (l, 0, j)),
        out_shape=jax.ShapeDtypeStruct((2, MROWS, ncol), F32),
        compiler_params=_params(None, ("arbitrary", "arbitrary")),
    )(c24, wmod, bmod)


def _mod_bwd(c24, dmod, wmod):
    ncol = wmod.shape[2]

    def body(c_ref, d_ref, w_ref, dw_ref, dsc_ref):
        l, j = pl.program_id(0), pl.program_id(1)
        sc = _silu(c_ref[...]).astype(BF16)
        dm = d_ref[...].astype(BF16)
        dw_ref[...] = _tn(sc, dm)
        _acc(dsc_ref, _nt(dm, w_ref[...].astype(BF16)), jnp.logical_and(l == 0, j == 0))

    return pl.pallas_call(
        body, name="mod_bwd", grid=(2, ncol // MCOL),
        in_specs=[pl.BlockSpec((MROWS, D), lambda l, j: (0, 0)), pl.BlockSpec((None, MROWS, MCOL), lambda l, j: (l, 0, j)),
                  pl.BlockSpec((None, D, MCOL), lambda l, j: (l, 0, j))],
        out_specs=[pl.BlockSpec((None, D, MCOL), lambda l, j: (l, 0, j)), pl.BlockSpec((MROWS, D), lambda l, j: (0, 0))],
        out_shape=[jax.ShapeDtypeStruct((2, D, ncol), F32), jax.ShapeDtypeStruct((MROWS, D), F32)],
        compiler_params=_params(None, ("arbitrary", "arbitrary")),
    )(c24, dmod, wmod)


def _bmod_cctx_grad(dmod_full, dsc_parts, cctx):
    def body(d_ref, p_ref, c_ref, db_ref, dc_ref):
        db_ref[...] = jnp.sum(d_ref[...], axis=1, keepdims=True)
        tot = jnp.zeros((8, D), F32)
        for s in range(N_CHIP):
            tot = tot + p_ref[s]
        x = c_ref[...]
        sg = jax.nn.sigmoid(x)
        dc_ref[...] = jnp.sum(tot, axis=0, keepdims=True) * (sg * (1.0 + x * (1.0 - sg)))

    return pl.pallas_call(
        body, name="bmod_cctx_grad",
        out_shape=[jax.ShapeDtypeStruct((2, 1, N_MOD * D), F32), jax.ShapeDtypeStruct((1, D), F32)],
    )(dmod_full, dsc_parts, cctx)


def _adam_math(w, g, m, v):
    m = ADAM_B1 * m + (1.0 - ADAM_B1) * g
    v = ADAM_B2 * v + (1.0 - ADAM_B2) * (g * g)
    m_hat = m / (1.0 - ADAM_B1 ** ADAM_STEP)
    v_hat = v / (1.0 - ADAM_B2 ** ADAM_STEP)
    delta = -ADAM_LR * (m_hat / (jnp.sqrt(v_hat) + ADAM_EPS) + ADAM_WD * w)
    return delta, m, v


def _adam(w, g, m, v):
    R, Cc = w.shape
    if R * Cc * 4 <= (1 << 20):
        RB = R
    else:
        RB = 1 << (((1 << 18) // Cc).bit_length() - 1)
        assert R % RB == 0

    def body(w_ref, g_ref, m_ref, v_ref, d_ref, mo_ref, vo_ref):
        d, mn, vn = _adam_math(w_ref[...], g_ref[...], m_ref[...], v_ref[...])
        d_ref[...] = d
        mo_ref[...] = mn
        vo_ref[...] = vn

    spec = pl.BlockSpec((RB, Cc), lambda i: (i, 0))
    return pl.pallas_call(
        body, name="adam", grid=(R // RB,), in_specs=[spec] * 4, out_specs=[spec] * 3,
        out_shape=[jax.ShapeDtypeStruct((R, Cc), F32)] * 3,
        compiler_params=_params(None, ("arbitrary",)),
    )(w, g, m, v)


def _adam_layer(w, g, m, v, layer, prev):
    shape = w.shape
    n, cols = shape[0], shape[-1]
    w3, m3, v3 = (t.reshape(n, -1, cols) for t in (w, m, v))
    g2 = g.reshape(-1, cols)
    R = g2.shape[0]
    RB = max(r for r in range(8, (1 << 18) // cols + 1, 8) if R % r == 0)

    def body(w_ref, g_ref, m_ref, v_ref, *refs):
        go_ref, d_ref, mo_ref, vo_ref = refs[-4:]
        gt = g_ref[...]
        d, mn, vn = _adam_math(w_ref[...], gt, m_ref[...], v_ref[...])
        go_ref[...] = gt
        d_ref[...] = d
        mo_ref[...] = mn
        vo_ref[...] = vn

    lay = pl.BlockSpec((None, RB, cols), lambda i: (layer, i, 0))
    flat = pl.BlockSpec((RB, cols), lambda i: (i, 0))
    hold = [] if prev is None else [t.reshape(n, -1, cols) for t in prev]
    outs = pl.pallas_call(
        body, name="adam_layer", grid=(R // RB,),
        in_specs=[lay, flat, lay, lay] + [pl.BlockSpec(memory_space=pl.ANY)] * len(hold), out_specs=[lay] * 4,
        out_shape=[jax.ShapeDtypeStruct(w3.shape, F32)] * 4,
        input_output_aliases={4 + k: k for k in range(len(hold))},
        compiler_params=_params(None, ("arbitrary",)),
    )(w3, g2, m3, v3, *hold)
    return [o.reshape(shape) for o in outs]


def _sum_devices(parts):
    K = parts.shape[1]

    def body(p_ref, o_ref):
        tot = p_ref[0]
        for i in range(1, N_DEV):
            tot = tot + p_ref[i]
        o_ref[...] = tot

    return pl.pallas_call(body, name="sum_devices", out_shape=jax.ShapeDtypeStruct((K, 128), F32))(parts)


def _sum_pieces(own, others):
    R = own.shape[0]
    RB = 96 if R % 96 == 0 else 32

    def body(a_ref, r_ref, o_ref):
        tot = a_ref[...].astype(F32)
        for i in range(N_DEV - 1):
            tot = tot + r_ref[i].astype(F32)
        o_ref[...] = tot

    return pl.pallas_call(
        body, name="sum_pieces", grid=(R // RB,),
        in_specs=[pl.BlockSpec((RB, D), lambda i: (i, 0)), pl.BlockSpec((N_DEV - 1, RB, D), lambda i: (0, i, 0))],
        out_specs=pl.BlockSpec((RB, D), lambda i: (i, 0)),
        out_shape=jax.ShapeDtypeStruct((R, D), F32),
        compiler_params=_params(None, ("arbitrary",)),
    )(own, others)


def _coords():
    return lax.axis_index("x"), lax.axis_index("y"), lax.axis_index("c")


_FLIPS = [(fx, fy, fc) for fx in (0, 1) for fy in (0, 1) for fc in (0, 1)][1:]


def _all_gather_small(buf):
    K = buf.shape[0]

    def body(in_ref, out_ref, send_sems, recv_sems, local_sem):
        x, y, c = _coords()
        me = 4 * x + 2 * y + c
        mine = pltpu.make_async_copy(in_ref, out_ref.at[me], local_sem)
        mine.start()
        sends = []
        for k, (fx, fy, fc) in enumerate(_FLIPS):
            peer = (x ^ fx, y ^ fy, c ^ fc)
            cp = pltpu.make_async_remote_copy(src_ref=in_ref, dst_ref=out_ref.at[me], send_sem=send_sems.at[k],
                                              recv_sem=recv_sems.at[k], device_id=peer, device_id_type=MESH)
            cp.start()
            sends.append(cp)
        for k, (fx, fy, fc) in enumerate(_FLIPS):
            src = 4 * (x ^ fx) + 2 * (y ^ fy) + (c ^ fc)
            pltpu.make_async_remote_copy(src_ref=in_ref, dst_ref=out_ref.at[src], send_sem=send_sems.at[k],
                                         recv_sem=recv_sems.at[k], device_id=(x, y, c), device_id_type=MESH).wait_recv()
        for cp in sends:
            cp.wait_send()
        mine.wait()

    return pl.pallas_call(
        body, name="all_gather_small",
        in_specs=[pl.BlockSpec(memory_space=pltpu.VMEM)], out_specs=pl.BlockSpec(memory_space=pltpu.VMEM),
        out_shape=jax.ShapeDtypeStruct((N_DEV, K, 128), F32),
        scratch_shapes=[pltpu.SemaphoreType.DMA((7,)), pltpu.SemaphoreType.DMA((7,)), pltpu.SemaphoreType.DMA],
        compiler_params=_params(VMEM_BIG),
    )(buf)


_CHIP_FLIPS = [(1, 0), (0, 1), (1, 1)]


_HBM = pl.BlockSpec(memory_space=pltpu.HBM)
_SEMS = pl.BlockSpec(memory_space=pltpu.SEMAPHORE)
_EFFECT = pltpu.SideEffectType.DATAFLOW_SIDE_EFFECTING


def _in_hbm(v):
    return pltpu.with_memory_space_constraint(v, pltpu.HBM)


def _copies_start(name, srcs, lands, n_sems, issue, after):
    ns, nl = len(srcs), len(lands)

    def body(*refs):
        src_refs, land_refs = refs[:ns], refs[ns:ns + nl]
        out = refs[ns + nl + 1:]
        issue(src_refs, land_refs, out[:nl], out[nl:2 * nl])
        out[-1][...] = jnp.zeros((8, 128), F32)

    outs = pl.pallas_call(
        body, name=name, in_specs=[_HBM] * (ns + nl) + [pl.BlockSpec(memory_space=pl.ANY)],
        out_specs=[_SEMS] * (2 * nl) + [_HBM] * (ns + nl) + [pl.BlockSpec(memory_space=pltpu.VMEM)],
        out_shape=[pltpu.SemaphoreType.DMA((n_sems,))] * (2 * nl) + [pltpu.HBM(v.shape, v.dtype) for v in (*srcs, *lands)]
        + [jax.ShapeDtypeStruct((8, 128), F32)],
        input_output_aliases={i: 2 * nl + i for i in range(ns + nl)},
        compiler_params=pltpu.CompilerParams(has_side_effects=_EFFECT),
    )(*[_in_hbm(v) for v in (*srcs, *lands)], after)
    return outs[:nl], outs[nl:2 * nl], outs[2 * nl:2 * nl + ns], outs[2 * nl + ns:2 * nl + ns + nl], outs[-1]


def _copies_wait(name, srcs, lands, send_sems, recv_sems, finish, after):
    after = list(after) if isinstance(after, (list, tuple)) else [after]
    ns, nl = len(srcs), len(lands)

    def body(*refs):
        src_refs, land_refs = refs[:ns], refs[ns:ns + nl]
        finish(src_refs, land_refs, refs[ns + nl:ns + 2 * nl], refs[ns + 2 * nl:ns + 3 * nl])

    outs = pl.pallas_call(
        body, name=name, in_specs=[_HBM] * (ns + nl) + [_SEMS] * (2 * nl) + [pl.BlockSpec(memory_space=pl.ANY)] * len(after),
        out_specs=[_HBM] * (ns + nl), out_shape=[pltpu.HBM(v.shape, v.dtype) for v in (*srcs, *lands)],
        input_output_aliases={i: i for i in range(ns + nl)},
        compiler_params=pltpu.CompilerParams(has_side_effects=_EFFECT),
    )(*srcs, *lands, *send_sems, *recv_sems, *after)
    return outs[:ns], outs[ns:]


def _own_slab(land, mine, index):
    return lax.dynamic_update_slice_in_dim(land, mine[:, None], index, axis=1)


def _gather_start(units, after):
    x, y, c = _coords()
    chip = 2 * x + y
    lands = [_own_slab(lax.empty((u.shape[0], N_CHIP) + u.shape[1:], u.dtype), u, chip) for u in units]

    def issue(src_refs, land_refs, send_sems, recv_sems):
        x, y, c = _coords()
        s_me = 2 * x + y
        for i, (src, land) in enumerate(zip(src_refs, land_refs)):
            for j, (fx, fy) in enumerate(_CHIP_FLIPS):
                pltpu.make_async_remote_copy(src_ref=src, dst_ref=land.at[:, s_me], send_sem=send_sems[i].at[j],
                                             recv_sem=recv_sems[i].at[j], device_id=(x ^ fx, y ^ fy, c),
                                             device_id_type=MESH).start()

    return _copies_start("gather_start", units, lands, 3, issue, after)


def _gather_wait(tag, started, which, after):
    send_sems, recv_sems, srcs, lands, _ = started

    def finish(src_refs, land_refs, ssems, rsems):
        x, y, c = _coords()
        for src, land, ss, rs in zip(src_refs, land_refs, ssems, rsems):
            for j in range(3):
                cp = pltpu.make_async_remote_copy(src_ref=src, dst_ref=land.at[:, 0], send_sem=ss.at[j], recv_sem=rs.at[j],
                                                  device_id=(x, y, c), device_id_type=MESH)
                cp.wait_send()
                cp.wait_recv()

    _, done = _copies_wait("gather_wait_" + tag, [srcs[i] for i in which], [lands[i] for i in which],
                           [send_sems[i] for i in which], [recv_sems[i] for i in which], finish, after)
    return [d.reshape(d.shape[0], N_CHIP * d.shape[2], D) for d in done]


def _piece_rows(grads):
    return [g.shape[2] for g in grads]


def _scatter_start(tag, grads, after):
    rows = _piece_rows(grads)
    land = lax.empty((N_DEV - 1, sum(rows), D), grads[0].dtype)

    def issue(src_refs, land_refs, send_sems, recv_sems):
        x, y, c = _coords()
        for k, (fx, fy, fc) in enumerate(_FLIPS):
            px, py, pc = x ^ fx, y ^ fy, c ^ fc
            off = 0
            for src, n in zip(src_refs, rows):
                pltpu.make_async_remote_copy(src_ref=src.at[2 * px + py, pc], dst_ref=land_refs[0].at[k, pl.ds(off, n)],
                                             send_sem=send_sems[0].at[k], recv_sem=recv_sems[0].at[k],
                                             device_id=(px, py, pc), device_id_type=MESH).start()
                off += n

    return _copies_start("scatter_start_" + tag, grads, [land], N_DEV - 1, issue, after)


def _scatter_wait(tag, started, after):
    send_sems, recv_sems, srcs, lands, _ = started

    def finish(src_refs, land_refs, ssems, rsems):
        x, y, c = _coords()
        for k in range(N_DEV - 1):
            cp = pltpu.make_async_remote_copy(src_ref=land_refs[0].at[0], dst_ref=land_refs[0].at[0], send_sem=ssems[0].at[k],
                                              recv_sem=rsems[0].at[k], device_id=(x, y, c), device_id_type=MESH)
            cp.wait_send()
            cp.wait_recv()

    grads, (others,) = _copies_wait("scatter_wait_" + tag, srcs, lands, send_sems, recv_sems, finish, after)
    return grads, others


def _swap_start(tag, mine, rows, after):
    x, y, c = _coords()
    offs = [sum(rows[:t]) for t in range(len(rows))]
    lands = [lax.dynamic_update_slice_in_dim(lax.empty((2, n, D), mine.dtype), mine[o:o + n][None], c, axis=0)
             for o, n in zip(offs, rows)]

    def issue(src_refs, land_refs, send_sems, recv_sems):
        x, y, c = _coords()
        for t, (o, n) in enumerate(zip(offs, rows)):
            pltpu.make_async_remote_copy(src_ref=src_refs[0].at[pl.ds(o, n)], dst_ref=land_refs[t].at[c],
                                         send_sem=send_sems[t].at[0], recv_sem=recv_sems[t].at[0],
                                         device_id=(x, y, 1 - c), device_id_type=MESH).start()

    return _copies_start("swap_start_" + tag, [mine], lands, 1, issue, after)


def _swap_wait(tag, started, after):
    send_sems, recv_sems, srcs, lands, _ = started

    def finish(src_refs, land_refs, ssems, rsems):
        x, y, c = _coords()
        for land, ss, rs in zip(land_refs, ssems, rsems):
            cp = pltpu.make_async_remote_copy(src_ref=land.at[0], dst_ref=land.at[0], send_sem=ss.at[0], recv_sem=rs.at[0],
                                              device_id=(x, y, c), device_id_type=MESH)
            cp.wait_send()
            cp.wait_recv()

    return _copies_wait("swap_wait_" + tag, srcs, lands, send_sems, recv_sems, finish, after)[1]


def _size(shape):
    n = 1
    for d in shape:
        n *= d
    return n


def _pack(arrays):
    return jnp.concatenate([jnp.pad(a.reshape(-1).astype(F32), (0, (-a.size) % 1024)).reshape(-1, 128) for a in arrays], axis=0)


def _unpack(buf, shapes):
    out, row = [], 0
    for s in shapes:
        n = _size(s)
        nrows = 8 * -(-n // 1024)
        out.append(buf[row:row + nrows].reshape(-1)[:n].reshape(s))
        row += nrows
    return out


def _block_diag(pw):
    bd = jnp.zeros((PW, PW), F32)
    g = PW // 4
    for i in range(4):
        bd = bd.at[i * g:(i + 1) * g, i * g:(i + 1) * g].set(pw[i])
    return bd


def _lanes(v):
    return jnp.broadcast_to(v.reshape(NH, 1, 1), (NH, 1, HD))


def _layer_fwd(z, mod, normg, wget, small):
    S = z.shape[1]
    cos, sin = _rope_tables(S)
    fwd_sched, _ = _schedules(S)
    wa = wget("a", z)
    z1, f_a = _ffn_fwd(z, mod[:, :, 0], normg[0], *wa)
    wm = wget("m", z1)
    pp, q, k, v, gg, pc = _mix_in_fwd(z1, mod[:, :, 1], normg[1], wm[0], cos, sin)
    po, co = _pool_conv_fwd(pp, pc, small["bd"], small["pscale"], small["dw"], small["db"])
    ro = (small["gng"], small["lng"], small["lnb"])
    oa, ob = _retention(q, k, v, small["dec_f"], small["dec_b"], *fwd_sched)
    z2, out = _mix_out_fwd(z1, po, oa, ob, gg, co, ro, mod[:, :, 1], wm[1])
    wb = wget("b", z2)
    z3, f_b = _ffn_fwd(z2, mod[:, :, 2], normg[2], *wb)
    saved = dict(z=z, f_a=f_a, z1=z1, pp=pp, q=q, k=k, v=v, gg=gg, pc=pc, po=po, co=co, oa=oa, ob=ob, out=out, z2=z2, f_b=f_b,
                 wa=wa, wm=wm, wb=wb)
    return z3, saved


def _layer_bwd(dz3, sv, mod, normg, small, emit, tok):
    S = dz3.shape[1]
    B = dz3.shape[0]
    T = B * S
    cos, sin = _rope_tables(S)
    fwd_sched, bwd_sched = _schedules(S)
    wa, wm, wb = sv["wa"], sv["wm"], sv["wb"]
    dz2, dmod_b, dg_b, gw1t_b, gw3t_b, gw2_b = _ffn_bwd(sv["z2"], dz3, sv["f_b"], mod[:, :, 2] + tok, normg[2], *wb)
    tok = emit("b", [gw1t_b, gw3t_b, gw2_b])
    mod_m = mod[:, :, 1] + tok
    ro = (small["gng"], small["lng"], small["lnb"])
    dpo, do, dgg, dco, cat, dout, dmod_gate, dgng, dlng, dlnb = _mix_out_bwd(
        dz2, sv["out"], sv["po"], sv["oa"], sv["ob"], sv["gg"], sv["co"], ro, mod_m, wm[1])
    gwout = _tn_matmul(cat.reshape(T, D), dout.reshape(T, D))
    dqa, dqb = _retention(do, sv["v"], sv["k"], small["dec_f"], small["dec_b"], *fwd_sched)
    dka, dkb = _retention(sv["v"], do, sv["q"], small["dec_f"], small["dec_b"], *bwd_sched)
    dva, dvb = _retention(sv["k"], sv["q"], do, small["dec_f"], small["dec_b"], *bwd_sched)
    ddec = _retention_ddecay(sv["q"], sv["k"], sv["v"], do, small["dec_f"], small["dec_b"], *fwd_sched)
    dq, dk = _rope_bwd(dqa, dqb, dka, dkb, cos, sin)
    dpp, dpc, dbd, dps, ddw, ddb = _pool_conv_bwd(sv["pp"], sv["pc"], dpo, dco, small["bd"], small["pscale"],
                                                   small["dw"], small["db"])
    dz1, h, dp, dmod_m, dg_m = _mix_in_bwd(sv["z1"], dz2, dpp, dq, dk, dva, dvb, dgg, dpc, mod_m, normg[1], wm[0])
    gwint = _tn_matmul(dp.reshape(T, F), h.reshape(T, D))
    tok = emit("m", [gwint, gwout])
    dz, dmod_a, dg_a, gw1t_a, gw3t_a, gw2_a = _ffn_bwd(sv["z"], dz1, sv["f_a"], mod[:, :, 0] + tok, normg[0], *wa)
    tok = emit("a", [gw1t_a, gw3t_a, gw2_a])
    dmod = jnp.stack([dmod_a, dmod_m + dmod_gate, dmod_b], axis=2)
    dnormg = jnp.stack([dg_a, dg_m, dg_b], axis=0)
    g = PW // 4
    dpool_w = jnp.stack([dbd[i * g:(i + 1) * g, i * g:(i + 1) * g] for i in range(4)], axis=0)
    sm = dict(pool_w=dpool_w, pool_scale=dps[0], dec_f=ddec[:, 0, 0], dec_b=ddec[:, 1, 0], gng=dgng[0],
              conv_dw=ddw[0:CONV_K], conv_b=ddb[0], conv_ln_g=dlng[0], conv_ln_b=dlnb[0])
    return dz, dmod, dnormg, sm, tok


def _small_params(pool_w, pool_scale, dec_f, dec_b, gng, conv_dw, conv_b, lng, lnb):
    return dict(bd=_block_diag(pool_w), pscale=pool_scale.reshape(1, PW), dec_f=_lanes(dec_f), dec_b=_lanes(dec_b),
                gng=gng.reshape(1, RW), dw=jnp.pad(conv_dw, ((0, 1), (0, 0))), db=conv_b.reshape(1, PW),
                lng=lng.reshape(1, PW), lnb=lnb.reshape(1, PW))


_WEIGHTS = ["c_ctx", "w_mod", "b_mod", "norm_g", "ffn_w1", "ffn_w3", "ffn_w2", "w_in", "w_out", "pool_w", "pool_scale",
            "ret_decay_fwd", "ret_decay_bwd", "ret_gn_g", "conv_dw", "conv_b", "conv_ln_g", "conv_ln_b", "final_g"]
_BIG = ["w_mod", "ffn_w1", "ffn_w3", "ffn_w2", "w_in", "w_out"]
_SMALL = [n for n in _WEIGHTS if n not in _BIG]


def _adam_any(w, g, m, v):
    shape = w.shape
    cols = shape[-1] if w.ndim >= 2 else 128
    outs = _adam(w.reshape(-1, cols), g.reshape(-1, cols), m.reshape(-1, cols), v.reshape(-1, cols))
    return [o.reshape(shape) for o in outs]


def _step(a):
    x, c, ctx = a["x"], a["c"], a["ctx"]
    B = x.shape[0]
    nex = N_DEV * B
    assert nex + B <= MROWS and ctx.shape[1] == LC and x.shape[1] % TM == 0
    xi, yi, ci = _coords()
    me = 4 * xi + 2 * yi + ci
    chip = 2 * xi + yi
    ncol = a["w_mod"].shape[2]

    def t_bf16(w):
        return jnp.swapaxes(w, -1, -2).astype(BF16)

    w1t, w3t, w2 = t_bf16(a["ffn_w1"]), t_bf16(a["ffn_w3"]), a["ffn_w2"].astype(BF16)
    wint, wout = t_bf16(a["w_in"]), a["w_out"].astype(BF16)
    units = []
    for l in range(2):
        units += [jnp.stack([w1t[l, 0], w3t[l, 0], w2[l, 0]]), wint[l][None], wout[l][None],
                  jnp.stack([w1t[l, 1], w3t[l, 1], w2[l, 1]])]

    shapes1 = [(B, D), (2, 3, D // N_CHIP), (2, CONV_K, PW // N_CHIP)]
    g1 = _all_gather_small(_pack([c, a["norm_g"], a["conv_dw"]]))
    per = [_unpack(g1[d], shapes1) for d in range(N_DEV)]
    c_all = jnp.concatenate([per[d][0] for d in range(N_DEV)], axis=0)
    norm_g_full = jnp.concatenate([per[2 * s][1] for s in range(N_CHIP)], axis=-1)
    conv_dw_full = jnp.concatenate([per[2 * s][2] for s in range(N_CHIP)], axis=-1)
    cctx = a["c_ctx"].reshape(1, D)
    c24 = jnp.concatenate([c_all] + [cctx] * B + [jnp.zeros((MROWS - nex - B, D), F32)], axis=0)

    bsh = lax.dynamic_slice(a["b_mod"], (0, chip * ncol), (2, ncol)).reshape(2, 1, ncol)
    mod_raw = _mod_fwd(c24, a["w_mod"], bsh)
    g2 = _all_gather_small(_pack([mod_raw]))
    mod_full = jnp.concatenate([_unpack(g2[2 * s], [(2, MROWS, ncol)])[0] for s in range(N_CHIP)], axis=-1)
    mods = []
    for l in range(2):
        lat = lax.dynamic_slice(mod_full[l], (B * me, 0), (B, N_MOD * D))
        cx = jnp.broadcast_to(mod_full[l, nex][None], (B, N_MOD * D))
        mods.append(jnp.stack([cx, lat], axis=1).reshape(B, 2, 3, 3, D))

    started = _gather_start(units, mod_full)

    def wget_of(l):
        def wget(stage, after):
            if stage == "m":
                win, wo = _gather_wait(f"m{l}", started, [4 * l + 1, 4 * l + 2], after)
                return (win, 0), (wo, 0)
            (g,) = _gather_wait(f"{stage}{l}", started, [4 * l + (0 if stage == "a" else 3)], after)
            return (g, 0), (g, 1), (g, 2)
        return wget

    smalls = [_small_params(a["pool_w"][l], a["pool_scale"][l], a["ret_decay_fwd"][l], a["ret_decay_bwd"][l],
                            a["ret_gn_g"][l], conv_dw_full[l], a["conv_b"][l], a["conv_ln_g"][l], a["conv_ln_b"][l])
              for l in range(2)]
    normgs = [norm_g_full[l].reshape(3, 1, D) for l in range(2)]
    z = jnp.concatenate([ctx, x], axis=1)
    saved = []
    for l in range(2):
        z, sv = _layer_fwd(z, mods[l], normgs[l], wget_of(l), smalls[l])
        saved.append(sv)
    dz, dfinal_g, loss_part = _head(z, a["loss_target"], a["final_g"].reshape(1, D))

    scattering, swapping, reduced = [], [], {}

    def reduce_previous(after):
        tag, st, rows = scattering.pop()
        grads, others = _scatter_wait(tag, st, after)
        own = jnp.concatenate([lax.dynamic_slice(g, (chip, ci, 0, 0), (1, 1) + g.shape[2:]).reshape(g.shape[2:])
                               for g in grads], axis=0)
        mine = _sum_pieces(own, others)
        sw = _swap_start(tag, mine, rows, mine)
        if swapping:
            ptag, psw = swapping.pop()
            reduced[ptag] = _swap_wait(ptag, psw, sw[4])
        swapping.append((tag, sw))
        return sw[4]

    def emit_of(l):
        def emit(stage, grads):
            grads = [g.reshape(N_CHIP, 2, g.shape[0] // (2 * N_CHIP), D) for g in grads]
            st = _scatter_start(f"{stage}{l}", grads, grads[0])
            tok = st[4][0, 0]
            if scattering:
                tok = tok + reduce_previous(st[4])[0, 0]
            scattering.append((f"{stage}{l}", st, _piece_rows(grads)))
            return tok
        return emit

    back = [None, None]
    tok = jnp.zeros((), F32)
    for l in (1, 0):
        dz, dmod, dnormg, sm, tok = _layer_bwd(dz, saved[l], mods[l], normgs[l], smalls[l], emit_of(l), tok)
        back[l] = (dmod, dnormg, None, sm)
    grad_x = dz[:, LC:]
    grads = {}

    dmods = [back[l][0].reshape(B, 2, N_MOD * D) for l in range(2)]
    pack_a = _pack([jnp.stack([dm[:, 1] for dm in dmods])])
    ka = pack_a.shape[0]
    sm = [back[l][3] for l in range(2)]
    sum_list = [jnp.stack([dm[:, 0] for dm in dmods]), jnp.stack([back[l][1][:, 0] for l in range(2)])]
    sm_keys = ["pool_w", "pool_scale", "dec_f", "dec_b", "gng", "conv_dw", "conv_b", "conv_ln_g", "conv_ln_b"]
    sum_list += [jnp.stack([sm[l][k] for l in range(2)]) for k in sm_keys]
    sum_list += [dfinal_g[0], loss_part[0, 0:1]]
    sum_shapes = [s.shape for s in sum_list]
    g3 = _all_gather_small(jnp.concatenate([pack_a, _pack(sum_list)], axis=0))
    dmx_all = jnp.concatenate([_unpack(g3[d, :ka], [(2, B, N_MOD * D)])[0] for d in range(N_DEV)], axis=1)
    summed = _unpack(_sum_devices(g3[:, ka:]), sum_shapes)
    dmy, dnorm_full = summed[0], summed[1]
    sgrad = dict(zip(sm_keys, summed[2:2 + len(sm_keys)]))
    loss = summed[-1].reshape(())

    dmod24 = jnp.concatenate([dmx_all, dmy, jnp.zeros((2, MROWS - nex - B, N_MOD * D), F32)], axis=1)
    dmod_my = lax.dynamic_slice(dmod24, (0, 0, chip * ncol), (2, MROWS, ncol))
    grads["w_mod"], dsc = _mod_bwd(c24, dmod_my, a["w_mod"])
    g4 = _all_gather_small(_pack([dsc[nex:nex + 8]]))
    dsc_parts = jnp.stack([_unpack(g4[2 * s], [(8, D)])[0] for s in range(N_CHIP)])
    dbmod, dcctx = _bmod_cctx_grad(dmod24, dsc_parts, cctx)

    grads["c_ctx"] = dcctx[0]
    grads["b_mod"] = dbmod.reshape(2, N_MOD * D)
    grads["norm_g"] = lax.dynamic_slice(dnorm_full, (0, 0, chip * (D // N_CHIP)), (2, 3, D // N_CHIP))
    grads["pool_w"] = sgrad["pool_w"]
    grads["pool_scale"] = sgrad["pool_scale"]
    grads["ret_decay_fwd"] = sgrad["dec_f"]
    grads["ret_decay_bwd"] = sgrad["dec_b"]
    grads["ret_gn_g"] = sgrad["gng"]
    grads["conv_dw"] = lax.dynamic_slice(sgrad["conv_dw"], (0, 0, chip * (PW // N_CHIP)), (2, CONV_K, PW // N_CHIP))
    grads["conv_b"] = sgrad["conv_b"]
    grads["conv_ln_g"] = sgrad["conv_ln_g"]
    grads["conv_ln_b"] = sgrad["conv_ln_b"]
    grads["final_g"] = summed[-2]

    delta, new_m, new_v = {}, {}, {}
    delta["w_mod"], new_m["w_mod"], new_v["w_mod"] = _adam_any(a["w_mod"], grads["w_mod"], a["m_w_mod"], a["v_w_mod"])
    shapes_s = [a[n].shape for n in _SMALL]
    packed = _adam(_pack([a[n] for n in _SMALL]), _pack([grads[n] for n in _SMALL]),
                   _pack([a["m_" + n] for n in _SMALL]), _pack([a["v_" + n] for n in _SMALL]))
    for res, out in zip(packed, (delta, new_m, new_v)):
        for n, val in zip(_SMALL, _unpack(res, shapes_s)):
            out[n] = val

    def layer_grads(l):
        ffn = [[h.reshape(-1, D) for h in reduced[f"{stage}{l}"]] for stage in "ab"]
        win, wo = [h.reshape(-1, D) for h in reduced[f"m{l}"]]
        return dict(ffn_w1=jnp.stack([ffn[i][0] for i in range(2)]), ffn_w3=jnp.stack([ffn[i][1] for i in range(2)]),
                    ffn_w2=jnp.stack([ffn[i][2] for i in range(2)]), w_in=win, w_out=wo)

    turned = ("ffn_w1", "ffn_w3", "w_in")

    def adam_of(n, g, layer, prev):
        t = (lambda u: jnp.swapaxes(u, -1, -2)) if n in turned else (lambda u: u)
        return _adam_layer(t(a[n]), g, t(a["m_" + n]), t(a["v_" + n]), layer, prev)

    g1 = layer_grads(1)
    half = {n: adam_of(n, g1[n], 1, None) for n in _BIG[1:]}

    last = reduce_previous([delta["w_mod"], packed[0]] + [half[n][1] for n in _BIG[1:]])
    tag, sw = swapping.pop()
    reduced[tag] = _swap_wait(tag, sw, last)
    g0 = layer_grads(0)
    for n in _BIG[1:]:
        res = adam_of(n, g0[n], 0, half[n])
        grads[n], delta[n], new_m[n], new_v[n] = [jnp.swapaxes(r, -1, -2) for r in res] if n in turned else res
    return (loss, grad_x, *[grads[n] for n in _WEIGHTS], *[delta[n] for n in _WEIGHTS],
            *[new_m[n] for n in _WEIGHTS], *[new_v[n] for n in _WEIGHTS])


def kernel(x, c, ctx, c_ctx, w_mod, b_mod, norm_g, ffn_w1, ffn_w3, ffn_w2, w_in, w_out, pool_w, pool_scale, ret_decay_fwd, ret_decay_bwd, ret_gn_g, conv_dw, conv_b, conv_ln_g, conv_ln_b, final_g, loss_target, m_c_ctx, m_w_mod, m_b_mod, m_norm_g, m_ffn_w1, m_ffn_w3, m_ffn_w2, m_w_in, m_w_out, m_pool_w, m_pool_scale, m_ret_decay_fwd, m_ret_decay_bwd, m_ret_gn_g, m_conv_dw, m_conv_b, m_conv_ln_g, m_conv_ln_b, m_final_g, v_c_ctx, v_w_mod, v_b_mod, v_norm_g, v_ffn_w1, v_ffn_w3, v_ffn_w2, v_w_in, v_w_out, v_pool_w, v_pool_scale, v_ret_decay_fwd, v_ret_decay_bwd, v_ret_gn_g, v_conv_dw, v_conv_b, v_conv_ln_g, v_conv_ln_b, v_final_g):
    return _step(dict(locals()))
```

```python
import functools

import jax
import jax.numpy as jnp
from jax import lax
from jax.experimental import pallas as pl
from jax.experimental.pallas import tpu as pltpu

F32 = jnp.float32
BF16 = jnp.bfloat16

D = 1024
F = 2816
FH = 1408
N_MOD = 9
LC = 256
TM = 256
HD = 128
NH = 4
RW = 512
PW = 256
CONV_K = 31
GRID_W = 64
EPS = 1e-6
K_SCALE = HD ** -0.5
N_DEV = 8
N_CHIP = 4
SLAB = F // N_CHIP
HSLAB = SLAB // 2
OSLAB = D // N_CHIP
HOSLAB = OSLAB // 2
VMEM_BIG = 60 * 1024 * 1024
MESH = pl.DeviceIdType.MESH

ADAM_LR = 0.001
ADAM_B1 = 0.9
ADAM_B2 = 0.999
ADAM_EPS = 1e-08
ADAM_WD = 0.01
ADAM_STEP = 10


def _nt(a, b):
    return lax.dot_general(a, b, (((1,), (1,)), ((), ())), preferred_element_type=F32)


def _nn(a, b):
    return lax.dot_general(a, b, (((1,), (0,)), ((), ())), preferred_element_type=F32)


def _tn(a, b):
    return lax.dot_general(a, b, (((0,), (0,)), ((), ())), preferred_element_type=F32)


def _params(vmem=None, sem=None):
    return pltpu.CompilerParams(dimension_semantics=sem, vmem_limit_bytes=vmem)


def _rms_mod(z, g, shift, scale):
    y = z * lax.rsqrt(jnp.mean(z * z, axis=-1, keepdims=True) + EPS)
    return (y * g) * (1.0 + scale) + shift


def _acc(ref, val, first):
    @pl.when(first)
    def _():
        ref[...] = val

    @pl.when(jnp.logical_not(first))
    def _():
        ref[...] += val


def _tok(width):
    return pl.BlockSpec((None, TM, width), lambda b, t: (b, t, 0))


def _modspec():
    return pl.BlockSpec((None, None, 3, D), lambda b, t: (b, jnp.minimum(t, 1), 0, 0))


def _const(shape):
    nd = len(shape)
    return pl.BlockSpec(shape, lambda b, t: (0,) * nd)


def _wspec(w):
    stack, idx = w
    return pl.BlockSpec((None,) + stack.shape[1:], lambda b, t: (idx, 0, 0), pipeline_mode=pl.Buffered(1))


def _ffn_fwd(z, mod, g, w1t, w3t, w2):
    B, S, _ = z.shape

    def body(z_ref, mod_ref, g_ref, w1_ref, w3_ref, w2_ref, zo_ref, f_ref):
        zt = z_ref[...]
        h = _rms_mod(zt, g_ref[...], mod_ref[0:1, :], mod_ref[1:2, :]).astype(BF16)
        f = jnp.zeros((TM, D), F32)
        for c in range(F // FH):
            rows = slice(c * FH, (c + 1) * FH)
            u1 = _nt(h, w1_ref[rows, :])
            u3 = _nt(h, w3_ref[rows, :])
            a = (u1 * jax.nn.sigmoid(u1) * u3).astype(BF16)
            f = f + _nn(a, w2_ref[rows, :])
        f_ref[...] = f
        zo_ref[...] = zt + 0.5 * mod_ref[2:3, :] * f

    return pl.pallas_call(
        body, name="ffn_fwd", grid=(B, S // TM),
        in_specs=[_tok(D), _modspec(), _const((1, D)), _wspec(w1t), _wspec(w3t), _wspec(w2)],
        out_specs=[_tok(D), _tok(D)],
        out_shape=[jax.ShapeDtypeStruct((B, S, D), F32)] * 2,
        compiler_params=_params(VMEM_BIG, ("arbitrary", "arbitrary")),
    )(z, mod, g, w1t[0], w3t[0], w2[0])


def _ffn_bwd(z, dzo, f, mod, g, w1t, w3t, w2, each=None):
    B, S, _ = z.shape

    def body(z_ref, dzo_ref, f_ref, mod_ref, g_ref, w1_ref, w3_ref, w2_ref,
             dz_ref, h_ref, du1_ref, du3_ref, a_ref, do_ref, dmod_ref, dg_ref):
        b, t = pl.program_id(0), pl.program_id(1)
        zt = z_ref[...]
        dzo = dzo_ref[...]
        gate = mod_ref[2:3, :]
        h32, vjp_h = jax.vjp(_rms_mod, zt, g_ref[...], mod_ref[0:1, :], mod_ref[1:2, :])
        h = h32.astype(BF16)
        h_ref[...] = h
        do = (0.5 * gate * dzo).astype(BF16)
        do_ref[...] = do
        dgate = jnp.sum(0.5 * f_ref[...] * dzo, axis=0, keepdims=True)
        dh = jnp.zeros((TM, D), F32)
        for c in range(F // FH):
            rows = slice(c * FH, (c + 1) * FH)
            u1 = _nt(h, w1_ref[rows, :])
            u3 = _nt(h, w3_ref[rows, :])
            sg = jax.nn.sigmoid(u1)
            s = u1 * sg
            a_ref[:, rows] = (s * u3).astype(BF16)
            da = _nt(do, w2_ref[rows, :])
            du3 = (da * s).astype(BF16)
            du1 = (da * u3 * (sg * (1.0 + u1 * (1.0 - sg)))).astype(BF16)
            du1_ref[:, rows] = du1
            du3_ref[:, rows] = du3
            dh = dh + _nn(du1, w1_ref[rows, :]) + _nn(du3, w3_ref[rows, :])
        dz_h, dg, dshift, dscale = vjp_h(dh)
        dz_ref[...] = dzo + dz_h
        _acc(dmod_ref, jnp.concatenate([dshift, dscale, dgate], axis=0), t <= 1)
        _acc(dg_ref, dg, jnp.logical_and(b == 0, t == 0))

    T = B * S
    outs = pl.pallas_call(
        body, name="ffn_bwd", grid=(B, S // TM),
        in_specs=[_tok(D), _tok(D), _tok(D), _modspec(), _const((1, D)), _wspec(w1t), _wspec(w3t), _wspec(w2)],
        out_specs=[_tok(D), _tok(D), _tok(F), _tok(F), _tok(F), _tok(D), _modspec(), _const((1, D))],
        out_shape=[jax.ShapeDtypeStruct((B, S, D), F32), jax.ShapeDtypeStruct((B, S, D), BF16),
                   jax.ShapeDtypeStruct((B, S, F), BF16), jax.ShapeDtypeStruct((B, S, F), BF16),
                   jax.ShapeDtypeStruct((B, S, F), BF16), jax.ShapeDtypeStruct((B, S, D), BF16),
                   jax.ShapeDtypeStruct((B, 2, 3, D), F32), jax.ShapeDtypeStruct((1, D), F32)],
        compiler_params=_params(VMEM_BIG, ("arbitrary", "arbitrary")),
    )(z, dzo, f, mod, g, w1t[0], w3t[0], w2[0])
    dz, h, du1, du3, a, do, dmod, dg = outs
    grads = []
    for lhs, rhs in ((du1, h), (du3, h), (a, do)):
        grads.append(_tn_matmul(lhs.reshape(T, F), rhs.reshape(T, D)))
        if each is not None:
            each(grads[-1])
    return (dz, dmod, dg, *grads)


def _tn_matmul(a, b):
    T, M = a.shape
    N = b.shape[1]
    MB = FH if M > FH else M
    TT = next(t for t in (1152, 1024, 768, 512, TM) if T % t == 0)
    nt = T // TT

    def body(a_ref, b_ref, o_ref, acc_ref):
        t = pl.program_id(1)
        prod = _tn(a_ref[...], b_ref[...])
        _acc(acc_ref, prod, t == 0)

        @pl.when(t == nt - 1)
        def _():
            o_ref[...] = acc_ref[...].astype(BF16)

    return pl.pallas_call(
        body, name="tn_matmul", grid=(M // MB, nt),
        in_specs=[pl.BlockSpec((TT, MB), lambda i, t: (t, i)), pl.BlockSpec((TT, N), lambda i, t: (t, 0))],
        out_specs=pl.BlockSpec((MB, N), lambda i, t: (i, 0)),
        out_shape=jax.ShapeDtypeStruct((M, N), BF16),
        scratch_shapes=[pltpu.VMEM((MB, N), F32)],
        compiler_params=_params(VMEM_BIG, ("arbitrary", "arbitrary")),
    )(a, b)


def _swap32(x):
    n = x.shape[1]
    lane = lax.broadcasted_iota(jnp.int32, x.shape, 1)
    return jnp.where((lane % 64) < 32, pltpu.roll(x, n - 32, 1), pltpu.roll(x, 32, 1))


def _rope(x, cos, sin):
    return x * cos + _swap32(x) * sin


def _rope_t(dy, cos, sin):
    return dy * cos + _swap32(dy * sin)


def _rope_tables(S):
    L = S - LC
    n_freq = HD // 4
    inv = 10000.0 ** (-jnp.arange(n_freq, dtype=F32) / n_freq)
    i = jnp.arange(L)
    row = (i // GRID_W).astype(F32)
    col = (i % GRID_W).astype(F32)
    ang_r = row[:, None] * inv[None]
    ang_c = col[:, None] * inv[None]
    ang = jnp.concatenate([ang_r, ang_r, ang_c, ang_c], axis=1)
    ang = jnp.concatenate([jnp.zeros((LC, HD), F32), ang], axis=0)
    sign = jnp.where((jnp.arange(HD) % 64) < 32, -1.0, 1.0).astype(F32)
    return jnp.cos(ang), jnp.sin(ang) * sign[None]


def _tabspec():
    return pl.BlockSpec((TM, HD), lambda b, t: (t, 0))


def _mix_in_fwd(z, mod, g, wint, cos, sin):
    B, S, _ = z.shape

    def body(z_ref, mod_ref, g_ref, w_ref, cos_ref, sin_ref, pp_ref, q_ref, k_ref, v_ref, gg_ref, pc_ref):
        h = _rms_mod(z_ref[...], g_ref[...], mod_ref[0:1, :], mod_ref[1:2, :]).astype(BF16)
        p = _nt(h, w_ref[...])
        cos = jnp.tile(cos_ref[...], (1, NH))
        sin = jnp.tile(sin_ref[...], (1, NH))
        pp_ref[...] = p[:, 0:PW]
        q_ref[...] = _rope(p[:, PW:PW + RW], cos, sin)
        k_ref[...] = _rope(p[:, PW + RW:PW + 2 * RW], cos, sin) * K_SCALE
        v_ref[...] = p[:, PW + 2 * RW:PW + 3 * RW]
        gg_ref[...] = p[:, PW + 3 * RW:PW + 4 * RW]
        pc_ref[...] = p[:, PW + 4 * RW:]

    return pl.pallas_call(
        body, name="mix_in_fwd", grid=(B, S // TM),
        in_specs=[_tok(D), _modspec(), _const((1, D)), _wspec(wint), _tabspec(), _tabspec()],
        out_specs=[_tok(PW), _tok(RW), _tok(RW), _tok(RW), _tok(RW), _tok(2 * PW)],
        out_shape=[jax.ShapeDtypeStruct((B, S, PW), F32)] + [jax.ShapeDtypeStruct((B, S, RW), F32)] * 5,
        compiler_params=_params(VMEM_BIG, ("arbitrary", "arbitrary")),
    )(z, mod, g, wint[0], cos, sin)


def _mix_in_bwd(z, dzo, dpp, dqa, dqb, dka, dkb, dva, dvb, dgg, dpc, mod, g, wint, cos, sin):
    B, S, _ = z.shape

    def body(z_ref, dzo_ref, dpp_ref, dqa_ref, dqb_ref, dka_ref, dkb_ref, dva_ref, dvb_ref, dgg_ref, dpc_ref, mod_ref, g_ref,
             w_ref, cos_ref, sin_ref, dz_ref, h_ref, dp_ref, dmod_ref, dg_ref):
        b, t = pl.program_id(0), pl.program_id(1)
        h32, vjp_h = jax.vjp(_rms_mod, z_ref[...], g_ref[...], mod_ref[0:1, :], mod_ref[1:2, :])
        h_ref[...] = h32.astype(BF16)
        cos = jnp.tile(cos_ref[...], (1, NH))
        sin = jnp.tile(sin_ref[...], (1, NH))
        dq = _rope_t(dqa_ref[...] + dqb_ref[...], cos, sin)
        dk = _rope_t(dka_ref[...] + dkb_ref[...], cos, sin) * K_SCALE
        dp = jnp.concatenate([dpp_ref[...], dq, dk, dva_ref[...] + dvb_ref[...], dgg_ref[...], dpc_ref[...]],
                             axis=1).astype(BF16)
        dp_ref[...] = dp
        dh = _nn(dp, w_ref[...])
        dz_h, dg, dshift, dscale = vjp_h(dh)
        dz_ref[...] = dzo_ref[...] + dz_h
        _acc(dmod_ref, jnp.concatenate([dshift, dscale, jnp.zeros_like(dshift)], axis=0), t <= 1)
        _acc(dg_ref, dg, jnp.logical_and(b == 0, t == 0))

    return pl.pallas_call(
        body, name="mix_in_bwd", grid=(B, S // TM),
        in_specs=[_tok(D), _tok(D), _tok(PW)] + [_tok(RW)] * 7 + [_tok(2 * PW), _modspec(), _const((1, D)), _wspec(wint),
                                                                  _tabspec(), _tabspec()],
        out_specs=[_tok(D), _tok(D), _tok(F), _modspec(), _const((1, D))],
        out_shape=[jax.ShapeDtypeStruct((B, S, D), F32), jax.ShapeDtypeStruct((B, S, D), BF16),
                   jax.ShapeDtypeStruct((B, S, F), BF16), jax.ShapeDtypeStruct((B, 2, 3, D), F32),
                   jax.ShapeDtypeStruct((1, D), F32)],
        compiler_params=_params(VMEM_BIG, ("arbitrary", "arbitrary")),
    )(z, dzo, dpp, dqa, dqb, dka, dkb, dva, dvb, dgg, dpc, mod, g, wint[0], cos, sin)


def _log_sigmoid(x):
    return jnp.minimum(x, 0.0) - jnp.log(1.0 + jnp.exp(-jnp.abs(x)))


def _retention(a, b, c, dec_a, dec_b, sched_a, sched_b):
    B, S, _ = a.shape
    C = TM

    def body(a_ref, b_ref, c_ref, da_ref, db_ref, oa_ref, ob_ref):
        ii = lax.broadcasted_iota(jnp.int32, (C, C), 0)
        jj = lax.broadcasted_iota(jnp.int32, (C, C), 1)
        pos = lax.broadcasted_iota(jnp.int32, (C, 1), 0).astype(F32)
        for dec_ref, o_ref, (order, causal, strict) in ((da_ref, oa_ref, sched_a), (db_ref, ob_ref, sched_b)):
            lg = _log_sigmoid(dec_ref[...])
            lg1 = lg[:, 0:1]
            dist = ((ii - jj) if causal else (jj - ii)).astype(F32)
            mask = (dist > 0.0) if strict else (dist >= 0.0)
            decay = jnp.where(mask, jnp.exp(jnp.maximum(dist, 0.0) * lg1), 0.0)
            p = pos if causal else (C - 1.0 - pos)
            w_q = jnp.exp((p + 1.0) * lg1)
            w_k = jnp.exp((C - 1.0 - p) * lg1)
            chunk_decay = jnp.exp(C * lg)
            state = jnp.zeros((HD, HD), F32)
            for n in order:
                rows = pl.ds(n * C, C)
                at, bt, ct = a_ref[rows, :], b_ref[rows, :], c_ref[rows, :]
                cb = ct.astype(BF16)
                scores = _nt(at.astype(BF16), bt.astype(BF16)) * decay
                o = _nn(scores.astype(BF16), cb)
                o = o + _nn((at * w_q).astype(BF16), state.astype(BF16))
                o_ref[rows, :] = o
                state = chunk_decay * state + _tn((bt * w_k).astype(BF16), cb)

    seq = pl.BlockSpec((None, S, HD), lambda b, h: (b, 0, h))
    dspec = pl.BlockSpec((None, 1, HD), lambda b, h: (h, 0, 0))
    return pl.pallas_call(
        body, name="retention", grid=(B, NH),
        in_specs=[seq, seq, seq, dspec, dspec], out_specs=[seq, seq],
        out_shape=[jax.ShapeDtypeStruct((B, S, RW), F32)] * 2,
        compiler_params=_params(VMEM_BIG, ("arbitrary", "arbitrary")),
    )(a, b, c, dec_a, dec_b)


def _retention_ddecay(q, k, v, do, dec_a, dec_b, sched_a, sched_b):
    B, S, _ = q.shape
    C = TM

    def body(q_ref, k_ref, v_ref, do_ref, da_ref, db_ref, o_ref):
        ii = lax.broadcasted_iota(jnp.int32, (C, C), 0)
        jj = lax.broadcasted_iota(jnp.int32, (C, C), 1)
        pos = lax.broadcasted_iota(jnp.int32, (C, 1), 0).astype(F32)
        vals = []
        for dec_ref, (order, causal, strict) in ((da_ref, sched_a), (db_ref, sched_b)):
            x = dec_ref[...]
            lg = _log_sigmoid(x)
            lg1 = lg[:, 0:1]
            dist = ((ii - jj) if causal else (jj - ii)).astype(F32)
            mask = (dist > 0.0) if strict else (dist >= 0.0)
            ddecay = jnp.where(mask, dist * jnp.exp(jnp.maximum(dist, 0.0) * lg1), 0.0)
            p = pos if causal else (C - 1.0 - pos)
            w_q = jnp.exp((p + 1.0) * lg1)
            w_k = jnp.exp((C - 1.0 - p) * lg1)
            chunk_decay = jnp.exp(C * lg)
            state = jnp.zeros((HD, HD), F32)
            dstate = jnp.zeros((HD, HD), F32)
            tot = jnp.zeros((), F32)
            for n in order:
                rows = pl.ds(n * C, C)
                qt, kt, vt, dot = q_ref[rows, :], k_ref[rows, :], v_ref[rows, :], do_ref[rows, :]
                vb = vt.astype(BF16)
                scores = _nt(qt.astype(BF16), kt.astype(BF16))
                dscores = _nt(dot.astype(BF16), vb)
                qw = (qt * w_q).astype(BF16)
                cross = _nn(qw, state.astype(BF16))
                dcross = _nn(qw, dstate.astype(BF16))
                tot = tot + jnp.sum(scores * dscores * ddecay) + jnp.sum(((p + 1.0) * cross + dcross) * dot)
                kv = _tn((kt * w_k).astype(BF16), vb)
                dkv = _tn((kt * ((C - 1.0 - p) * w_k)).astype(BF16), vb)
                dstate = chunk_decay * (dstate + C * state) + dkv
                state = chunk_decay * state + kv
            vals.append(tot * jax.nn.sigmoid(-x))
        row = lax.broadcasted_iota(jnp.int32, (8, HD), 0)
        tile = jnp.where(row == 0, vals[0], 0.0) + jnp.where(row == 1, vals[1], 0.0)
        _acc(o_ref, tile, pl.program_id(1) == 0)

    seq = pl.BlockSpec((None, S, HD), lambda h, b: (b, 0, h))
    dspec = pl.BlockSpec((None, 1, HD), lambda h, b: (h, 0, 0))
    return pl.pallas_call(
        body, name="retention_ddecay", grid=(NH, B),
        in_specs=[seq, seq, seq, seq, dspec, dspec], out_specs=pl.BlockSpec((None, 8, HD), lambda h, b: (h, 0, 0)),
        out_shape=jax.ShapeDtypeStruct((NH, 8, HD), F32),
        compiler_params=_params(VMEM_BIG, ("arbitrary", "arbitrary")),
    )(q, k, v, do, dec_a, dec_b)


def _schedules(S):
    n = S // TM
    lat_up = tuple(range(1, n))
    lat_down = tuple(range(n - 1, 0, -1))
    fwd = (((0,) + lat_up, True, False), ((0,) + lat_down, False, True))
    bwd = ((lat_down + (0,), False, False), (lat_up + (0,), True, True))
    return fwd, bwd


def _shift_rows(x, d):
    if d == 0:
        return x
    S = x.shape[0]
    t = lax.broadcasted_iota(jnp.int32, x.shape, 0)
    tt = t + d
    lo = jnp.where(t < LC, 0, LC)
    hi = jnp.where(t < LC, LC, S)
    return jnp.where((tt >= lo) & (tt < hi), pltpu.roll(x, (-d) % S, 0), 0.0)


@functools.partial(jax.custom_vjp, nondiff_argnums=(1,))
def _shift(x, d):
    return _shift_rows(x, d)


_shift.defvjp(lambda x, d: (_shift_rows(x, d), None), lambda d, _, g: (_shift_rows(g, -d),))


def _pool_fn(p, bd, pscale):
    lane = lax.broadcasted_iota(jnp.int32, p.shape, 1)
    grp = lane // (PW // 4)
    half = jnp.where(grp == 0, 1, jnp.where(grp == 1, 2, jnp.where(grp == 2, 4, 8)))
    ones = jnp.ones(p.shape, F32)
    acc = jnp.zeros(p.shape, F32)
    cnt = jnp.zeros(p.shape, F32)
    for d in range(-8, 8):
        inwin = ((d >= -half) & (d < half)).astype(F32)
        acc = acc + _shift(p, d) * inwin
        cnt = cnt + _shift_rows(ones, d) * inwin
    pooled = acc / cnt - p
    mixed = _nn(pooled.astype(BF16), bd.astype(BF16))
    return mixed * pscale


def _dwconv_raw(zc, dw):
    y = jnp.zeros(zc.shape, F32)
    for k in range(CONV_K):
        y = y + _shift_rows(zc, k - CONV_K // 2) * dw[k:k + 1, :]
    return y


@jax.custom_vjp
def _dwconv(zc, dw):
    return _dwconv_raw(zc, dw)


def _dwconv_fwd(zc, dw):
    return _dwconv_raw(zc, dw), (zc, dw)


def _dwconv_bwd(res, g):
    zc, dw = res
    dz = jnp.zeros(zc.shape, F32)
    ddw = jnp.zeros(dw.shape, F32)
    row = lax.broadcasted_iota(jnp.int32, dw.shape, 0)
    for k in range(CONV_K):
        dz = dz + _shift_rows(g, CONV_K // 2 - k) * dw[k:k + 1, :]
        r = jnp.sum(g * _shift_rows(zc, k - CONV_K // 2), axis=0, keepdims=True)
        ddw = ddw + jnp.where(row == k, r, 0.0)
    return dz, ddw


_dwconv.defvjp(_dwconv_fwd, _dwconv_bwd)


def _conv_fn(u, dw, db):
    zc = u[:, :PW] * jax.nn.sigmoid(u[:, PW:])
    return _dwconv(zc, dw) + db


def _ln_swish(y, lng, lnb):
    mu = jnp.mean(y, axis=-1, keepdims=True)
    yc = y - mu
    var = jnp.mean(yc * yc, axis=-1, keepdims=True)
    yn = yc * lax.rsqrt(var + EPS) * lng + lnb
    return yn * jax.nn.sigmoid(yn)


def _seq(shape, single=False):
    return pl.BlockSpec((None,) + shape, lambda b: (b, 0, 0), pipeline_mode=pl.Buffered(1) if single else None)


def _c1(shape):
    nd = len(shape)
    return pl.BlockSpec(shape, lambda b: (0,) * nd)


def _seq_apply(fn, name, xs, consts, width):
    B, S, w = xs.shape

    def body(x_ref, *refs):
        refs[-1][...] = fn(x_ref[...], *[r[...] for r in refs[:-1]])

    return pl.pallas_call(
        body, name=name, grid=(B,),
        in_specs=[_seq((S, w))] + [_c1(c.shape) for c in consts], out_specs=_seq((S, width)),
        out_shape=jax.ShapeDtypeStruct((B, S, width), F32),
        compiler_params=_params(VMEM_BIG, ("arbitrary",)),
    )(xs, *consts)


def _seq_vjp(fn, name, xs, consts, dout):
    B, S, w = xs.shape
    n = len(consts)

    def body(x_ref, d_ref, *refs):
        first = pl.program_id(0) == 0
        _, vjp = jax.vjp(fn, x_ref[...], *[r[...] for r in refs[:n]])
        grads = vjp(d_ref[...])
        refs[n][...] = grads[0]
        for ref, val in zip(refs[n + 1:], grads[1:]):
            _acc(ref, val, first)

    return pl.pallas_call(
        body, name=name, grid=(B,),
        in_specs=[_seq((S, w), True), _seq((S, dout.shape[2]), True)] + [_c1(c.shape) for c in consts],
        out_specs=[_seq((S, w))] + [_c1(c.shape) for c in consts],
        out_shape=[jax.ShapeDtypeStruct((B, S, w), F32)] + [jax.ShapeDtypeStruct(c.shape, F32) for c in consts],
        compiler_params=_params(VMEM_BIG, ("arbitrary",)),
    )(xs, dout, *consts)


def _pool_conv_fwd(pp, pc, bd, pscale, dw, db):
    return (_seq_apply(_pool_fn, "pool_fwd", pp, (bd, pscale), PW),
            _seq_apply(_conv_fn, "conv_fwd", pc, (dw, db), PW))


def _pool_conv_bwd(pp, pc, dpo, dco, bd, pscale, dw, db):
    dpp, dbd, dps = _seq_vjp(_pool_fn, "pool_bwd", pp, (bd, pscale), dpo)
    dpc, ddw, ddb = _seq_vjp(_conv_fn, "conv_bwd", pc, (dw, db), dco)
    return dpp, dpc, dbd, dps, ddw, ddb


def _cat_fn(po, oa, ob, gg, co, gng, lng, lnb):
    o = oa + ob
    outs = []
    for h in range(NH):
        oh = o[:, h * HD:(h + 1) * HD]
        mu = jnp.mean(oh, axis=-1, keepdims=True)
        oc = oh - mu
        var = jnp.mean(oc * oc, axis=-1, keepdims=True)
        outs.append(oc * lax.rsqrt(var + EPS))
    ret = jnp.concatenate(outs, axis=1) * gng * (gg * jax.nn.sigmoid(gg))
    return jnp.concatenate([po, ret, _ln_swish(co, lng, lnb)], axis=1)


def _mix_out_fwd(z, po, oa, ob, gg, co, ro, mod, wout):
    B, S, _ = z.shape

    def body(z_ref, po_ref, oa_ref, ob_ref, gg_ref, co_ref, gn_ref, lg_ref, lb_ref, mod_ref, w_ref, zo_ref, out_ref):
        cat = _cat_fn(po_ref[...], oa_ref[...], ob_ref[...], gg_ref[...], co_ref[...], gn_ref[...], lg_ref[...], lb_ref[...])
        out = _nn(cat.astype(BF16), w_ref[...])
        out_ref[...] = out
        zo_ref[...] = z_ref[...] + mod_ref[2:3, :] * out

    return pl.pallas_call(
        body, name="mix_out_fwd", grid=(B, S // TM),
        in_specs=[_tok(D), _tok(PW), _tok(RW), _tok(RW), _tok(RW), _tok(PW), _const((1, RW)), _const((1, PW)),
                  _const((1, PW)), _modspec(), _wspec(wout)],
        out_specs=[_tok(D), _tok(D)],
        out_shape=[jax.ShapeDtypeStruct((B, S, D), F32)] * 2,
        compiler_params=_params(None, ("arbitrary", "arbitrary")),
    )(z, po, oa, ob, gg, co, *ro, mod, wout[0])


def _mix_out_bwd(dzo, out, po, oa, ob, gg, co, ro, mod, wout):
    B, S, _ = dzo.shape

    def body(dzo_ref, out_ref, po_ref, oa_ref, ob_ref, gg_ref, co_ref, gn_ref, lg_ref, lb_ref, mod_ref, w_ref,
             dpo_ref, do_ref, dgg_ref, dco_ref, cat_ref, dout_ref, dmod_ref, dgn_ref, dlg_ref, dlb_ref):
        b, t = pl.program_id(0), pl.program_id(1)
        dzo = dzo_ref[...]
        cat, vjp = jax.vjp(_cat_fn, po_ref[...], oa_ref[...], ob_ref[...], gg_ref[...], co_ref[...], gn_ref[...],
                           lg_ref[...], lb_ref[...])
        cat_ref[...] = cat.astype(BF16)
        dout = (mod_ref[2:3, :] * dzo).astype(BF16)
        dout_ref[...] = dout
        dgate = jnp.sum(out_ref[...] * dzo, axis=0, keepdims=True)
        dcat = _nt(dout, w_ref[...])
        dpo, doa, _, dgg, dco, dgn, dlg, dlb = vjp(dcat)
        dpo_ref[...] = dpo
        do_ref[...] = doa
        dgg_ref[...] = dgg
        dco_ref[...] = dco
        zero = jnp.zeros_like(dgate)
        _acc(dmod_ref, jnp.concatenate([zero, zero, dgate], axis=0), t <= 1)
        first = jnp.logical_and(b == 0, t == 0)
        _acc(dgn_ref, dgn, first)
        _acc(dlg_ref, dlg, first)
        _acc(dlb_ref, dlb, first)

    return pl.pallas_call(
        body, name="mix_out_bwd", grid=(B, S // TM),
        in_specs=[_tok(D), _tok(D), _tok(PW), _tok(RW), _tok(RW), _tok(RW), _tok(PW), _const((1, RW)), _const((1, PW)),
                  _const((1, PW)), _modspec(), _wspec(wout)],
        out_specs=[_tok(PW), _tok(RW), _tok(RW), _tok(PW), _tok(D), _tok(D), _modspec(), _const((1, RW)),
                   _const((1, PW)), _const((1, PW))],
        out_shape=[jax.ShapeDtypeStruct((B, S, PW), F32), jax.ShapeDtypeStruct((B, S, RW), F32),
                   jax.ShapeDtypeStruct((B, S, RW), F32), jax.ShapeDtypeStruct((B, S, PW), F32),
                   jax.ShapeDtypeStruct((B, S, D), BF16), jax.ShapeDtypeStruct((B, S, D), BF16),
                   jax.ShapeDtypeStruct((B, 2, 3, D), F32), jax.ShapeDtypeStruct((1, RW), F32),
                   jax.ShapeDtypeStruct((1, PW), F32), jax.ShapeDtypeStruct((1, PW), F32)],
        compiler_params=_params(None, ("arbitrary", "arbitrary")),
    )(dzo, out, po, oa, ob, gg, co, *ro, mod, wout[0])


def _rms(z, g):
    return z * lax.rsqrt(jnp.mean(z * z, axis=-1, keepdims=True) + EPS) * g


def _head(z, target, fg):
    B, S, _ = z.shape

    def body(z_ref, t_ref, g_ref, dz_ref, dg_ref, loss_ref):
        b, t = pl.program_id(0), pl.program_id(1)
        first = jnp.logical_and(b == 0, t == 0)

        @pl.when(t == 0)
        def _():
            dz_ref[...] = jnp.zeros((TM, D), F32)

        @pl.when(first)
        def _():
            dg_ref[...] = jnp.zeros((1, D), F32)
            loss_ref[...] = jnp.zeros((8, 128), F32)

        @pl.when(t > 0)
        def _():
            y, vjp = jax.vjp(_rms, z_ref[...], g_ref[...])
            err = y - t_ref[...]
            dz, dg = vjp(err * (1.0 / D))
            dz_ref[...] = dz
            dg_ref[...] += dg
            loss_ref[...] += 0.5 * jnp.sum(err * err) * (1.0 / D)

    return pl.pallas_call(
        body, name="head", grid=(B, S // TM),
        in_specs=[_tok(D), pl.BlockSpec((None, TM, D), lambda b, t: (b, jnp.maximum(t - 1, 0), 0)), _const((1, D))],
        out_specs=[_tok(D), _const((1, D)), _const((8, 128))],
        out_shape=[jax.ShapeDtypeStruct((B, S, D), F32), jax.ShapeDtypeStruct((1, D), F32),
                   jax.ShapeDtypeStruct((8, 128), F32)],
        compiler_params=_params(None, ("arbitrary", "arbitrary")),
    )(z, target, fg)


MROWS = 24
MCOL = 768


def _silu(x):
    return x * jax.nn.sigmoid(x)


def _mod_fwd(c24, wmod, bmod):
    ncol = wmod.shape[2]

    def body(c_ref, w_ref, b_ref, o_ref):
        sc = _silu(c_ref[...]).astype(BF16)
        o_ref[...] = _nn(sc, w_ref[...].astype(BF16)) + b_ref[...]

    return pl.pallas_call(
        body, name="mod_fwd", grid=(2, ncol // MCOL),
        in_specs=[pl.BlockSpec((MROWS, D), lambda l, j: (0, 0)), pl.BlockSpec((None, D, MCOL), lambda l, j: (l, 0, j)),
                  pl.BlockSpec((None, 1, MCOL), lambda l, j: (l, 0, j))],
        out_specs=pl.BlockSpec((None, MROWS, MCOL), lambda l, j: (l, 0, j)),
        out_shape=jax.ShapeDtypeStruct((2, MROWS, ncol), F32),
        compiler_params=_params(None, ("arbitrary", "arbitrary")),
    )(c24, wmod, bmod)


def _mod_bwd(c24, dmod, wmod):
    ncol = wmod.shape[2]

    def body(c_ref, d_ref, w_ref, dw_ref, dsc_ref):
        l, j = pl.program_id(0), pl.program_id(1)
        sc = _silu(c_ref[...]).astype(BF16)
        dm = d_ref[...].astype(BF16)
        dw_ref[...] = _tn(sc, dm)
        _acc(dsc_ref, _nt(dm, w_ref[...].astype(BF16)), jnp.logical_and(l == 0, j == 0))

    return pl.pallas_call(
        body, name="mod_bwd", grid=(2, ncol // MCOL),
        in_specs=[pl.BlockSpec((MROWS, D), lambda l, j: (0, 0)), pl.BlockSpec((None, MROWS, MCOL), lambda l, j: (l, 0, j)),
                  pl.BlockSpec((None, D, MCOL), lambda l, j: (l, 0, j))],
        out_specs=[pl.BlockSpec((None, D, MCOL), lambda l, j: (l, 0, j)), pl.BlockSpec((MROWS, D), lambda l, j: (0, 0))],
        out_shape=[jax.ShapeDtypeStruct((2, D, ncol), F32), jax.ShapeDtypeStruct((MROWS, D), F32)],
        compiler_params=_params(None, ("arbitrary", "arbitrary")),
    )(c24, dmod, wmod)


def _bmod_cctx_grad(dmod_full, dsc_parts, cctx):
    def body(d_ref, p_ref, c_ref, db_ref, dc_ref):
        db_ref[...] = jnp.sum(d_ref[...], axis=1, keepdims=True)
        tot = jnp.zeros((8, D), F32)
        for s in range(N_CHIP):
            tot = tot + p_ref[s]
        x = c_ref[...]
        sg = jax.nn.sigmoid(x)
        dc_ref[...] = jnp.sum(tot, axis=0, keepdims=True) * (sg * (1.0 + x * (1.0 - sg)))

    return pl.pallas_call(
        body, name="bmod_cctx_grad",
        out_shape=[jax.ShapeDtypeStruct((2, 1, N_MOD * D), F32), jax.ShapeDtypeStruct((1, D), F32)],
    )(dmod_full, dsc_parts, cctx)


def _adam_math(w, g, m, v):
    m = ADAM_B1 * m + (1.0 - ADAM_B1) * g
    v = ADAM_B2 * v + (1.0 - ADAM_B2) * (g * g)
    m_hat = m / (1.0 - ADAM_B1 ** ADAM_STEP)
    v_hat = v / (1.0 - ADAM_B2 ** ADAM_STEP)
    delta = -ADAM_LR * (m_hat / (jnp.sqrt(v_hat) + ADAM_EPS) + ADAM_WD * w)
    return delta, m, v


def _adam(w, g, m, v):
    R, Cc = w.shape
    if R * Cc * 4 <= (1 << 20):
        RB = R
    else:
        RB = 1 << (((1 << 18) // Cc).bit_length() - 1)
        assert R % RB == 0

    def body(w_ref, g_ref, m_ref, v_ref, d_ref, mo_ref, vo_ref):
        d, mn, vn = _adam_math(w_ref[...], g_ref[...], m_ref[...], v_ref[...])
        d_ref[...] = d
        mo_ref[...] = mn
        vo_ref[...] = vn

    spec = pl.BlockSpec((RB, Cc), lambda i: (i, 0))
    return pl.pallas_call(
        body, name="adam", grid=(R // RB,), in_specs=[spec] * 4, out_specs=[spec] * 3,
        out_shape=[jax.ShapeDtypeStruct((R, Cc), F32)] * 3,
        compiler_params=_params(None, ("arbitrary",)),
    )(w, g, m, v)


def _adam_layer(w, g, m, v, layer, prev):
    shape = w.shape
    n, cols = shape[0], shape[-1]
    w3, m3, v3 = (t.reshape(n, -1, cols) for t in (w, m, v))
    g2 = g.reshape(-1, cols)
    R = g2.shape[0]
    RB = max(r for r in range(8, (1 << 18) // cols + 1, 8) if R % r == 0)

    def body(w_ref, g_ref, m_ref, v_ref, *refs):
        go_ref, d_ref, mo_ref, vo_ref = refs[-4:]
        gt = g_ref[...]
        d, mn, vn = _adam_math(w_ref[...], gt, m_ref[...], v_ref[...])
        go_ref[...] = gt
        d_ref[...] = d
        mo_ref[...] = mn
        vo_ref[...] = vn

    lay = pl.BlockSpec((None, RB, cols), lambda i: (layer, i, 0))
    flat = pl.BlockSpec((RB, cols), lambda i: (i, 0))
    hold = [] if prev is None else [t.reshape(n, -1, cols) for t in prev]
    outs = pl.pallas_call(
        body, name="adam_layer", grid=(R // RB,),
        in_specs=[lay, flat, lay, lay] + [pl.BlockSpec(memory_space=pl.ANY)] * len(hold), out_specs=[lay] * 4,
        out_shape=[jax.ShapeDtypeStruct(w3.shape, F32)] * 4,
        input_output_aliases={4 + k: k for k in range(len(hold))},
        compiler_params=_params(None, ("arbitrary",)),
    )(w3, g2, m3, v3, *hold)
    return [o.reshape(shape) for o in outs]


def _sum_devices(parts):
    K = parts.shape[1]

    def body(p_ref, o_ref):
        tot = p_ref[0]
        for i in range(1, N_DEV):
            tot = tot + p_ref[i]
        o_ref[...] = tot

    return pl.pallas_call(body, name="sum_devices", out_shape=jax.ShapeDtypeStruct((K, 128), F32))(parts)


def _sum_pieces(own, others):
    R = own.shape[0]
    RB = 96 if R % 96 == 0 else 32

    def body(a_ref, r_ref, o_ref):
        tot = a_ref[...].astype(F32)
        for i in range(N_DEV - 1):
            tot = tot + r_ref[i].astype(F32)
        o_ref[...] = tot

    return pl.pallas_call(
        body, name="sum_pieces", grid=(R // RB,),
        in_specs=[pl.BlockSpec((RB, D), lambda i: (i, 0)), pl.BlockSpec((N_DEV - 1, RB, D), lambda i: (0, i, 0))],
        out_specs=pl.BlockSpec((RB, D), lambda i: (i, 0)),
        out_shape=jax.ShapeDtypeStruct((R, D), F32),
        compiler_params=_params(None, ("arbitrary",)),
    )(own, others)


def _coords():
    return lax.axis_index("x"), lax.axis_index("y"), lax.axis_index("c")


_FLIPS = [(fx, fy, fc) for fx in (0, 1) for fy in (0, 1) for fc in (0, 1)][1:]


def _all_gather_small(buf):
    K = buf.shape[0]

    def body(in_ref, out_ref, send_sems, recv_sems, local_sem):
        x, y, c = _coords()
        me = 4 * x + 2 * y + c
        mine = pltpu.make_async_copy(in_ref, out_ref.at[me], local_sem)
        mine.start()
        sends = []
        for k, (fx, fy, fc) in enumerate(_FLIPS):
            peer = (x ^ fx, y ^ fy, c ^ fc)
            cp = pltpu.make_async_remote_copy(src_ref=in_ref, dst_ref=out_ref.at[me], send_sem=send_sems.at[k],
                                              recv_sem=recv_sems.at[k], device_id=peer, device_id_type=MESH)
            cp.start()
            sends.append(cp)
        for k, (fx, fy, fc) in enumerate(_FLIPS):
            src = 4 * (x ^ fx) + 2 * (y ^ fy) + (c ^ fc)
            pltpu.make_async_remote_copy(src_ref=in_ref, dst_ref=out_ref.at[src], send_sem=send_sems.at[k],
                                         recv_sem=recv_sems.at[k], device_id=(x, y, c), device_id_type=MESH).wait_recv()
        for cp in sends:
            cp.wait_send()
        mine.wait()

    return pl.pallas_call(
        body, name="all_gather_small",
        in_specs=[pl.BlockSpec(memory_space=pltpu.VMEM)], out_specs=pl.BlockSpec(memory_space=pltpu.VMEM),
        out_shape=jax.ShapeDtypeStruct((N_DEV, K, 128), F32),
        scratch_shapes=[pltpu.SemaphoreType.DMA((7,)), pltpu.SemaphoreType.DMA((7,)), pltpu.SemaphoreType.DMA],
        compiler_params=_params(VMEM_BIG),
    )(buf)


_CHIP_FLIPS = [(1, 0), (0, 1), (1, 1)]


_HBM = pl.BlockSpec(memory_space=pltpu.HBM)
_SEMS = pl.BlockSpec(memory_space=pltpu.SEMAPHORE)
_EFFECT = pltpu.SideEffectType.DATAFLOW_SIDE_EFFECTING


def _in_hbm(v):
    return pltpu.with_memory_space_constraint(v, pltpu.HBM)


def _copies_start(name, srcs, lands, n_sems, issue, after):
    ns, nl = len(srcs), len(lands)

    def body(*refs):
        src_refs, land_refs = refs[:ns], refs[ns:ns + nl]
        out = refs[ns + nl + 1:]
        issue(src_refs, land_refs, out[:nl], out[nl:2 * nl])
        out[-1][...] = jnp.zeros((8, 128), F32)

    outs = pl.pallas_call(
        body, name=name, in_specs=[_HBM] * (ns + nl) + [pl.BlockSpec(memory_space=pl.ANY)],
        out_specs=[_SEMS] * (2 * nl) + [_HBM] * (ns + nl) + [pl.BlockSpec(memory_space=pltpu.VMEM)],
        out_shape=[pltpu.SemaphoreType.DMA((n_sems,))] * (2 * nl) + [pltpu.HBM(v.shape, v.dtype) for v in (*srcs, *lands)]
        + [jax.ShapeDtypeStruct((8, 128), F32)],
        input_output_aliases={i: 2 * nl + i for i in range(ns + nl)},
        compiler_params=pltpu.CompilerParams(has_side_effects=_EFFECT),
    )(*[_in_hbm(v) for v in (*srcs, *lands)], after)
    return outs[:nl], outs[nl:2 * nl], outs[2 * nl:2 * nl + ns], outs[2 * nl + ns:2 * nl + ns + nl], outs[-1]


def _copies_wait(name, srcs, lands, send_sems, recv_sems, finish, after):
    after = list(after) if isinstance(after, (list, tuple)) else [after]
    ns, nl = len(srcs), len(lands)

    def body(*refs):
        src_refs, land_refs = refs[:ns], refs[ns:ns + nl]
        finish(src_refs, land_refs, refs[ns + nl:ns + 2 * nl], refs[ns + 2 * nl:ns + 3 * nl])

    outs = pl.pallas_call(
        body, name=name, in_specs=[_HBM] * (ns + nl) + [_SEMS] * (2 * nl) + [pl.BlockSpec(memory_space=pl.ANY)] * len(after),
        out_specs=[_HBM] * (ns + nl), out_shape=[pltpu.HBM(v.shape, v.dtype) for v in (*srcs, *lands)],
        input_output_aliases={i: i for i in range(ns + nl)},
        compiler_params=pltpu.CompilerParams(has_side_effects=_EFFECT),
    )(*srcs, *lands, *send_sems, *recv_sems, *after)
    return outs[:ns], outs[ns:]


def _own_slab(land, mine, index):
    return lax.dynamic_update_slice_in_dim(land, mine[:, None], index, axis=1)


def _gather_start(units, after):
    x, y, c = _coords()
    chip = 2 * x + y
    lands = [_own_slab(lax.empty((u.shape[0], N_CHIP) + u.shape[1:], u.dtype), u, chip) for u in units]

    def issue(src_refs, land_refs, send_sems, recv_sems):
        x, y, c = _coords()
        s_me = 2 * x + y
        for i, (src, land) in enumerate(zip(src_refs, land_refs)):
            for j, (fx, fy) in enumerate(_CHIP_FLIPS):
                pltpu.make_async_remote_copy(src_ref=src, dst_ref=land.at[:, s_me], send_sem=send_sems[i].at[j],
                                             recv_sem=recv_sems[i].at[j], device_id=(x ^ fx, y ^ fy, c),
                                             device_id_type=MESH).start()

    return _copies_start("gather_start", units, lands, 3, issue, after)


def _gather_wait(tag, started, which, after):
    send_sems, recv_sems, srcs, lands, _ = started

    def finish(src_refs, land_refs, ssems, rsems):
        x, y, c = _coords()
        for src, land, ss, rs in zip(src_refs, land_refs, ssems, rsems):
            for j in range(3):
                cp = pltpu.make_async_remote_copy(src_ref=src, dst_ref=land.at[:, 0], send_sem=ss.at[j], recv_sem=rs.at[j],
                                                  device_id=(x, y, c), device_id_type=MESH)
                cp.wait_send()
                cp.wait_recv()

    _, done = _copies_wait("gather_wait_" + tag, [srcs[i] for i in which], [lands[i] for i in which],
                           [send_sems[i] for i in which], [recv_sems[i] for i in which], finish, after)
    return [d.reshape(d.shape[0], N_CHIP * d.shape[2], D) for d in done]


def _piece_rows(grads):
    return [g.shape[2] for g in grads]


def _scatter_start(tag, grads, after):
    rows = _piece_rows(grads)
    land = lax.empty((N_DEV - 1, sum(rows), D), grads[0].dtype)

    def issue(src_refs, land_refs, send_sems, recv_sems):
        x, y, c = _coords()
        for k, (fx, fy, fc) in enumerate(_FLIPS):
            px, py, pc = x ^ fx, y ^ fy, c ^ fc
            off = 0
            for src, n in zip(src_refs, rows):
                pltpu.make_async_remote_copy(src_ref=src.at[2 * px + py, pc], dst_ref=land_refs[0].at[k, pl.ds(off, n)],
                                             send_sem=send_sems[0].at[k], recv_sem=recv_sems[0].at[k],
                                             device_id=(px, py, pc), device_id_type=MESH).start()
                off += n

    return _copies_start("scatter_start_" + tag, grads, [land], N_DEV - 1, issue, after)


def _scatter_wait(tag, started, after):
    send_sems, recv_sems, srcs, lands, _ = started

    def finish(src_refs, land_refs, ssems, rsems):
        x, y, c = _coords()
        for k in range(N_DEV - 1):
            cp = pltpu.make_async_remote_copy(src_ref=land_refs[0].at[0], dst_ref=land_refs[0].at[0], send_sem=ssems[0].at[k],
                                              recv_sem=rsems[0].at[k], device_id=(x, y, c), device_id_type=MESH)
            cp.wait_send()
            cp.wait_recv()

    grads, (others,) = _copies_wait("scatter_wait_" + tag, srcs, lands, send_sems, recv_sems, finish, after)
    return grads, others


def _swap_start(tag, mine, rows, after):
    x, y, c = _coords()
    offs = [sum(rows[:t]) for t in range(len(rows))]
    lands = [lax.dynamic_update_slice_in_dim(lax.empty((2, n, D), mine.dtype), mine[o:o + n][None], c, axis=0)
             for o, n in zip(offs, rows)]

    def issue(src_refs, land_refs, send_sems, recv_sems):
        x, y, c = _coords()
        for t, (o, n) in enumerate(zip(offs, rows)):
            pltpu.make_async_remote_copy(src_ref=src_refs[0].at[pl.ds(o, n)], dst_ref=land_refs[t].at[c],
                                         send_sem=send_sems[t].at[0], recv_sem=recv_sems[t].at[0],
                                         device_id=(x, y, 1 - c), device_id_type=MESH).start()

    return _copies_start("swap_start_" + tag, [mine], lands, 1, issue, after)


def _swap_wait(tag, started, after):
    send_sems, recv_sems, srcs, lands, _ = started

    def finish(src_refs, land_refs, ssems, rsems):
        x, y, c = _coords()
        for land, ss, rs in zip(land_refs, ssems, rsems):
            cp = pltpu.make_async_remote_copy(src_ref=land.at[0], dst_ref=land.at[0], send_sem=ss.at[0], recv_sem=rs.at[0],
                                              device_id=(x, y, c), device_id_type=MESH)
            cp.wait_send()
            cp.wait_recv()

    return _copies_wait("swap_wait_" + tag, srcs, lands, send_sems, recv_sems, finish, after)[1]


def _size(shape):
    n = 1
    for d in shape:
        n *= d
    return n


def _pack(arrays):
    return jnp.concatenate([jnp.pad(a.reshape(-1).astype(F32), (0, (-a.size) % 1024)).reshape(-1, 128) for a in arrays], axis=0)


def _unpack(buf, shapes):
    out, row = [], 0
    for s in shapes:
        n = _size(s)
        nrows = 8 * -(-n // 1024)
        out.append(buf[row:row + nrows].reshape(-1)[:n].reshape(s))
        row += nrows
    return out


def _block_diag(pw):
    bd = jnp.zeros((PW, PW), F32)
    g = PW // 4
    for i in range(4):
        bd = bd.at[i * g:(i + 1) * g, i * g:(i + 1) * g].set(pw[i])
    return bd


def _lanes(v):
    return jnp.broadcast_to(v.reshape(NH, 1, 1), (NH, 1, HD))


def _layer_fwd(z, mod, normg, wget, small):
    S = z.shape[1]
    cos, sin = _rope_tables(S)
    fwd_sched, _ = _schedules(S)
    wa = wget("a", z)
    z1, f_a = _ffn_fwd(z, mod[:, :, 0], normg[0], *wa)
    wm = wget("m", z1)
    pp, q, k, v, gg, pc = _mix_in_fwd(z1, mod[:, :, 1], normg[1], wm[0], cos, sin)
    po, co = _pool_conv_fwd(pp, pc, small["bd"], small["pscale"], small["dw"], small["db"])
    ro = (small["gng"], small["lng"], small["lnb"])
    oa, ob = _retention(q, k, v, small["dec_f"], small["dec_b"], *fwd_sched)
    z2, out = _mix_out_fwd(z1, po, oa, ob, gg, co, ro, mod[:, :, 1], wm[1])
    wb = wget("b", z2)
    z3, f_b = _ffn_fwd(z2, mod[:, :, 2], normg[2], *wb)
    saved = dict(z=z, f_a=f_a, z1=z1, pp=pp, q=q, k=k, v=v, gg=gg, pc=pc, po=po, co=co, oa=oa, ob=ob, out=out, z2=z2, f_b=f_b,
                 wa=wa, wm=wm, wb=wb)
    return z3, saved


def _layer_bwd(dz3, sv, mod, normg, small, emit, tok, last):
    S = dz3.shape[1]
    B = dz3.shape[0]
    T = B * S
    cos, sin = _rope_tables(S)
    fwd_sched, bwd_sched = _schedules(S)
    wa, wm, wb = sv["wa"], sv["wm"], sv["wb"]
    dz2, dmod_b, dg_b, gw1t_b, gw3t_b, gw2_b = _ffn_bwd(sv["z2"], dz3, sv["f_b"], mod[:, :, 2] + tok, normg[2], *wb)
    tok = emit("b", [gw1t_b, gw3t_b, gw2_b])
    mod_m = mod[:, :, 1] + tok
    ro = (small["gng"], small["lng"], small["lnb"])
    dpo, do, dgg, dco, cat, dout, dmod_gate, dgng, dlng, dlnb = _mix_out_bwd(
        dz2, sv["out"], sv["po"], sv["oa"], sv["ob"], sv["gg"], sv["co"], ro, mod_m, wm[1])
    gwout = _tn_matmul(cat.reshape(T, D), dout.reshape(T, D))
    dqa, dqb = _retention(do, sv["v"], sv["k"], small["dec_f"], small["dec_b"], *fwd_sched)
    dka, dkb = _retention(sv["v"], do, sv["q"], small["dec_f"], small["dec_b"], *bwd_sched)
    dva, dvb = _retention(sv["k"], sv["q"], do, small["dec_f"], small["dec_b"], *bwd_sched)
    ddec = _retention_ddecay(sv["q"], sv["k"], sv["v"], do, small["dec_f"], small["dec_b"], *fwd_sched)
    dpp, dpc, dbd, dps, ddw, ddb = _pool_conv_bwd(sv["pp"], sv["pc"], dpo, dco, small["bd"], small["pscale"],
                                                   small["dw"], small["db"])
    dz1, h, dp, dmod_m, dg_m = _mix_in_bwd(sv["z1"], dz2, dpp, dqa, dqb, dka, dkb, dva, dvb, dgg, dpc, mod_m, normg[1],
                                           wm[0], cos, sin)
    gwint = _tn_matmul(dp.reshape(T, F), h.reshape(T, D))
    tok = emit("m", [gwint, gwout])
    if last:
        dz, dmod_a, dg_a = _ffn_bwd(sv["z"], dz1, sv["f_a"], mod[:, :, 0] + tok, normg[0], *wa,
                                    each=lambda grad: emit("a", [grad]))[:3]
    else:
        dz, dmod_a, dg_a, gw1t_a, gw3t_a, gw2_a = _ffn_bwd(sv["z"], dz1, sv["f_a"], mod[:, :, 0] + tok, normg[0], *wa)
        tok = emit("a", [gw1t_a, gw3t_a, gw2_a])
    dmod = jnp.stack([dmod_a, dmod_m + dmod_gate, dmod_b], axis=2)
    dnormg = jnp.stack([dg_a, dg_m, dg_b], axis=0)
    g = PW // 4
    dpool_w = jnp.stack([dbd[i * g:(i + 1) * g, i * g:(i + 1) * g] for i in range(4)], axis=0)
    sm = dict(pool_w=dpool_w, pool_scale=dps[0], dec_f=ddec[:, 0, 0], dec_b=ddec[:, 1, 0], gng=dgng[0],
              conv_dw=ddw[0:CONV_K], conv_b=ddb[0], conv_ln_g=dlng[0], conv_ln_b=dlnb[0])
    return dz, dmod, dnormg, sm, tok


def _small_params(pool_w, pool_scale, dec_f, dec_b, gng, conv_dw, conv_b, lng, lnb):
    return dict(bd=_block_diag(pool_w), pscale=pool_scale.reshape(1, PW), dec_f=_lanes(dec_f), dec_b=_lanes(dec_b),
                gng=gng.reshape(1, RW), dw=jnp.pad(conv_dw, ((0, 1), (0, 0))), db=conv_b.reshape(1, PW),
                lng=lng.reshape(1, PW), lnb=lnb.reshape(1, PW))


_WEIGHTS = ["c_ctx", "w_mod", "b_mod", "norm_g", "ffn_w1", "ffn_w3", "ffn_w2", "w_in", "w_out", "pool_w", "pool_scale",
            "ret_decay_fwd", "ret_decay_bwd", "ret_gn_g", "conv_dw", "conv_b", "conv_ln_g", "conv_ln_b", "final_g"]
_BIG = ["w_mod", "ffn_w1", "ffn_w3", "ffn_w2", "w_in", "w_out"]
_SMALL = [n for n in _WEIGHTS if n not in _BIG]


def _adam_any(w, g, m, v):
    shape = w.shape
    cols = shape[-1] if w.ndim >= 2 else 128
    outs = _adam(w.reshape(-1, cols), g.reshape(-1, cols), m.reshape(-1, cols), v.reshape(-1, cols))
    return [o.reshape(shape) for o in outs]


def _step(a):
    x, c, ctx = a["x"], a["c"], a["ctx"]
    B = x.shape[0]
    nex = N_DEV * B
    assert nex + B <= MROWS and ctx.shape[1] == LC and x.shape[1] % TM == 0
    xi, yi, ci = _coords()
    me = 4 * xi + 2 * yi + ci
    chip = 2 * xi + yi
    ncol = a["w_mod"].shape[2]

    def t_bf16(w):
        return jnp.swapaxes(w, -1, -2).astype(BF16)

    w1t, w3t, w2 = t_bf16(a["ffn_w1"]), t_bf16(a["ffn_w3"]), a["ffn_w2"].astype(BF16)
    wint, wout = t_bf16(a["w_in"]), a["w_out"].astype(BF16)
    units = []
    for l in range(2):
        units += [jnp.stack([w1t[l, 0], w3t[l, 0], w2[l, 0]]), wint[l][None], wout[l][None],
                  jnp.stack([w1t[l, 1], w3t[l, 1], w2[l, 1]])]

    shapes1 = [(B, D), (2, 3, D // N_CHIP), (2, CONV_K, PW // N_CHIP)]
    g1 = _all_gather_small(_pack([c, a["norm_g"], a["conv_dw"]]))
    per = [_unpack(g1[d], shapes1) for d in range(N_DEV)]
    c_all = jnp.concatenate([per[d][0] for d in range(N_DEV)], axis=0)
    norm_g_full = jnp.concatenate([per[2 * s][1] for s in range(N_CHIP)], axis=-1)
    conv_dw_full = jnp.concatenate([per[2 * s][2] for s in range(N_CHIP)], axis=-1)
    cctx = a["c_ctx"].reshape(1, D)
    c24 = jnp.concatenate([c_all] + [cctx] * B + [jnp.zeros((MROWS - nex - B, D), F32)], axis=0)

    bsh = lax.dynamic_slice(a["b_mod"], (0, chip * ncol), (2, ncol)).reshape(2, 1, ncol)
    mod_raw = _mod_fwd(c24, a["w_mod"], bsh)
    g2 = _all_gather_small(_pack([mod_raw]))
    mod_full = jnp.concatenate([_unpack(g2[2 * s], [(2, MROWS, ncol)])[0] for s in range(N_CHIP)], axis=-1)
    mods = []
    for l in range(2):
        lat = lax.dynamic_slice(mod_full[l], (B * me, 0), (B, N_MOD * D))
        cx = jnp.broadcast_to(mod_full[l, nex][None], (B, N_MOD * D))
        mods.append(jnp.stack([cx, lat], axis=1).reshape(B, 2, 3, 3, D))

    started = _gather_start(units, mod_full)

    def wget_of(l):
        def wget(stage, after):
            if stage == "m":
                win, wo = _gather_wait(f"m{l}", started, [4 * l + 1, 4 * l + 2], after)
                return (win, 0), (wo, 0)
            (g,) = _gather_wait(f"{stage}{l}", started, [4 * l + (0 if stage == "a" else 3)], after)
            return (g, 0), (g, 1), (g, 2)
        return wget

    smalls = [_small_params(a["pool_w"][l], a["pool_scale"][l], a["ret_decay_fwd"][l], a["ret_decay_bwd"][l],
                            a["ret_gn_g"][l], conv_dw_full[l], a["conv_b"][l], a["conv_ln_g"][l], a["conv_ln_b"][l])
              for l in range(2)]
    normgs = [norm_g_full[l].reshape(3, 1, D) for l in range(2)]
    z = jnp.concatenate([ctx, x], axis=1)
    saved = []
    for l in range(2):
        z, sv = _layer_fwd(z, mods[l], normgs[l], wget_of(l), smalls[l])
        saved.append(sv)
    dz, dfinal_g, loss_part = _head(z, a["loss_target"], a["final_g"].reshape(1, D))

    scattering, swapping, reduced, newest = [], [], {}, []

    def reduce_oldest(after):
        tag, st, rows = scattering.pop(0)
        grads, others = _scatter_wait(tag, st, after)
        own = jnp.concatenate([lax.dynamic_slice(g, (chip, ci, 0, 0), (1, 1) + g.shape[2:]).reshape(g.shape[2:])
                               for g in grads], axis=0)
        mine = _sum_pieces(own, others)
        sw = _swap_start(tag, mine, rows, mine)
        if swapping:
            ptag, psw = swapping.pop()
            reduced[ptag] = _swap_wait(ptag, psw, sw[4])
        swapping.append((tag, sw))
        return sw[4]

    def emit_of(l):
        def emit(stage, grads):
            grads = [g.reshape(N_CHIP, 2, g.shape[0] // (2 * N_CHIP), D) for g in grads]
            stem = f"{stage}{l}"
            same = sum(1 for t, _, _ in scattering if t.split("_")[0] == stem)
            tag = stem if same == 0 else f"{stem}_{same}"
            st = _scatter_start(tag, grads, grads[0])
            tok = st[4][0, 0]
            while scattering and scattering[0][0].split("_")[0] != stem:
                tok = tok + reduce_oldest(st[4])[0, 0]
            scattering.append((tag, st, _piece_rows(grads)))
            newest[:] = [st[4]] + (newest if same else [])
            return tok
        return emit

    back = [None, None]
    tok = jnp.zeros((), F32)
    for l in (1, 0):
        dz, dmod, dnormg, sm, tok = _layer_bwd(dz, saved[l], mods[l], normgs[l], smalls[l], emit_of(l), tok, l == 0)
        back[l] = (dmod, dnormg, None, sm)
    grad_x = dz[:, LC:]
    grads = {}

    dmods = [back[l][0].reshape(B, 2, N_MOD * D) for l in range(2)]
    pack_a = _pack([jnp.stack([dm[:, 1] for dm in dmods])])
    ka = pack_a.shape[0]
    sm = [back[l][3] for l in range(2)]
    sum_list = [jnp.stack([dm[:, 0] for dm in dmods]), jnp.stack([back[l][1][:, 0] for l in range(2)])]
    sm_keys = ["pool_w", "pool_scale", "dec_f", "dec_b", "gng", "conv_dw", "conv_b", "conv_ln_g", "conv_ln_b"]
    sum_list += [jnp.stack([sm[l][k] for l in range(2)]) for k in sm_keys]
    sum_list += [dfinal_g[0], loss_part[0, 0:1]]
    sum_shapes = [s.shape for s in sum_list]
    g3 = _all_gather_small(jnp.concatenate([pack_a, _pack(sum_list)], axis=0))
    dmx_all = jnp.concatenate([_unpack(g3[d, :ka], [(2, B, N_MOD * D)])[0] for d in range(N_DEV)], axis=1)
    summed = _unpack(_sum_devices(g3[:, ka:]), sum_shapes)
    dmy, dnorm_full = summed[0], summed[1]
    sgrad = dict(zip(sm_keys, summed[2:2 + len(sm_keys)]))
    loss = summed[-1].reshape(())

    dmod24 = jnp.concatenate([dmx_all, dmy, jnp.zeros((2, MROWS - nex - B, N_MOD * D), F32)], axis=1)
    dmod_my = lax.dynamic_slice(dmod24, (0, 0, chip * ncol), (2, MROWS, ncol))
    grads["w_mod"], dsc = _mod_bwd(c24, dmod_my, a["w_mod"])
    g4 = _all_gather_small(_pack([dsc[nex:nex + 8]]))
    dsc_parts = jnp.stack([_unpack(g4[2 * s], [(8, D)])[0] for s in range(N_CHIP)])
    dbmod, dcctx = _bmod_cctx_grad(dmod24, dsc_parts, cctx)

    grads["c_ctx"] = dcctx[0]
    grads["b_mod"] = dbmod.reshape(2, N_MOD * D)
    grads["norm_g"] = lax.dynamic_slice(dnorm_full, (0, 0, chip * (D // N_CHIP)), (2, 3, D // N_CHIP))
    grads["pool_w"] = sgrad["pool_w"]
    grads["pool_scale"] = sgrad["pool_scale"]
    grads["ret_decay_fwd"] = sgrad["dec_f"]
    grads["ret_decay_bwd"] = sgrad["dec_b"]
    grads["ret_gn_g"] = sgrad["gng"]
    grads["conv_dw"] = lax.dynamic_slice(sgrad["conv_dw"], (0, 0, chip * (PW // N_CHIP)), (2, CONV_K, PW // N_CHIP))
    grads["conv_b"] = sgrad["conv_b"]
    grads["conv_ln_g"] = sgrad["conv_ln_g"]
    grads["conv_ln_b"] = sgrad["conv_ln_b"]
    grads["final_g"] = summed[-2]

    delta, new_m, new_v = {}, {}, {}
    delta["w_mod"], new_m["w_mod"], new_v["w_mod"] = _adam_any(a["w_mod"], grads["w_mod"], a["m_w_mod"], a["v_w_mod"])
    shapes_s = [a[n].shape for n in _SMALL]
    packed = _adam(_pack([a[n] for n in _SMALL]), _pack([grads[n] for n in _SMALL]),
                   _pack([a["m_" + n] for n in _SMALL]), _pack([a["v_" + n] for n in _SMALL]))
    for res, out in zip(packed, (delta, new_m, new_v)):
        for n, val in zip(_SMALL, _unpack(res, shapes_s)):
            out[n] = val

    def group(stem):
        out, k = list(reduced[stem]), 1
        while f"{stem}_{k}" in reduced:
            out += reduced[f"{stem}_{k}"]
            k += 1
        return out

    def layer_grads(l):
        ffn = [[h.reshape(-1, D) for h in group(f"{stage}{l}")] for stage in "ab"]
        win, wo = [h.reshape(-1, D) for h in reduced[f"m{l}"]]
        return dict(ffn_w1=jnp.stack([ffn[i][0] for i in range(2)]), ffn_w3=jnp.stack([ffn[i][1] for i in range(2)]),
                    ffn_w2=jnp.stack([ffn[i][2] for i in range(2)]), w_in=win, w_out=wo)

    turned = ("ffn_w1", "ffn_w3", "w_in")

    def adam_of(n, g, layer, prev):
        t = (lambda u: jnp.swapaxes(u, -1, -2)) if n in turned else (lambda u: u)
        return _adam_layer(t(a[n]), g, t(a["m_" + n]), t(a["v_" + n]), layer, prev)

    g1 = layer_grads(1)
    half = {n: adam_of(n, g1[n], 1, None) for n in _BIG[1:]}

    last = [delta["w_mod"], packed[0]] + [half[n][1] for n in _BIG[1:]] + newest
    while scattering:
        last = reduce_oldest(last)
    tag, sw = swapping.pop()
    reduced[tag] = _swap_wait(tag, sw, last)
    g0 = layer_grads(0)
    for n in _BIG[1:]:
        res = adam_of(n, g0[n], 0, half[n])
        grads[n], delta[n], new_m[n], new_v[n] = [jnp.swapaxes(r, -1, -2) for r in res] if n in turned else res
    return (loss, grad_x, *[grads[n] for n in _WEIGHTS], *[delta[n] for n in _WEIGHTS],
            *[new_m[n] for n in _WEIGHTS], *[new_v[n] for n in _WEIGHTS])


def kernel(x, c, ctx, c_ctx, w_mod, b_mod, norm_g, ffn_w1, ffn_w3, ffn_w2, w_in, w_out, pool_w, pool_scale, ret_decay_fwd, ret_decay_bwd, ret_gn_g, conv_dw, conv_b, conv_ln_g, conv_ln_b, final_g, loss_target, m_c_ctx, m_w_mod, m_b_mod, m_norm_g, m_ffn_w1, m_ffn_w3, m_ffn_w2, m_w_in, m_w_out, m_pool_w, m_pool_scale, m_ret_decay_fwd, m_ret_decay_bwd, m_ret_gn_g, m_conv_dw, m_conv_b, m_conv_ln_g, m_conv_ln_b, m_final_g, v_c_ctx, v_w_mod, v_b_mod, v_norm_g, v_ffn_w1, v_ffn_w3, v_ffn_w2, v_w_in, v_w_out, v_pool_w, v_pool_scale, v_ret_decay_fwd, v_ret_decay_bwd, v_ret_gn_g, v_conv_dw, v_conv_b, v_conv_ln_g, v_conv_ln_b, v_final_g):
    return _step(dict(locals()))
```

```python
import functools

import jax
import jax.numpy as jnp
from jax import lax
from jax.experimental import pallas as pl
from jax.experimental.pallas import tpu as pltpu

F32 = jnp.float32
BF16 = jnp.bfloat16

D = 1024
F = 2816
FH = 1408
N_MOD = 9
LC = 256
TM = 256
HD = 128
NH = 4
RW = 512
PW = 256
CONV_K = 31
GRID_W = 64
EPS = 1e-6
K_SCALE = HD ** -0.5
N_DEV = 8
N_CHIP = 4
SLAB = F // N_CHIP
HSLAB = SLAB // 2
OSLAB = D // N_CHIP
HOSLAB = OSLAB // 2
VMEM_BIG = 60 * 1024 * 1024
MESH = pl.DeviceIdType.MESH

ADAM_LR = 0.001
ADAM_B1 = 0.9
ADAM_B2 = 0.999
ADAM_EPS = 1e-08
ADAM_WD = 0.01
ADAM_STEP = 10


def _nt(a, b):
    return lax.dot_general(a, b, (((1,), (1,)), ((), ())), preferred_element_type=F32)


def _nn(a, b):
    return lax.dot_general(a, b, (((1,), (0,)), ((), ())), preferred_element_type=F32)


def _tn(a, b):
    return lax.dot_general(a, b, (((0,), (0,)), ((), ())), preferred_element_type=F32)


def _params(vmem=None, sem=None):
    return pltpu.CompilerParams(dimension_semantics=sem, vmem_limit_bytes=vmem)


def _rms_mod(z, g, shift, scale):
    y = z * lax.rsqrt(jnp.mean(z * z, axis=-1, keepdims=True) + EPS)
    return (y * g) * (1.0 + scale) + shift


def _acc(ref, val, first):
    @pl.when(first)
    def _():
        ref[...] = val

    @pl.when(jnp.logical_not(first))
    def _():
        ref[...] += val


def _tok(width):
    return pl.BlockSpec((None, TM, width), lambda b, t: (b, t, 0))


def _modspec():
    return pl.BlockSpec((None, None, 3, D), lambda b, t: (b, jnp.minimum(t, 1), 0, 0))


def _const(shape):
    nd = len(shape)
    return pl.BlockSpec(shape, lambda b, t: (0,) * nd)


def _wspec(w):
    stack, idx = w
    return pl.BlockSpec((None,) + stack.shape[1:], lambda b, t: (idx, 0, 0), pipeline_mode=pl.Buffered(1))


def _ffn_fwd(z, mod, g, w1t, w3t, w2):
    B, S, _ = z.shape

    def body(z_ref, mod_ref, g_ref, w1_ref, w3_ref, w2_ref, zo_ref, f_ref):
        zt = z_ref[...]
        h = _rms_mod(zt, g_ref[...], mod_ref[0:1, :], mod_ref[1:2, :]).astype(BF16)
        f = jnp.zeros((TM, D), F32)
        for c in range(F // FH):
            rows = slice(c * FH, (c + 1) * FH)
            u1 = _nt(h, w1_ref[rows, :])
            u3 = _nt(h, w3_ref[rows, :])
            a = (u1 * jax.nn.sigmoid(u1) * u3).astype(BF16)
            f = f + _nn(a, w2_ref[rows, :])
        f_ref[...] = f
        zo_ref[...] = zt + 0.5 * mod_ref[2:3, :] * f

    return pl.pallas_call(
        body, name="ffn_fwd", grid=(B, S // TM),
        in_specs=[_tok(D), _modspec(), _const((1, D)), _wspec(w1t), _wspec(w3t), _wspec(w2)],
        out_specs=[_tok(D), _tok(D)],
        out_shape=[jax.ShapeDtypeStruct((B, S, D), F32)] * 2,
        compiler_params=_params(VMEM_BIG, ("arbitrary", "arbitrary")),
    )(z, mod, g, w1t[0], w3t[0], w2[0])


def _ffn_bwd(z, dzo, f, mod, g, w1t, w3t, w2, each=None):
    B, S, _ = z.shape

    def body(z_ref, dzo_ref, f_ref, mod_ref, g_ref, w1_ref, w3_ref, w2_ref,
             dz_ref, h_ref, du1_ref, du3_ref, a_ref, do_ref, dmod_ref, dg_ref):
        b, t = pl.program_id(0), pl.program_id(1)
        zt = z_ref[...]
        dzo = dzo_ref[...]
        gate = mod_ref[2:3, :]
        h32, vjp_h = jax.vjp(_rms_mod, zt, g_ref[...], mod_ref[0:1, :], mod_ref[1:2, :])
        h = h32.astype(BF16)
        h_ref[...] = h
        do = (0.5 * gate * dzo).astype(BF16)
        do_ref[...] = do
        dgate = jnp.sum(0.5 * f_ref[...] * dzo, axis=0, keepdims=True)
        dh = jnp.zeros((TM, D), F32)
        for c in range(F // FH):
            rows = slice(c * FH, (c + 1) * FH)
            u1 = _nt(h, w1_ref[rows, :])
            u3 = _nt(h, w3_ref[rows, :])
            sg = jax.nn.sigmoid(u1)
            s = u1 * sg
            a_ref[:, rows] = (s * u3).astype(BF16)
            da = _nt(do, w2_ref[rows, :])
            du3 = (da * s).astype(BF16)
            du1 = (da * u3 * (sg * (1.0 + u1 * (1.0 - sg)))).astype(BF16)
            du1_ref[:, rows] = du1
            du3_ref[:, rows] = du3
            dh = dh + _nn(du1, w1_ref[rows, :]) + _nn(du3, w3_ref[rows, :])
        dz_h, dg, dshift, dscale = vjp_h(dh)
        dz_ref[...] = dzo + dz_h
        _acc(dmod_ref, jnp.concatenate([dshift, dscale, dgate], axis=0), t <= 1)
        _acc(dg_ref, dg, jnp.logical_and(b == 0, t == 0))

    T = B * S
    outs = pl.pallas_call(
        body, name="ffn_bwd", grid=(B, S // TM),
        in_specs=[_tok(D), _tok(D), _tok(D), _modspec(), _const((1, D)), _wspec(w1t), _wspec(w3t), _wspec(w2)],
        out_specs=[_tok(D), _tok(D), _tok(F), _tok(F), _tok(F), _tok(D), _modspec(), _const((1, D))],
        out_shape=[jax.ShapeDtypeStruct((B, S, D), F32), jax.ShapeDtypeStruct((B, S, D), BF16),
                   jax.ShapeDtypeStruct((B, S, F), BF16), jax.ShapeDtypeStruct((B, S, F), BF16),
                   jax.ShapeDtypeStruct((B, S, F), BF16), jax.ShapeDtypeStruct((B, S, D), BF16),
                   jax.ShapeDtypeStruct((B, 2, 3, D), F32), jax.ShapeDtypeStruct((1, D), F32)],
        compiler_params=_params(VMEM_BIG, ("arbitrary", "arbitrary")),
    )(z, dzo, f, mod, g, w1t[0], w3t[0], w2[0])
    dz, h, du1, du3, a, do, dmod, dg = outs
    grads = []
    for lhs, rhs in ((du1, h), (du3, h), (a, do)):
        grads.append(_tn_matmul(lhs.reshape(T, F), rhs.reshape(T, D)))
        if each is not None:
            each(grads[-1])
    return (dz, dmod, dg, *grads)


def _tn_matmul(a, b):
    T, M = a.shape
    N = b.shape[1]
    MB = FH if M > FH else M
    TT = next(t for t in (1152, 1024, 768, 512, TM) if T % t == 0)
    nt = T // TT

    def body(a_ref, b_ref, o_ref, acc_ref):
        t = pl.program_id(1)
        prod = _tn(a_ref[...], b_ref[...])
        _acc(acc_ref, prod, t == 0)

        @pl.when(t == nt - 1)
        def _():
            o_ref[...] = acc_ref[...].astype(BF16)

    return pl.pallas_call(
        body, name="tn_matmul", grid=(M // MB, nt),
        in_specs=[pl.BlockSpec((TT, MB), lambda i, t: (t, i)), pl.BlockSpec((TT, N), lambda i, t: (t, 0))],
        out_specs=pl.BlockSpec((MB, N), lambda i, t: (i, 0)),
        out_shape=jax.ShapeDtypeStruct((M, N), BF16),
        scratch_shapes=[pltpu.VMEM((MB, N), F32)],
        compiler_params=_params(VMEM_BIG, ("arbitrary", "arbitrary")),
    )(a, b)


def _swap32(x):
    n = x.shape[1]
    lane = lax.broadcasted_iota(jnp.int32, x.shape, 1)
    return jnp.where((lane % 64) < 32, pltpu.roll(x, n - 32, 1), pltpu.roll(x, 32, 1))


def _rope(x, cos, sin):
    return x * cos + _swap32(x) * sin


def _rope_t(dy, cos, sin):
    return dy * cos + _swap32(dy * sin)


def _rope_tables(S):
    L = S - LC
    n_freq = HD // 4
    inv = 10000.0 ** (-jnp.arange(n_freq, dtype=F32) / n_freq)
    i = jnp.arange(L)
    row = (i // GRID_W).astype(F32)
    col = (i % GRID_W).astype(F32)
    ang_r = row[:, None] * inv[None]
    ang_c = col[:, None] * inv[None]
    ang = jnp.concatenate([ang_r, ang_r, ang_c, ang_c], axis=1)
    ang = jnp.concatenate([jnp.zeros((LC, HD), F32), ang], axis=0)
    sign = jnp.where((jnp.arange(HD) % 64) < 32, -1.0, 1.0).astype(F32)
    return jnp.cos(ang), jnp.sin(ang) * sign[None]


def _tabspec():
    return pl.BlockSpec((TM, HD), lambda b, t: (t, 0))


def _mix_in_fwd(z, mod, g, wint, cos, sin):
    B, S, _ = z.shape

    def body(z_ref, mod_ref, g_ref, w_ref, cos_ref, sin_ref, pp_ref, q_ref, k_ref, v_ref, gg_ref, pc_ref):
        h = _rms_mod(z_ref[...], g_ref[...], mod_ref[0:1, :], mod_ref[1:2, :]).astype(BF16)
        p = _nt(h, w_ref[...])
        cos = jnp.tile(cos_ref[...], (1, NH))
        sin = jnp.tile(sin_ref[...], (1, NH))
        pp_ref[...] = p[:, 0:PW]
        q_ref[...] = _rope(p[:, PW:PW + RW], cos, sin)
        k_ref[...] = _rope(p[:, PW + RW:PW + 2 * RW], cos, sin) * K_SCALE
        v_ref[...] = p[:, PW + 2 * RW:PW + 3 * RW]
        gg_ref[...] = p[:, PW + 3 * RW:PW + 4 * RW]
        pc_ref[...] = p[:, PW + 4 * RW:]

    return pl.pallas_call(
        body, name="mix_in_fwd", grid=(B, S // TM),
        in_specs=[_tok(D), _modspec(), _const((1, D)), _wspec(wint), _tabspec(), _tabspec()],
        out_specs=[_tok(PW), _tok(RW), _tok(RW), _tok(RW), _tok(RW), _tok(2 * PW)],
        out_shape=[jax.ShapeDtypeStruct((B, S, PW), F32)] + [jax.ShapeDtypeStruct((B, S, RW), F32)] * 5,
        compiler_params=_params(VMEM_BIG, ("arbitrary", "arbitrary")),
    )(z, mod, g, wint[0], cos, sin)


def _mix_in_bwd(z, dzo, dpp, dqa, dqb, dka, dkb, dva, dvb, dgg, dpc, mod, g, wint, cos, sin):
    B, S, _ = z.shape

    def body(z_ref, dzo_ref, dpp_ref, dqa_ref, dqb_ref, dka_ref, dkb_ref, dva_ref, dvb_ref, dgg_ref, dpc_ref, mod_ref, g_ref,
             w_ref, cos_ref, sin_ref, dz_ref, h_ref, dp_ref, dmod_ref, dg_ref):
        b, t = pl.program_id(0), pl.program_id(1)
        h32, vjp_h = jax.vjp(_rms_mod, z_ref[...], g_ref[...], mod_ref[0:1, :], mod_ref[1:2, :])
        h_ref[...] = h32.astype(BF16)
        cos = jnp.tile(cos_ref[...], (1, NH))
        sin = jnp.tile(sin_ref[...], (1, NH))
        dq = _rope_t(dqa_ref[...] + dqb_ref[...], cos, sin)
        dk = _rope_t(dka_ref[...] + dkb_ref[...], cos, sin) * K_SCALE
        dp = jnp.concatenate([dpp_ref[...], dq, dk, dva_ref[...] + dvb_ref[...], dgg_ref[...], dpc_ref[...]],
                             axis=1).astype(BF16)
        dp_ref[...] = dp
        dh = _nn(dp, w_ref[...])
        dz_h, dg, dshift, dscale = vjp_h(dh)
        dz_ref[...] = dzo_ref[...] + dz_h
        _acc(dmod_ref, jnp.concatenate([dshift, dscale, jnp.zeros_like(dshift)], axis=0), t <= 1)
        _acc(dg_ref, dg, jnp.logical_and(b == 0, t == 0))

    return pl.pallas_call(
        body, name="mix_in_bwd", grid=(B, S // TM),
        in_specs=[_tok(D), _tok(D), _tok(PW)] + [_tok(RW)] * 7 + [_tok(2 * PW), _modspec(), _const((1, D)), _wspec(wint),
                                                                  _tabspec(), _tabspec()],
        out_specs=[_tok(D), _tok(D), _tok(F), _modspec(), _const((1, D))],
        out_shape=[jax.ShapeDtypeStruct((B, S, D), F32), jax.ShapeDtypeStruct((B, S, D), BF16),
                   jax.ShapeDtypeStruct((B, S, F), BF16), jax.ShapeDtypeStruct((B, 2, 3, D), F32),
                   jax.ShapeDtypeStruct((1, D), F32)],
        compiler_params=_params(VMEM_BIG, ("arbitrary", "arbitrary")),
    )(z, dzo, dpp, dqa, dqb, dka, dkb, dva, dvb, dgg, dpc, mod, g, wint[0], cos, sin)


def _log_sigmoid(x):
    return jnp.minimum(x, 0.0) - jnp.log(1.0 + jnp.exp(-jnp.abs(x)))


def _retention(a, b, c, dec_a, dec_b, sched_a, sched_b):
    B, S, _ = a.shape
    C = TM

    def body(a_ref, b_ref, c_ref, da_ref, db_ref, oa_ref, ob_ref):
        ii = lax.broadcasted_iota(jnp.int32, (C, C), 0)
        jj = lax.broadcasted_iota(jnp.int32, (C, C), 1)
        pos = lax.broadcasted_iota(jnp.int32, (C, 1), 0).astype(F32)
        for dec_ref, o_ref, (order, causal, strict) in ((da_ref, oa_ref, sched_a), (db_ref, ob_ref, sched_b)):
            lg = _log_sigmoid(dec_ref[...])
            lg1 = lg[:, 0:1]
            dist = ((ii - jj) if causal else (jj - ii)).astype(F32)
            mask = (dist > 0.0) if strict else (dist >= 0.0)
            decay = jnp.where(mask, jnp.exp(jnp.maximum(dist, 0.0) * lg1), 0.0)
            p = pos if causal else (C - 1.0 - pos)
            w_q = jnp.exp((p + 1.0) * lg1)
            w_k = jnp.exp((C - 1.0 - p) * lg1)
            chunk_decay = jnp.exp(C * lg)
            state = jnp.zeros((HD, HD), F32)
            for n in order:
                rows = pl.ds(n * C, C)
                at, bt, ct = a_ref[rows, :], b_ref[rows, :], c_ref[rows, :]
                cb = ct.astype(BF16)
                scores = _nt(at.astype(BF16), bt.astype(BF16)) * decay
                o = _nn(scores.astype(BF16), cb)
                o = o + _nn((at * w_q).astype(BF16), state.astype(BF16))
                o_ref[rows, :] = o
                state = chunk_decay * state + _tn((bt * w_k).astype(BF16), cb)

    seq = pl.BlockSpec((None, S, HD), lambda b, h: (b, 0, h))
    dspec = pl.BlockSpec((None, 1, HD), lambda b, h: (h, 0, 0))
    return pl.pallas_call(
        body, name="retention", grid=(B, NH),
        in_specs=[seq, seq, seq, dspec, dspec], out_specs=[seq, seq],
        out_shape=[jax.ShapeDtypeStruct((B, S, RW), F32)] * 2,
        compiler_params=_params(VMEM_BIG, ("arbitrary", "arbitrary")),
    )(a, b, c, dec_a, dec_b)


def _retention_ddecay(q, k, v, do, dec_a, dec_b, sched_a, sched_b):
    B, S, _ = q.shape
    C = TM

    def body(q_ref, k_ref, v_ref, do_ref, da_ref, db_ref, o_ref):
        ii = lax.broadcasted_iota(jnp.int32, (C, C), 0)
        jj = lax.broadcasted_iota(jnp.int32, (C, C), 1)
        pos = lax.broadcasted_iota(jnp.int32, (C, 1), 0).astype(F32)
        vals = []
        for dec_ref, (order, causal, strict) in ((da_ref, sched_a), (db_ref, sched_b)):
            x = dec_ref[...]
            lg = _log_sigmoid(x)
            lg1 = lg[:, 0:1]
            dist = ((ii - jj) if causal else (jj - ii)).astype(F32)
            mask = (dist > 0.0) if strict else (dist >= 0.0)
            ddecay = jnp.where(mask, dist * jnp.exp(jnp.maximum(dist, 0.0) * lg1), 0.0)
            p = pos if causal else (C - 1.0 - pos)
            w_q = jnp.exp((p + 1.0) * lg1)
            w_k = jnp.exp((C - 1.0 - p) * lg1)
            chunk_decay = jnp.exp(C * lg)
            state = jnp.zeros((HD, HD), F32)
            dstate = jnp.zeros((HD, HD), F32)
            tot = jnp.zeros((), F32)
            for n in order:
                rows = pl.ds(n * C, C)
                qt, kt, vt, dot = q_ref[rows, :], k_ref[rows, :], v_ref[rows, :], do_ref[rows, :]
                vb = vt.astype(BF16)
                scores = _nt(qt.astype(BF16), kt.astype(BF16))
                dscores = _nt(dot.astype(BF16), vb)
                qw = (qt * w_q).astype(BF16)
                cross = _nn(qw, state.astype(BF16))
                dcross = _nn(qw, dstate.astype(BF16))
                tot = tot + jnp.sum(scores * dscores * ddecay) + jnp.sum(((p + 1.0) * cross + dcross) * dot)
                kv = _tn((kt * w_k).astype(BF16), vb)
                dkv = _tn((kt * ((C - 1.0 - p) * w_k)).astype(BF16), vb)
                dstate = chunk_decay * (dstate + C * state) + dkv
                state = chunk_decay * state + kv
            vals.append(tot * jax.nn.sigmoid(-x))
        row = lax.broadcasted_iota(jnp.int32, (8, HD), 0)
        tile = jnp.where(row == 0, vals[0], 0.0) + jnp.where(row == 1, vals[1], 0.0)
        _acc(o_ref, tile, pl.program_id(1) == 0)

    seq = pl.BlockSpec((None, S, HD), lambda h, b: (b, 0, h))
    dspec = pl.BlockSpec((None, 1, HD), lambda h, b: (h, 0, 0))
    return pl.pallas_call(
        body, name="retention_ddecay", grid=(NH, B),
        in_specs=[seq, seq, seq, seq, dspec, dspec], out_specs=pl.BlockSpec((None, 8, HD), lambda h, b: (h, 0, 0)),
        out_shape=jax.ShapeDtypeStruct((NH, 8, HD), F32),
        compiler_params=_params(VMEM_BIG, ("arbitrary", "arbitrary")),
    )(q, k, v, do, dec_a, dec_b)


def _schedules(S):
    n = S // TM
    lat_up = tuple(range(1, n))
    lat_down = tuple(range(n - 1, 0, -1))
    fwd = (((0,) + lat_up, True, False), ((0,) + lat_down, False, True))
    bwd = ((lat_down + (0,), False, False), (lat_up + (0,), True, True))
    return fwd, bwd


def _shift_rows(x, d):
    if d == 0:
        return x
    S = x.shape[0]
    t = lax.broadcasted_iota(jnp.int32, x.shape, 0)
    tt = t + d
    lo = jnp.where(t < LC, 0, LC)
    hi = jnp.where(t < LC, LC, S)
    return jnp.where((tt >= lo) & (tt < hi), pltpu.roll(x, (-d) % S, 0), 0.0)


@functools.partial(jax.custom_vjp, nondiff_argnums=(1,))
def _shift(x, d):
    return _shift_rows(x, d)


_shift.defvjp(lambda x, d: (_shift_rows(x, d), None), lambda d, _, g: (_shift_rows(g, -d),))


def _pool_fn(p, bd, pscale):
    lane = lax.broadcasted_iota(jnp.int32, p.shape, 1)
    grp = lane // (PW // 4)
    half = jnp.where(grp == 0, 1, jnp.where(grp == 1, 2, jnp.where(grp == 2, 4, 8)))
    ones = jnp.ones(p.shape, F32)
    acc = jnp.zeros(p.shape, F32)
    cnt = jnp.zeros(p.shape, F32)
    for d in range(-8, 8):
        inwin = ((d >= -half) & (d < half)).astype(F32)
        acc = acc + _shift(p, d) * inwin
        cnt = cnt + _shift_rows(ones, d) * inwin
    pooled = acc / cnt - p
    mixed = _nn(pooled.astype(BF16), bd.astype(BF16))
    return mixed * pscale


def _dwconv_raw(zc, dw):
    y = jnp.zeros(zc.shape, F32)
    for k in range(CONV_K):
        y = y + _shift_rows(zc, k - CONV_K // 2) * dw[k:k + 1, :]
    return y


@jax.custom_vjp
def _dwconv(zc, dw):
    return _dwconv_raw(zc, dw)


def _dwconv_fwd(zc, dw):
    return _dwconv_raw(zc, dw), (zc, dw)


def _dwconv_bwd(res, g):
    zc, dw = res
    dz = jnp.zeros(zc.shape, F32)
    ddw = jnp.zeros(dw.shape, F32)
    row = lax.broadcasted_iota(jnp.int32, dw.shape, 0)
    for k in range(CONV_K):
        dz = dz + _shift_rows(g, CONV_K // 2 - k) * dw[k:k + 1, :]
        r = jnp.sum(g * _shift_rows(zc, k - CONV_K // 2), axis=0, keepdims=True)
        ddw = ddw + jnp.where(row == k, r, 0.0)
    return dz, ddw


_dwconv.defvjp(_dwconv_fwd, _dwconv_bwd)


def _conv_fn(u, dw, db):
    zc = u[:, :PW] * jax.nn.sigmoid(u[:, PW:])
    return _dwconv(zc, dw) + db


def _ln_swish(y, lng, lnb):
    mu = jnp.mean(y, axis=-1, keepdims=True)
    yc = y - mu
    var = jnp.mean(yc * yc, axis=-1, keepdims=True)
    yn = yc * lax.rsqrt(var + EPS) * lng + lnb
    return yn * jax.nn.sigmoid(yn)


def _seq(shape, single=False):
    return pl.BlockSpec((None,) + shape, lambda b: (b, 0, 0), pipeline_mode=pl.Buffered(1) if single else None)


def _c1(shape):
    nd = len(shape)
    return pl.BlockSpec(shape, lambda b: (0,) * nd)


def _seq_apply(fn, name, xs, consts, width):
    B, S, w = xs.shape

    def body(x_ref, *refs):
        refs[-1][...] = fn(x_ref[...], *[r[...] for r in refs[:-1]])

    return pl.pallas_call(
        body, name=name, grid=(B,),
        in_specs=[_seq((S, w))] + [_c1(c.shape) for c in consts], out_specs=_seq((S, width)),
        out_shape=jax.ShapeDtypeStruct((B, S, width), F32),
        compiler_params=_params(VMEM_BIG, ("arbitrary",)),
    )(xs, *consts)


def _seq_vjp(fn, name, xs, consts, dout):
    B, S, w = xs.shape
    n = len(consts)

    def body(x_ref, d_ref, *refs):
        first = pl.program_id(0) == 0
        _, vjp = jax.vjp(fn, x_ref[...], *[r[...] for r in refs[:n]])
        grads = vjp(d_ref[...])
        refs[n][...] = grads[0]
        for ref, val in zip(refs[n + 1:], grads[1:]):
            _acc(ref, val, first)

    return pl.pallas_call(
        body, name=name, grid=(B,),
        in_specs=[_seq((S, w), True), _seq((S, dout.shape[2]), True)] + [_c1(c.shape) for c in consts],
        out_specs=[_seq((S, w))] + [_c1(c.shape) for c in consts],
        out_shape=[jax.ShapeDtypeStruct((B, S, w), F32)] + [jax.ShapeDtypeStruct(c.shape, F32) for c in consts],
        compiler_params=_params(VMEM_BIG, ("arbitrary",)),
    )(xs, dout, *consts)


def _pool_conv_fwd(pp, pc, bd, pscale, dw, db):
    return (_seq_apply(_pool_fn, "pool_fwd", pp, (bd, pscale), PW),
            _seq_apply(_conv_fn, "conv_fwd", pc, (dw, db), PW))


def _pool_conv_bwd(pp, pc, dpo, dco, bd, pscale, dw, db):
    dpp, dbd, dps = _seq_vjp(_pool_fn, "pool_bwd", pp, (bd, pscale), dpo)
    dpc, ddw, ddb = _seq_vjp(_conv_fn, "conv_bwd", pc, (dw, db), dco)
    return dpp, dpc, dbd, dps, ddw, ddb


def _cat_fn(po, oa, ob, gg, co, gng, lng, lnb):
    o = oa + ob
    outs = []
    for h in range(NH):
        oh = o[:, h * HD:(h + 1) * HD]
        mu = jnp.mean(oh, axis=-1, keepdims=True)
        oc = oh - mu
        var = jnp.mean(oc * oc, axis=-1, keepdims=True)
        outs.append(oc * lax.rsqrt(var + EPS))
    ret = jnp.concatenate(outs, axis=1) * gng * (gg * jax.nn.sigmoid(gg))
    return jnp.concatenate([po, ret, _ln_swish(co, lng, lnb)], axis=1)


def _mix_out_fwd(z, po, oa, ob, gg, co, ro, mod, wout):
    B, S, _ = z.shape

    def body(z_ref, po_ref, oa_ref, ob_ref, gg_ref, co_ref, gn_ref, lg_ref, lb_ref, mod_ref, w_ref, zo_ref, out_ref):
        cat = _cat_fn(po_ref[...], oa_ref[...], ob_ref[...], gg_ref[...], co_ref[...], gn_ref[...], lg_ref[...], lb_ref[...])
        out = _nn(cat.astype(BF16), w_ref[...])
        out_ref[...] = out
        zo_ref[...] = z_ref[...] + mod_ref[2:3, :] * out

    return pl.pallas_call(
        body, name="mix_out_fwd", grid=(B, S // TM),
        in_specs=[_tok(D), _tok(PW), _tok(RW), _tok(RW), _tok(RW), _tok(PW), _const((1, RW)), _const((1, PW)),
                  _const((1, PW)), _modspec(), _wspec(wout)],
        out_specs=[_tok(D), _tok(D)],
        out_shape=[jax.ShapeDtypeStruct((B, S, D), F32)] * 2,
        compiler_params=_params(None, ("arbitrary", "arbitrary")),
    )(z, po, oa, ob, gg, co, *ro, mod, wout[0])


def _mix_out_bwd(dzo, out, po, oa, ob, gg, co, ro, mod, wout):
    B, S, _ = dzo.shape

    def body(dzo_ref, out_ref, po_ref, oa_ref, ob_ref, gg_ref, co_ref, gn_ref, lg_ref, lb_ref, mod_ref, w_ref,
             dpo_ref, do_ref, dgg_ref, dco_ref, cat_ref, dout_ref, dmod_ref, dgn_ref, dlg_ref, dlb_ref):
        b, t = pl.program_id(0), pl.program_id(1)
        dzo = dzo_ref[...]
        cat, vjp = jax.vjp(_cat_fn, po_ref[...], oa_ref[...], ob_ref[...], gg_ref[...], co_ref[...], gn_ref[...],
                           lg_ref[...], lb_ref[...])
        cat_ref[...] = cat.astype(BF16)
        dout = (mod_ref[2:3, :] * dzo).astype(BF16)
        dout_ref[...] = dout
        dgate = jnp.sum(out_ref[...] * dzo, axis=0, keepdims=True)
        dcat = _nt(dout, w_ref[...])
        dpo, doa, _, dgg, dco, dgn, dlg, dlb = vjp(dcat)
        dpo_ref[...] = dpo
        do_ref[...] = doa
        dgg_ref[...] = dgg
        dco_ref[...] = dco
        zero = jnp.zeros_like(dgate)
        _acc(dmod_ref, jnp.concatenate([zero, zero, dgate], axis=0), t <= 1)
        first = jnp.logical_and(b == 0, t == 0)
        _acc(dgn_ref, dgn, first)
        _acc(dlg_ref, dlg, first)
        _acc(dlb_ref, dlb, first)

    return pl.pallas_call(
        body, name="mix_out_bwd", grid=(B, S // TM),
        in_specs=[_tok(D), _tok(D), _tok(PW), _tok(RW), _tok(RW), _tok(RW), _tok(PW), _const((1, RW)), _const((1, PW)),
                  _const((1, PW)), _modspec(), _wspec(wout)],
        out_specs=[_tok(PW), _tok(RW), _tok(RW), _tok(PW), _tok(D), _tok(D), _modspec(), _const((1, RW)),
                   _const((1, PW)), _const((1, PW))],
        out_shape=[jax.ShapeDtypeStruct((B, S, PW), F32), jax.ShapeDtypeStruct((B, S, RW), F32),
                   jax.ShapeDtypeStruct((B, S, RW), F32), jax.ShapeDtypeStruct((B, S, PW), F32),
                   jax.ShapeDtypeStruct((B, S, D), BF16), jax.ShapeDtypeStruct((B, S, D), BF16),
                   jax.ShapeDtypeStruct((B, 2, 3, D), F32), jax.ShapeDtypeStruct((1, RW), F32),
                   jax.ShapeDtypeStruct((1, PW), F32), jax.ShapeDtypeStruct((1, PW), F32)],
        compiler_params=_params(None, ("arbitrary", "arbitrary")),
    )(dzo, out, po, oa, ob, gg, co, *ro, mod, wout[0])


def _rms(z, g):
    return z * lax.rsqrt(jnp.mean(z * z, axis=-1, keepdims=True) + EPS) * g


def _head(z, target, fg):
    B, S, _ = z.shape

    def body(z_ref, t_ref, g_ref, dz_ref, dg_ref, loss_ref):
        b, t = pl.program_id(0), pl.program_id(1)
        first = jnp.logical_and(b == 0, t == 0)

        @pl.when(t == 0)
        def _():
            dz_ref[...] = jnp.zeros((TM, D), F32)

        @pl.when(first)
        def _():
            dg_ref[...] = jnp.zeros((1, D), F32)
            loss_ref[...] = jnp.zeros((8, 128), F32)

        @pl.when(t > 0)
        def _():
            y, vjp = jax.vjp(_rms, z_ref[...], g_ref[...])
            err = y - t_ref[...]
            dz, dg = vjp(err * (1.0 / D))
            dz_ref[...] = dz
            dg_ref[...] += dg
            loss_ref[...] += 0.5 * jnp.sum(err * err) * (1.0 / D)

    return pl.pallas_call(
        body, name="head", grid=(B, S // TM),
        in_specs=[_tok(D), pl.BlockSpec((None, TM, D), lambda b, t: (b, jnp.maximum(t - 1, 0), 0)), _const((1, D))],
        out_specs=[_tok(D), _const((1, D)), _const((8, 128))],
        out_shape=[jax.ShapeDtypeStruct((B, S, D), F32), jax.ShapeDtypeStruct((1, D), F32),
                   jax.ShapeDtypeStruct((8, 128), F32)],
        compiler_params=_params(None, ("arbitrary", "arbitrary")),
    )(z, target, fg)


MROWS = 24
MCOL = 768


def _silu(x):
    return x * jax.nn.sigmoid(x)


def _mod_fwd(c24, wmod, bmod):
    ncol = wmod.shape[2]

    def body(c_ref, w_ref, b_ref, o_ref):
        sc = _silu(c_ref[...]).astype(BF16)
        o_ref[...] = _nn(sc, w_ref[...].astype(BF16)) + b_ref[...]

    return pl.pallas_call(
        body, name="mod_fwd", grid=(2, ncol // MCOL),
        in_specs=[pl.BlockSpec((MROWS, D), lambda l, j: (0, 0)), pl.BlockSpec((None, D, MCOL), lambda l, j: (l, 0, j)),
                  pl.BlockSpec((None, 1, MCOL), lambda l, j: (l, 0, j))],
        out_specs=pl.BlockSpec((None, MROWS, MCOL), lambda l, j: (l, 0, j)),
        out_shape=jax.ShapeDtypeStruct((2, MROWS, ncol), F32),
        compiler_params=_params(None, ("arbitrary", "arbitrary")),
    )(c24, wmod, bmod)


def _mod_bwd(c24, dmod, wmod):
    ncol = wmod.shape[2]

    def body(c_ref, d_ref, w_ref, dw_ref, dsc_ref):
        l, j = pl.program_id(0), pl.program_id(1)
        sc = _silu(c_ref[...]).astype(BF16)
        dm = d_ref[...].astype(BF16)
        dw_ref[...] = _tn(sc, dm)
        _acc(dsc_ref, _nt(dm, w_ref[...].astype(BF16)), jnp.logical_and(l == 0, j == 0))

    return pl.pallas_call(
        body, name="mod_bwd", grid=(2, ncol // MCOL),
        in_specs=[pl.BlockSpec((MROWS, D), lambda l, j: (0, 0)), pl.BlockSpec((None, MROWS, MCOL), lambda l, j: (l, 0, j)),
                  pl.BlockSpec((None, D, MCOL), lambda l, j: (l, 0, j))],
        out_specs=[pl.BlockSpec((None, D, MCOL), lambda l, j: (l, 0, j)), pl.BlockSpec((MROWS, D), lambda l, j: (0, 0))],
        out_shape=[jax.ShapeDtypeStruct((2, D, ncol), F32), jax.ShapeDtypeStruct((MROWS, D), F32)],
        compiler_params=_params(None, ("arbitrary", "arbitrary")),
    )(c24, dmod, wmod)


def _bmod_cctx_grad(dmod_full, dsc_parts, cctx):
    def body(d_ref, p_ref, c_ref, db_ref, dc_ref):
        db_ref[...] = jnp.sum(d_ref[...], axis=1, keepdims=True)
        tot = jnp.zeros((8, D), F32)
        for s in range(N_CHIP):
            tot = tot + p_ref[s]
        x = c_ref[...]
        sg = jax.nn.sigmoid(x)
        dc_ref[...] = jnp.sum(tot, axis=0, keepdims=True) * (sg * (1.0 + x * (1.0 - sg)))

    return pl.pallas_call(
        body, name="bmod_cctx_grad",
        out_shape=[jax.ShapeDtypeStruct((2, 1, N_MOD * D), F32), jax.ShapeDtypeStruct((1, D), F32)],
    )(dmod_full, dsc_parts, cctx)


def _adam_math(w, g, m, v):
    m = ADAM_B1 * m + (1.0 - ADAM_B1) * g
    v = ADAM_B2 * v + (1.0 - ADAM_B2) * (g * g)
    m_hat = m / (1.0 - ADAM_B1 ** ADAM_STEP)
    v_hat = v / (1.0 - ADAM_B2 ** ADAM_STEP)
    delta = -ADAM_LR * (m_hat / (jnp.sqrt(v_hat) + ADAM_EPS) + ADAM_WD * w)
    return delta, m, v


def _adam(w, g, m, v):
    R, Cc = w.shape
    if R * Cc * 4 <= (1 << 20):
        RB = R
    else:
        RB = 1 << (((1 << 18) // Cc).bit_length() - 1)
        assert R % RB == 0

    def body(w_ref, g_ref, m_ref, v_ref, d_ref, mo_ref, vo_ref):
        d, mn, vn = _adam_math(w_ref[...], g_ref[...], m_ref[...], v_ref[...])
        d_ref[...] = d
        mo_ref[...] = mn
        vo_ref[...] = vn

    spec = pl.BlockSpec((RB, Cc), lambda i: (i, 0))
    return pl.pallas_call(
        body, name="adam", grid=(R // RB,), in_specs=[spec] * 4, out_specs=[spec] * 3,
        out_shape=[jax.ShapeDtypeStruct((R, Cc), F32)] * 3,
        compiler_params=_params(None, ("arbitrary",)),
    )(w, g, m, v)


def _adam_layer(w, g, m, v, layer, prev):
    shape = w.shape
    n, cols = shape[0], shape[-1]
    w3, m3, v3 = (t.reshape(n, -1, cols) for t in (w, m, v))
    g2 = g.reshape(-1, cols)
    R = g2.shape[0]
    RB = max(r for r in range(8, (1 << 18) // cols + 1, 8) if R % r == 0)

    def body(w_ref, g_ref, m_ref, v_ref, *refs):
        go_ref, d_ref, mo_ref, vo_ref = refs[-4:]
        gt = g_ref[...]
        d, mn, vn = _adam_math(w_ref[...], gt, m_ref[...], v_ref[...])
        go_ref[...] = gt
        d_ref[...] = d
        mo_ref[...] = mn
        vo_ref[...] = vn

    lay = pl.BlockSpec((None, RB, cols), lambda i: (layer, i, 0))
    flat = pl.BlockSpec((RB, cols), lambda i: (i, 0))
    hold = [] if prev is None else [t.reshape(n, -1, cols) for t in prev]
    outs = pl.pallas_call(
        body, name="adam_layer", grid=(R // RB,),
        in_specs=[lay, flat, lay, lay] + [pl.BlockSpec(memory_space=pl.ANY)] * len(hold), out_specs=[lay] * 4,
        out_shape=[jax.ShapeDtypeStruct(w3.shape, F32)] * 4,
        input_output_aliases={4 + k: k for k in range(len(hold))},
        compiler_params=_params(None, ("arbitrary",)),
    )(w3, g2, m3, v3, *hold)
    return [o.reshape(shape) for o in outs]


def _sum_devices(parts):
    K = parts.shape[1]

    def body(p_ref, o_ref):
        tot = p_ref[0]
        for i in range(1, N_DEV):
            tot = tot + p_ref[i]
        o_ref[...] = tot

    return pl.pallas_call(body, name="sum_devices", out_shape=jax.ShapeDtypeStruct((K, 128), F32))(parts)


def _sum_pieces(own, others):
    R = own.shape[0]
    RB = 96 if R % 96 == 0 else 32

    def body(a_ref, r_ref, o_ref):
        tot = a_ref[...].astype(F32)
        for i in range(N_DEV - 1):
            tot = tot + r_ref[i].astype(F32)
        o_ref[...] = tot

    return pl.pallas_call(
        body, name="sum_pieces", grid=(R // RB,),
        in_specs=[pl.BlockSpec((RB, D), lambda i: (i, 0)), pl.BlockSpec((N_DEV - 1, RB, D), lambda i: (0, i, 0))],
        out_specs=pl.BlockSpec((RB, D), lambda i: (i, 0)),
        out_shape=jax.ShapeDtypeStruct((R, D), F32),
        compiler_params=_params(None, ("arbitrary",)),
    )(own, others)


def _coords():
    return lax.axis_index("x"), lax.axis_index("y"), lax.axis_index("c")


_FLIPS = [(fx, fy, fc) for fx in (0, 1) for fy in (0, 1) for fc in (0, 1)][1:]


def _all_gather_small(buf):
    K = buf.shape[0]

    def body(in_ref, out_ref, send_sems, recv_sems, local_sem):
        x, y, c = _coords()
        me = 4 * x + 2 * y + c
        mine = pltpu.make_async_copy(in_ref, out_ref.at[me], local_sem)
        mine.start()
        sends = []
        for k, (fx, fy, fc) in enumerate(_FLIPS):
            peer = (x ^ fx, y ^ fy, c ^ fc)
            cp = pltpu.make_async_remote_copy(src_ref=in_ref, dst_ref=out_ref.at[me], send_sem=send_sems.at[k],
                                              recv_sem=recv_sems.at[k], device_id=peer, device_id_type=MESH)
            cp.start()
            sends.append(cp)
        for k, (fx, fy, fc) in enumerate(_FLIPS):
            src = 4 * (x ^ fx) + 2 * (y ^ fy) + (c ^ fc)
            pltpu.make_async_remote_copy(src_ref=in_ref, dst_ref=out_ref.at[src], send_sem=send_sems.at[k],
                                         recv_sem=recv_sems.at[k], device_id=(x, y, c), device_id_type=MESH).wait_recv()
        for cp in sends:
            cp.wait_send()
        mine.wait()

    return pl.pallas_call(
        body, name="all_gather_small",
        in_specs=[pl.BlockSpec(memory_space=pltpu.VMEM)], out_specs=pl.BlockSpec(memory_space=pltpu.VMEM),
        out_shape=jax.ShapeDtypeStruct((N_DEV, K, 128), F32),
        scratch_shapes=[pltpu.SemaphoreType.DMA((7,)), pltpu.SemaphoreType.DMA((7,)), pltpu.SemaphoreType.DMA],
        compiler_params=_params(VMEM_BIG),
    )(buf)


_CHIP_FLIPS = [(1, 0), (0, 1), (1, 1)]


def _exchange_rows(buf, B):
    K = buf.shape[1]
    nex = N_DEV * B

    def body(in_ref, out_ref, send_sems, recv_sems, local_sems):
        x, y, c = _coords()
        s_me = 2 * x + y

        def parts(dev, chip_slot):
            return ((in_ref.at[pl.ds(B * dev, B)], out_ref.at[chip_slot, pl.ds(0, B)]),
                    (in_ref.at[pl.ds(nex, 1)], out_ref.at[chip_slot, pl.ds(B, 1)]))

        locals_ = [pltpu.make_async_copy(src, dst, local_sems.at[i]) for i, (src, dst) in enumerate(parts(4 * x + 2 * y + c, s_me))]
        for cp in locals_:
            cp.start()
        sends = []
        for j, (fx, fy) in enumerate(_CHIP_FLIPS):
            px, py = x ^ fx, y ^ fy
            for i, (src, dst) in enumerate(parts(4 * px + 2 * py + c, s_me)):
                cp = pltpu.make_async_remote_copy(src_ref=src, dst_ref=dst, send_sem=send_sems.at[2 * j + i],
                                                  recv_sem=recv_sems.at[2 * j + i], device_id=(px, py, c), device_id_type=MESH)
                cp.start()
                sends.append(cp)
        for j, (fx, fy) in enumerate(_CHIP_FLIPS):
            for i, (src, dst) in enumerate(parts(0, 2 * (x ^ fx) + (y ^ fy))):
                pltpu.make_async_remote_copy(src_ref=src, dst_ref=dst, send_sem=send_sems.at[2 * j + i],
                                             recv_sem=recv_sems.at[2 * j + i], device_id=(x, y, c), device_id_type=MESH).wait_recv()
        for cp in sends:
            cp.wait_send()
        for cp in locals_:
            cp.wait()

    return pl.pallas_call(
        body, name="exchange_rows",
        in_specs=[pl.BlockSpec(memory_space=pltpu.VMEM)], out_specs=pl.BlockSpec(memory_space=pltpu.VMEM),
        out_shape=jax.ShapeDtypeStruct((N_CHIP, B + 1, K, 128), F32),
        scratch_shapes=[pltpu.SemaphoreType.DMA((6,)), pltpu.SemaphoreType.DMA((6,)), pltpu.SemaphoreType.DMA((2,))],
    )(buf)


_HBM = pl.BlockSpec(memory_space=pltpu.HBM)
_SEMS = pl.BlockSpec(memory_space=pltpu.SEMAPHORE)
_EFFECT = pltpu.SideEffectType.DATAFLOW_SIDE_EFFECTING


def _in_hbm(v):
    return pltpu.with_memory_space_constraint(v, pltpu.HBM)


def _copies_start(name, srcs, lands, n_sems, issue, after):
    ns, nl = len(srcs), len(lands)

    def body(*refs):
        src_refs, land_refs = refs[:ns], refs[ns:ns + nl]
        out = refs[ns + nl + 1:]
        issue(src_refs, land_refs, out[:nl], out[nl:2 * nl])
        out[-1][...] = jnp.zeros((8, 128), F32)

    outs = pl.pallas_call(
        body, name=name, in_specs=[_HBM] * (ns + nl) + [pl.BlockSpec(memory_space=pl.ANY)],
        out_specs=[_SEMS] * (2 * nl) + [_HBM] * (ns + nl) + [pl.BlockSpec(memory_space=pltpu.VMEM)],
        out_shape=[pltpu.SemaphoreType.DMA((n_sems,))] * (2 * nl) + [pltpu.HBM(v.shape, v.dtype) for v in (*srcs, *lands)]
        + [jax.ShapeDtypeStruct((8, 128), F32)],
        input_output_aliases={i: 2 * nl + i for i in range(ns + nl)},
        compiler_params=pltpu.CompilerParams(has_side_effects=_EFFECT),
    )(*[_in_hbm(v) for v in (*srcs, *lands)], after)
    return outs[:nl], outs[nl:2 * nl], outs[2 * nl:2 * nl + ns], outs[2 * nl + ns:2 * nl + ns + nl], outs[-1]


def _copies_wait(name, srcs, lands, send_sems, recv_sems, finish, after):
    after = list(after) if isinstance(after, (list, tuple)) else [after]
    ns, nl = len(srcs), len(lands)

    def body(*refs):
        src_refs, land_refs = refs[:ns], refs[ns:ns + nl]
        finish(src_refs, land_refs, refs[ns + nl:ns + 2 * nl], refs[ns + 2 * nl:ns + 3 * nl])

    outs = pl.pallas_call(
        body, name=name, in_specs=[_HBM] * (ns + nl) + [_SEMS] * (2 * nl) + [pl.BlockSpec(memory_space=pl.ANY)] * len(after),
        out_specs=[_HBM] * (ns + nl), out_shape=[pltpu.HBM(v.shape, v.dtype) for v in (*srcs, *lands)],
        input_output_aliases={i: i for i in range(ns + nl)},
        compiler_params=pltpu.CompilerParams(has_side_effects=_EFFECT),
    )(*srcs, *lands, *send_sems, *recv_sems, *after)
    return outs[:ns], outs[ns:]


def _own_slab(land, mine, index):
    return lax.dynamic_update_slice_in_dim(land, mine[:, None], index, axis=1)


def _gather_start(units, after):
    x, y, c = _coords()
    chip = 2 * x + y
    lands = [_own_slab(lax.empty((u.shape[0], N_CHIP) + u.shape[1:], u.dtype), u, chip) for u in units]

    def issue(src_refs, land_refs, send_sems, recv_sems):
        x, y, c = _coords()
        s_me = 2 * x + y
        for i, (src, land) in enumerate(zip(src_refs, land_refs)):
            for j, (fx, fy) in enumerate(_CHIP_FLIPS):
                pltpu.make_async_remote_copy(src_ref=src, dst_ref=land.at[:, s_me], send_sem=send_sems[i].at[j],
                                             recv_sem=recv_sems[i].at[j], device_id=(x ^ fx, y ^ fy, c),
                                             device_id_type=MESH).start()

    return _copies_start("gather_start", units, lands, 3, issue, after)


def _gather_wait(tag, started, which, after):
    send_sems, recv_sems, srcs, lands, _ = started

    def finish(src_refs, land_refs, ssems, rsems):
        x, y, c = _coords()
        for src, land, ss, rs in zip(src_refs, land_refs, ssems, rsems):
            for j in range(3):
                cp = pltpu.make_async_remote_copy(src_ref=src, dst_ref=land.at[:, 0], send_sem=ss.at[j], recv_sem=rs.at[j],
                                                  device_id=(x, y, c), device_id_type=MESH)
                cp.wait_send()
                cp.wait_recv()

    _, done = _copies_wait("gather_wait_" + tag, [srcs[i] for i in which], [lands[i] for i in which],
                           [send_sems[i] for i in which], [recv_sems[i] for i in which], finish, after)
    return [d.reshape(d.shape[0], N_CHIP * d.shape[2], D) for d in done]


def _piece_rows(grads):
    return [g.shape[2] for g in grads]


def _scatter_start(tag, grads, after):
    rows = _piece_rows(grads)
    land = lax.empty((N_DEV - 1, sum(rows), D), grads[0].dtype)

    def issue(src_refs, land_refs, send_sems, recv_sems):
        x, y, c = _coords()
        for k, (fx, fy, fc) in enumerate(_FLIPS):
            px, py, pc = x ^ fx, y ^ fy, c ^ fc
            off = 0
            for src, n in zip(src_refs, rows):
                pltpu.make_async_remote_copy(src_ref=src.at[2 * px + py, pc], dst_ref=land_refs[0].at[k, pl.ds(off, n)],
                                             send_sem=send_sems[0].at[k], recv_sem=recv_sems[0].at[k],
                                             device_id=(px, py, pc), device_id_type=MESH).start()
                off += n

    return _copies_start("scatter_start_" + tag, grads, [land], N_DEV - 1, issue, after)


def _scatter_wait(tag, started, after):
    send_sems, recv_sems, srcs, lands, _ = started

    def finish(src_refs, land_refs, ssems, rsems):
        x, y, c = _coords()
        for k in range(N_DEV - 1):
            cp = pltpu.make_async_remote_copy(src_ref=land_refs[0].at[0], dst_ref=land_refs[0].at[0], send_sem=ssems[0].at[k],
                                              recv_sem=rsems[0].at[k], device_id=(x, y, c), device_id_type=MESH)
            cp.wait_send()
            cp.wait_recv()

    grads, (others,) = _copies_wait("scatter_wait_" + tag, srcs, lands, send_sems, recv_sems, finish, after)
    return grads, others


def _swap_start(tag, mine, rows, after):
    x, y, c = _coords()
    offs = [sum(rows[:t]) for t in range(len(rows))]
    lands = [lax.dynamic_update_slice_in_dim(lax.empty((2, n, D), mine.dtype), mine[o:o + n][None], c, axis=0)
             for o, n in zip(offs, rows)]

    def issue(src_refs, land_refs, send_sems, recv_sems):
        x, y, c = _coords()
        for t, (o, n) in enumerate(zip(offs, rows)):
            pltpu.make_async_remote_copy(src_ref=src_refs[0].at[pl.ds(o, n)], dst_ref=land_refs[t].at[c],
                                         send_sem=send_sems[t].at[0], recv_sem=recv_sems[t].at[0],
                                         device_id=(x, y, 1 - c), device_id_type=MESH).start()

    return _copies_start("swap_start_" + tag, [mine], lands, 1, issue, after)


def _swap_wait(tag, started, after):
    send_sems, recv_sems, srcs, lands, _ = started

    def finish(src_refs, land_refs, ssems, rsems):
        x, y, c = _coords()
        for land, ss, rs in zip(land_refs, ssems, rsems):
            cp = pltpu.make_async_remote_copy(src_ref=land.at[0], dst_ref=land.at[0], send_sem=ss.at[0], recv_sem=rs.at[0],
                                              device_id=(x, y, c), device_id_type=MESH)
            cp.wait_send()
            cp.wait_recv()

    return _copies_wait("swap_wait_" + tag, srcs, lands, send_sems, recv_sems, finish, after)[1]


def _size(shape):
    n = 1
    for d in shape:
        n *= d
    return n


def _pack(arrays):
    return jnp.concatenate([jnp.pad(a.reshape(-1).astype(F32), (0, (-a.size) % 1024)).reshape(-1, 128) for a in arrays], axis=0)


def _unpack(buf, shapes):
    out, row = [], 0
    for s in shapes:
        n = _size(s)
        nrows = 8 * -(-n // 1024)
        out.append(buf[row:row + nrows].reshape(-1)[:n].reshape(s))
        row += nrows
    return out


def _block_diag(pw):
    bd = jnp.zeros((PW, PW), F32)
    g = PW // 4
    for i in range(4):
        bd = bd.at[i * g:(i + 1) * g, i * g:(i + 1) * g].set(pw[i])
    return bd


def _lanes(v):
    return jnp.broadcast_to(v.reshape(NH, 1, 1), (NH, 1, HD))


def _layer_fwd(z, mod, normg, wget, small):
    S = z.shape[1]
    cos, sin = _rope_tables(S)
    fwd_sched, _ = _schedules(S)
    wa = wget("a", z)
    z1, f_a = _ffn_fwd(z, mod[:, :, 0], normg[0], *wa)
    wm = wget("m", z1)
    pp, q, k, v, gg, pc = _mix_in_fwd(z1, mod[:, :, 1], normg[1], wm[0], cos, sin)
    po, co = _pool_conv_fwd(pp, pc, small["bd"], small["pscale"], small["dw"], small["db"])
    ro = (small["gng"], small["lng"], small["lnb"])
    oa, ob = _retention(q, k, v, small["dec_f"], small["dec_b"], *fwd_sched)
    z2, out = _mix_out_fwd(z1, po, oa, ob, gg, co, ro, mod[:, :, 1], wm[1])
    wb = wget("b", z2)
    z3, f_b = _ffn_fwd(z2, mod[:, :, 2], normg[2], *wb)
    saved = dict(z=z, f_a=f_a, z1=z1, pp=pp, q=q, k=k, v=v, gg=gg, pc=pc, po=po, co=co, oa=oa, ob=ob, out=out, z2=z2, f_b=f_b,
                 wa=wa, wm=wm, wb=wb)
    return z3, saved


def _layer_bwd(dz3, sv, mod, normg, small, emit, tok, last):
    S = dz3.shape[1]
    B = dz3.shape[0]
    T = B * S
    cos, sin = _rope_tables(S)
    fwd_sched, bwd_sched = _schedules(S)
    wa, wm, wb = sv["wa"], sv["wm"], sv["wb"]
    dz2, dmod_b, dg_b, gw1t_b, gw3t_b, gw2_b = _ffn_bwd(sv["z2"], dz3, sv["f_b"], mod[:, :, 2] + tok, normg[2], *wb)
    tok = emit("b", [gw1t_b, gw3t_b, gw2_b])
    mod_m = mod[:, :, 1] + tok
    ro = (small["gng"], small["lng"], small["lnb"])
    dpo, do, dgg, dco, cat, dout, dmod_gate, dgng, dlng, dlnb = _mix_out_bwd(
        dz2, sv["out"], sv["po"], sv["oa"], sv["ob"], sv["gg"], sv["co"], ro, mod_m, wm[1])
    gwout = _tn_matmul(cat.reshape(T, D), dout.reshape(T, D))
    dqa, dqb = _retention(do, sv["v"], sv["k"], small["dec_f"], small["dec_b"], *fwd_sched)
    dka, dkb = _retention(sv["v"], do, sv["q"], small["dec_f"], small["dec_b"], *bwd_sched)
    dva, dvb = _retention(sv["k"], sv["q"], do, small["dec_f"], small["dec_b"], *bwd_sched)
    ddec = _retention_ddecay(sv["q"], sv["k"], sv["v"], do, small["dec_f"], small["dec_b"], *fwd_sched)
    dpp, dpc, dbd, dps, ddw, ddb = _pool_conv_bwd(sv["pp"], sv["pc"], dpo, dco, small["bd"], small["pscale"],
                                                   small["dw"], small["db"])
    dz1, h, dp, dmod_m, dg_m = _mix_in_bwd(sv["z1"], dz2, dpp, dqa, dqb, dka, dkb, dva, dvb, dgg, dpc, mod_m, normg[1],
                                           wm[0], cos, sin)
    gwint = _tn_matmul(dp.reshape(T, F), h.reshape(T, D))
    tok = emit("m", [gwint, gwout])
    if last:
        dz, dmod_a, dg_a = _ffn_bwd(sv["z"], dz1, sv["f_a"], mod[:, :, 0] + tok, normg[0], *wa,
                                    each=lambda grad: emit("a", [grad]))[:3]
    else:
        dz, dmod_a, dg_a, gw1t_a, gw3t_a, gw2_a = _ffn_bwd(sv["z"], dz1, sv["f_a"], mod[:, :, 0] + tok, normg[0], *wa)
        tok = emit("a", [gw1t_a, gw3t_a, gw2_a])
    dmod = jnp.stack([dmod_a, dmod_m + dmod_gate, dmod_b], axis=2)
    dnormg = jnp.stack([dg_a, dg_m, dg_b], axis=0)
    g = PW // 4
    dpool_w = jnp.stack([dbd[i * g:(i + 1) * g, i * g:(i + 1) * g] for i in range(4)], axis=0)
    sm = dict(pool_w=dpool_w, pool_scale=dps[0], dec_f=ddec[:, 0, 0], dec_b=ddec[:, 1, 0], gng=dgng[0],
              conv_dw=ddw[0:CONV_K], conv_b=ddb[0], conv_ln_g=dlng[0], conv_ln_b=dlnb[0])
    return dz, dmod, dnormg, sm, tok


def _small_params(pool_w, pool_scale, dec_f, dec_b, gng, conv_dw, conv_b, lng, lnb):
    return dict(bd=_block_diag(pool_w), pscale=pool_scale.reshape(1, PW), dec_f=_lanes(dec_f), dec_b=_lanes(dec_b),
                gng=gng.reshape(1, RW), dw=jnp.pad(conv_dw, ((0, 1), (0, 0))), db=conv_b.reshape(1, PW),
                lng=lng.reshape(1, PW), lnb=lnb.reshape(1, PW))


_WEIGHTS = ["c_ctx", "w_mod", "b_mod", "norm_g", "ffn_w1", "ffn_w3", "ffn_w2", "w_in", "w_out", "pool_w", "pool_scale",
            "ret_decay_fwd", "ret_decay_bwd", "ret_gn_g", "conv_dw", "conv_b", "conv_ln_g", "conv_ln_b", "final_g"]
_BIG = ["w_mod", "ffn_w1", "ffn_w3", "ffn_w2", "w_in", "w_out"]
_SMALL = [n for n in _WEIGHTS if n not in _BIG]


def _adam_any(w, g, m, v):
    shape = w.shape
    cols = shape[-1] if w.ndim >= 2 else 128
    outs = _adam(w.reshape(-1, cols), g.reshape(-1, cols), m.reshape(-1, cols), v.reshape(-1, cols))
    return [o.reshape(shape) for o in outs]


def _step(a):
    x, c, ctx = a["x"], a["c"], a["ctx"]
    B = x.shape[0]
    nex = N_DEV * B
    assert nex + B <= MROWS and ctx.shape[1] == LC and x.shape[1] % TM == 0
    xi, yi, ci = _coords()
    me = 4 * xi + 2 * yi + ci
    chip = 2 * xi + yi
    ncol = a["w_mod"].shape[2]

    def t_bf16(w):
        return jnp.swapaxes(w, -1, -2).astype(BF16)

    w1t, w3t, w2 = t_bf16(a["ffn_w1"]), t_bf16(a["ffn_w3"]), a["ffn_w2"].astype(BF16)
    wint, wout = t_bf16(a["w_in"]), a["w_out"].astype(BF16)
    units = []
    for l in range(2):
        units += [jnp.stack([w1t[l, 0], w3t[l, 0], w2[l, 0]]), wint[l][None], wout[l][None],
                  jnp.stack([w1t[l, 1], w3t[l, 1], w2[l, 1]])]

    shapes1 = [(B, D), (2, 3, D // N_CHIP), (2, CONV_K, PW // N_CHIP)]
    g1 = _all_gather_small(_pack([c, a["norm_g"], a["conv_dw"]]))
    per = [_unpack(g1[d], shapes1) for d in range(N_DEV)]
    c_all = jnp.concatenate([per[d][0] for d in range(N_DEV)], axis=0)
    norm_g_full = jnp.concatenate([per[2 * s][1] for s in range(N_CHIP)], axis=-1)
    conv_dw_full = jnp.concatenate([per[2 * s][2] for s in range(N_CHIP)], axis=-1)
    cctx = a["c_ctx"].reshape(1, D)
    c24 = jnp.concatenate([c_all] + [cctx] * B + [jnp.zeros((MROWS - nex - B, D), F32)], axis=0)

    bsh = lax.dynamic_slice(a["b_mod"], (0, chip * ncol), (2, ncol)).reshape(2, 1, ncol)
    mod_raw = _mod_fwd(c24, a["w_mod"], bsh)
    mine = _exchange_rows(jnp.swapaxes(mod_raw, 0, 1).reshape(MROWS, 2 * ncol // 128, 128), B)
    mod_mine = jnp.concatenate([mine[s].reshape(B + 1, 2, ncol) for s in range(N_CHIP)], axis=-1)
    mods = []
    for l in range(2):
        cx = jnp.broadcast_to(mod_mine[B, l][None], (B, N_MOD * D))
        mods.append(jnp.stack([cx, mod_mine[:B, l]], axis=1).reshape(B, 2, 3, 3, D))

    started = _gather_start(units, mod_mine)

    def wget_of(l):
        def wget(stage, after):
            if stage == "m":
                win, wo = _gather_wait(f"m{l}", started, [4 * l + 1, 4 * l + 2], after)
                return (win, 0), (wo, 0)
            (g,) = _gather_wait(f"{stage}{l}", started, [4 * l + (0 if stage == "a" else 3)], after)
            return (g, 0), (g, 1), (g, 2)
        return wget

    smalls = [_small_params(a["pool_w"][l], a["pool_scale"][l], a["ret_decay_fwd"][l], a["ret_decay_bwd"][l],
                            a["ret_gn_g"][l], conv_dw_full[l], a["conv_b"][l], a["conv_ln_g"][l], a["conv_ln_b"][l])
              for l in range(2)]
    normgs = [norm_g_full[l].reshape(3, 1, D) for l in range(2)]
    z = jnp.concatenate([ctx, x], axis=1)
    saved = []
    for l in range(2):
        z, sv = _layer_fwd(z, mods[l], normgs[l], wget_of(l), smalls[l])
        saved.append(sv)
    dz, dfinal_g, loss_part = _head(z, a["loss_target"], a["final_g"].reshape(1, D))

    scattering, swapping, reduced, newest = [], [], {}, []

    def reduce_oldest(after):
        tag, st, rows = scattering.pop(0)
        grads, others = _scatter_wait(tag, st, after)
        own = jnp.concatenate([lax.dynamic_slice(g, (chip, ci, 0, 0), (1, 1) + g.shape[2:]).reshape(g.shape[2:])
                               for g in grads], axis=0)
        mine = _sum_pieces(own, others)
        sw = _swap_start(tag, mine, rows, mine)
        if swapping:
            ptag, psw = swapping.pop()
            reduced[ptag] = _swap_wait(ptag, psw, sw[4])
        swapping.append((tag, sw))
        return sw[4]

    def emit_of(l):
        def emit(stage, grads):
            grads = [g.reshape(N_CHIP, 2, g.shape[0] // (2 * N_CHIP), D) for g in grads]
            stem = f"{stage}{l}"
            same = sum(1 for t, _, _ in scattering if t.split("_")[0] == stem)
            tag = stem if same == 0 else f"{stem}_{same}"
            st = _scatter_start(tag, grads, grads[0])
            tok = st[4][0, 0]
            while scattering and scattering[0][0].split("_")[0] != stem:
                tok = tok + reduce_oldest(st[4])[0, 0]
            scattering.append((tag, st, _piece_rows(grads)))
            newest[:] = [st[4]] + (newest if same else [])
            return tok
        return emit

    back = [None, None]
    tok = jnp.zeros((), F32)
    for l in (1, 0):
        dz, dmod, dnormg, sm, tok = _layer_bwd(dz, saved[l], mods[l], normgs[l], smalls[l], emit_of(l), tok, l == 0)
        back[l] = (dmod, dnormg, None, sm)
    grad_x = dz[:, LC:]
    grads = {}

    dmods = [back[l][0].reshape(B, 2, N_MOD * D) for l in range(2)]
    pack_a = _pack([jnp.stack([dm[:, 1] for dm in dmods])])
    ka = pack_a.shape[0]
    sm = [back[l][3] for l in range(2)]
    sum_list = [jnp.stack([dm[:, 0] for dm in dmods]), jnp.stack([back[l][1][:, 0] for l in range(2)])]
    sm_keys = ["pool_w", "pool_scale", "dec_f", "dec_b", "gng", "conv_dw", "conv_b", "conv_ln_g", "conv_ln_b"]
    sum_list += [jnp.stack([sm[l][k] for l in range(2)]) for k in sm_keys]
    sum_list += [dfinal_g[0], loss_part[0, 0:1]]
    sum_shapes = [s.shape for s in sum_list]
    g3 = _all_gather_small(jnp.concatenate([pack_a, _pack(sum_list)], axis=0))
    dmx_all = jnp.concatenate([_unpack(g3[d, :ka], [(2, B, N_MOD * D)])[0] for d in range(N_DEV)], axis=1)
    summed = _unpack(_sum_devices(g3[:, ka:]), sum_shapes)
    dmy, dnorm_full = summed[0], summed[1]
    sgrad = dict(zip(sm_keys, summed[2:2 + len(sm_keys)]))
    loss = summed[-1].reshape(())

    dmod24 = jnp.concatenate([dmx_all, dmy, jnp.zeros((2, MROWS - nex - B, N_MOD * D), F32)], axis=1)
    dmod_my = lax.dynamic_slice(dmod24, (0, 0, chip * ncol), (2, MROWS, ncol))
    grads["w_mod"], dsc = _mod_bwd(c24, dmod_my, a["w_mod"])
    g4 = _all_gather_small(_pack([dsc[nex:nex + 8]]))
    dsc_parts = jnp.stack([_unpack(g4[2 * s], [(8, D)])[0] for s in range(N_CHIP)])
    dbmod, dcctx = _bmod_cctx_grad(dmod24, dsc_parts, cctx)

    grads["c_ctx"] = dcctx[0]
    grads["b_mod"] = dbmod.reshape(2, N_MOD * D)
    grads["norm_g"] = lax.dynamic_slice(dnorm_full, (0, 0, chip * (D // N_CHIP)), (2, 3, D // N_CHIP))
    grads["pool_w"] = sgrad["pool_w"]
    grads["pool_scale"] = sgrad["pool_scale"]
    grads["ret_decay_fwd"] = sgrad["dec_f"]
    grads["ret_decay_bwd"] = sgrad["dec_b"]
    grads["ret_gn_g"] = sgrad["gng"]
    grads["conv_dw"] = lax.dynamic_slice(sgrad["conv_dw"], (0, 0, chip * (PW // N_CHIP)), (2, CONV_K, PW // N_CHIP))
    grads["conv_b"] = sgrad["conv_b"]
    grads["conv_ln_g"] = sgrad["conv_ln_g"]
    grads["conv_ln_b"] = sgrad["conv_ln_b"]
    grads["final_g"] = summed[-2]

    delta, new_m, new_v = {}, {}, {}
    delta["w_mod"], new_m["w_mod"], new_v["w_mod"] = _adam_any(a["w_mod"], grads["w_mod"], a["m_w_mod"], a["v_w_mod"])
    shapes_s = [a[n].shape for n in _SMALL]
    packed = _adam(_pack([a[n] for n in _SMALL]), _pack([grads[n] for n in _SMALL]),
                   _pack([a["m_" + n] for n in _SMALL]), _pack([a["v_" + n] for n in _SMALL]))
    for res, out in zip(packed, (delta, new_m, new_v)):
        for n, val in zip(_SMALL, _unpack(res, shapes_s)):
            out[n] = val

    def group(stem):
        out, k = list(reduced[stem]), 1
        while f"{stem}_{k}" in reduced:
            out += reduced[f"{stem}_{k}"]
            k += 1
        return out

    def layer_grads(l):
        ffn = [[h.reshape(-1, D) for h in group(f"{stage}{l}")] for stage in "ab"]
        win, wo = [h.reshape(-1, D) for h in reduced[f"m{l}"]]
        return dict(ffn_w1=jnp.stack([ffn[i][0] for i in range(2)]), ffn_w3=jnp.stack([ffn[i][1] for i in range(2)]),
                    ffn_w2=jnp.stack([ffn[i][2] for i in range(2)]), w_in=win, w_out=wo)

    turned = ("ffn_w1", "ffn_w3", "w_in")

    def adam_of(n, g, layer, prev):
        t = (lambda u: jnp.swapaxes(u, -1, -2)) if n in turned else (lambda u: u)
        return _adam_layer(t(a[n]), g, t(a["m_" + n]), t(a["v_" + n]), layer, prev)

    g1 = layer_grads(1)
    half = {n: adam_of(n, g1[n], 1, None) for n in _BIG[1:]}

    last = [delta["w_mod"], packed[0]] + [half[n][1] for n in _BIG[1:]] + newest
    while scattering:
        last = reduce_oldest(last)
    tag, sw = swapping.pop()
    reduced[tag] = _swap_wait(tag, sw, last)
    g0 = layer_grads(0)
    for n in _BIG[1:]:
        res = adam_of(n, g0[n], 0, half[n])
        grads[n], delta[n], new_m[n], new_v[n] = [jnp.swapaxes(r, -1, -2) for r in res] if n in turned else res
    return (loss, grad_x, *[grads[n] for n in _WEIGHTS], *[delta[n] for n in _WEIGHTS],
            *[new_m[n] for n in _WEIGHTS], *[new_v[n] for n in _WEIGHTS])


def kernel(x, c, ctx, c_ctx, w_mod, b_mod, norm_g, ffn_w1, ffn_w3, ffn_w2, w_in, w_out, pool_w, pool_scale, ret_decay_fwd, ret_decay_bwd, ret_gn_g, conv_dw, conv_b, conv_ln_g, conv_ln_b, final_g, loss_target, m_c_ctx, m_w_mod, m_b_mod, m_norm_g, m_ffn_w1, m_ffn_w3, m_ffn_w2, m_w_in, m_w_out, m_pool_w, m_pool_scale, m_ret_decay_fwd, m_ret_decay_bwd, m_ret_gn_g, m_conv_dw, m_conv_b, m_conv_ln_g, m_conv_ln_b, m_final_g, v_c_ctx, v_w_mod, v_b_mod, v_norm_g, v_ffn_w1, v_ffn_w3, v_ffn_w2, v_w_in, v_w_out, v_pool_w, v_pool_scale, v_ret_decay_fwd, v_ret_decay_bwd, v_ret_gn_g, v_conv_dw, v_conv_b, v_conv_ln_g, v_conv_ln_b, v_final_g):
    return _step(dict(locals()))
```

```python
import functools

import jax
import jax.numpy as jnp
from jax import lax
from jax.experimental import pallas as pl
from jax.experimental.pallas import tpu as pltpu

F32 = jnp.float32
BF16 = jnp.bfloat16

D = 1024
F = 2816
FH = 1408
N_MOD = 9
LC = 256
TM = 256
HD = 128
NH = 4
RW = 512
PW = 256
CONV_K = 31
GRID_W = 64
EPS = 1e-6
K_SCALE = HD ** -0.5
N_DEV = 8
N_CHIP = 4
SLAB = F // N_CHIP
HSLAB = SLAB // 2
OSLAB = D // N_CHIP
HOSLAB = OSLAB // 2
VMEM_BIG = 60 * 1024 * 1024
MESH = pl.DeviceIdType.MESH

ADAM_LR = 0.001
ADAM_B1 = 0.9
ADAM_B2 = 0.999
ADAM_EPS = 1e-08
ADAM_WD = 0.01
ADAM_STEP = 10


def _nt(a, b):
    return lax.dot_general(a, b, (((1,), (1,)), ((), ())), preferred_element_type=F32)


def _nn(a, b):
    return lax.dot_general(a, b, (((1,), (0,)), ((), ())), preferred_element_type=F32)


def _tn(a, b):
    return lax.dot_general(a, b, (((0,), (0,)), ((), ())), preferred_element_type=F32)


def _params(vmem=None, sem=None):
    return pltpu.CompilerParams(dimension_semantics=sem, vmem_limit_bytes=vmem)


def _rms_mod(z, g, shift, scale):
    y = z * lax.rsqrt(jnp.mean(z * z, axis=-1, keepdims=True) + EPS)
    return (y * g) * (1.0 + scale) + shift


def _acc(ref, val, first):
    @pl.when(first)
    def _():
        ref[...] = val

    @pl.when(jnp.logical_not(first))
    def _():
        ref[...] += val


def _tok(width):
    return pl.BlockSpec((None, TM, width), lambda b, t: (b, t, 0))


def _modspec():
    return pl.BlockSpec((None, None, 3, D), lambda b, t: (b, jnp.minimum(t, 1), 0, 0))


def _const(shape):
    nd = len(shape)
    return pl.BlockSpec(shape, lambda b, t: (0,) * nd)


def _wspec(w):
    stack, idx = w
    return pl.BlockSpec((None,) + stack.shape[1:], lambda b, t: (idx, 0, 0), pipeline_mode=pl.Buffered(1))


def _ffn_fwd(z, mod, g, w1t, w3t, w2):
    B, S, _ = z.shape

    def body(z_ref, mod_ref, g_ref, w1_ref, w3_ref, w2_ref, zo_ref, f_ref):
        zt = z_ref[...]
        h = _rms_mod(zt, g_ref[...], mod_ref[0:1, :], mod_ref[1:2, :]).astype(BF16)
        f = jnp.zeros((TM, D), F32)
        for c in range(F // FH):
            rows = slice(c * FH, (c + 1) * FH)
            u1 = _nt(h, w1_ref[rows, :])
            u3 = _nt(h, w3_ref[rows, :])
            a = (u1 * jax.nn.sigmoid(u1) * u3).astype(BF16)
            f = f + _nn(a, w2_ref[rows, :])
        f_ref[...] = f
        zo_ref[...] = zt + 0.5 * mod_ref[2:3, :] * f

    return pl.pallas_call(
        body, name="ffn_fwd", grid=(B, S // TM),
        in_specs=[_tok(D), _modspec(), _const((1, D)), _wspec(w1t), _wspec(w3t), _wspec(w2)],
        out_specs=[_tok(D), _tok(D)],
        out_shape=[jax.ShapeDtypeStruct((B, S, D), F32)] * 2,
        compiler_params=_params(VMEM_BIG, ("arbitrary", "arbitrary")),
    )(z, mod, g, w1t[0], w3t[0], w2[0])


def _ffn_bwd(z, dzo, f, mod, g, w1t, w3t, w2, each=None):
    B, S, _ = z.shape

    def body(z_ref, dzo_ref, f_ref, mod_ref, g_ref, w1_ref, w3_ref, w2_ref,
             dz_ref, h_ref, du1_ref, du3_ref, a_ref, do_ref, dmod_ref, dg_ref):
        b, t = pl.program_id(0), pl.program_id(1)
        zt = z_ref[...]
        dzo = dzo_ref[...]
        gate = mod_ref[2:3, :]
        h32, vjp_h = jax.vjp(_rms_mod, zt, g_ref[...], mod_ref[0:1, :], mod_ref[1:2, :])
        h = h32.astype(BF16)
        h_ref[...] = h
        do = (0.5 * gate * dzo).astype(BF16)
        do_ref[...] = do
        dgate = jnp.sum(0.5 * f_ref[...] * dzo, axis=0, keepdims=True)
        dh = jnp.zeros((TM, D), F32)
        for c in range(F // FH):
            rows = slice(c * FH, (c + 1) * FH)
            u1 = _nt(h, w1_ref[rows, :])
            u3 = _nt(h, w3_ref[rows, :])
            sg = jax.nn.sigmoid(u1)
            s = u1 * sg
            a_ref[:, rows] = (s * u3).astype(BF16)
            da = _nt(do, w2_ref[rows, :])
            du3 = (da * s).astype(BF16)
            du1 = (da * u3 * (sg * (1.0 + u1 * (1.0 - sg)))).astype(BF16)
            du1_ref[:, rows] = du1
            du3_ref[:, rows] = du3
            dh = dh + _nn(du1, w1_ref[rows, :]) + _nn(du3, w3_ref[rows, :])
        dz_h, dg, dshift, dscale = vjp_h(dh)
        dz_ref[...] = dzo + dz_h
        _acc(dmod_ref, jnp.concatenate([dshift, dscale, dgate], axis=0), t <= 1)
        _acc(dg_ref, dg, jnp.logical_and(b == 0, t == 0))

    T = B * S
    outs = pl.pallas_call(
        body, name="ffn_bwd", grid=(B, S // TM),
        in_specs=[_tok(D), _tok(D), _tok(D), _modspec(), _const((1, D)), _wspec(w1t), _wspec(w3t), _wspec(w2)],
        out_specs=[_tok(D), _tok(D), _tok(F), _tok(F), _tok(F), _tok(D), _modspec(), _const((1, D))],
        out_shape=[jax.ShapeDtypeStruct((B, S, D), F32), jax.ShapeDtypeStruct((B, S, D), BF16),
                   jax.ShapeDtypeStruct((B, S, F), BF16), jax.ShapeDtypeStruct((B, S, F), BF16),
                   jax.ShapeDtypeStruct((B, S, F), BF16), jax.ShapeDtypeStruct((B, S, D), BF16),
                   jax.ShapeDtypeStruct((B, 2, 3, D), F32), jax.ShapeDtypeStruct((1, D), F32)],
        compiler_params=_params(VMEM_BIG, ("arbitrary", "arbitrary")),
    )(z, dzo, f, mod, g, w1t[0], w3t[0], w2[0])
    dz, h, du1, du3, a, do, dmod, dg = outs
    grads = []
    for lhs, rhs in ((du1, h), (du3, h), (a, do)):
        grads.append(_tn_matmul(lhs.reshape(T, F), rhs.reshape(T, D)))
        if each is not None:
            each(grads[-1])
    return (dz, dmod, dg, *grads)


def _tn_matmul(a, b):
    T, M = a.shape
    N = b.shape[1]
    MB = FH if M > FH else M
    TT = next(t for t in (1152, 1024, 768, 512, TM) if T % t == 0)
    nt = T // TT

    def body(a_ref, b_ref, o_ref, acc_ref):
        t = pl.program_id(1)
        prod = _tn(a_ref[...], b_ref[...])
        _acc(acc_ref, prod, t == 0)

        @pl.when(t == nt - 1)
        def _():
            o_ref[...] = acc_ref[...].astype(BF16)

    return pl.pallas_call(
        body, name="tn_matmul", grid=(M // MB, nt),
        in_specs=[pl.BlockSpec((TT, MB), lambda i, t: (t, i)), pl.BlockSpec((TT, N), lambda i, t: (t, 0))],
        out_specs=pl.BlockSpec((MB, N), lambda i, t: (i, 0)),
        out_shape=jax.ShapeDtypeStruct((M, N), BF16),
        scratch_shapes=[pltpu.VMEM((MB, N), F32)],
        compiler_params=_params(VMEM_BIG, ("arbitrary", "arbitrary")),
    )(a, b)


def _swap32(x):
    n = x.shape[1]
    lane = lax.broadcasted_iota(jnp.int32, x.shape, 1)
    return jnp.where((lane % 64) < 32, pltpu.roll(x, n - 32, 1), pltpu.roll(x, 32, 1))


def _rope(x, cos, sin):
    return x * cos + _swap32(x) * sin


def _rope_t(dy, cos, sin):
    return dy * cos + _swap32(dy * sin)


def _rope_tables(S):
    L = S - LC
    n_freq = HD // 4
    inv = 10000.0 ** (-jnp.arange(n_freq, dtype=F32) / n_freq)
    i = jnp.arange(L)
    row = (i // GRID_W).astype(F32)
    col = (i % GRID_W).astype(F32)
    ang_r = row[:, None] * inv[None]
    ang_c = col[:, None] * inv[None]
    ang = jnp.concatenate([ang_r, ang_r, ang_c, ang_c], axis=1)
    ang = jnp.concatenate([jnp.zeros((LC, HD), F32), ang], axis=0)
    sign = jnp.where((jnp.arange(HD) % 64) < 32, -1.0, 1.0).astype(F32)
    return jnp.cos(ang), jnp.sin(ang) * sign[None]


def _tabspec():
    return pl.BlockSpec((TM, HD), lambda b, t: (t, 0))


def _mix_in_fwd(z, mod, g, wint, cos, sin):
    B, S, _ = z.shape

    def body(z_ref, mod_ref, g_ref, w_ref, cos_ref, sin_ref, pp_ref, q_ref, k_ref, v_ref, gg_ref, pc_ref):
        h = _rms_mod(z_ref[...], g_ref[...], mod_ref[0:1, :], mod_ref[1:2, :]).astype(BF16)
        p = _nt(h, w_ref[...])
        cos = jnp.tile(cos_ref[...], (1, NH))
        sin = jnp.tile(sin_ref[...], (1, NH))
        pp_ref[...] = p[:, 0:PW]
        q_ref[...] = _rope(p[:, PW:PW + RW], cos, sin)
        k_ref[...] = _rope(p[:, PW + RW:PW + 2 * RW], cos, sin) * K_SCALE
        v_ref[...] = p[:, PW + 2 * RW:PW + 3 * RW]
        gg_ref[...] = p[:, PW + 3 * RW:PW + 4 * RW]
        pc_ref[...] = p[:, PW + 4 * RW:]

    return pl.pallas_call(
        body, name="mix_in_fwd", grid=(B, S // TM),
        in_specs=[_tok(D), _modspec(), _const((1, D)), _wspec(wint), _tabspec(), _tabspec()],
        out_specs=[_tok(PW), _tok(RW), _tok(RW), _tok(RW), _tok(RW), _tok(2 * PW)],
        out_shape=[jax.ShapeDtypeStruct((B, S, PW), F32)] + [jax.ShapeDtypeStruct((B, S, RW), F32)] * 5,
        compiler_params=_params(VMEM_BIG, ("arbitrary", "arbitrary")),
    )(z, mod, g, wint[0], cos, sin)


def _mix_in_bwd(z, dzo, dpp, dqa, dqb, dka, dkb, dva, dvb, dgg, dpc, mod, g, wint, cos, sin):
    B, S, _ = z.shape

    def body(z_ref, dzo_ref, dpp_ref, dqa_ref, dqb_ref, dka_ref, dkb_ref, dva_ref, dvb_ref, dgg_ref, dpc_ref, mod_ref, g_ref,
             w_ref, cos_ref, sin_ref, dz_ref, h_ref, dp_ref, dmod_ref, dg_ref):
        b, t = pl.program_id(0), pl.program_id(1)
        h32, vjp_h = jax.vjp(_rms_mod, z_ref[...], g_ref[...], mod_ref[0:1, :], mod_ref[1:2, :])
        h_ref[...] = h32.astype(BF16)
        cos = jnp.tile(cos_ref[...], (1, NH))
        sin = jnp.tile(sin_ref[...], (1, NH))
        dq = _rope_t(dqa_ref[...] + dqb_ref[...], cos, sin)
        dk = _rope_t(dka_ref[...] + dkb_ref[...], cos, sin) * K_SCALE
        dp = jnp.concatenate([dpp_ref[...], dq, dk, dva_ref[...] + dvb_ref[...], dgg_ref[...], dpc_ref[...]],
                             axis=1).astype(BF16)
        dp_ref[...] = dp
        dh = _nn(dp, w_ref[...])
        dz_h, dg, dshift, dscale = vjp_h(dh)
        dz_ref[...] = dzo_ref[...] + dz_h
        _acc(dmod_ref, jnp.concatenate([dshift, dscale, jnp.zeros_like(dshift)], axis=0), t <= 1)
        _acc(dg_ref, dg, jnp.logical_and(b == 0, t == 0))

    return pl.pallas_call(
        body, name="mix_in_bwd", grid=(B, S // TM),
        in_specs=[_tok(D), _tok(D), _tok(PW)] + [_tok(RW)] * 7 + [_tok(2 * PW), _modspec(), _const((1, D)), _wspec(wint),
                                                                  _tabspec(), _tabspec()],
        out_specs=[_tok(D), _tok(D), _tok(F), _modspec(), _const((1, D))],
        out_shape=[jax.ShapeDtypeStruct((B, S, D), F32), jax.ShapeDtypeStruct((B, S, D), BF16),
                   jax.ShapeDtypeStruct((B, S, F), BF16), jax.ShapeDtypeStruct((B, 2, 3, D), F32),
                   jax.ShapeDtypeStruct((1, D), F32)],
        compiler_params=_params(VMEM_BIG, ("arbitrary", "arbitrary")),
    )(z, dzo, dpp, dqa, dqb, dka, dkb, dva, dvb, dgg, dpc, mod, g, wint[0], cos, sin)


def _log_sigmoid(x):
    return jnp.minimum(x, 0.0) - jnp.log(1.0 + jnp.exp(-jnp.abs(x)))


def _retention(a, b, c, dec_a, dec_b, sched_a, sched_b):
    B, S, _ = a.shape
    C = TM

    def body(a_ref, b_ref, c_ref, da_ref, db_ref, oa_ref, ob_ref):
        ii = lax.broadcasted_iota(jnp.int32, (C, C), 0)
        jj = lax.broadcasted_iota(jnp.int32, (C, C), 1)
        pos = lax.broadcasted_iota(jnp.int32, (C, 1), 0).astype(F32)
        for dec_ref, o_ref, (order, causal, strict) in ((da_ref, oa_ref, sched_a), (db_ref, ob_ref, sched_b)):
            lg = _log_sigmoid(dec_ref[...])
            lg1 = lg[:, 0:1]
            dist = ((ii - jj) if causal else (jj - ii)).astype(F32)
            mask = (dist > 0.0) if strict else (dist >= 0.0)
            decay = jnp.where(mask, jnp.exp(jnp.maximum(dist, 0.0) * lg1), 0.0)
            p = pos if causal else (C - 1.0 - pos)
            w_q = jnp.exp((p + 1.0) * lg1)
            w_k = jnp.exp((C - 1.0 - p) * lg1)
            chunk_decay = jnp.exp(C * lg)
            state = jnp.zeros((HD, HD), F32)
            for n in order:
                rows = pl.ds(n * C, C)
                at, bt, ct = a_ref[rows, :], b_ref[rows, :], c_ref[rows, :]
                cb = ct.astype(BF16)
                scores = _nt(at.astype(BF16), bt.astype(BF16)) * decay
                o = _nn(scores.astype(BF16), cb)
                o = o + _nn((at * w_q).astype(BF16), state.astype(BF16))
                o_ref[rows, :] = o
                state = chunk_decay * state + _tn((bt * w_k).astype(BF16), cb)

    seq = pl.BlockSpec((None, S, HD), lambda b, h: (b, 0, h))
    dspec = pl.BlockSpec((None, 1, HD), lambda b, h: (h, 0, 0))
    return pl.pallas_call(
        body, name="retention", grid=(B, NH),
        in_specs=[seq, seq, seq, dspec, dspec], out_specs=[seq, seq],
        out_shape=[jax.ShapeDtypeStruct((B, S, RW), F32)] * 2,
        compiler_params=_params(VMEM_BIG, ("arbitrary", "arbitrary")),
    )(a, b, c, dec_a, dec_b)


def _retention_ddecay(q, k, v, do, dec_a, dec_b, sched_a, sched_b):
    B, S, _ = q.shape
    C = TM

    def body(q_ref, k_ref, v_ref, do_ref, da_ref, db_ref, o_ref):
        ii = lax.broadcasted_iota(jnp.int32, (C, C), 0)
        jj = lax.broadcasted_iota(jnp.int32, (C, C), 1)
        pos = lax.broadcasted_iota(jnp.int32, (C, 1), 0).astype(F32)
        vals = []
        for dec_ref, (order, causal, strict) in ((da_ref, sched_a), (db_ref, sched_b)):
            x = dec_ref[...]
            lg = _log_sigmoid(x)
            lg1 = lg[:, 0:1]
            dist = ((ii - jj) if causal else (jj - ii)).astype(F32)
            mask = (dist > 0.0) if strict else (dist >= 0.0)
            ddecay = jnp.where(mask, dist * jnp.exp(jnp.maximum(dist, 0.0) * lg1), 0.0)
            p = pos if causal else (C - 1.0 - pos)
            w_q = jnp.exp((p + 1.0) * lg1)
            w_k = jnp.exp((C - 1.0 - p) * lg1)
            chunk_decay = jnp.exp(C * lg)
            state = jnp.zeros((HD, HD), F32)
            dstate = jnp.zeros((HD, HD), F32)
            tot = jnp.zeros((), F32)
            for n in order:
                rows = pl.ds(n * C, C)
                qt, kt, vt, dot = q_ref[rows, :], k_ref[rows, :], v_ref[rows, :], do_ref[rows, :]
                vb = vt.astype(BF16)
                scores = _nt(qt.astype(BF16), kt.astype(BF16))
                dscores = _nt(dot.astype(BF16), vb)
                qw = (qt * w_q).astype(BF16)
                cross = _nn(qw, state.astype(BF16))
                dcross = _nn(qw, dstate.astype(BF16))
                tot = tot + jnp.sum(scores * dscores * ddecay) + jnp.sum(((p + 1.0) * cross + dcross) * dot)
                kv = _tn((kt * w_k).astype(BF16), vb)
                dkv = _tn((kt * ((C - 1.0 - p) * w_k)).astype(BF16), vb)
                dstate = chunk_decay * (dstate + C * state) + dkv
                state = chunk_decay * state + kv
            vals.append(tot * jax.nn.sigmoid(-x))
        row = lax.broadcasted_iota(jnp.int32, (8, HD), 0)
        tile = jnp.where(row == 0, vals[0], 0.0) + jnp.where(row == 1, vals[1], 0.0)
        _acc(o_ref, tile, pl.program_id(1) == 0)

    seq = pl.BlockSpec((None, S, HD), lambda h, b: (b, 0, h))
    dspec = pl.BlockSpec((None, 1, HD), lambda h, b: (h, 0, 0))
    return pl.pallas_call(
        body, name="retention_ddecay", grid=(NH, B),
        in_specs=[seq, seq, seq, seq, dspec, dspec], out_specs=pl.BlockSpec((None, 8, HD), lambda h, b: (h, 0, 0)),
        out_shape=jax.ShapeDtypeStruct((NH, 8, HD), F32),
        compiler_params=_params(VMEM_BIG, ("arbitrary", "arbitrary")),
    )(q, k, v, do, dec_a, dec_b)


def _schedules(S):
    n = S // TM
    lat_up = tuple(range(1, n))
    lat_down = tuple(range(n - 1, 0, -1))
    fwd = (((0,) + lat_up, True, False), ((0,) + lat_down, False, True))
    bwd = ((lat_down + (0,), False, False), (lat_up + (0,), True, True))
    return fwd, bwd


def _shift_rows(x, d):
    if d == 0:
        return x
    S = x.shape[0]
    t = lax.broadcasted_iota(jnp.int32, x.shape, 0)
    tt = t + d
    lo = jnp.where(t < LC, 0, LC)
    hi = jnp.where(t < LC, LC, S)
    return jnp.where((tt >= lo) & (tt < hi), pltpu.roll(x, (-d) % S, 0), 0.0)


@functools.partial(jax.custom_vjp, nondiff_argnums=(1,))
def _shift(x, d):
    return _shift_rows(x, d)


_shift.defvjp(lambda x, d: (_shift_rows(x, d), None), lambda d, _, g: (_shift_rows(g, -d),))


def _pool_fn(p, bd, pscale):
    lane = lax.broadcasted_iota(jnp.int32, p.shape, 1)
    grp = lane // (PW // 4)
    half = jnp.where(grp == 0, 1, jnp.where(grp == 1, 2, jnp.where(grp == 2, 4, 8)))
    ones = jnp.ones(p.shape, F32)
    acc = jnp.zeros(p.shape, F32)
    cnt = jnp.zeros(p.shape, F32)
    for d in range(-8, 8):
        inwin = ((d >= -half) & (d < half)).astype(F32)
        acc = acc + _shift(p, d) * inwin
        cnt = cnt + _shift_rows(ones, d) * inwin
    pooled = acc / cnt - p
    mixed = _nn(pooled.astype(BF16), bd.astype(BF16))
    return mixed * pscale


def _dwconv_raw(zc, dw):
    y = jnp.zeros(zc.shape, F32)
    for k in range(CONV_K):
        y = y + _shift_rows(zc, k - CONV_K // 2) * dw[k:k + 1, :]
    return y


@jax.custom_vjp
def _dwconv(zc, dw):
    return _dwconv_raw(zc, dw)


def _dwconv_fwd(zc, dw):
    return _dwconv_raw(zc, dw), (zc, dw)


def _dwconv_bwd(res, g):
    zc, dw = res
    dz = jnp.zeros(zc.shape, F32)
    ddw = jnp.zeros(dw.shape, F32)
    row = lax.broadcasted_iota(jnp.int32, dw.shape, 0)
    for k in range(CONV_K):
        dz = dz + _shift_rows(g, CONV_K // 2 - k) * dw[k:k + 1, :]
        r = jnp.sum(g * _shift_rows(zc, k - CONV_K // 2), axis=0, keepdims=True)
        ddw = ddw + jnp.where(row == k, r, 0.0)
    return dz, ddw


_dwconv.defvjp(_dwconv_fwd, _dwconv_bwd)


def _conv_fn(u, dw, db):
    zc = u[:, :PW] * jax.nn.sigmoid(u[:, PW:])
    return _dwconv(zc, dw) + db


def _ln_swish(y, lng, lnb):
    mu = jnp.mean(y, axis=-1, keepdims=True)
    yc = y - mu
    var = jnp.mean(yc * yc, axis=-1, keepdims=True)
    yn = yc * lax.rsqrt(var + EPS) * lng + lnb
    return yn * jax.nn.sigmoid(yn)


def _seq(shape, single=False):
    return pl.BlockSpec((None,) + shape, lambda b: (b, 0, 0), pipeline_mode=pl.Buffered(1) if single else None)


def _c1(shape):
    nd = len(shape)
    return pl.BlockSpec(shape, lambda b: (0,) * nd)


def _seq_apply(fn, name, xs, consts, width):
    B, S, w = xs.shape

    def body(x_ref, *refs):
        refs[-1][...] = fn(x_ref[...], *[r[...] for r in refs[:-1]])

    return pl.pallas_call(
        body, name=name, grid=(B,),
        in_specs=[_seq((S, w))] + [_c1(c.shape) for c in consts], out_specs=_seq((S, width)),
        out_shape=jax.ShapeDtypeStruct((B, S, width), F32),
        compiler_params=_params(VMEM_BIG, ("arbitrary",)),
    )(xs, *consts)


def _seq_vjp(fn, name, xs, consts, dout):
    B, S, w = xs.shape
    n = len(consts)

    def body(x_ref, d_ref, *refs):
        first = pl.program_id(0) == 0
        _, vjp = jax.vjp(fn, x_ref[...], *[r[...] for r in refs[:n]])
        grads = vjp(d_ref[...])
        refs[n][...] = grads[0]
        for ref, val in zip(refs[n + 1:], grads[1:]):
            _acc(ref, val, first)

    return pl.pallas_call(
        body, name=name, grid=(B,),
        in_specs=[_seq((S, w), True), _seq((S, dout.shape[2]), True)] + [_c1(c.shape) for c in consts],
        out_specs=[_seq((S, w))] + [_c1(c.shape) for c in consts],
        out_shape=[jax.ShapeDtypeStruct((B, S, w), F32)] + [jax.ShapeDtypeStruct(c.shape, F32) for c in consts],
        compiler_params=_params(VMEM_BIG, ("arbitrary",)),
    )(xs, dout, *consts)


def _pool_conv_fwd(pp, pc, bd, pscale, dw, db):
    return (_seq_apply(_pool_fn, "pool_fwd", pp, (bd, pscale), PW),
            _seq_apply(_conv_fn, "conv_fwd", pc, (dw, db), PW))


def _pool_conv_bwd(pp, pc, dpo, dco, bd, pscale, dw, db):
    dpp, dbd, dps = _seq_vjp(_pool_fn, "pool_bwd", pp, (bd, pscale), dpo)
    dpc, ddw, ddb = _seq_vjp(_conv_fn, "conv_bwd", pc, (dw, db), dco)
    return dpp, dpc, dbd, dps, ddw, ddb


def _cat_fn(po, oa, ob, gg, co, gng, lng, lnb):
    o = oa + ob
    outs = []
    for h in range(NH):
        oh = o[:, h * HD:(h + 1) * HD]
        mu = jnp.mean(oh, axis=-1, keepdims=True)
        oc = oh - mu
        var = jnp.mean(oc * oc, axis=-1, keepdims=True)
        outs.append(oc * lax.rsqrt(var + EPS))
    ret = jnp.concatenate(outs, axis=1) * gng * (gg * jax.nn.sigmoid(gg))
    return jnp.concatenate([po, ret, _ln_swish(co, lng, lnb)], axis=1)


def _mix_out_fwd(z, po, oa, ob, gg, co, ro, mod, wout):
    B, S, _ = z.shape

    def body(z_ref, po_ref, oa_ref, ob_ref, gg_ref, co_ref, gn_ref, lg_ref, lb_ref, mod_ref, w_ref, zo_ref, out_ref):
        cat = _cat_fn(po_ref[...], oa_ref[...], ob_ref[...], gg_ref[...], co_ref[...], gn_ref[...], lg_ref[...], lb_ref[...])
        out = _nn(cat.astype(BF16), w_ref[...])
        out_ref[...] = out
        zo_ref[...] = z_ref[...] + mod_ref[2:3, :] * out

    return pl.pallas_call(
        body, name="mix_out_fwd", grid=(B, S // TM),
        in_specs=[_tok(D), _tok(PW), _tok(RW), _tok(RW), _tok(RW), _tok(PW), _const((1, RW)), _const((1, PW)),
                  _const((1, PW)), _modspec(), _wspec(wout)],
        out_specs=[_tok(D), _tok(D)],
        out_shape=[jax.ShapeDtypeStruct((B, S, D), F32)] * 2,
        compiler_params=_params(None, ("arbitrary", "arbitrary")),
    )(z, po, oa, ob, gg, co, *ro, mod, wout[0])


def _mix_out_bwd(dzo, out, po, oa, ob, gg, co, ro, mod, wout):
    B, S, _ = dzo.shape

    def body(dzo_ref, out_ref, po_ref, oa_ref, ob_ref, gg_ref, co_ref, gn_ref, lg_ref, lb_ref, mod_ref, w_ref,
             dpo_ref, do_ref, dgg_ref, dco_ref, cat_ref, dout_ref, dmod_ref, dgn_ref, dlg_ref, dlb_ref):
        b, t = pl.program_id(0), pl.program_id(1)
        dzo = dzo_ref[...]
        cat, vjp = jax.vjp(_cat_fn, po_ref[...], oa_ref[...], ob_ref[...], gg_ref[...], co_ref[...], gn_ref[...],
                           lg_ref[...], lb_ref[...])
        cat_ref[...] = cat.astype(BF16)
        dout = (mod_ref[2:3, :] * dzo).astype(BF16)
        dout_ref[...] = dout
        dgate = jnp.sum(out_ref[...] * dzo, axis=0, keepdims=True)
        dcat = _nt(dout, w_ref[...])
        dpo, doa, _, dgg, dco, dgn, dlg, dlb = vjp(dcat)
        dpo_ref[...] = dpo
        do_ref[...] = doa
        dgg_ref[...] = dgg
        dco_ref[...] = dco
        zero = jnp.zeros_like(dgate)
        _acc(dmod_ref, jnp.concatenate([zero, zero, dgate], axis=0), t <= 1)
        first = jnp.logical_and(b == 0, t == 0)
        _acc(dgn_ref, dgn, first)
        _acc(dlg_ref, dlg, first)
        _acc(dlb_ref, dlb, first)

    return pl.pallas_call(
        body, name="mix_out_bwd", grid=(B, S // TM),
        in_specs=[_tok(D), _tok(D), _tok(PW), _tok(RW), _tok(RW), _tok(RW), _tok(PW), _const((1, RW)), _const((1, PW)),
                  _const((1, PW)), _modspec(), _wspec(wout)],
        out_specs=[_tok(PW), _tok(RW), _tok(RW), _tok(PW), _tok(D), _tok(D), _modspec(), _const((1, RW)),
                   _const((1, PW)), _const((1, PW))],
        out_shape=[jax.ShapeDtypeStruct((B, S, PW), F32), jax.ShapeDtypeStruct((B, S, RW), F32),
                   jax.ShapeDtypeStruct((B, S, RW), F32), jax.ShapeDtypeStruct((B, S, PW), F32),
                   jax.ShapeDtypeStruct((B, S, D), BF16), jax.ShapeDtypeStruct((B, S, D), BF16),
                   jax.ShapeDtypeStruct((B, 2, 3, D), F32), jax.ShapeDtypeStruct((1, RW), F32),
                   jax.ShapeDtypeStruct((1, PW), F32), jax.ShapeDtypeStruct((1, PW), F32)],
        compiler_params=_params(None, ("arbitrary", "arbitrary")),
    )(dzo, out, po, oa, ob, gg, co, *ro, mod, wout[0])


def _rms(z, g):
    return z * lax.rsqrt(jnp.mean(z * z, axis=-1, keepdims=True) + EPS) * g


def _head(z, target, fg):
    B, S, _ = z.shape

    def body(z_ref, t_ref, g_ref, dz_ref, dg_ref, loss_ref):
        b, t = pl.program_id(0), pl.program_id(1)
        first = jnp.logical_and(b == 0, t == 0)

        @pl.when(t == 0)
        def _():
            dz_ref[...] = jnp.zeros((TM, D), F32)

        @pl.when(first)
        def _():
            dg_ref[...] = jnp.zeros((1, D), F32)
            loss_ref[...] = jnp.zeros((8, 128), F32)

        @pl.when(t > 0)
        def _():
            y, vjp = jax.vjp(_rms, z_ref[...], g_ref[...])
            err = y - t_ref[...]
            dz, dg = vjp(err * (1.0 / D))
            dz_ref[...] = dz
            dg_ref[...] += dg
            loss_ref[...] += 0.5 * jnp.sum(err * err) * (1.0 / D)

    return pl.pallas_call(
        body, name="head", grid=(B, S // TM),
        in_specs=[_tok(D), pl.BlockSpec((None, TM, D), lambda b, t: (b, jnp.maximum(t - 1, 0), 0)), _const((1, D))],
        out_specs=[_tok(D), _const((1, D)), _const((8, 128))],
        out_shape=[jax.ShapeDtypeStruct((B, S, D), F32), jax.ShapeDtypeStruct((1, D), F32),
                   jax.ShapeDtypeStruct((8, 128), F32)],
        compiler_params=_params(None, ("arbitrary", "arbitrary")),
    )(z, target, fg)


MROWS = 24
MCOL = 768


def _silu(x):
    return x * jax.nn.sigmoid(x)


def _mod_fwd(c24, wmod, bmod):
    ncol = wmod.shape[2]

    def body(c_ref, w_ref, b_ref, o_ref):
        sc = _silu(c_ref[...]).astype(BF16)
        o_ref[...] = _nn(sc, w_ref[...].astype(BF16)) + b_ref[...]

    return pl.pallas_call(
        body, name="mod_fwd", grid=(2, ncol // MCOL),
        in_specs=[pl.BlockSpec((MROWS, D), lambda l, j: (0, 0)), pl.BlockSpec((None, D, MCOL), lambda l, j: (l, 0, j)),
                  pl.BlockSpec((None, 1, MCOL), lambda l, j: (l, 0, j))],
        out_specs=pl.BlockSpec((None, MROWS, MCOL), lambda l, j: (l, 0, j)),
        out_shape=jax.ShapeDtypeStruct((2, MROWS, ncol), F32),
        compiler_params=_params(None, ("arbitrary", "arbitrary")),
    )(c24, wmod, bmod)


def _mod_bwd(c24, dmod, wmod):
    ncol = wmod.shape[2]

    def body(c_ref, d_ref, w_ref, dw_ref, dsc_ref):
        l, j = pl.program_id(0), pl.program_id(1)
        sc = _silu(c_ref[...]).astype(BF16)
        dm = d_ref[...].astype(BF16)
        dw_ref[...] = _tn(sc, dm)
        _acc(dsc_ref, _nt(dm, w_ref[...].astype(BF16)), jnp.logical_and(l == 0, j == 0))

    return pl.pallas_call(
        body, name="mod_bwd", grid=(2, ncol // MCOL),
        in_specs=[pl.BlockSpec((MROWS, D), lambda l, j: (0, 0)), pl.BlockSpec((None, MROWS, MCOL), lambda l, j: (l, 0, j)),
                  pl.BlockSpec((None, D, MCOL), lambda l, j: (l, 0, j))],
        out_specs=[pl.BlockSpec((None, D, MCOL), lambda l, j: (l, 0, j)), pl.BlockSpec((MROWS, D), lambda l, j: (0, 0))],
        out_shape=[jax.ShapeDtypeStruct((2, D, ncol), F32), jax.ShapeDtypeStruct((MROWS, D), F32)],
        compiler_params=_params(None, ("arbitrary", "arbitrary")),
    )(c24, dmod, wmod)


def _bmod_cctx_grad(dmod_full, dsc_parts, cctx):
    def body(d_ref, p_ref, c_ref, db_ref, dc_ref):
        db_ref[...] = jnp.sum(d_ref[...], axis=1, keepdims=True)
        tot = jnp.zeros((8, D), F32)
        for s in range(N_CHIP):
            tot = tot + p_ref[s]
        x = c_ref[...]
        sg = jax.nn.sigmoid(x)
        dc_ref[...] = jnp.sum(tot, axis=0, keepdims=True) * (sg * (1.0 + x * (1.0 - sg)))

    return pl.pallas_call(
        body, name="bmod_cctx_grad",
        out_shape=[jax.ShapeDtypeStruct((2, 1, N_MOD * D), F32), jax.ShapeDtypeStruct((1, D), F32)],
    )(dmod_full, dsc_parts, cctx)


def _adam_math(w, g, m, v):
    m = ADAM_B1 * m + (1.0 - ADAM_B1) * g
    v = ADAM_B2 * v + (1.0 - ADAM_B2) * (g * g)
    m_hat = m / (1.0 - ADAM_B1 ** ADAM_STEP)
    v_hat = v / (1.0 - ADAM_B2 ** ADAM_STEP)
    delta = -ADAM_LR * (m_hat / (jnp.sqrt(v_hat) + ADAM_EPS) + ADAM_WD * w)
    return delta, m, v


def _adam(w, g, m, v):
    R, Cc = w.shape
    if R * Cc * 4 <= (1 << 20):
        RB = R
    else:
        RB = 1 << (((1 << 18) // Cc).bit_length() - 1)
        assert R % RB == 0

    def body(w_ref, g_ref, m_ref, v_ref, d_ref, mo_ref, vo_ref):
        d, mn, vn = _adam_math(w_ref[...], g_ref[...], m_ref[...], v_ref[...])
        d_ref[...] = d
        mo_ref[...] = mn
        vo_ref[...] = vn

    spec = pl.BlockSpec((RB, Cc), lambda i: (i, 0))
    return pl.pallas_call(
        body, name="adam", grid=(R // RB,), in_specs=[spec] * 4, out_specs=[spec] * 3,
        out_shape=[jax.ShapeDtypeStruct((R, Cc), F32)] * 3,
        compiler_params=_params(None, ("arbitrary",)),
    )(w, g, m, v)


def _adam_block(w, g, m, v, block, n, prev):
    shape = w.shape
    cols = shape[-1]
    w3, m3, v3 = (t.reshape(n, -1, cols) for t in (w, m, v))
    g2 = g.reshape(-1, cols)
    R = g2.shape[0]
    RB = max(r for r in range(8, (1 << 18) // cols + 1, 8) if R % r == 0)

    def body(w_ref, g_ref, m_ref, v_ref, *refs):
        go_ref, d_ref, mo_ref, vo_ref = refs[-4:]
        gt = g_ref[...]
        d, mn, vn = _adam_math(w_ref[...], gt, m_ref[...], v_ref[...])
        go_ref[...] = gt
        d_ref[...] = d
        mo_ref[...] = mn
        vo_ref[...] = vn

    lay = pl.BlockSpec((None, RB, cols), lambda i: (block, i, 0))
    flat = pl.BlockSpec((RB, cols), lambda i: (i, 0))
    hold = [] if prev is None else [t.reshape(n, -1, cols) for t in prev]
    outs = pl.pallas_call(
        body, name="adam_block", grid=(R // RB,),
        in_specs=[lay, flat, lay, lay] + [pl.BlockSpec(memory_space=pl.ANY)] * len(hold), out_specs=[lay] * 4,
        out_shape=[jax.ShapeDtypeStruct(w3.shape, F32)] * 4,
        input_output_aliases={4 + k: k for k in range(len(hold))},
        compiler_params=_params(None, ("arbitrary",)),
    )(w3, g2, m3, v3, *hold)
    return [o.reshape(shape) for o in outs]


def _sum_devices(parts):
    K = parts.shape[1]

    def body(p_ref, o_ref):
        tot = p_ref[0]
        for i in range(1, N_DEV):
            tot = tot + p_ref[i]
        o_ref[...] = tot

    return pl.pallas_call(body, name="sum_devices", out_shape=jax.ShapeDtypeStruct((K, 128), F32))(parts)


def _sum_pieces(own, others):
    R = own.shape[0]
    RB = 96 if R % 96 == 0 else 32

    def body(a_ref, r_ref, o_ref):
        tot = a_ref[...].astype(F32)
        for i in range(N_DEV - 1):
            tot = tot + r_ref[i].astype(F32)
        o_ref[...] = tot

    return pl.pallas_call(
        body, name="sum_pieces", grid=(R // RB,),
        in_specs=[pl.BlockSpec((RB, D), lambda i: (i, 0)), pl.BlockSpec((N_DEV - 1, RB, D), lambda i: (0, i, 0))],
        out_specs=pl.BlockSpec((RB, D), lambda i: (i, 0)),
        out_shape=jax.ShapeDtypeStruct((R, D), F32),
        compiler_params=_params(None, ("arbitrary",)),
    )(own, others)


def _coords():
    return lax.axis_index("x"), lax.axis_index("y"), lax.axis_index("c")


_FLIPS = [(fx, fy, fc) for fx in (0, 1) for fy in (0, 1) for fc in (0, 1)][1:]


def _all_gather_small(buf):
    K = buf.shape[0]

    def body(in_ref, out_ref, send_sems, recv_sems, local_sem):
        x, y, c = _coords()
        me = 4 * x + 2 * y + c
        mine = pltpu.make_async_copy(in_ref, out_ref.at[me], local_sem)
        mine.start()
        sends = []
        for k, (fx, fy, fc) in enumerate(_FLIPS):
            peer = (x ^ fx, y ^ fy, c ^ fc)
            cp = pltpu.make_async_remote_copy(src_ref=in_ref, dst_ref=out_ref.at[me], send_sem=send_sems.at[k],
                                              recv_sem=recv_sems.at[k], device_id=peer, device_id_type=MESH)
            cp.start()
            sends.append(cp)
        for k, (fx, fy, fc) in enumerate(_FLIPS):
            src = 4 * (x ^ fx) + 2 * (y ^ fy) + (c ^ fc)
            pltpu.make_async_remote_copy(src_ref=in_ref, dst_ref=out_ref.at[src], send_sem=send_sems.at[k],
                                         recv_sem=recv_sems.at[k], device_id=(x, y, c), device_id_type=MESH).wait_recv()
        for cp in sends:
            cp.wait_send()
        mine.wait()

    return pl.pallas_call(
        body, name="all_gather_small",
        in_specs=[pl.BlockSpec(memory_space=pltpu.VMEM)], out_specs=pl.BlockSpec(memory_space=pltpu.VMEM),
        out_shape=jax.ShapeDtypeStruct((N_DEV, K, 128), F32),
        scratch_shapes=[pltpu.SemaphoreType.DMA((7,)), pltpu.SemaphoreType.DMA((7,)), pltpu.SemaphoreType.DMA],
        compiler_params=_params(VMEM_BIG),
    )(buf)


_CHIP_FLIPS = [(1, 0), (0, 1), (1, 1)]


def _exchange_rows(buf, B):
    K = buf.shape[1]
    nex = N_DEV * B

    def body(in_ref, out_ref, send_sems, recv_sems, local_sems):
        x, y, c = _coords()
        s_me = 2 * x + y

        def parts(dev, chip_slot):
            return ((in_ref.at[pl.ds(B * dev, B)], out_ref.at[chip_slot, pl.ds(0, B)]),
                    (in_ref.at[pl.ds(nex, 1)], out_ref.at[chip_slot, pl.ds(B, 1)]))

        locals_ = [pltpu.make_async_copy(src, dst, local_sems.at[i]) for i, (src, dst) in enumerate(parts(4 * x + 2 * y + c, s_me))]
        for cp in locals_:
            cp.start()
        sends = []
        for j, (fx, fy) in enumerate(_CHIP_FLIPS):
            px, py = x ^ fx, y ^ fy
            for i, (src, dst) in enumerate(parts(4 * px + 2 * py + c, s_me)):
                cp = pltpu.make_async_remote_copy(src_ref=src, dst_ref=dst, send_sem=send_sems.at[2 * j + i],
                                                  recv_sem=recv_sems.at[2 * j + i], device_id=(px, py, c), device_id_type=MESH)
                cp.start()
                sends.append(cp)
        for j, (fx, fy) in enumerate(_CHIP_FLIPS):
            for i, (src, dst) in enumerate(parts(0, 2 * (x ^ fx) + (y ^ fy))):
                pltpu.make_async_remote_copy(src_ref=src, dst_ref=dst, send_sem=send_sems.at[2 * j + i],
                                             recv_sem=recv_sems.at[2 * j + i], device_id=(x, y, c), device_id_type=MESH).wait_recv()
        for cp in sends:
            cp.wait_send()
        for cp in locals_:
            cp.wait()

    return pl.pallas_call(
        body, name="exchange_rows",
        in_specs=[pl.BlockSpec(memory_space=pltpu.VMEM)], out_specs=pl.BlockSpec(memory_space=pltpu.VMEM),
        out_shape=jax.ShapeDtypeStruct((N_CHIP, B + 1, K, 128), F32),
        scratch_shapes=[pltpu.SemaphoreType.DMA((6,)), pltpu.SemaphoreType.DMA((6,)), pltpu.SemaphoreType.DMA((2,))],
    )(buf)


_HBM = pl.BlockSpec(memory_space=pltpu.HBM)
_SEMS = pl.BlockSpec(memory_space=pltpu.SEMAPHORE)
_EFFECT = pltpu.SideEffectType.DATAFLOW_SIDE_EFFECTING


def _in_hbm(v):
    return pltpu.with_memory_space_constraint(v, pltpu.HBM)


def _copies_start(name, srcs, lands, n_sems, issue, after):
    ns, nl = len(srcs), len(lands)

    def body(*refs):
        src_refs, land_refs = refs[:ns], refs[ns:ns + nl]
        out = refs[ns + nl + 1:]
        issue(src_refs, land_refs, out[:nl], out[nl:2 * nl])
        out[-1][...] = jnp.zeros((8, 128), F32)

    outs = pl.pallas_call(
        body, name=name, in_specs=[_HBM] * (ns + nl) + [pl.BlockSpec(memory_space=pl.ANY)],
        out_specs=[_SEMS] * (2 * nl) + [_HBM] * (ns + nl) + [pl.BlockSpec(memory_space=pltpu.VMEM)],
        out_shape=[pltpu.SemaphoreType.DMA((n_sems,))] * (2 * nl) + [pltpu.HBM(v.shape, v.dtype) for v in (*srcs, *lands)]
        + [jax.ShapeDtypeStruct((8, 128), F32)],
        input_output_aliases={i: 2 * nl + i for i in range(ns + nl)},
        compiler_params=pltpu.CompilerParams(has_side_effects=_EFFECT),
    )(*[_in_hbm(v) for v in (*srcs, *lands)], after)
    return outs[:nl], outs[nl:2 * nl], outs[2 * nl:2 * nl + ns], outs[2 * nl + ns:2 * nl + ns + nl], outs[-1]


def _copies_wait(name, srcs, lands, send_sems, recv_sems, finish, after):
    after = list(after) if isinstance(after, (list, tuple)) else [after]
    ns, nl = len(srcs), len(lands)

    def body(*refs):
        src_refs, land_refs = refs[:ns], refs[ns:ns + nl]
        finish(src_refs, land_refs, refs[ns + nl:ns + 2 * nl], refs[ns + 2 * nl:ns + 3 * nl])

    outs = pl.pallas_call(
        body, name=name, in_specs=[_HBM] * (ns + nl) + [_SEMS] * (2 * nl) + [pl.BlockSpec(memory_space=pl.ANY)] * len(after),
        out_specs=[_HBM] * (ns + nl), out_shape=[pltpu.HBM(v.shape, v.dtype) for v in (*srcs, *lands)],
        input_output_aliases={i: i for i in range(ns + nl)},
        compiler_params=pltpu.CompilerParams(has_side_effects=_EFFECT),
    )(*srcs, *lands, *send_sems, *recv_sems, *after)
    return outs[:ns], outs[ns:]


def _own_slab(land, mine, index):
    return lax.dynamic_update_slice_in_dim(land, mine[:, None], index, axis=1)


def _gather_start(units, after):
    x, y, c = _coords()
    chip = 2 * x + y
    lands = [_own_slab(lax.empty((u.shape[0], N_CHIP) + u.shape[1:], u.dtype), u, chip) for u in units]

    def issue(src_refs, land_refs, send_sems, recv_sems):
        x, y, c = _coords()
        s_me = 2 * x + y
        for i, (src, land) in enumerate(zip(src_refs, land_refs)):
            for j, (fx, fy) in enumerate(_CHIP_FLIPS):
                pltpu.make_async_remote_copy(src_ref=src, dst_ref=land.at[:, s_me], send_sem=send_sems[i].at[j],
                                             recv_sem=recv_sems[i].at[j], device_id=(x ^ fx, y ^ fy, c),
                                             device_id_type=MESH).start()

    return _copies_start("gather_start", units, lands, 3, issue, after)


def _gather_wait(tag, started, which, after):
    send_sems, recv_sems, srcs, lands, _ = started

    def finish(src_refs, land_refs, ssems, rsems):
        x, y, c = _coords()
        for src, land, ss, rs in zip(src_refs, land_refs, ssems, rsems):
            for j in range(3):
                cp = pltpu.make_async_remote_copy(src_ref=src, dst_ref=land.at[:, 0], send_sem=ss.at[j], recv_sem=rs.at[j],
                                                  device_id=(x, y, c), device_id_type=MESH)
                cp.wait_send()
                cp.wait_recv()

    _, done = _copies_wait("gather_wait_" + tag, [srcs[i] for i in which], [lands[i] for i in which],
                           [send_sems[i] for i in which], [recv_sems[i] for i in which], finish, after)
    return [d.reshape(d.shape[0], N_CHIP * d.shape[2], D) for d in done]


def _piece_rows(grads):
    return [g.shape[2] for g in grads]


def _scatter_start(tag, grads, after):
    rows = _piece_rows(grads)
    land = lax.empty((N_DEV - 1, sum(rows), D), grads[0].dtype)

    def issue(src_refs, land_refs, send_sems, recv_sems):
        x, y, c = _coords()
        for k, (fx, fy, fc) in enumerate(_FLIPS):
            px, py, pc = x ^ fx, y ^ fy, c ^ fc
            off = 0
            for src, n in zip(src_refs, rows):
                pltpu.make_async_remote_copy(src_ref=src.at[2 * px + py, pc], dst_ref=land_refs[0].at[k, pl.ds(off, n)],
                                             send_sem=send_sems[0].at[k], recv_sem=recv_sems[0].at[k],
                                             device_id=(px, py, pc), device_id_type=MESH).start()
                off += n

    return _copies_start("scatter_start_" + tag, grads, [land], N_DEV - 1, issue, after)


def _scatter_wait(tag, started, after):
    send_sems, recv_sems, srcs, lands, _ = started

    def finish(src_refs, land_refs, ssems, rsems):
        x, y, c = _coords()
        for k in range(N_DEV - 1):
            cp = pltpu.make_async_remote_copy(src_ref=land_refs[0].at[0], dst_ref=land_refs[0].at[0], send_sem=ssems[0].at[k],
                                              recv_sem=rsems[0].at[k], device_id=(x, y, c), device_id_type=MESH)
            cp.wait_send()
            cp.wait_recv()

    grads, (others,) = _copies_wait("scatter_wait_" + tag, srcs, lands, send_sems, recv_sems, finish, after)
    return grads, others


def _swap_start(tag, mine, rows, after):
    x, y, c = _coords()
    offs = [sum(rows[:t]) for t in range(len(rows))]
    lands = [lax.dynamic_update_slice_in_dim(lax.empty((2, n, D), mine.dtype), mine[o:o + n][None], c, axis=0)
             for o, n in zip(offs, rows)]

    def issue(src_refs, land_refs, send_sems, recv_sems):
        x, y, c = _coords()
        for t, (o, n) in enumerate(zip(offs, rows)):
            pltpu.make_async_remote_copy(src_ref=src_refs[0].at[pl.ds(o, n)], dst_ref=land_refs[t].at[c],
                                         send_sem=send_sems[t].at[0], recv_sem=recv_sems[t].at[0],
                                         device_id=(x, y, 1 - c), device_id_type=MESH).start()

    return _copies_start("swap_start_" + tag, [mine], lands, 1, issue, after)


def _swap_wait(tag, started, after):
    send_sems, recv_sems, srcs, lands, _ = started

    def finish(src_refs, land_refs, ssems, rsems):
        x, y, c = _coords()
        for land, ss, rs in zip(land_refs, ssems, rsems):
            cp = pltpu.make_async_remote_copy(src_ref=land.at[0], dst_ref=land.at[0], send_sem=ss.at[0], recv_sem=rs.at[0],
                                              device_id=(x, y, c), device_id_type=MESH)
            cp.wait_send()
            cp.wait_recv()

    return _copies_wait("swap_wait_" + tag, srcs, lands, send_sems, recv_sems, finish, after)[1]


def _size(shape):
    n = 1
    for d in shape:
        n *= d
    return n


def _pack(arrays):
    return jnp.concatenate([jnp.pad(a.reshape(-1).astype(F32), (0, (-a.size) % 1024)).reshape(-1, 128) for a in arrays], axis=0)


def _unpack(buf, shapes):
    out, row = [], 0
    for s in shapes:
        n = _size(s)
        nrows = 8 * -(-n // 1024)
        out.append(buf[row:row + nrows].reshape(-1)[:n].reshape(s))
        row += nrows
    return out


def _block_diag(pw):
    bd = jnp.zeros((PW, PW), F32)
    g = PW // 4
    for i in range(4):
        bd = bd.at[i * g:(i + 1) * g, i * g:(i + 1) * g].set(pw[i])
    return bd


def _lanes(v):
    return jnp.broadcast_to(v.reshape(NH, 1, 1), (NH, 1, HD))


def _layer_fwd(z, mod, normg, wget, small):
    S = z.shape[1]
    cos, sin = _rope_tables(S)
    fwd_sched, _ = _schedules(S)
    wa = wget("a", z)
    z1, f_a = _ffn_fwd(z, mod[:, :, 0], normg[0], *wa)
    wm = wget("m", z1)
    pp, q, k, v, gg, pc = _mix_in_fwd(z1, mod[:, :, 1], normg[1], wm[0], cos, sin)
    po, co = _pool_conv_fwd(pp, pc, small["bd"], small["pscale"], small["dw"], small["db"])
    ro = (small["gng"], small["lng"], small["lnb"])
    oa, ob = _retention(q, k, v, small["dec_f"], small["dec_b"], *fwd_sched)
    z2, out = _mix_out_fwd(z1, po, oa, ob, gg, co, ro, mod[:, :, 1], wm[1])
    wb = wget("b", z2)
    z3, f_b = _ffn_fwd(z2, mod[:, :, 2], normg[2], *wb)
    saved = dict(z=z, f_a=f_a, z1=z1, pp=pp, q=q, k=k, v=v, gg=gg, pc=pc, po=po, co=co, oa=oa, ob=ob, out=out, z2=z2, f_b=f_b,
                 wa=wa, wm=wm, wb=wb)
    return z3, saved


def _layer_bwd(dz3, sv, mod, normg, small, emit, tok, last):
    S = dz3.shape[1]
    B = dz3.shape[0]
    T = B * S
    cos, sin = _rope_tables(S)
    fwd_sched, bwd_sched = _schedules(S)
    wa, wm, wb = sv["wa"], sv["wm"], sv["wb"]
    dz2, dmod_b, dg_b, gw1t_b, gw3t_b, gw2_b = _ffn_bwd(sv["z2"], dz3, sv["f_b"], mod[:, :, 2] + tok, normg[2], *wb)
    tok = emit("b", [gw1t_b, gw3t_b, gw2_b])
    mod_m = mod[:, :, 1] + tok
    ro = (small["gng"], small["lng"], small["lnb"])
    dpo, do, dgg, dco, cat, dout, dmod_gate, dgng, dlng, dlnb = _mix_out_bwd(
        dz2, sv["out"], sv["po"], sv["oa"], sv["ob"], sv["gg"], sv["co"], ro, mod_m, wm[1])
    gwout = _tn_matmul(cat.reshape(T, D), dout.reshape(T, D))
    dqa, dqb = _retention(do, sv["v"], sv["k"], small["dec_f"], small["dec_b"], *fwd_sched)
    dka, dkb = _retention(sv["v"], do, sv["q"], small["dec_f"], small["dec_b"], *bwd_sched)
    dva, dvb = _retention(sv["k"], sv["q"], do, small["dec_f"], small["dec_b"], *bwd_sched)
    ddec = _retention_ddecay(sv["q"], sv["k"], sv["v"], do, small["dec_f"], small["dec_b"], *fwd_sched)
    dpp, dpc, dbd, dps, ddw, ddb = _pool_conv_bwd(sv["pp"], sv["pc"], dpo, dco, small["bd"], small["pscale"],
                                                   small["dw"], small["db"])
    dz1, h, dp, dmod_m, dg_m = _mix_in_bwd(sv["z1"], dz2, dpp, dqa, dqb, dka, dkb, dva, dvb, dgg, dpc, mod_m, normg[1],
                                           wm[0], cos, sin)
    gwint = _tn_matmul(dp.reshape(T, F), h.reshape(T, D))
    tok = emit("m", [gwint, gwout])
    if last:
        dz, dmod_a, dg_a = _ffn_bwd(sv["z"], dz1, sv["f_a"], mod[:, :, 0] + tok, normg[0], *wa,
                                    each=lambda grad: emit("a", [grad]))[:3]
    else:
        dz, dmod_a, dg_a, gw1t_a, gw3t_a, gw2_a = _ffn_bwd(sv["z"], dz1, sv["f_a"], mod[:, :, 0] + tok, normg[0], *wa)
        tok = emit("a", [gw1t_a, gw3t_a, gw2_a])
    dmod = jnp.stack([dmod_a, dmod_m + dmod_gate, dmod_b], axis=2)
    dnormg = jnp.stack([dg_a, dg_m, dg_b], axis=0)
    g = PW // 4
    dpool_w = jnp.stack([dbd[i * g:(i + 1) * g, i * g:(i + 1) * g] for i in range(4)], axis=0)
    sm = dict(pool_w=dpool_w, pool_scale=dps[0], dec_f=ddec[:, 0, 0], dec_b=ddec[:, 1, 0], gng=dgng[0],
              conv_dw=ddw[0:CONV_K], conv_b=ddb[0], conv_ln_g=dlng[0], conv_ln_b=dlnb[0])
    return dz, dmod, dnormg, sm, tok


def _small_params(pool_w, pool_scale, dec_f, dec_b, gng, conv_dw, conv_b, lng, lnb):
    return dict(bd=_block_diag(pool_w), pscale=pool_scale.reshape(1, PW), dec_f=_lanes(dec_f), dec_b=_lanes(dec_b),
                gng=gng.reshape(1, RW), dw=jnp.pad(conv_dw, ((0, 1), (0, 0))), db=conv_b.reshape(1, PW),
                lng=lng.reshape(1, PW), lnb=lnb.reshape(1, PW))


_WEIGHTS = ["c_ctx", "w_mod", "b_mod", "norm_g", "ffn_w1", "ffn_w3", "ffn_w2", "w_in", "w_out", "pool_w", "pool_scale",
            "ret_decay_fwd", "ret_decay_bwd", "ret_gn_g", "conv_dw", "conv_b", "conv_ln_g", "conv_ln_b", "final_g"]
_BIG = ["w_mod", "ffn_w1", "ffn_w3", "ffn_w2", "w_in", "w_out"]
_SMALL = [n for n in _WEIGHTS if n not in _BIG]


def _adam_any(w, g, m, v):
    shape = w.shape
    cols = shape[-1] if w.ndim >= 2 else 128
    outs = _adam(w.reshape(-1, cols), g.reshape(-1, cols), m.reshape(-1, cols), v.reshape(-1, cols))
    return [o.reshape(shape) for o in outs]


def _step(a):
    x, c, ctx = a["x"], a["c"], a["ctx"]
    B = x.shape[0]
    nex = N_DEV * B
    assert nex + B <= MROWS and ctx.shape[1] == LC and x.shape[1] % TM == 0
    xi, yi, ci = _coords()
    me = 4 * xi + 2 * yi + ci
    chip = 2 * xi + yi
    ncol = a["w_mod"].shape[2]

    def t_bf16(w):
        return jnp.swapaxes(w, -1, -2).astype(BF16)

    w1t, w3t, w2 = t_bf16(a["ffn_w1"]), t_bf16(a["ffn_w3"]), a["ffn_w2"].astype(BF16)
    wint, wout = t_bf16(a["w_in"]), a["w_out"].astype(BF16)
    units = []
    for l in range(2):
        units += [jnp.stack([w1t[l, 0], w3t[l, 0], w2[l, 0]]), wint[l][None], wout[l][None],
                  jnp.stack([w1t[l, 1], w3t[l, 1], w2[l, 1]])]

    shapes1 = [(B, D), (2, 3, D // N_CHIP), (2, CONV_K, PW // N_CHIP)]
    g1 = _all_gather_small(_pack([c, a["norm_g"], a["conv_dw"]]))
    per = [_unpack(g1[d], shapes1) for d in range(N_DEV)]
    c_all = jnp.concatenate([per[d][0] for d in range(N_DEV)], axis=0)
    norm_g_full = jnp.concatenate([per[2 * s][1] for s in range(N_CHIP)], axis=-1)
    conv_dw_full = jnp.concatenate([per[2 * s][2] for s in range(N_CHIP)], axis=-1)
    cctx = a["c_ctx"].reshape(1, D)
    c24 = jnp.concatenate([c_all] + [cctx] * B + [jnp.zeros((MROWS - nex - B, D), F32)], axis=0)

    bsh = lax.dynamic_slice(a["b_mod"], (0, chip * ncol), (2, ncol)).reshape(2, 1, ncol)
    mod_raw = _mod_fwd(c24, a["w_mod"], bsh)
    mine = _exchange_rows(jnp.swapaxes(mod_raw, 0, 1).reshape(MROWS, 2 * ncol // 128, 128), B)
    mod_mine = jnp.concatenate([mine[s].reshape(B + 1, 2, ncol) for s in range(N_CHIP)], axis=-1)
    mods = []
    for l in range(2):
        cx = jnp.broadcast_to(mod_mine[B, l][None], (B, N_MOD * D))
        mods.append(jnp.stack([cx, mod_mine[:B, l]], axis=1).reshape(B, 2, 3, 3, D))

    started = _gather_start(units, mod_mine)

    def wget_of(l):
        def wget(stage, after):
            if stage == "m":
                win, wo = _gather_wait(f"m{l}", started, [4 * l + 1, 4 * l + 2], after)
                return (win, 0), (wo, 0)
            (g,) = _gather_wait(f"{stage}{l}", started, [4 * l + (0 if stage == "a" else 3)], after)
            return (g, 0), (g, 1), (g, 2)
        return wget

    smalls = [_small_params(a["pool_w"][l], a["pool_scale"][l], a["ret_decay_fwd"][l], a["ret_decay_bwd"][l],
                            a["ret_gn_g"][l], conv_dw_full[l], a["conv_b"][l], a["conv_ln_g"][l], a["conv_ln_b"][l])
              for l in range(2)]
    normgs = [norm_g_full[l].reshape(3, 1, D) for l in range(2)]
    z = jnp.concatenate([ctx, x], axis=1)
    saved = []
    for l in range(2):
        z, sv = _layer_fwd(z, mods[l], normgs[l], wget_of(l), smalls[l])
        saved.append(sv)
    dz, dfinal_g, loss_part = _head(z, a["loss_target"], a["final_g"].reshape(1, D))

    scattering, swapping, reduced, newest = [], [], {}, []

    def reduce_oldest(after):
        tag, st, rows = scattering.pop(0)
        grads, others = _scatter_wait(tag, st, after)
        own = jnp.concatenate([lax.dynamic_slice(g, (chip, ci, 0, 0), (1, 1) + g.shape[2:]).reshape(g.shape[2:])
                               for g in grads], axis=0)
        mine = _sum_pieces(own, others)
        sw = _swap_start(tag, mine, rows, mine)
        if swapping:
            ptag, psw = swapping.pop()
            reduced[ptag] = _swap_wait(ptag, psw, sw[4])
        swapping.append((tag, sw))
        return sw[4]

    def emit_of(l):
        def emit(stage, grads):
            grads = [g.reshape(N_CHIP, 2, g.shape[0] // (2 * N_CHIP), D) for g in grads]
            stem = f"{stage}{l}"
            same = sum(1 for t, _, _ in scattering if t.split("_")[0] == stem)
            tag = stem if same == 0 else f"{stem}_{same}"
            st = _scatter_start(tag, grads, grads[0])
            tok = st[4][0, 0]
            while scattering and scattering[0][0].split("_")[0] != stem:
                tok = tok + reduce_oldest(st[4])[0, 0]
            scattering.append((tag, st, _piece_rows(grads)))
            newest[:] = [st[4]] + (newest if same else [])
            return tok
        return emit

    back = [None, None]
    tok = jnp.zeros((), F32)
    for l in (1, 0):
        dz, dmod, dnormg, sm, tok = _layer_bwd(dz, saved[l], mods[l], normgs[l], smalls[l], emit_of(l), tok, l == 0)
        back[l] = (dmod, dnormg, None, sm)
    grad_x = dz[:, LC:]
    grads = {}

    dmods = [back[l][0].reshape(B, 2, N_MOD * D) for l in range(2)]
    pack_a = _pack([jnp.stack([dm[:, 1] for dm in dmods])])
    ka = pack_a.shape[0]
    sm = [back[l][3] for l in range(2)]
    sum_list = [jnp.stack([dm[:, 0] for dm in dmods]), jnp.stack([back[l][1][:, 0] for l in range(2)])]
    sm_keys = ["pool_w", "pool_scale", "dec_f", "dec_b", "gng", "conv_dw", "conv_b", "conv_ln_g", "conv_ln_b"]
    sum_list += [jnp.stack([sm[l][k] for l in range(2)]) for k in sm_keys]
    sum_list += [dfinal_g[0], loss_part[0, 0:1]]
    sum_shapes = [s.shape for s in sum_list]
    g3 = _all_gather_small(jnp.concatenate([pack_a, _pack(sum_list)], axis=0))
    dmx_all = jnp.concatenate([_unpack(g3[d, :ka], [(2, B, N_MOD * D)])[0] for d in range(N_DEV)], axis=1)
    summed = _unpack(_sum_devices(g3[:, ka:]), sum_shapes)
    dmy, dnorm_full = summed[0], summed[1]
    sgrad = dict(zip(sm_keys, summed[2:2 + len(sm_keys)]))
    loss = summed[-1].reshape(())

    dmod24 = jnp.concatenate([dmx_all, dmy, jnp.zeros((2, MROWS - nex - B, N_MOD * D), F32)], axis=1)
    dmod_my = lax.dynamic_slice(dmod24, (0, 0, chip * ncol), (2, MROWS, ncol))
    grads["w_mod"], dsc = _mod_bwd(c24, dmod_my, a["w_mod"])
    g4 = _all_gather_small(_pack([dsc[nex:nex + 8]]))
    dsc_parts = jnp.stack([_unpack(g4[2 * s], [(8, D)])[0] for s in range(N_CHIP)])
    dbmod, dcctx = _bmod_cctx_grad(dmod24, dsc_parts, cctx)

    grads["c_ctx"] = dcctx[0]
    grads["b_mod"] = dbmod.reshape(2, N_MOD * D)
    grads["norm_g"] = lax.dynamic_slice(dnorm_full, (0, 0, chip * (D // N_CHIP)), (2, 3, D // N_CHIP))
    grads["pool_w"] = sgrad["pool_w"]
    grads["pool_scale"] = sgrad["pool_scale"]
    grads["ret_decay_fwd"] = sgrad["dec_f"]
    grads["ret_decay_bwd"] = sgrad["dec_b"]
    grads["ret_gn_g"] = sgrad["gng"]
    grads["conv_dw"] = lax.dynamic_slice(sgrad["conv_dw"], (0, 0, chip * (PW // N_CHIP)), (2, CONV_K, PW // N_CHIP))
    grads["conv_b"] = sgrad["conv_b"]
    grads["conv_ln_g"] = sgrad["conv_ln_g"]
    grads["conv_ln_b"] = sgrad["conv_ln_b"]
    grads["final_g"] = summed[-2]

    delta, new_m, new_v = {}, {}, {}
    delta["w_mod"], new_m["w_mod"], new_v["w_mod"] = _adam_any(a["w_mod"], grads["w_mod"], a["m_w_mod"], a["v_w_mod"])
    shapes_s = [a[n].shape for n in _SMALL]
    packed = _adam(_pack([a[n] for n in _SMALL]), _pack([grads[n] for n in _SMALL]),
                   _pack([a["m_" + n] for n in _SMALL]), _pack([a["v_" + n] for n in _SMALL]))
    for res, out in zip(packed, (delta, new_m, new_v)):
        for n, val in zip(_SMALL, _unpack(res, shapes_s)):
            out[n] = val

    def group(stem):
        out, k = list(reduced[stem]), 1
        while f"{stem}_{k}" in reduced:
            out += reduced[f"{stem}_{k}"]
            k += 1
        return [h.reshape(-1, D) for h in out]

    turned = ("ffn_w1", "ffn_w3", "w_in")
    done = {n: None for n in _BIG[1:]}

    def adam_of(n, block, nblocks, g):
        t = (lambda u: jnp.swapaxes(u, -1, -2)) if n in turned else (lambda u: u)
        done[n] = _adam_block(t(a[n]), g, t(a["m_" + n]), t(a["v_" + n]), block, nblocks, done[n])

    def adam_stage(stage, l):
        mats = group(f"{stage}{l}")
        if stage == "m":
            adam_of("w_in", l, 2, mats[0])
            adam_of("w_out", l, 2, mats[1])
        else:
            for n, g in zip(("ffn_w1", "ffn_w3", "ffn_w2"), mats):
                adam_of(n, 2 * l + (stage == "b"), 4, g)

    for stage, l in (("b", 1), ("m", 1), ("a", 1), ("b", 0)):
        adam_stage(stage, l)

    last = [delta["w_mod"], packed[0]] + [done[n][1] for n in _BIG[1:]] + newest
    while scattering:
        last = reduce_oldest(last)
    tag, sw = swapping.pop()
    reduced[tag] = _swap_wait(tag, sw, last)
    adam_stage("m", 0)
    adam_stage("a", 0)
    for n in _BIG[1:]:
        grads[n], delta[n], new_m[n], new_v[n] = [jnp.swapaxes(r, -1, -2) for r in done[n]] if n in turned else done[n]
    return (loss, grad_x, *[grads[n] for n in _WEIGHTS], *[delta[n] for n in _WEIGHTS],
            *[new_m[n] for n in _WEIGHTS], *[new_v[n] for n in _WEIGHTS])


def kernel(x, c, ctx, c_ctx, w_mod, b_mod, norm_g, ffn_w1, ffn_w3, ffn_w2, w_in, w_out, pool_w, pool_scale, ret_decay_fwd, ret_decay_bwd, ret_gn_g, conv_dw, conv_b, conv_ln_g, conv_ln_b, final_g, loss_target, m_c_ctx, m_w_mod, m_b_mod, m_norm_g, m_ffn_w1, m_ffn_w3, m_ffn_w2, m_w_in, m_w_out, m_pool_w, m_pool_scale, m_ret_decay_fwd, m_ret_decay_bwd, m_ret_gn_g, m_conv_dw, m_conv_b, m_conv_ln_g, m_conv_ln_b, m_final_g, v_c_ctx, v_w_mod, v_b_mod, v_norm_g, v_ffn_w1, v_ffn_w3, v_ffn_w2, v_w_in, v_w_out, v_pool_w, v_pool_scale, v_ret_decay_fwd, v_ret_decay_bwd, v_ret_gn_g, v_conv_dw, v_conv_b, v_conv_ln_g, v_conv_ln_b, v_final_g):
    return _step(dict(locals()))
```

```python
import functools

import jax
import jax.numpy as jnp
from jax import lax
from jax.experimental import pallas as pl
from jax.experimental.pallas import tpu as pltpu

F32 = jnp.float32
BF16 = jnp.bfloat16

D = 1024
F = 2816
FH = 1408
N_MOD = 9
LC = 256
TM = 256
HD = 128
NH = 4
RW = 512
PW = 256
CONV_K = 31
GRID_W = 64
EPS = 1e-6
K_SCALE = HD ** -0.5
N_DEV = 8
N_CHIP = 4
SLAB = F // N_CHIP
HSLAB = SLAB // 2
OSLAB = D // N_CHIP
HOSLAB = OSLAB // 2
VMEM_BIG = 60 * 1024 * 1024
MESH = pl.DeviceIdType.MESH

ADAM_LR = 0.001
ADAM_B1 = 0.9
ADAM_B2 = 0.999
ADAM_EPS = 1e-08
ADAM_WD = 0.01
ADAM_STEP = 10


def _nt(a, b):
    return lax.dot_general(a, b, (((1,), (1,)), ((), ())), preferred_element_type=F32)


def _nn(a, b):
    return lax.dot_general(a, b, (((1,), (0,)), ((), ())), preferred_element_type=F32)


def _tn(a, b):
    return lax.dot_general(a, b, (((0,), (0,)), ((), ())), preferred_element_type=F32)


def _params(vmem=None, sem=None):
    return pltpu.CompilerParams(dimension_semantics=sem, vmem_limit_bytes=vmem)


def _rms_mod(z, g, shift, scale):
    y = z * lax.rsqrt(jnp.mean(z * z, axis=-1, keepdims=True) + EPS)
    return (y * g) * (1.0 + scale) + shift


def _acc(ref, val, first):
    @pl.when(first)
    def _():
        ref[...] = val

    @pl.when(jnp.logical_not(first))
    def _():
        ref[...] += val


def _tok(width):
    return pl.BlockSpec((None, TM, width), lambda b, t: (b, t, 0))


def _modspec():
    return pl.BlockSpec((None, None, 3, D), lambda b, t: (b, jnp.minimum(t, 1), 0, 0))


def _const(shape):
    nd = len(shape)
    return pl.BlockSpec(shape, lambda b, t: (0,) * nd)


def _wspec(w):
    stack, idx = w
    return pl.BlockSpec((None,) + stack.shape[1:], lambda b, t: (idx, 0, 0), pipeline_mode=pl.Buffered(1))


def _ffn_fwd(z, mod, g, w1t, w3t, w2):
    B, S, _ = z.shape

    def body(z_ref, mod_ref, g_ref, w1_ref, w3_ref, w2_ref, zo_ref, f_ref):
        zt = z_ref[...]
        h = _rms_mod(zt, g_ref[...], mod_ref[0:1, :], mod_ref[1:2, :]).astype(BF16)
        f = jnp.zeros((TM, D), F32)
        for c in range(F // FH):
            rows = slice(c * FH, (c + 1) * FH)
            u1 = _nt(h, w1_ref[rows, :])
            u3 = _nt(h, w3_ref[rows, :])
            a = (u1 * jax.nn.sigmoid(u1) * u3).astype(BF16)
            f = f + _nn(a, w2_ref[rows, :])
        f_ref[...] = f
        zo_ref[...] = zt + 0.5 * mod_ref[2:3, :] * f

    return pl.pallas_call(
        body, name="ffn_fwd", grid=(B, S // TM),
        in_specs=[_tok(D), _modspec(), _const((1, D)), _wspec(w1t), _wspec(w3t), _wspec(w2)],
        out_specs=[_tok(D), _tok(D)],
        out_shape=[jax.ShapeDtypeStruct((B, S, D), F32)] * 2,
        compiler_params=_params(VMEM_BIG, ("arbitrary", "arbitrary")),
    )(z, mod, g, w1t[0], w3t[0], w2[0])


def _ffn_bwd(z, dzo, f, mod, g, w1t, w3t, w2, each=None):
    B, S, _ = z.shape

    def body(z_ref, dzo_ref, f_ref, mod_ref, g_ref, w1_ref, w3_ref, w2_ref,
             dz_ref, h_ref, du1_ref, du3_ref, a_ref, do_ref, dmod_ref, dg_ref):
        b, t = pl.program_id(0), pl.program_id(1)
        zt = z_ref[...]
        dzo = dzo_ref[...]
        gate = mod_ref[2:3, :]
        h32, vjp_h = jax.vjp(_rms_mod, zt, g_ref[...], mod_ref[0:1, :], mod_ref[1:2, :])
        h = h32.astype(BF16)
        h_ref[...] = h
        do = (0.5 * gate * dzo).astype(BF16)
        do_ref[...] = do
        dgate = jnp.sum(0.5 * f_ref[...] * dzo, axis=0, keepdims=True)
        dh = jnp.zeros((TM, D), F32)
        for c in range(F // FH):
            rows = slice(c * FH, (c + 1) * FH)
            u1 = _nt(h, w1_ref[rows, :])
            u3 = _nt(h, w3_ref[rows, :])
            sg = jax.nn.sigmoid(u1)
            s = u1 * sg
            a_ref[:, rows] = (s * u3).astype(BF16)
            da = _nt(do, w2_ref[rows, :])
            du3 = (da * s).astype(BF16)
            du1 = (da * u3 * (sg * (1.0 + u1 * (1.0 - sg)))).astype(BF16)
            du1_ref[:, rows] = du1
            du3_ref[:, rows] = du3
            dh = dh + _nn(du1, w1_ref[rows, :]) + _nn(du3, w3_ref[rows, :])
        dz_h, dg, dshift, dscale = vjp_h(dh)
        dz_ref[...] = dzo + dz_h
        _acc(dmod_ref, jnp.concatenate([dshift, dscale, dgate], axis=0), t <= 1)
        _acc(dg_ref, dg, jnp.logical_and(b == 0, t == 0))

    T = B * S
    outs = pl.pallas_call(
        body, name="ffn_bwd", grid=(B, S // TM),
        in_specs=[_tok(D), _tok(D), _tok(D), _modspec(), _const((1, D)), _wspec(w1t), _wspec(w3t), _wspec(w2)],
        out_specs=[_tok(D), _tok(D), _tok(F), _tok(F), _tok(F), _tok(D), _modspec(), _const((1, D))],
        out_shape=[jax.ShapeDtypeStruct((B, S, D), F32), jax.ShapeDtypeStruct((B, S, D), BF16),
                   jax.ShapeDtypeStruct((B, S, F), BF16), jax.ShapeDtypeStruct((B, S, F), BF16),
                   jax.ShapeDtypeStruct((B, S, F), BF16), jax.ShapeDtypeStruct((B, S, D), BF16),
                   jax.ShapeDtypeStruct((B, 2, 3, D), F32), jax.ShapeDtypeStruct((1, D), F32)],
        compiler_params=_params(VMEM_BIG, ("arbitrary", "arbitrary")),
    )(z, dzo, f, mod, g, w1t[0], w3t[0], w2[0])
    dz, h, du1, du3, a, do, dmod, dg = outs
    grads = []
    for lhs, rhs in ((du1, h), (du3, h), (a, do)):
        grads.append(_tn_matmul(lhs.reshape(T, F), rhs.reshape(T, D)))
        if each is not None:
            each(grads[-1])
    return (dz, dmod, dg, *grads)


def _tn_matmul(a, b):
    T, M = a.shape
    N = b.shape[1]
    MB = FH if M > FH else M
    TT = next(t for t in (1152, 1024, 768, 512, TM) if T % t == 0)
    nt = T // TT

    def body(a_ref, b_ref, o_ref, acc_ref):
        t = pl.program_id(1)
        prod = _tn(a_ref[...], b_ref[...])
        _acc(acc_ref, prod, t == 0)

        @pl.when(t == nt - 1)
        def _():
            o_ref[...] = acc_ref[...].astype(BF16)

    return pl.pallas_call(
        body, name="tn_matmul", grid=(M // MB, nt),
        in_specs=[pl.BlockSpec((TT, MB), lambda i, t: (t, i)), pl.BlockSpec((TT, N), lambda i, t: (t, 0))],
        out_specs=pl.BlockSpec((MB, N), lambda i, t: (i, 0)),
        out_shape=jax.ShapeDtypeStruct((M, N), BF16),
        scratch_shapes=[pltpu.VMEM((MB, N), F32)],
        compiler_params=_params(VMEM_BIG, ("arbitrary", "arbitrary")),
    )(a, b)


def _swap32(x):
    n = x.shape[1]
    lane = lax.broadcasted_iota(jnp.int32, x.shape, 1)
    return jnp.where((lane % 64) < 32, pltpu.roll(x, n - 32, 1), pltpu.roll(x, 32, 1))


def _rope(x, cos, sin):
    return x * cos + _swap32(x) * sin


def _rope_t(dy, cos, sin):
    return dy * cos + _swap32(dy * sin)


def _rope_tables(S):
    L = S - LC
    n_freq = HD // 4
    inv = 10000.0 ** (-jnp.arange(n_freq, dtype=F32) / n_freq)
    i = jnp.arange(L)
    row = (i // GRID_W).astype(F32)
    col = (i % GRID_W).astype(F32)
    ang_r = row[:, None] * inv[None]
    ang_c = col[:, None] * inv[None]
    ang = jnp.concatenate([ang_r, ang_r, ang_c, ang_c], axis=1)
    ang = jnp.concatenate([jnp.zeros((LC, HD), F32), ang], axis=0)
    sign = jnp.where((jnp.arange(HD) % 64) < 32, -1.0, 1.0).astype(F32)
    return jnp.cos(ang), jnp.sin(ang) * sign[None]


def _tabspec():
    return pl.BlockSpec((TM, HD), lambda b, t: (t, 0))


def _mix_in_fwd(z, mod, g, wint, cos, sin):
    B, S, _ = z.shape

    def body(z_ref, mod_ref, g_ref, w_ref, cos_ref, sin_ref, pp_ref, q_ref, k_ref, v_ref, gg_ref, pc_ref):
        h = _rms_mod(z_ref[...], g_ref[...], mod_ref[0:1, :], mod_ref[1:2, :]).astype(BF16)
        p = _nt(h, w_ref[...])
        cos = jnp.tile(cos_ref[...], (1, NH))
        sin = jnp.tile(sin_ref[...], (1, NH))
        pp_ref[...] = p[:, 0:PW]
        q_ref[...] = _rope(p[:, PW:PW + RW], cos, sin)
        k_ref[...] = _rope(p[:, PW + RW:PW + 2 * RW], cos, sin) * K_SCALE
        v_ref[...] = p[:, PW + 2 * RW:PW + 3 * RW]
        gg_ref[...] = p[:, PW + 3 * RW:PW + 4 * RW]
        pc_ref[...] = p[:, PW + 4 * RW:]

    return pl.pallas_call(
        body, name="mix_in_fwd", grid=(B, S // TM),
        in_specs=[_tok(D), _modspec(), _const((1, D)), _wspec(wint), _tabspec(), _tabspec()],
        out_specs=[_tok(PW), _tok(RW), _tok(RW), _tok(RW), _tok(RW), _tok(2 * PW)],
        out_shape=[jax.ShapeDtypeStruct((B, S, PW), F32)] + [jax.ShapeDtypeStruct((B, S, RW), F32)] * 5,
        compiler_params=_params(VMEM_BIG, ("arbitrary", "arbitrary")),
    )(z, mod, g, wint[0], cos, sin)


def _mix_in_bwd(z, dzo, dpp, dqa, dqb, dka, dkb, dva, dvb, dgg, dpc, mod, g, wint, cos, sin):
    B, S, _ = z.shape

    def body(z_ref, dzo_ref, dpp_ref, dqa_ref, dqb_ref, dka_ref, dkb_ref, dva_ref, dvb_ref, dgg_ref, dpc_ref, mod_ref, g_ref,
             w_ref, cos_ref, sin_ref, dz_ref, h_ref, dp_ref, dmod_ref, dg_ref):
        b, t = pl.program_id(0), pl.program_id(1)
        h32, vjp_h = jax.vjp(_rms_mod, z_ref[...], g_ref[...], mod_ref[0:1, :], mod_ref[1:2, :])
        h_ref[...] = h32.astype(BF16)
        cos = jnp.tile(cos_ref[...], (1, NH))
        sin = jnp.tile(sin_ref[...], (1, NH))
        dq = _rope_t(dqa_ref[...] + dqb_ref[...], cos, sin)
        dk = _rope_t(dka_ref[...] + dkb_ref[...], cos, sin) * K_SCALE
        dp = jnp.concatenate([dpp_ref[...], dq, dk, dva_ref[...] + dvb_ref[...], dgg_ref[...], dpc_ref[...]],
                             axis=1).astype(BF16)
        dp_ref[...] = dp
        dh = _nn(dp, w_ref[...])
        dz_h, dg, dshift, dscale = vjp_h(dh)
        dz_ref[...] = dzo_ref[...] + dz_h
        _acc(dmod_ref, jnp.concatenate([dshift, dscale, jnp.zeros_like(dshift)], axis=0), t <= 1)
        _acc(dg_ref, dg, jnp.logical_and(b == 0, t == 0))

    return pl.pallas_call(
        body, name="mix_in_bwd", grid=(B, S // TM),
        in_specs=[_tok(D), _tok(D), _tok(PW)] + [_tok(RW)] * 7 + [_tok(2 * PW), _modspec(), _const((1, D)), _wspec(wint),
                                                                  _tabspec(), _tabspec()],
        out_specs=[_tok(D), _tok(D), _tok(F), _modspec(), _const((1, D))],
        out_shape=[jax.ShapeDtypeStruct((B, S, D), F32), jax.ShapeDtypeStruct((B, S, D), BF16),
                   jax.ShapeDtypeStruct((B, S, F), BF16), jax.ShapeDtypeStruct((B, 2, 3, D), F32),
                   jax.ShapeDtypeStruct((1, D), F32)],
        compiler_params=_params(VMEM_BIG, ("arbitrary", "arbitrary")),
    )(z, dzo, dpp, dqa, dqb, dka, dkb, dva, dvb, dgg, dpc, mod, g, wint[0], cos, sin)


def _log_sigmoid(x):
    return jnp.minimum(x, 0.0) - jnp.log(1.0 + jnp.exp(-jnp.abs(x)))


def _retention(a, b, c, dec_a, dec_b, sched_a, sched_b):
    B, S, _ = a.shape
    C = TM

    def body(a_ref, b_ref, c_ref, da_ref, db_ref, oa_ref, ob_ref):
        ii = lax.broadcasted_iota(jnp.int32, (C, C), 0)
        jj = lax.broadcasted_iota(jnp.int32, (C, C), 1)
        pos = lax.broadcasted_iota(jnp.int32, (C, 1), 0).astype(F32)
        for dec_ref, o_ref, (order, causal, strict) in ((da_ref, oa_ref, sched_a), (db_ref, ob_ref, sched_b)):
            lg = _log_sigmoid(dec_ref[...])
            lg1 = lg[:, 0:1]
            dist = ((ii - jj) if causal else (jj - ii)).astype(F32)
            mask = (dist > 0.0) if strict else (dist >= 0.0)
            decay = jnp.where(mask, jnp.exp(jnp.maximum(dist, 0.0) * lg1), 0.0)
            p = pos if causal else (C - 1.0 - pos)
            w_q = jnp.exp((p + 1.0) * lg1)
            w_k = jnp.exp((C - 1.0 - p) * lg1)
            chunk_decay = jnp.exp(C * lg)
            state = jnp.zeros((HD, HD), F32)
            for n in order:
                rows = pl.ds(n * C, C)
                at, bt, ct = a_ref[rows, :], b_ref[rows, :], c_ref[rows, :]
                cb = ct.astype(BF16)
                scores = _nt(at.astype(BF16), bt.astype(BF16)) * decay
                o = _nn(scores.astype(BF16), cb)
                o = o + _nn((at * w_q).astype(BF16), state.astype(BF16))
                o_ref[rows, :] = o
                state = chunk_decay * state + _tn((bt * w_k).astype(BF16), cb)

    seq = pl.BlockSpec((None, S, HD), lambda b, h: (b, 0, h))
    dspec = pl.BlockSpec((None, 1, HD), lambda b, h: (h, 0, 0))
    return pl.pallas_call(
        body, name="retention", grid=(B, NH),
        in_specs=[seq, seq, seq, dspec, dspec], out_specs=[seq, seq],
        out_shape=[jax.ShapeDtypeStruct((B, S, RW), F32)] * 2,
        compiler_params=_params(VMEM_BIG, ("arbitrary", "arbitrary")),
    )(a, b, c, dec_a, dec_b)


def _retention_ddecay(q, k, v, do, dec_a, dec_b, sched_a, sched_b):
    B, S, _ = q.shape
    C = TM

    def body(q_ref, k_ref, v_ref, do_ref, da_ref, db_ref, o_ref):
        ii = lax.broadcasted_iota(jnp.int32, (C, C), 0)
        jj = lax.broadcasted_iota(jnp.int32, (C, C), 1)
        pos = lax.broadcasted_iota(jnp.int32, (C, 1), 0).astype(F32)
        vals = []
        for dec_ref, (order, causal, strict) in ((da_ref, sched_a), (db_ref, sched_b)):
            x = dec_ref[...]
            lg = _log_sigmoid(x)
            lg1 = lg[:, 0:1]
            dist = ((ii - jj) if causal else (jj - ii)).astype(F32)
            mask = (dist > 0.0) if strict else (dist >= 0.0)
            ddecay = jnp.where(mask, dist * jnp.exp(jnp.maximum(dist, 0.0) * lg1), 0.0)
            p = pos if causal else (C - 1.0 - pos)
            w_q = jnp.exp((p + 1.0) * lg1)
            w_k = jnp.exp((C - 1.0 - p) * lg1)
            chunk_decay = jnp.exp(C * lg)
            state = jnp.zeros((HD, HD), F32)
            dstate = jnp.zeros((HD, HD), F32)
            tot = jnp.zeros((), F32)
            for n in order:
                rows = pl.ds(n * C, C)
                qt, kt, vt, dot = q_ref[rows, :], k_ref[rows, :], v_ref[rows, :], do_ref[rows, :]
                vb = vt.astype(BF16)
                scores = _nt(qt.astype(BF16), kt.astype(BF16))
                dscores = _nt(dot.astype(BF16), vb)
                qw = (qt * w_q).astype(BF16)
                cross = _nn(qw, state.astype(BF16))
                dcross = _nn(qw, dstate.astype(BF16))
                tot = tot + jnp.sum(scores * dscores * ddecay) + jnp.sum(((p + 1.0) * cross + dcross) * dot)
                kv = _tn((kt * w_k).astype(BF16), vb)
                dkv = _tn((kt * ((C - 1.0 - p) * w_k)).astype(BF16), vb)
                dstate = chunk_decay * (dstate + C * state) + dkv
                state = chunk_decay * state + kv
            vals.append(tot * jax.nn.sigmoid(-x))
        row = lax.broadcasted_iota(jnp.int32, (8, HD), 0)
        tile = jnp.where(row == 0, vals[0], 0.0) + jnp.where(row == 1, vals[1], 0.0)
        _acc(o_ref, tile, pl.program_id(1) == 0)

    seq = pl.BlockSpec((None, S, HD), lambda h, b: (b, 0, h))
    dspec = pl.BlockSpec((None, 1, HD), lambda h, b: (h, 0, 0))
    return pl.pallas_call(
        body, name="retention_ddecay", grid=(NH, B),
        in_specs=[seq, seq, seq, seq, dspec, dspec], out_specs=pl.BlockSpec((None, 8, HD), lambda h, b: (h, 0, 0)),
        out_shape=jax.ShapeDtypeStruct((NH, 8, HD), F32),
        compiler_params=_params(VMEM_BIG, ("arbitrary", "arbitrary")),
    )(q, k, v, do, dec_a, dec_b)


def _schedules(S):
    n = S // TM
    lat_up = tuple(range(1, n))
    lat_down = tuple(range(n - 1, 0, -1))
    fwd = (((0,) + lat_up, True, False), ((0,) + lat_down, False, True))
    bwd = ((lat_down + (0,), False, False), (lat_up + (0,), True, True))
    return fwd, bwd


def _shift_rows(x, d):
    if d == 0:
        return x
    S = x.shape[0]
    t = lax.broadcasted_iota(jnp.int32, x.shape, 0)
    tt = t + d
    lo = jnp.where(t < LC, 0, LC)
    hi = jnp.where(t < LC, LC, S)
    return jnp.where((tt >= lo) & (tt < hi), pltpu.roll(x, (-d) % S, 0), 0.0)


@functools.partial(jax.custom_vjp, nondiff_argnums=(1,))
def _shift(x, d):
    return _shift_rows(x, d)


_shift.defvjp(lambda x, d: (_shift_rows(x, d), None), lambda d, _, g: (_shift_rows(g, -d),))


def _pool_fn(p, bd, pscale):
    lane = lax.broadcasted_iota(jnp.int32, p.shape, 1)
    grp = lane // (PW // 4)
    half = jnp.where(grp == 0, 1, jnp.where(grp == 1, 2, jnp.where(grp == 2, 4, 8)))
    ones = jnp.ones(p.shape, F32)
    acc = jnp.zeros(p.shape, F32)
    cnt = jnp.zeros(p.shape, F32)
    for d in range(-8, 8):
        inwin = ((d >= -half) & (d < half)).astype(F32)
        acc = acc + _shift(p, d) * inwin
        cnt = cnt + _shift_rows(ones, d) * inwin
    pooled = acc / cnt - p
    mixed = _nn(pooled.astype(BF16), bd.astype(BF16))
    return mixed * pscale


def _dwconv_raw(zc, dw):
    y = jnp.zeros(zc.shape, F32)
    for k in range(CONV_K):
        y = y + _shift_rows(zc, k - CONV_K // 2) * dw[k:k + 1, :]
    return y


@jax.custom_vjp
def _dwconv(zc, dw):
    return _dwconv_raw(zc, dw)


def _dwconv_fwd(zc, dw):
    return _dwconv_raw(zc, dw), (zc, dw)


def _dwconv_bwd(res, g):
    zc, dw = res
    dz = jnp.zeros(zc.shape, F32)
    ddw = jnp.zeros(dw.shape, F32)
    row = lax.broadcasted_iota(jnp.int32, dw.shape, 0)
    for k in range(CONV_K):
        dz = dz + _shift_rows(g, CONV_K // 2 - k) * dw[k:k + 1, :]
        r = jnp.sum(g * _shift_rows(zc, k - CONV_K // 2), axis=0, keepdims=True)
        ddw = ddw + jnp.where(row == k, r, 0.0)
    return dz, ddw


_dwconv.defvjp(_dwconv_fwd, _dwconv_bwd)


def _conv_fn(u, dw, db):
    zc = u[:, :PW] * jax.nn.sigmoid(u[:, PW:])
    return _dwconv(zc, dw) + db


def _ln_swish(y, lng, lnb):
    mu = jnp.mean(y, axis=-1, keepdims=True)
    yc = y - mu
    var = jnp.mean(yc * yc, axis=-1, keepdims=True)
    yn = yc * lax.rsqrt(var + EPS) * lng + lnb
    return yn * jax.nn.sigmoid(yn)


def _seq(shape, single=False):
    return pl.BlockSpec((None,) + shape, lambda b: (b, 0, 0), pipeline_mode=pl.Buffered(1) if single else None)


def _c1(shape):
    nd = len(shape)
    return pl.BlockSpec(shape, lambda b: (0,) * nd)


def _seq_apply(fn, name, xs, consts, width):
    B, S, w = xs.shape

    def body(x_ref, *refs):
        refs[-1][...] = fn(x_ref[...], *[r[...] for r in refs[:-1]])

    return pl.pallas_call(
        body, name=name, grid=(B,),
        in_specs=[_seq((S, w))] + [_c1(c.shape) for c in consts], out_specs=_seq((S, width)),
        out_shape=jax.ShapeDtypeStruct((B, S, width), F32),
        compiler_params=_params(VMEM_BIG, ("arbitrary",)),
    )(xs, *consts)


def _seq_vjp(fn, name, xs, consts, dout):
    B, S, w = xs.shape
    n = len(consts)

    def body(x_ref, d_ref, *refs):
        first = pl.program_id(0) == 0
        _, vjp = jax.vjp(fn, x_ref[...], *[r[...] for r in refs[:n]])
        grads = vjp(d_ref[...])
        refs[n][...] = grads[0]
        for ref, val in zip(refs[n + 1:], grads[1:]):
            _acc(ref, val, first)

    return pl.pallas_call(
        body, name=name, grid=(B,),
        in_specs=[_seq((S, w), True), _seq((S, dout.shape[2]), True)] + [_c1(c.shape) for c in consts],
        out_specs=[_seq((S, w))] + [_c1(c.shape) for c in consts],
        out_shape=[jax.ShapeDtypeStruct((B, S, w), F32)] + [jax.ShapeDtypeStruct(c.shape, F32) for c in consts],
        compiler_params=_params(VMEM_BIG, ("arbitrary",)),
    )(xs, dout, *consts)


def _pool_conv_fwd(pp, pc, bd, pscale, dw, db):
    return (_seq_apply(_pool_fn, "pool_fwd", pp, (bd, pscale), PW),
            _seq_apply(_conv_fn, "conv_fwd", pc, (dw, db), PW))


def _pool_conv_bwd(pp, pc, dpo, dco, bd, pscale, dw, db):
    dpp, dbd, dps = _seq_vjp(_pool_fn, "pool_bwd", pp, (bd, pscale), dpo)
    dpc, ddw, ddb = _seq_vjp(_conv_fn, "conv_bwd", pc, (dw, db), dco)
    return dpp, dpc, dbd, dps, ddw, ddb


def _cat_fn(po, oa, ob, gg, co, gng, lng, lnb):
    o = oa + ob
    outs = []
    for h in range(NH):
        oh = o[:, h * HD:(h + 1) * HD]
        mu = jnp.mean(oh, axis=-1, keepdims=True)
        oc = oh - mu
        var = jnp.mean(oc * oc, axis=-1, keepdims=True)
        outs.append(oc * lax.rsqrt(var + EPS))
    ret = jnp.concatenate(outs, axis=1) * gng * (gg * jax.nn.sigmoid(gg))
    return jnp.concatenate([po, ret, _ln_swish(co, lng, lnb)], axis=1)


def _mix_out_fwd(z, po, oa, ob, gg, co, ro, mod, wout):
    B, S, _ = z.shape

    def body(z_ref, po_ref, oa_ref, ob_ref, gg_ref, co_ref, gn_ref, lg_ref, lb_ref, mod_ref, w_ref, zo_ref, out_ref):
        cat = _cat_fn(po_ref[...], oa_ref[...], ob_ref[...], gg_ref[...], co_ref[...], gn_ref[...], lg_ref[...], lb_ref[...])
        out = _nn(cat.astype(BF16), w_ref[...])
        out_ref[...] = out
        zo_ref[...] = z_ref[...] + mod_ref[2:3, :] * out

    return pl.pallas_call(
        body, name="mix_out_fwd", grid=(B, S // TM),
        in_specs=[_tok(D), _tok(PW), _tok(RW), _tok(RW), _tok(RW), _tok(PW), _const((1, RW)), _const((1, PW)),
                  _const((1, PW)), _modspec(), _wspec(wout)],
        out_specs=[_tok(D), _tok(D)],
        out_shape=[jax.ShapeDtypeStruct((B, S, D), F32)] * 2,
        compiler_params=_params(None, ("arbitrary", "arbitrary")),
    )(z, po, oa, ob, gg, co, *ro, mod, wout[0])


def _mix_out_bwd(dzo, out, po, oa, ob, gg, co, ro, mod, wout):
    B, S, _ = dzo.shape

    def body(dzo_ref, out_ref, po_ref, oa_ref, ob_ref, gg_ref, co_ref, gn_ref, lg_ref, lb_ref, mod_ref, w_ref,
             dpo_ref, do_ref, dgg_ref, dco_ref, cat_ref, dout_ref, dmod_ref, dgn_ref, dlg_ref, dlb_ref):
        b, t = pl.program_id(0), pl.program_id(1)
        dzo = dzo_ref[...]
        cat, vjp = jax.vjp(_cat_fn, po_ref[...], oa_ref[...], ob_ref[...], gg_ref[...], co_ref[...], gn_ref[...],
                           lg_ref[...], lb_ref[...])
        cat_ref[...] = cat.astype(BF16)
        dout = (mod_ref[2:3, :] * dzo).astype(BF16)
        dout_ref[...] = dout
        dgate = jnp.sum(out_ref[...] * dzo, axis=0, keepdims=True)
        dcat = _nt(dout, w_ref[...])
        dpo, doa, _, dgg, dco, dgn, dlg, dlb = vjp(dcat)
        dpo_ref[...] = dpo
        do_ref[...] = doa
        dgg_ref[...] = dgg
        dco_ref[...] = dco
        zero = jnp.zeros_like(dgate)
        _acc(dmod_ref, jnp.concatenate([zero, zero, dgate], axis=0), t <= 1)
        first = jnp.logical_and(b == 0, t == 0)
        _acc(dgn_ref, dgn, first)
        _acc(dlg_ref, dlg, first)
        _acc(dlb_ref, dlb, first)

    return pl.pallas_call(
        body, name="mix_out_bwd", grid=(B, S // TM),
        in_specs=[_tok(D), _tok(D), _tok(PW), _tok(RW), _tok(RW), _tok(RW), _tok(PW), _const((1, RW)), _const((1, PW)),
                  _const((1, PW)), _modspec(), _wspec(wout)],
        out_specs=[_tok(PW), _tok(RW), _tok(RW), _tok(PW), _tok(D), _tok(D), _modspec(), _const((1, RW)),
                   _const((1, PW)), _const((1, PW))],
        out_shape=[jax.ShapeDtypeStruct((B, S, PW), F32), jax.ShapeDtypeStruct((B, S, RW), F32),
                   jax.ShapeDtypeStruct((B, S, RW), F32), jax.ShapeDtypeStruct((B, S, PW), F32),
                   jax.ShapeDtypeStruct((B, S, D), BF16), jax.ShapeDtypeStruct((B, S, D), BF16),
                   jax.ShapeDtypeStruct((B, 2, 3, D), F32), jax.ShapeDtypeStruct((1, RW), F32),
                   jax.ShapeDtypeStruct((1, PW), F32), jax.ShapeDtypeStruct((1, PW), F32)],
        compiler_params=_params(None, ("arbitrary", "arbitrary")),
    )(dzo, out, po, oa, ob, gg, co, *ro, mod, wout[0])


def _rms(z, g):
    return z * lax.rsqrt(jnp.mean(z * z, axis=-1, keepdims=True) + EPS) * g


def _head(z, target, fg):
    B, S, _ = z.shape

    def body(z_ref, t_ref, g_ref, dz_ref, dg_ref, loss_ref):
        b, t = pl.program_id(0), pl.program_id(1)
        first = jnp.logical_and(b == 0, t == 0)

        @pl.when(t == 0)
        def _():
            dz_ref[...] = jnp.zeros((TM, D), F32)

        @pl.when(first)
        def _():
            dg_ref[...] = jnp.zeros((1, D), F32)
            loss_ref[...] = jnp.zeros((8, 128), F32)

        @pl.when(t > 0)
        def _():
            y, vjp = jax.vjp(_rms, z_ref[...], g_ref[...])
            err = y - t_ref[...]
            dz, dg = vjp(err * (1.0 / D))
            dz_ref[...] = dz
            dg_ref[...] += dg
            loss_ref[...] += 0.5 * jnp.sum(err * err) * (1.0 / D)

    return pl.pallas_call(
        body, name="head", grid=(B, S // TM),
        in_specs=[_tok(D), pl.BlockSpec((None, TM, D), lambda b, t: (b, jnp.maximum(t - 1, 0), 0)), _const((1, D))],
        out_specs=[_tok(D), _const((1, D)), _const((8, 128))],
        out_shape=[jax.ShapeDtypeStruct((B, S, D), F32), jax.ShapeDtypeStruct((1, D), F32),
                   jax.ShapeDtypeStruct((8, 128), F32)],
        compiler_params=_params(None, ("arbitrary", "arbitrary")),
    )(z, target, fg)


MROWS = 24
MCOL = 768


def _silu(x):
    return x * jax.nn.sigmoid(x)


def _mod_fwd(c24, wmod, bmod):
    ncol = wmod.shape[2]

    def body(c_ref, w_ref, b_ref, o_ref):
        sc = _silu(c_ref[...]).astype(BF16)
        o_ref[...] = _nn(sc, w_ref[...].astype(BF16)) + b_ref[...]

    return pl.pallas_call(
        body, name="mod_fwd", grid=(2, ncol // MCOL),
        in_specs=[pl.BlockSpec((MROWS, D), lambda l, j: (0, 0)), pl.BlockSpec((None, D, MCOL), lambda l, j: (l, 0, j)),
                  pl.BlockSpec((None, 1, MCOL), lambda l, j: (l, 0, j))],
        out_specs=pl.BlockSpec((None, MROWS, MCOL), lambda l, j: (l, 0, j)),
        out_shape=jax.ShapeDtypeStruct((2, MROWS, ncol), F32),
        compiler_params=_params(None, ("arbitrary", "arbitrary")),
    )(c24, wmod, bmod)


def _mod_bwd(c24, dmod, wmod):
    ncol = wmod.shape[2]

    def body(c_ref, d_ref, w_ref, dw_ref, dsc_ref):
        l, j = pl.program_id(0), pl.program_id(1)
        sc = _silu(c_ref[...]).astype(BF16)
        dm = d_ref[...].astype(BF16)
        dw_ref[...] = _tn(sc, dm)
        _acc(dsc_ref, _nt(dm, w_ref[...].astype(BF16)), jnp.logical_and(l == 0, j == 0))

    return pl.pallas_call(
        body, name="mod_bwd", grid=(2, ncol // MCOL),
        in_specs=[pl.BlockSpec((MROWS, D), lambda l, j: (0, 0)), pl.BlockSpec((None, MROWS, MCOL), lambda l, j: (l, 0, j)),
                  pl.BlockSpec((None, D, MCOL), lambda l, j: (l, 0, j))],
        out_specs=[pl.BlockSpec((None, D, MCOL), lambda l, j: (l, 0, j)), pl.BlockSpec((MROWS, D), lambda l, j: (0, 0))],
        out_shape=[jax.ShapeDtypeStruct((2, D, ncol), F32), jax.ShapeDtypeStruct((MROWS, D), F32)],
        compiler_params=_params(None, ("arbitrary", "arbitrary")),
    )(c24, dmod, wmod)


def _bmod_cctx_grad(dmod_full, dsc_parts, cctx):
    def body(d_ref, p_ref, c_ref, db_ref, dc_ref):
        db_ref[...] = jnp.sum(d_ref[...], axis=1, keepdims=True)
        tot = jnp.zeros((8, D), F32)
        for s in range(N_CHIP):
            tot = tot + p_ref[s]
        x = c_ref[...]
        sg = jax.nn.sigmoid(x)
        dc_ref[...] = jnp.sum(tot, axis=0, keepdims=True) * (sg * (1.0 + x * (1.0 - sg)))

    return pl.pallas_call(
        body, name="bmod_cctx_grad",
        out_shape=[jax.ShapeDtypeStruct((2, 1, N_MOD * D), F32), jax.ShapeDtypeStruct((1, D), F32)],
    )(dmod_full, dsc_parts, cctx)


def _adam_math(w, g, m, v):
    m = ADAM_B1 * m + (1.0 - ADAM_B1) * g
    v = ADAM_B2 * v + (1.0 - ADAM_B2) * (g * g)
    m_hat = m / (1.0 - ADAM_B1 ** ADAM_STEP)
    v_hat = v / (1.0 - ADAM_B2 ** ADAM_STEP)
    delta = -ADAM_LR * (m_hat / (jnp.sqrt(v_hat) + ADAM_EPS) + ADAM_WD * w)
    return delta, m, v


def _adam(w, g, m, v):
    R, Cc = w.shape
    if R * Cc * 4 <= (1 << 20):
        RB = R
    else:
        RB = 1 << (((1 << 18) // Cc).bit_length() - 1)
        assert R % RB == 0

    def body(w_ref, g_ref, m_ref, v_ref, d_ref, mo_ref, vo_ref):
        d, mn, vn = _adam_math(w_ref[...], g_ref[...], m_ref[...], v_ref[...])
        d_ref[...] = d
        mo_ref[...] = mn
        vo_ref[...] = vn

    spec = pl.BlockSpec((RB, Cc), lambda i: (i, 0))
    return pl.pallas_call(
        body, name="adam", grid=(R // RB,), in_specs=[spec] * 4, out_specs=[spec] * 3,
        out_shape=[jax.ShapeDtypeStruct((R, Cc), F32)] * 3,
        compiler_params=_params(None, ("arbitrary",)),
    )(w, g, m, v)


def _adam_block(w, g, m, v, block, n, prev):
    shape = w.shape
    cols = shape[-1]
    w3, m3, v3 = (t.reshape(n, -1, cols) for t in (w, m, v))
    g2 = g.reshape(-1, cols)
    R = g2.shape[0]
    RB = max(r for r in range(8, (1 << 18) // cols + 1, 8) if R % r == 0)

    def body(w_ref, g_ref, m_ref, v_ref, *refs):
        go_ref, d_ref, mo_ref, vo_ref = refs[-4:]
        gt = g_ref[...]
        d, mn, vn = _adam_math(w_ref[...], gt, m_ref[...], v_ref[...])
        go_ref[...] = gt
        d_ref[...] = d
        mo_ref[...] = mn
        vo_ref[...] = vn

    lay = pl.BlockSpec((None, RB, cols), lambda i: (block, i, 0))
    flat = pl.BlockSpec((RB, cols), lambda i: (i, 0))
    hold = [] if prev is None else [t.reshape(n, -1, cols) for t in prev]
    outs = pl.pallas_call(
        body, name="adam_block", grid=(R // RB,),
        in_specs=[lay, flat, lay, lay] + [pl.BlockSpec(memory_space=pl.ANY)] * len(hold), out_specs=[lay] * 4,
        out_shape=[jax.ShapeDtypeStruct(w3.shape, F32)] * 4,
        input_output_aliases={4 + k: k for k in range(len(hold))},
        compiler_params=_params(None, ("arbitrary",)),
    )(w3, g2, m3, v3, *hold)
    return [o.reshape(shape) for o in outs]


def _sum_devices(parts):
    K = parts.shape[1]

    def body(p_ref, o_ref):
        tot = p_ref[0]
        for i in range(1, N_DEV):
            tot = tot + p_ref[i]
        o_ref[...] = tot

    return pl.pallas_call(body, name="sum_devices", out_shape=jax.ShapeDtypeStruct((K, 128), F32))(parts)


def _sum_pieces(own, others):
    R = own.shape[0]
    RB = 96 if R % 96 == 0 else 32

    def body(a_ref, r_ref, o_ref):
        tot = a_ref[...].astype(F32)
        for i in range(N_DEV - 1):
            tot = tot + r_ref[i].astype(F32)
        o_ref[...] = tot

    return pl.pallas_call(
        body, name="sum_pieces", grid=(R // RB,),
        in_specs=[pl.BlockSpec((RB, D), lambda i: (i, 0)), pl.BlockSpec((N_DEV - 1, RB, D), lambda i: (0, i, 0))],
        out_specs=pl.BlockSpec((RB, D), lambda i: (i, 0)),
        out_shape=jax.ShapeDtypeStruct((R, D), F32),
        compiler_params=_params(None, ("arbitrary",)),
    )(own, others)


def _coords():
    return lax.axis_index("x"), lax.axis_index("y"), lax.axis_index("c")


_FLIPS = [(fx, fy, fc) for fx in (0, 1) for fy in (0, 1) for fc in (0, 1)][1:]


def _all_gather_small(buf):
    K = buf.shape[0]

    def body(in_ref, out_ref, send_sems, recv_sems, local_sem):
        x, y, c = _coords()
        me = 4 * x + 2 * y + c
        mine = pltpu.make_async_copy(in_ref, out_ref.at[me], local_sem)
        mine.start()
        sends = []
        for k, (fx, fy, fc) in enumerate(_FLIPS):
            peer = (x ^ fx, y ^ fy, c ^ fc)
            cp = pltpu.make_async_remote_copy(src_ref=in_ref, dst_ref=out_ref.at[me], send_sem=send_sems.at[k],
                                              recv_sem=recv_sems.at[k], device_id=peer, device_id_type=MESH)
            cp.start()
            sends.append(cp)
        for k, (fx, fy, fc) in enumerate(_FLIPS):
            src = 4 * (x ^ fx) + 2 * (y ^ fy) + (c ^ fc)
            pltpu.make_async_remote_copy(src_ref=in_ref, dst_ref=out_ref.at[src], send_sem=send_sems.at[k],
                                         recv_sem=recv_sems.at[k], device_id=(x, y, c), device_id_type=MESH).wait_recv()
        for cp in sends:
            cp.wait_send()
        mine.wait()

    return pl.pallas_call(
        body, name="all_gather_small",
        in_specs=[pl.BlockSpec(memory_space=pltpu.VMEM)], out_specs=pl.BlockSpec(memory_space=pltpu.VMEM),
        out_shape=jax.ShapeDtypeStruct((N_DEV, K, 128), F32),
        scratch_shapes=[pltpu.SemaphoreType.DMA((7,)), pltpu.SemaphoreType.DMA((7,)), pltpu.SemaphoreType.DMA],
        compiler_params=_params(VMEM_BIG),
    )(buf)


_CHIP_FLIPS = [(1, 0), (0, 1), (1, 1)]


def _exchange_rows(buf, B):
    K = buf.shape[1]
    nex = N_DEV * B

    def body(in_ref, out_ref, send_sems, recv_sems, local_sems):
        x, y, c = _coords()
        s_me = 2 * x + y

        def parts(dev, chip_slot):
            return ((in_ref.at[pl.ds(B * dev, B)], out_ref.at[chip_slot, pl.ds(0, B)]),
                    (in_ref.at[pl.ds(nex, 1)], out_ref.at[chip_slot, pl.ds(B, 1)]))

        locals_ = [pltpu.make_async_copy(src, dst, local_sems.at[i]) for i, (src, dst) in enumerate(parts(4 * x + 2 * y + c, s_me))]
        for cp in locals_:
            cp.start()
        sends = []
        for j, (fx, fy) in enumerate(_CHIP_FLIPS):
            px, py = x ^ fx, y ^ fy
            for i, (src, dst) in enumerate(parts(4 * px + 2 * py + c, s_me)):
                cp = pltpu.make_async_remote_copy(src_ref=src, dst_ref=dst, send_sem=send_sems.at[2 * j + i],
                                                  recv_sem=recv_sems.at[2 * j + i], device_id=(px, py, c), device_id_type=MESH)
                cp.start()
                sends.append(cp)
        for j, (fx, fy) in enumerate(_CHIP_FLIPS):
            for i, (src, dst) in enumerate(parts(0, 2 * (x ^ fx) + (y ^ fy))):
                pltpu.make_async_remote_copy(src_ref=src, dst_ref=dst, send_sem=send_sems.at[2 * j + i],
                                             recv_sem=recv_sems.at[2 * j + i], device_id=(x, y, c), device_id_type=MESH).wait_recv()
        for cp in sends:
            cp.wait_send()
        for cp in locals_:
            cp.wait()

    return pl.pallas_call(
        body, name="exchange_rows",
        in_specs=[pl.BlockSpec(memory_space=pltpu.VMEM)], out_specs=pl.BlockSpec(memory_space=pltpu.VMEM),
        out_shape=jax.ShapeDtypeStruct((N_CHIP, B + 1, K, 128), F32),
        scratch_shapes=[pltpu.SemaphoreType.DMA((6,)), pltpu.SemaphoreType.DMA((6,)), pltpu.SemaphoreType.DMA((2,))],
    )(buf)


_HBM = pl.BlockSpec(memory_space=pltpu.HBM)
_SEMS = pl.BlockSpec(memory_space=pltpu.SEMAPHORE)
_EFFECT = pltpu.SideEffectType.DATAFLOW_SIDE_EFFECTING


def _in_hbm(v):
    return pltpu.with_memory_space_constraint(v, pltpu.HBM)


def _copies_start(name, srcs, lands, n_sems, issue, after):
    ns, nl = len(srcs), len(lands)

    def body(*refs):
        src_refs, land_refs = refs[:ns], refs[ns:ns + nl]
        out = refs[ns + nl + 1:]
        issue(src_refs, land_refs, out[:nl], out[nl:2 * nl])
        out[-1][...] = jnp.zeros((8, 128), F32)

    outs = pl.pallas_call(
        body, name=name, in_specs=[_HBM] * (ns + nl) + [pl.BlockSpec(memory_space=pl.ANY)],
        out_specs=[_SEMS] * (2 * nl) + [_HBM] * (ns + nl) + [pl.BlockSpec(memory_space=pltpu.VMEM)],
        out_shape=[pltpu.SemaphoreType.DMA((n_sems,))] * (2 * nl) + [pltpu.HBM(v.shape, v.dtype) for v in (*srcs, *lands)]
        + [jax.ShapeDtypeStruct((8, 128), F32)],
        input_output_aliases={i: 2 * nl + i for i in range(ns + nl)},
        compiler_params=pltpu.CompilerParams(has_side_effects=_EFFECT),
    )(*[_in_hbm(v) for v in (*srcs, *lands)], after)
    return outs[:nl], outs[nl:2 * nl], outs[2 * nl:2 * nl + ns], outs[2 * nl + ns:2 * nl + ns + nl], outs[-1]


def _copies_wait(name, srcs, lands, send_sems, recv_sems, finish, after):
    after = list(after) if isinstance(after, (list, tuple)) else [after]
    ns, nl = len(srcs), len(lands)

    def body(*refs):
        src_refs, land_refs = refs[:ns], refs[ns:ns + nl]
        finish(src_refs, land_refs, refs[ns + nl:ns + 2 * nl], refs[ns + 2 * nl:ns + 3 * nl])

    outs = pl.pallas_call(
        body, name=name, in_specs=[_HBM] * (ns + nl) + [_SEMS] * (2 * nl) + [pl.BlockSpec(memory_space=pl.ANY)] * len(after),
        out_specs=[_HBM] * (ns + nl), out_shape=[pltpu.HBM(v.shape, v.dtype) for v in (*srcs, *lands)],
        input_output_aliases={i: i for i in range(ns + nl)},
        compiler_params=pltpu.CompilerParams(has_side_effects=_EFFECT),
    )(*srcs, *lands, *send_sems, *recv_sems, *after)
    return outs[:ns], outs[ns:]


def _own_slab(land, mine, index):
    return lax.dynamic_update_slice_in_dim(land, mine[:, None], index, axis=1)


def _gather_start(tag, units, after):
    x, y, c = _coords()
    chip = 2 * x + y
    lands = [_own_slab(lax.empty((u.shape[0], N_CHIP) + u.shape[1:], u.dtype), u, chip) for u in units]

    def issue(src_refs, land_refs, send_sems, recv_sems):
        x, y, c = _coords()
        s_me = 2 * x + y
        for i, (src, land) in enumerate(zip(src_refs, land_refs)):
            for j, (fx, fy) in enumerate(_CHIP_FLIPS):
                pltpu.make_async_remote_copy(src_ref=src, dst_ref=land.at[:, s_me], send_sem=send_sems[i].at[j],
                                             recv_sem=recv_sems[i].at[j], device_id=(x ^ fx, y ^ fy, c),
                                             device_id_type=MESH).start()

    return _copies_start("gather_start_" + tag, units, lands, 3, issue, after)


def _gather_wait(tag, started, which, after):
    send_sems, recv_sems, srcs, lands, _ = started

    def finish(src_refs, land_refs, ssems, rsems):
        x, y, c = _coords()
        for src, land, ss, rs in zip(src_refs, land_refs, ssems, rsems):
            for j in range(3):
                cp = pltpu.make_async_remote_copy(src_ref=src, dst_ref=land.at[:, 0], send_sem=ss.at[j], recv_sem=rs.at[j],
                                                  device_id=(x, y, c), device_id_type=MESH)
                cp.wait_send()
                cp.wait_recv()

    _, done = _copies_wait("gather_wait_" + tag, [srcs[i] for i in which], [lands[i] for i in which],
                           [send_sems[i] for i in which], [recv_sems[i] for i in which], finish, after)
    return [d.reshape(d.shape[0], N_CHIP * d.shape[2], D) for d in done]


def _piece_rows(grads):
    return [g.shape[2] for g in grads]


def _scatter_start(tag, grads, after):
    rows = _piece_rows(grads)
    land = lax.empty((N_DEV - 1, sum(rows), D), grads[0].dtype)

    def issue(src_refs, land_refs, send_sems, recv_sems):
        x, y, c = _coords()
        for k, (fx, fy, fc) in enumerate(_FLIPS):
            px, py, pc = x ^ fx, y ^ fy, c ^ fc
            off = 0
            for src, n in zip(src_refs, rows):
                pltpu.make_async_remote_copy(src_ref=src.at[2 * px + py, pc], dst_ref=land_refs[0].at[k, pl.ds(off, n)],
                                             send_sem=send_sems[0].at[k], recv_sem=recv_sems[0].at[k],
                                             device_id=(px, py, pc), device_id_type=MESH).start()
                off += n

    return _copies_start("scatter_start_" + tag, grads, [land], N_DEV - 1, issue, after)


def _scatter_wait(tag, started, after):
    send_sems, recv_sems, srcs, lands, _ = started

    def finish(src_refs, land_refs, ssems, rsems):
        x, y, c = _coords()
        for k in range(N_DEV - 1):
            cp = pltpu.make_async_remote_copy(src_ref=land_refs[0].at[0], dst_ref=land_refs[0].at[0], send_sem=ssems[0].at[k],
                                              recv_sem=rsems[0].at[k], device_id=(x, y, c), device_id_type=MESH)
            cp.wait_send()
            cp.wait_recv()

    grads, (others,) = _copies_wait("scatter_wait_" + tag, srcs, lands, send_sems, recv_sems, finish, after)
    return grads, others


def _swap_start(tag, mine, rows, after):
    x, y, c = _coords()
    offs = [sum(rows[:t]) for t in range(len(rows))]
    lands = [lax.dynamic_update_slice_in_dim(lax.empty((2, n, D), mine.dtype), mine[o:o + n][None], c, axis=0)
             for o, n in zip(offs, rows)]

    def issue(src_refs, land_refs, send_sems, recv_sems):
        x, y, c = _coords()
        for t, (o, n) in enumerate(zip(offs, rows)):
            pltpu.make_async_remote_copy(src_ref=src_refs[0].at[pl.ds(o, n)], dst_ref=land_refs[t].at[c],
                                         send_sem=send_sems[t].at[0], recv_sem=recv_sems[t].at[0],
                                         device_id=(x, y, 1 - c), device_id_type=MESH).start()

    return _copies_start("swap_start_" + tag, [mine], lands, 1, issue, after)


def _swap_wait(tag, started, after):
    send_sems, recv_sems, srcs, lands, _ = started

    def finish(src_refs, land_refs, ssems, rsems):
        x, y, c = _coords()
        for land, ss, rs in zip(land_refs, ssems, rsems):
            cp = pltpu.make_async_remote_copy(src_ref=land.at[0], dst_ref=land.at[0], send_sem=ss.at[0], recv_sem=rs.at[0],
                                              device_id=(x, y, c), device_id_type=MESH)
            cp.wait_send()
            cp.wait_recv()

    return _copies_wait("swap_wait_" + tag, srcs, lands, send_sems, recv_sems, finish, after)[1]


def _size(shape):
    n = 1
    for d in shape:
        n *= d
    return n


def _pack(arrays):
    return jnp.concatenate([jnp.pad(a.reshape(-1).astype(F32), (0, (-a.size) % 1024)).reshape(-1, 128) for a in arrays], axis=0)


def _unpack(buf, shapes):
    out, row = [], 0
    for s in shapes:
        n = _size(s)
        nrows = 8 * -(-n // 1024)
        out.append(buf[row:row + nrows].reshape(-1)[:n].reshape(s))
        row += nrows
    return out


def _block_diag(pw):
    bd = jnp.zeros((PW, PW), F32)
    g = PW // 4
    for i in range(4):
        bd = bd.at[i * g:(i + 1) * g, i * g:(i + 1) * g].set(pw[i])
    return bd


def _lanes(v):
    return jnp.broadcast_to(v.reshape(NH, 1, 1), (NH, 1, HD))


def _layer_fwd(z, mod, normg, wget, small):
    S = z.shape[1]
    cos, sin = _rope_tables(S)
    fwd_sched, _ = _schedules(S)
    wa = wget("a", z)
    z1, f_a = _ffn_fwd(z, mod[:, :, 0], normg[0], *wa)
    wm = wget("m", z1)
    pp, q, k, v, gg, pc = _mix_in_fwd(z1, mod[:, :, 1], normg[1], wm[0], cos, sin)
    po, co = _pool_conv_fwd(pp, pc, small["bd"], small["pscale"], small["dw"], small["db"])
    ro = (small["gng"], small["lng"], small["lnb"])
    oa, ob = _retention(q, k, v, small["dec_f"], small["dec_b"], *fwd_sched)
    z2, out = _mix_out_fwd(z1, po, oa, ob, gg, co, ro, mod[:, :, 1], wm[1])
    wb = wget("b", z2)
    z3, f_b = _ffn_fwd(z2, mod[:, :, 2], normg[2], *wb)
    saved = dict(z=z, f_a=f_a, z1=z1, pp=pp, q=q, k=k, v=v, gg=gg, pc=pc, po=po, co=co, oa=oa, ob=ob, out=out, z2=z2, f_b=f_b,
                 wa=wa, wm=wm, wb=wb)
    return z3, saved


def _layer_bwd(dz3, sv, mod, normg, small, emit, tok, last):
    S = dz3.shape[1]
    B = dz3.shape[0]
    T = B * S
    cos, sin = _rope_tables(S)
    fwd_sched, bwd_sched = _schedules(S)
    wa, wm, wb = sv["wa"], sv["wm"], sv["wb"]
    dz2, dmod_b, dg_b, gw1t_b, gw3t_b, gw2_b = _ffn_bwd(sv["z2"], dz3, sv["f_b"], mod[:, :, 2] + tok, normg[2], *wb)
    tok = emit("b", [gw1t_b, gw3t_b, gw2_b])
    mod_m = mod[:, :, 1] + tok
    ro = (small["gng"], small["lng"], small["lnb"])
    dpo, do, dgg, dco, cat, dout, dmod_gate, dgng, dlng, dlnb = _mix_out_bwd(
        dz2, sv["out"], sv["po"], sv["oa"], sv["ob"], sv["gg"], sv["co"], ro, mod_m, wm[1])
    gwout = _tn_matmul(cat.reshape(T, D), dout.reshape(T, D))
    dqa, dqb = _retention(do, sv["v"], sv["k"], small["dec_f"], small["dec_b"], *fwd_sched)
    dka, dkb = _retention(sv["v"], do, sv["q"], small["dec_f"], small["dec_b"], *bwd_sched)
    dva, dvb = _retention(sv["k"], sv["q"], do, small["dec_f"], small["dec_b"], *bwd_sched)
    ddec = _retention_ddecay(sv["q"], sv["k"], sv["v"], do, small["dec_f"], small["dec_b"], *fwd_sched)
    dpp, dpc, dbd, dps, ddw, ddb = _pool_conv_bwd(sv["pp"], sv["pc"], dpo, dco, small["bd"], small["pscale"],
                                                   small["dw"], small["db"])
    dz1, h, dp, dmod_m, dg_m = _mix_in_bwd(sv["z1"], dz2, dpp, dqa, dqb, dka, dkb, dva, dvb, dgg, dpc, mod_m, normg[1],
                                           wm[0], cos, sin)
    gwint = _tn_matmul(dp.reshape(T, F), h.reshape(T, D))
    tok = emit("m", [gwint, gwout])
    if last:
        dz, dmod_a, dg_a = _ffn_bwd(sv["z"], dz1, sv["f_a"], mod[:, :, 0] + tok, normg[0], *wa,
                                    each=lambda grad: emit("a", [grad]))[:3]
    else:
        dz, dmod_a, dg_a, gw1t_a, gw3t_a, gw2_a = _ffn_bwd(sv["z"], dz1, sv["f_a"], mod[:, :, 0] + tok, normg[0], *wa)
        tok = emit("a", [gw1t_a, gw3t_a, gw2_a])
    dmod = jnp.stack([dmod_a, dmod_m + dmod_gate, dmod_b], axis=2)
    dnormg = jnp.stack([dg_a, dg_m, dg_b], axis=0)
    g = PW // 4
    dpool_w = jnp.stack([dbd[i * g:(i + 1) * g, i * g:(i + 1) * g] for i in range(4)], axis=0)
    sm = dict(pool_w=dpool_w, pool_scale=dps[0], dec_f=ddec[:, 0, 0], dec_b=ddec[:, 1, 0], gng=dgng[0],
              conv_dw=ddw[0:CONV_K], conv_b=ddb[0], conv_ln_g=dlng[0], conv_ln_b=dlnb[0])
    return dz, dmod, dnormg, sm, tok


def _small_params(pool_w, pool_scale, dec_f, dec_b, gng, conv_dw, conv_b, lng, lnb):
    return dict(bd=_block_diag(pool_w), pscale=pool_scale.reshape(1, PW), dec_f=_lanes(dec_f), dec_b=_lanes(dec_b),
                gng=gng.reshape(1, RW), dw=jnp.pad(conv_dw, ((0, 1), (0, 0))), db=conv_b.reshape(1, PW),
                lng=lng.reshape(1, PW), lnb=lnb.reshape(1, PW))


_WEIGHTS = ["c_ctx", "w_mod", "b_mod", "norm_g", "ffn_w1", "ffn_w3", "ffn_w2", "w_in", "w_out", "pool_w", "pool_scale",
            "ret_decay_fwd", "ret_decay_bwd", "ret_gn_g", "conv_dw", "conv_b", "conv_ln_g", "conv_ln_b", "final_g"]
_BIG = ["w_mod", "ffn_w1", "ffn_w3", "ffn_w2", "w_in", "w_out"]
_SMALL = [n for n in _WEIGHTS if n not in _BIG]


def _adam_any(w, g, m, v):
    shape = w.shape
    cols = shape[-1] if w.ndim >= 2 else 128
    outs = _adam(w.reshape(-1, cols), g.reshape(-1, cols), m.reshape(-1, cols), v.reshape(-1, cols))
    return [o.reshape(shape) for o in outs]


def _step(a):
    x, c, ctx = a["x"], a["c"], a["ctx"]
    B = x.shape[0]
    nex = N_DEV * B
    assert nex + B <= MROWS and ctx.shape[1] == LC and x.shape[1] % TM == 0
    xi, yi, ci = _coords()
    me = 4 * xi + 2 * yi + ci
    chip = 2 * xi + yi
    ncol = a["w_mod"].shape[2]

    def ffn_unit(l, i, tok):
        return jnp.stack([(jnp.swapaxes(a["ffn_w1"][l, i], -1, -2) + tok).astype(BF16),
                          (jnp.swapaxes(a["ffn_w3"][l, i], -1, -2) + tok).astype(BF16), (a["ffn_w2"][l, i] + tok).astype(BF16)])

    def later_units(tok):
        units = []
        for l in range(2):
            units += [ffn_unit(l, 0, tok), (jnp.swapaxes(a["w_in"][l], -1, -2) + tok).astype(BF16)[None],
                      (a["w_out"][l] + tok).astype(BF16)[None], ffn_unit(l, 1, tok)]
        return units[1:]

    shapes1 = [(B, D), (2, 3, D // N_CHIP), (2, CONV_K, PW // N_CHIP)]
    g1 = _all_gather_small(_pack([c, a["norm_g"], a["conv_dw"]]))
    per = [_unpack(g1[d], shapes1) for d in range(N_DEV)]
    c_all = jnp.concatenate([per[d][0] for d in range(N_DEV)], axis=0)
    norm_g_full = jnp.concatenate([per[2 * s][1] for s in range(N_CHIP)], axis=-1)
    conv_dw_full = jnp.concatenate([per[2 * s][2] for s in range(N_CHIP)], axis=-1)
    cctx = a["c_ctx"].reshape(1, D)
    c24 = jnp.concatenate([c_all] + [cctx] * B + [jnp.zeros((MROWS - nex - B, D), F32)], axis=0)

    bsh = lax.dynamic_slice(a["b_mod"], (0, chip * ncol), (2, ncol)).reshape(2, 1, ncol)
    mod_raw = _mod_fwd(c24, a["w_mod"], bsh)
    mine = _exchange_rows(jnp.swapaxes(mod_raw, 0, 1).reshape(MROWS, 2 * ncol // 128, 128), B)
    mod_mine = jnp.concatenate([mine[s].reshape(B + 1, 2, ncol) for s in range(N_CHIP)], axis=-1)
    mods = []
    for l in range(2):
        cx = jnp.broadcast_to(mod_mine[B, l][None], (B, N_MOD * D))
        mods.append(jnp.stack([cx, mod_mine[:B, l]], axis=1).reshape(B, 2, 3, 3, D))

    first = _gather_start("first", [ffn_unit(0, 0, jnp.zeros((), F32))], mod_mine)
    rest = _gather_start("rest", later_units(first[4][0, 0]), first[4])
    started = tuple(x + y for x, y in zip(first[:4], rest[:4])) + (rest[4],)

    def wget_of(l):
        def wget(stage, after):
            if stage == "m":
                win, wo = _gather_wait(f"m{l}", started, [4 * l + 1, 4 * l + 2], after)
                return (win, 0), (wo, 0)
            (g,) = _gather_wait(f"{stage}{l}", started, [4 * l + (0 if stage == "a" else 3)], after)
            return (g, 0), (g, 1), (g, 2)
        return wget

    smalls = [_small_params(a["pool_w"][l], a["pool_scale"][l], a["ret_decay_fwd"][l], a["ret_decay_bwd"][l],
                            a["ret_gn_g"][l], conv_dw_full[l], a["conv_b"][l], a["conv_ln_g"][l], a["conv_ln_b"][l])
              for l in range(2)]
    normgs = [norm_g_full[l].reshape(3, 1, D) for l in range(2)]
    z = jnp.concatenate([ctx, x], axis=1)
    saved = []
    for l in range(2):
        z, sv = _layer_fwd(z, mods[l], normgs[l], wget_of(l), smalls[l])
        saved.append(sv)
    dz, dfinal_g, loss_part = _head(z, a["loss_target"], a["final_g"].reshape(1, D))

    scattering, swapping, reduced, newest = [], [], {}, []

    def reduce_oldest(after):
        tag, st, rows = scattering.pop(0)
        grads, others = _scatter_wait(tag, st, after)
        own = jnp.concatenate([lax.dynamic_slice(g, (chip, ci, 0, 0), (1, 1) + g.shape[2:]).reshape(g.shape[2:])
                               for g in grads], axis=0)
        mine = _sum_pieces(own, others)
        sw = _swap_start(tag, mine, rows, mine)
        if swapping:
            ptag, psw = swapping.pop()
            reduced[ptag] = _swap_wait(ptag, psw, sw[4])
        swapping.append((tag, sw))
        return sw[4]

    def emit_of(l):
        def emit(stage, grads):
            grads = [g.reshape(N_CHIP, 2, g.shape[0] // (2 * N_CHIP), D) for g in grads]
            stem = f"{stage}{l}"
            same = sum(1 for t, _, _ in scattering if t.split("_")[0] == stem)
            tag = stem if same == 0 else f"{stem}_{same}"
            st = _scatter_start(tag, grads, grads[0])
            tok = st[4][0, 0]
            while scattering and scattering[0][0].split("_")[0] != stem:
                tok = tok + reduce_oldest(st[4])[0, 0]
            scattering.append((tag, st, _piece_rows(grads)))
            newest[:] = [st[4]] + (newest if same else [])
            return tok
        return emit

    back = [None, None]
    tok = jnp.zeros((), F32)
    for l in (1, 0):
        dz, dmod, dnormg, sm, tok = _layer_bwd(dz, saved[l], mods[l], normgs[l], smalls[l], emit_of(l), tok, l == 0)
        back[l] = (dmod, dnormg, None, sm)
    grad_x = dz[:, LC:]
    grads = {}

    dmods = [back[l][0].reshape(B, 2, N_MOD * D) for l in range(2)]
    pack_a = _pack([jnp.stack([dm[:, 1] for dm in dmods])])
    ka = pack_a.shape[0]
    sm = [back[l][3] for l in range(2)]
    sum_list = [jnp.stack([dm[:, 0] for dm in dmods]), jnp.stack([back[l][1][:, 0] for l in range(2)])]
    sm_keys = ["pool_w", "pool_scale", "dec_f", "dec_b", "gng", "conv_dw", "conv_b", "conv_ln_g", "conv_ln_b"]
    sum_list += [jnp.stack([sm[l][k] for l in range(2)]) for k in sm_keys]
    sum_list += [dfinal_g[0], loss_part[0, 0:1]]
    sum_shapes = [s.shape for s in sum_list]
    g3 = _all_gather_small(jnp.concatenate([pack_a, _pack(sum_list)], axis=0))
    dmx_all = jnp.concatenate([_unpack(g3[d, :ka], [(2, B, N_MOD * D)])[0] for d in range(N_DEV)], axis=1)
    summed = _unpack(_sum_devices(g3[:, ka:]), sum_shapes)
    dmy, dnorm_full = summed[0], summed[1]
    sgrad = dict(zip(sm_keys, summed[2:2 + len(sm_keys)]))
    loss = summed[-1].reshape(())

    dmod24 = jnp.concatenate([dmx_all, dmy, jnp.zeros((2, MROWS - nex - B, N_MOD * D), F32)], axis=1)
    dmod_my = lax.dynamic_slice(dmod24, (0, 0, chip * ncol), (2, MROWS, ncol))
    grads["w_mod"], dsc = _mod_bwd(c24, dmod_my, a["w_mod"])
    g4 = _all_gather_small(_pack([dsc[nex:nex + 8]]))
    dsc_parts = jnp.stack([_unpack(g4[2 * s], [(8, D)])[0] for s in range(N_CHIP)])
    dbmod, dcctx = _bmod_cctx_grad(dmod24, dsc_parts, cctx)

    grads["c_ctx"] = dcctx[0]
    grads["b_mod"] = dbmod.reshape(2, N_MOD * D)
    grads["norm_g"] = lax.dynamic_slice(dnorm_full, (0, 0, chip * (D // N_CHIP)), (2, 3, D // N_CHIP))
    grads["pool_w"] = sgrad["pool_w"]
    grads["pool_scale"] = sgrad["pool_scale"]
    grads["ret_decay_fwd"] = sgrad["dec_f"]
    grads["ret_decay_bwd"] = sgrad["dec_b"]
    grads["ret_gn_g"] = sgrad["gng"]
    grads["conv_dw"] = lax.dynamic_slice(sgrad["conv_dw"], (0, 0, chip * (PW // N_CHIP)), (2, CONV_K, PW // N_CHIP))
    grads["conv_b"] = sgrad["conv_b"]
    grads["conv_ln_g"] = sgrad["conv_ln_g"]
    grads["conv_ln_b"] = sgrad["conv_ln_b"]
    grads["final_g"] = summed[-2]

    delta, new_m, new_v = {}, {}, {}
    delta["w_mod"], new_m["w_mod"], new_v["w_mod"] = _adam_any(a["w_mod"], grads["w_mod"], a["m_w_mod"], a["v_w_mod"])
    shapes_s = [a[n].shape for n in _SMALL]
    packed = _adam(_pack([a[n] for n in _SMALL]), _pack([grads[n] for n in _SMALL]),
                   _pack([a["m_" + n] for n in _SMALL]), _pack([a["v_" + n] for n in _SMALL]))
    for res, out in zip(packed, (delta, new_m, new_v)):
        for n, val in zip(_SMALL, _unpack(res, shapes_s)):
            out[n] = val

    def group(stem):
        out, k = list(reduced[stem]), 1
        while f"{stem}_{k}" in reduced:
            out += reduced[f"{stem}_{k}"]
            k += 1
        return [h.reshape(-1, D) for h in out]

    turned = ("ffn_w1", "ffn_w3", "w_in")
    done = {n: None for n in _BIG[1:]}

    def adam_of(n, block, nblocks, g):
        t = (lambda u: jnp.swapaxes(u, -1, -2)) if n in turned else (lambda u: u)
        done[n] = _adam_block(t(a[n]), g, t(a["m_" + n]), t(a["v_" + n]), block, nblocks, done[n])

    def adam_stage(stage, l):
        mats = group(f"{stage}{l}")
        if stage == "m":
            adam_of("w_in", l, 2, mats[0])
            adam_of("w_out", l, 2, mats[1])
        else:
            for n, g in zip(("ffn_w1", "ffn_w3", "ffn_w2"), mats):
                adam_of(n, 2 * l + (stage == "b"), 4, g)

    for stage, l in (("b", 1), ("m", 1), ("a", 1), ("b", 0)):
        adam_stage(stage, l)

    last = [delta["w_mod"], packed[0]] + [done[n][1] for n in _BIG[1:]] + newest
    while scattering:
        last = reduce_oldest(last)
    tag, sw = swapping.pop()
    reduced[tag] = _swap_wait(tag, sw, last)
    adam_stage("m", 0)
    adam_stage("a", 0)
    for n in _BIG[1:]:
        grads[n], delta[n], new_m[n], new_v[n] = [jnp.swapaxes(r, -1, -2) for r in done[n]] if n in turned else done[n]
    return (loss, grad_x, *[grads[n] for n in _WEIGHTS], *[delta[n] for n in _WEIGHTS],
            *[new_m[n] for n in _WEIGHTS], *[new_v[n] for n in _WEIGHTS])


def kernel(x, c, ctx, c_ctx, w_mod, b_mod, norm_g, ffn_w1, ffn_w3, ffn_w2, w_in, w_out, pool_w, pool_scale, ret_decay_fwd, ret_decay_bwd, ret_gn_g, conv_dw, conv_b, conv_ln_g, conv_ln_b, final_g, loss_target, m_c_ctx, m_w_mod, m_b_mod, m_norm_g, m_ffn_w1, m_ffn_w3, m_ffn_w2, m_w_in, m_w_out, m_pool_w, m_pool_scale, m_ret_decay_fwd, m_ret_decay_bwd, m_ret_gn_g, m_conv_dw, m_conv_b, m_conv_ln_g, m_conv_ln_b, m_final_g, v_c_ctx, v_w_mod, v_b_mod, v_norm_g, v_ffn_w1, v_ffn_w3, v_ffn_w2, v_w_in, v_w_out, v_pool_w, v_pool_scale, v_ret_decay_fwd, v_ret_decay_bwd, v_ret_gn_g, v_conv_dw, v_conv_b, v_conv_ln_g, v_conv_ln_b, v_final_g):
    return _step(dict(locals()))
```

```python
import functools

import jax
import jax.numpy as jnp
from jax import lax
from jax.experimental import pallas as pl
from jax.experimental.pallas import tpu as pltpu

F32 = jnp.float32
BF16 = jnp.bfloat16

D = 1024
F = 2816
FH = 1408
N_MOD = 9
LC = 256
TM = 256
HD = 128
NH = 4
RW = 512
PW = 256
CONV_K = 31
GRID_W = 64
EPS = 1e-6
K_SCALE = HD ** -0.5
N_DEV = 8
N_CHIP = 4
SLAB = F // N_CHIP
HSLAB = SLAB // 2
OSLAB = D // N_CHIP
HOSLAB = OSLAB // 2
VMEM_BIG = 60 * 1024 * 1024
MESH = pl.DeviceIdType.MESH

ADAM_LR = 0.001
ADAM_B1 = 0.9
ADAM_B2 = 0.999
ADAM_EPS = 1e-08
ADAM_WD = 0.01
ADAM_STEP = 10


def _nt(a, b):
    return lax.dot_general(a, b, (((1,), (1,)), ((), ())), preferred_element_type=F32)


def _nn(a, b):
    return lax.dot_general(a, b, (((1,), (0,)), ((), ())), preferred_element_type=F32)


def _tn(a, b):
    return lax.dot_general(a, b, (((0,), (0,)), ((), ())), preferred_element_type=F32)


def _params(vmem=None, sem=None):
    return pltpu.CompilerParams(dimension_semantics=sem, vmem_limit_bytes=vmem)


def _rms_mod(z, g, shift, scale):
    y = z * lax.rsqrt(jnp.mean(z * z, axis=-1, keepdims=True) + EPS)
    return (y * g) * (1.0 + scale) + shift


def _acc(ref, val, first):
    @pl.when(first)
    def _():
        ref[...] = val

    @pl.when(jnp.logical_not(first))
    def _():
        ref[...] += val


def _tok(width):
    return pl.BlockSpec((None, TM, width), lambda b, t: (b, t, 0))


def _modspec():
    return pl.BlockSpec((None, None, 3, D), lambda b, t: (b, jnp.minimum(t, 1), 0, 0))


def _const(shape):
    nd = len(shape)
    return pl.BlockSpec(shape, lambda b, t: (0,) * nd)


def _wspec(w):
    stack, idx = w
    return pl.BlockSpec((None,) + stack.shape[1:], lambda b, t: (idx, 0, 0), pipeline_mode=pl.Buffered(1))


def _ffn_fwd(z, mod, g, w1t, w3t, w2):
    B, S, _ = z.shape

    def body(z_ref, mod_ref, g_ref, w1_ref, w3_ref, w2_ref, zo_ref, f_ref):
        zt = z_ref[...]
        h = _rms_mod(zt, g_ref[...], mod_ref[0:1, :], mod_ref[1:2, :]).astype(BF16)
        f = jnp.zeros((TM, D), F32)
        for c in range(F // FH):
            rows = slice(c * FH, (c + 1) * FH)
            u1 = _nt(h, w1_ref[rows, :])
            u3 = _nt(h, w3_ref[rows, :])
            a = (u1 * jax.nn.sigmoid(u1) * u3).astype(BF16)
            f = f + _nn(a, w2_ref[rows, :])
        f_ref[...] = f
        zo_ref[...] = zt + 0.5 * mod_ref[2:3, :] * f

    return pl.pallas_call(
        body, name="ffn_fwd", grid=(B, S // TM),
        in_specs=[_tok(D), _modspec(), _const((1, D)), _wspec(w1t), _wspec(w3t), _wspec(w2)],
        out_specs=[_tok(D), _tok(D)],
        out_shape=[jax.ShapeDtypeStruct((B, S, D), F32)] * 2,
        compiler_params=_params(VMEM_BIG, ("arbitrary", "arbitrary")),
    )(z, mod, g, w1t[0], w3t[0], w2[0])


def _ffn_bwd(z, dzo, f, mod, g, w1t, w3t, w2, each=None):
    B, S, _ = z.shape

    def body(z_ref, dzo_ref, f_ref, mod_ref, g_ref, w1_ref, w3_ref, w2_ref,
             dz_ref, h_ref, du1_ref, du3_ref, a_ref, do_ref, dmod_ref, dg_ref):
        b, t = pl.program_id(0), pl.program_id(1)
        zt = z_ref[...]
        dzo = dzo_ref[...]
        gate = mod_ref[2:3, :]
        h32, vjp_h = jax.vjp(_rms_mod, zt, g_ref[...], mod_ref[0:1, :], mod_ref[1:2, :])
        h = h32.astype(BF16)
        h_ref[...] = h
        do = (0.5 * gate * dzo).astype(BF16)
        do_ref[...] = do
        dgate = jnp.sum(0.5 * f_ref[...] * dzo, axis=0, keepdims=True)
        dh = jnp.zeros((TM, D), F32)
        for c in range(F // FH):
            rows = slice(c * FH, (c + 1) * FH)
            u1 = _nt(h, w1_ref[rows, :])
            u3 = _nt(h, w3_ref[rows, :])
            sg = jax.nn.sigmoid(u1)
            s = u1 * sg
            a_ref[:, rows] = (s * u3).astype(BF16)
            da = _nt(do, w2_ref[rows, :])
            du3 = (da * s).astype(BF16)
            du1 = (da * u3 * (sg * (1.0 + u1 * (1.0 - sg)))).astype(BF16)
            du1_ref[:, rows] = du1
            du3_ref[:, rows] = du3
            dh = dh + _nn(du1, w1_ref[rows, :]) + _nn(du3, w3_ref[rows, :])
        dz_h, dg, dshift, dscale = vjp_h(dh)
        dz_ref[...] = dzo + dz_h
        _acc(dmod_ref, jnp.concatenate([dshift, dscale, dgate], axis=0), t <= 1)
        _acc(dg_ref, dg, jnp.logical_and(b == 0, t == 0))

    T = B * S
    outs = pl.pallas_call(
        body, name="ffn_bwd", grid=(B, S // TM),
        in_specs=[_tok(D), _tok(D), _tok(D), _modspec(), _const((1, D)), _wspec(w1t), _wspec(w3t), _wspec(w2)],
        out_specs=[_tok(D), _tok(D), _tok(F), _tok(F), _tok(F), _tok(D), _modspec(), _const((1, D))],
        out_shape=[jax.ShapeDtypeStruct((B, S, D), F32), jax.ShapeDtypeStruct((B, S, D), BF16),
                   jax.ShapeDtypeStruct((B, S, F), BF16), jax.ShapeDtypeStruct((B, S, F), BF16),
                   jax.ShapeDtypeStruct((B, S, F), BF16), jax.ShapeDtypeStruct((B, S, D), BF16),
                   jax.ShapeDtypeStruct((B, 2, 3, D), F32), jax.ShapeDtypeStruct((1, D), F32)],
        compiler_params=_params(VMEM_BIG, ("arbitrary", "arbitrary")),
    )(z, dzo, f, mod, g, w1t[0], w3t[0], w2[0])
    dz, h, du1, du3, a, do, dmod, dg = outs
    grads = []
    for lhs, rhs in ((du1, h), (du3, h), (a, do)):
        grads.append(_tn_matmul(lhs.reshape(T, F), rhs.reshape(T, D)))
        if each is not None:
            each(grads[-1])
    return (dz, dmod, dg, *grads)


def _tn_matmul(a, b):
    T, M = a.shape
    N = b.shape[1]
    MB = M
    TT = next(t for t in (1152, 1024, 768, 512, TM) if T % t == 0)
    nt = T // TT

    def body(a_ref, b_ref, o_ref, acc_ref):
        t = pl.program_id(1)
        prod = _tn(a_ref[...], b_ref[...])
        _acc(acc_ref, prod, t == 0)

        @pl.when(t == nt - 1)
        def _():
            o_ref[...] = acc_ref[...].astype(BF16)

    return pl.pallas_call(
        body, name="tn_matmul", grid=(M // MB, nt),
        in_specs=[pl.BlockSpec((TT, MB), lambda i, t: (t, i)), pl.BlockSpec((TT, N), lambda i, t: (t, 0))],
        out_specs=pl.BlockSpec((MB, N), lambda i, t: (i, 0)),
        out_shape=jax.ShapeDtypeStruct((M, N), BF16),
        scratch_shapes=[pltpu.VMEM((MB, N), F32)],
        compiler_params=_params(VMEM_BIG, ("arbitrary", "arbitrary")),
    )(a, b)


def _swap32(x):
    n = x.shape[1]
    lane = lax.broadcasted_iota(jnp.int32, x.shape, 1)
    return jnp.where((lane % 64) < 32, pltpu.roll(x, n - 32, 1), pltpu.roll(x, 32, 1))


def _rope(x, cos, sin):
    return x * cos + _swap32(x) * sin


def _rope_t(dy, cos, sin):
    return dy * cos + _swap32(dy * sin)


def _rope_tables(S):
    L = S - LC
    n_freq = HD // 4
    inv = 10000.0 ** (-jnp.arange(n_freq, dtype=F32) / n_freq)
    i = jnp.arange(L)
    row = (i // GRID_W).astype(F32)
    col = (i % GRID_W).astype(F32)
    ang_r = row[:, None] * inv[None]
    ang_c = col[:, None] * inv[None]
    ang = jnp.concatenate([ang_r, ang_r, ang_c, ang_c], axis=1)
    ang = jnp.concatenate([jnp.zeros((LC, HD), F32), ang], axis=0)
    sign = jnp.where((jnp.arange(HD) % 64) < 32, -1.0, 1.0).astype(F32)
    return jnp.cos(ang), jnp.sin(ang) * sign[None]


def _tabspec():
    return pl.BlockSpec((TM, HD), lambda b, t: (t, 0))


def _mix_in_fwd(z, mod, g, wint, cos, sin):
    B, S, _ = z.shape

    def body(z_ref, mod_ref, g_ref, w_ref, cos_ref, sin_ref, pp_ref, q_ref, k_ref, v_ref, gg_ref, pc_ref):
        h = _rms_mod(z_ref[...], g_ref[...], mod_ref[0:1, :], mod_ref[1:2, :]).astype(BF16)
        p = _nt(h, w_ref[...])
        cos = jnp.tile(cos_ref[...], (1, NH))
        sin = jnp.tile(sin_ref[...], (1, NH))
        pp_ref[...] = p[:, 0:PW]
        q_ref[...] = _rope(p[:, PW:PW + RW], cos, sin)
        k_ref[...] = _rope(p[:, PW + RW:PW + 2 * RW], cos, sin) * K_SCALE
        v_ref[...] = p[:, PW + 2 * RW:PW + 3 * RW]
        gg_ref[...] = p[:, PW + 3 * RW:PW + 4 * RW]
        pc_ref[...] = p[:, PW + 4 * RW:]

    return pl.pallas_call(
        body, name="mix_in_fwd", grid=(B, S // TM),
        in_specs=[_tok(D), _modspec(), _const((1, D)), _wspec(wint), _tabspec(), _tabspec()],
        out_specs=[_tok(PW), _tok(RW), _tok(RW), _tok(RW), _tok(RW), _tok(2 * PW)],
        out_shape=[jax.ShapeDtypeStruct((B, S, PW), F32)] + [jax.ShapeDtypeStruct((B, S, RW), F32)] * 5,
        compiler_params=_params(VMEM_BIG, ("arbitrary", "arbitrary")),
    )(z, mod, g, wint[0], cos, sin)


def _mix_in_bwd(z, dzo, dpp, dqa, dqb, dka, dkb, dva, dvb, dgg, dpc, mod, g, wint, cos, sin):
    B, S, _ = z.shape

    def body(z_ref, dzo_ref, dpp_ref, dqa_ref, dqb_ref, dka_ref, dkb_ref, dva_ref, dvb_ref, dgg_ref, dpc_ref, mod_ref, g_ref,
             w_ref, cos_ref, sin_ref, dz_ref, h_ref, dp_ref, dmod_ref, dg_ref):
        b, t = pl.program_id(0), pl.program_id(1)
        h32, vjp_h = jax.vjp(_rms_mod, z_ref[...], g_ref[...], mod_ref[0:1, :], mod_ref[1:2, :])
        h_ref[...] = h32.astype(BF16)
        cos = jnp.tile(cos_ref[...], (1, NH))
        sin = jnp.tile(sin_ref[...], (1, NH))
        dq = _rope_t(dqa_ref[...] + dqb_ref[...], cos, sin)
        dk = _rope_t(dka_ref[...] + dkb_ref[...], cos, sin) * K_SCALE
        dp = jnp.concatenate([dpp_ref[...], dq, dk, dva_ref[...] + dvb_ref[...], dgg_ref[...], dpc_ref[...]],
                             axis=1).astype(BF16)
        dp_ref[...] = dp
        dh = _nn(dp, w_ref[...])
        dz_h, dg, dshift, dscale = vjp_h(dh)
        dz_ref[...] = dzo_ref[...] + dz_h
        _acc(dmod_ref, jnp.concatenate([dshift, dscale, jnp.zeros_like(dshift)], axis=0), t <= 1)
        _acc(dg_ref, dg, jnp.logical_and(b == 0, t == 0))

    return pl.pallas_call(
        body, name="mix_in_bwd", grid=(B, S // TM),
        in_specs=[_tok(D), _tok(D), _tok(PW)] + [_tok(RW)] * 7 + [_tok(2 * PW), _modspec(), _const((1, D)), _wspec(wint),
                                                                  _tabspec(), _tabspec()],
        out_specs=[_tok(D), _tok(D), _tok(F), _modspec(), _const((1, D))],
        out_shape=[jax.ShapeDtypeStruct((B, S, D), F32), jax.ShapeDtypeStruct((B, S, D), BF16),
                   jax.ShapeDtypeStruct((B, S, F), BF16), jax.ShapeDtypeStruct((B, 2, 3, D), F32),
                   jax.ShapeDtypeStruct((1, D), F32)],
        compiler_params=_params(VMEM_BIG, ("arbitrary", "arbitrary")),
    )(z, dzo, dpp, dqa, dqb, dka, dkb, dva, dvb, dgg, dpc, mod, g, wint[0], cos, sin)


def _log_sigmoid(x):
    return jnp.minimum(x, 0.0) - jnp.log(1.0 + jnp.exp(-jnp.abs(x)))


def _retention(a, b, c, dec_a, dec_b, sched_a, sched_b):
    B, S, _ = a.shape
    C = TM

    def body(a_ref, b_ref, c_ref, da_ref, db_ref, oa_ref, ob_ref):
        ii = lax.broadcasted_iota(jnp.int32, (C, C), 0)
        jj = lax.broadcasted_iota(jnp.int32, (C, C), 1)
        pos = lax.broadcasted_iota(jnp.int32, (C, 1), 0).astype(F32)
        for dec_ref, o_ref, (order, causal, strict) in ((da_ref, oa_ref, sched_a), (db_ref, ob_ref, sched_b)):
            lg = _log_sigmoid(dec_ref[...])
            lg1 = lg[:, 0:1]
            dist = ((ii - jj) if causal else (jj - ii)).astype(F32)
            mask = (dist > 0.0) if strict else (dist >= 0.0)
            decay = jnp.where(mask, jnp.exp(jnp.maximum(dist, 0.0) * lg1), 0.0)
            p = pos if causal else (C - 1.0 - pos)
            w_q = jnp.exp((p + 1.0) * lg1)
            w_k = jnp.exp((C - 1.0 - p) * lg1)
            chunk_decay = jnp.exp(C * lg)
            state = jnp.zeros((HD, HD), F32)
            for n in order:
                rows = pl.ds(n * C, C)
                at, bt, ct = a_ref[rows, :], b_ref[rows, :], c_ref[rows, :]
                cb = ct.astype(BF16)
                scores = _nt(at.astype(BF16), bt.astype(BF16)) * decay
                o = _nn(scores.astype(BF16), cb)
                o = o + _nn((at * w_q).astype(BF16), state.astype(BF16))
                o_ref[rows, :] = o
                state = chunk_decay * state + _tn((bt * w_k).astype(BF16), cb)

    seq = pl.BlockSpec((None, S, HD), lambda b, h: (b, 0, h))
    dspec = pl.BlockSpec((None, 1, HD), lambda b, h: (h, 0, 0))
    return pl.pallas_call(
        body, name="retention", grid=(B, NH),
        in_specs=[seq, seq, seq, dspec, dspec], out_specs=[seq, seq],
        out_shape=[jax.ShapeDtypeStruct((B, S, RW), F32)] * 2,
        compiler_params=_params(VMEM_BIG, ("arbitrary", "arbitrary")),
    )(a, b, c, dec_a, dec_b)


def _retention_ddecay(q, k, v, do, dec_a, dec_b, sched_a, sched_b):
    B, S, _ = q.shape
    C = TM

    def body(q_ref, k_ref, v_ref, do_ref, da_ref, db_ref, o_ref):
        ii = lax.broadcasted_iota(jnp.int32, (C, C), 0)
        jj = lax.broadcasted_iota(jnp.int32, (C, C), 1)
        pos = lax.broadcasted_iota(jnp.int32, (C, 1), 0).astype(F32)
        vals = []
        for dec_ref, (order, causal, strict) in ((da_ref, sched_a), (db_ref, sched_b)):
            x = dec_ref[...]
            lg = _log_sigmoid(x)
            lg1 = lg[:, 0:1]
            dist = ((ii - jj) if causal else (jj - ii)).astype(F32)
            mask = (dist > 0.0) if strict else (dist >= 0.0)
            ddecay = jnp.where(mask, dist * jnp.exp(jnp.maximum(dist, 0.0) * lg1), 0.0)
            p = pos if causal else (C - 1.0 - pos)
            w_q = jnp.exp((p + 1.0) * lg1)
            w_k = jnp.exp((C - 1.0 - p) * lg1)
            chunk_decay = jnp.exp(C * lg)
            state = jnp.zeros((HD, HD), F32)
            dstate = jnp.zeros((HD, HD), F32)
            tot = jnp.zeros((), F32)
            for n in order:
                rows = pl.ds(n * C, C)
                qt, kt, vt, dot = q_ref[rows, :], k_ref[rows, :], v_ref[rows, :], do_ref[rows, :]
                vb = vt.astype(BF16)
                scores = _nt(qt.astype(BF16), kt.astype(BF16))
                dscores = _nt(dot.astype(BF16), vb)
                qw = (qt * w_q).astype(BF16)
                cross = _nn(qw, state.astype(BF16))
                dcross = _nn(qw, dstate.astype(BF16))
                tot = tot + jnp.sum(scores * dscores * ddecay) + jnp.sum(((p + 1.0) * cross + dcross) * dot)
                kv = _tn((kt * w_k).astype(BF16), vb)
                dkv = _tn((kt * ((C - 1.0 - p) * w_k)).astype(BF16), vb)
                dstate = chunk_decay * (dstate + C * state) + dkv
                state = chunk_decay * state + kv
            vals.append(tot * jax.nn.sigmoid(-x))
        row = lax.broadcasted_iota(jnp.int32, (8, HD), 0)
        tile = jnp.where(row == 0, vals[0], 0.0) + jnp.where(row == 1, vals[1], 0.0)
        _acc(o_ref, tile, pl.program_id(1) == 0)

    seq = pl.BlockSpec((None, S, HD), lambda h, b: (b, 0, h))
    dspec = pl.BlockSpec((None, 1, HD), lambda h, b: (h, 0, 0))
    return pl.pallas_call(
        body, name="retention_ddecay", grid=(NH, B),
        in_specs=[seq, seq, seq, seq, dspec, dspec], out_specs=pl.BlockSpec((None, 8, HD), lambda h, b: (h, 0, 0)),
        out_shape=jax.ShapeDtypeStruct((NH, 8, HD), F32),
        compiler_params=_params(VMEM_BIG, ("arbitrary", "arbitrary")),
    )(q, k, v, do, dec_a, dec_b)


def _schedules(S):
    n = S // TM
    lat_up = tuple(range(1, n))
    lat_down = tuple(range(n - 1, 0, -1))
    fwd = (((0,) + lat_up, True, False), ((0,) + lat_down, False, True))
    bwd = ((lat_down + (0,), False, False), (lat_up + (0,), True, True))
    return fwd, bwd


def _shift_rows(x, d):
    if d == 0:
        return x
    S = x.shape[0]
    t = lax.broadcasted_iota(jnp.int32, x.shape, 0)
    tt = t + d
    lo = jnp.where(t < LC, 0, LC)
    hi = jnp.where(t < LC, LC, S)
    return jnp.where((tt >= lo) & (tt < hi), pltpu.roll(x, (-d) % S, 0), 0.0)


@functools.partial(jax.custom_vjp, nondiff_argnums=(1,))
def _shift(x, d):
    return _shift_rows(x, d)


_shift.defvjp(lambda x, d: (_shift_rows(x, d), None), lambda d, _, g: (_shift_rows(g, -d),))


def _pool_fn(p, bd, pscale):
    lane = lax.broadcasted_iota(jnp.int32, p.shape, 1)
    grp = lane // (PW // 4)
    half = jnp.where(grp == 0, 1, jnp.where(grp == 1, 2, jnp.where(grp == 2, 4, 8)))
    ones = jnp.ones(p.shape, F32)
    acc = jnp.zeros(p.shape, F32)
    cnt = jnp.zeros(p.shape, F32)
    for d in range(-8, 8):
        inwin = ((d >= -half) & (d < half)).astype(F32)
        acc = acc + _shift(p, d) * inwin
        cnt = cnt + _shift_rows(ones, d) * inwin
    pooled = acc / cnt - p
    mixed = _nn(pooled.astype(BF16), bd.astype(BF16))
    return mixed * pscale


def _dwconv_raw(zc, dw):
    y = jnp.zeros(zc.shape, F32)
    for k in range(CONV_K):
        y = y + _shift_rows(zc, k - CONV_K // 2) * dw[k:k + 1, :]
    return y


@jax.custom_vjp
def _dwconv(zc, dw):
    return _dwconv_raw(zc, dw)


def _dwconv_fwd(zc, dw):
    return _dwconv_raw(zc, dw), (zc, dw)


def _dwconv_bwd(res, g):
    zc, dw = res
    dz = jnp.zeros(zc.shape, F32)
    ddw = jnp.zeros(dw.shape, F32)
    row = lax.broadcasted_iota(jnp.int32, dw.shape, 0)
    for k in range(CONV_K):
        dz = dz + _shift_rows(g, CONV_K // 2 - k) * dw[k:k + 1, :]
        r = jnp.sum(g * _shift_rows(zc, k - CONV_K // 2), axis=0, keepdims=True)
        ddw = ddw + jnp.where(row == k, r, 0.0)
    return dz, ddw


_dwconv.defvjp(_dwconv_fwd, _dwconv_bwd)


def _conv_fn(u, dw, db):
    zc = u[:, :PW] * jax.nn.sigmoid(u[:, PW:])
    return _dwconv(zc, dw) + db


def _ln_swish(y, lng, lnb):
    mu = jnp.mean(y, axis=-1, keepdims=True)
    yc = y - mu
    var = jnp.mean(yc * yc, axis=-1, keepdims=True)
    yn = yc * lax.rsqrt(var + EPS) * lng + lnb
    return yn * jax.nn.sigmoid(yn)


def _seq(shape, single=False):
    return pl.BlockSpec((None,) + shape, lambda b: (b, 0, 0), pipeline_mode=pl.Buffered(1) if single else None)


def _c1(shape):
    nd = len(shape)
    return pl.BlockSpec(shape, lambda b: (0,) * nd)


def _seq_apply(fn, name, xs, consts, width):
    B, S, w = xs.shape

    def body(x_ref, *refs):
        refs[-1][...] = fn(x_ref[...], *[r[...] for r in refs[:-1]])

    return pl.pallas_call(
        body, name=name, grid=(B,),
        in_specs=[_seq((S, w))] + [_c1(c.shape) for c in consts], out_specs=_seq((S, width)),
        out_shape=jax.ShapeDtypeStruct((B, S, width), F32),
        compiler_params=_params(VMEM_BIG, ("arbitrary",)),
    )(xs, *consts)


def _seq_vjp(fn, name, xs, consts, dout):
    B, S, w = xs.shape
    n = len(consts)

    def body(x_ref, d_ref, *refs):
        first = pl.program_id(0) == 0
        _, vjp = jax.vjp(fn, x_ref[...], *[r[...] for r in refs[:n]])
        grads = vjp(d_ref[...])
        refs[n][...] = grads[0]
        for ref, val in zip(refs[n + 1:], grads[1:]):
            _acc(ref, val, first)

    return pl.pallas_call(
        body, name=name, grid=(B,),
        in_specs=[_seq((S, w), True), _seq((S, dout.shape[2]), True)] + [_c1(c.shape) for c in consts],
        out_specs=[_seq((S, w))] + [_c1(c.shape) for c in consts],
        out_shape=[jax.ShapeDtypeStruct((B, S, w), F32)] + [jax.ShapeDtypeStruct(c.shape, F32) for c in consts],
        compiler_params=_params(VMEM_BIG, ("arbitrary",)),
    )(xs, dout, *consts)


def _pool_conv_fwd(pp, pc, bd, pscale, dw, db):
    return (_seq_apply(_pool_fn, "pool_fwd", pp, (bd, pscale), PW),
            _seq_apply(_conv_fn, "conv_fwd", pc, (dw, db), PW))


def _pool_conv_bwd(pp, pc, dpo, dco, bd, pscale, dw, db):
    dpp, dbd, dps = _seq_vjp(_pool_fn, "pool_bwd", pp, (bd, pscale), dpo)
    dpc, ddw, ddb = _seq_vjp(_conv_fn, "conv_bwd", pc, (dw, db), dco)
    return dpp, dpc, dbd, dps, ddw, ddb


def _cat_fn(po, oa, ob, gg, co, gng, lng, lnb):
    o = oa + ob
    outs = []
    for h in range(NH):
        oh = o[:, h * HD:(h + 1) * HD]
        mu = jnp.mean(oh, axis=-1, keepdims=True)
        oc = oh - mu
        var = jnp.mean(oc * oc, axis=-1, keepdims=True)
        outs.append(oc * lax.rsqrt(var + EPS))
    ret = jnp.concatenate(outs, axis=1) * gng * (gg * jax.nn.sigmoid(gg))
    return jnp.concatenate([po, ret, _ln_swish(co, lng, lnb)], axis=1)


def _mix_out_fwd(z, po, oa, ob, gg, co, ro, mod, wout):
    B, S, _ = z.shape

    def body(z_ref, po_ref, oa_ref, ob_ref, gg_ref, co_ref, gn_ref, lg_ref, lb_ref, mod_ref, w_ref, zo_ref, out_ref):
        cat = _cat_fn(po_ref[...], oa_ref[...], ob_ref[...], gg_ref[...], co_ref[...], gn_ref[...], lg_ref[...], lb_ref[...])
        out = _nn(cat.astype(BF16), w_ref[...])
        out_ref[...] = out
        zo_ref[...] = z_ref[...] + mod_ref[2:3, :] * out

    return pl.pallas_call(
        body, name="mix_out_fwd", grid=(B, S // TM),
        in_specs=[_tok(D), _tok(PW), _tok(RW), _tok(RW), _tok(RW), _tok(PW), _const((1, RW)), _const((1, PW)),
                  _const((1, PW)), _modspec(), _wspec(wout)],
        out_specs=[_tok(D), _tok(D)],
        out_shape=[jax.ShapeDtypeStruct((B, S, D), F32)] * 2,
        compiler_params=_params(None, ("arbitrary", "arbitrary")),
    )(z, po, oa, ob, gg, co, *ro, mod, wout[0])


def _mix_out_bwd(dzo, out, po, oa, ob, gg, co, ro, mod, wout):
    B, S, _ = dzo.shape

    def body(dzo_ref, out_ref, po_ref, oa_ref, ob_ref, gg_ref, co_ref, gn_ref, lg_ref, lb_ref, mod_ref, w_ref,
             dpo_ref, do_ref, dgg_ref, dco_ref, cat_ref, dout_ref, dmod_ref, dgn_ref, dlg_ref, dlb_ref):
        b, t = pl.program_id(0), pl.program_id(1)
        dzo = dzo_ref[...]
        cat, vjp = jax.vjp(_cat_fn, po_ref[...], oa_ref[...], ob_ref[...], gg_ref[...], co_ref[...], gn_ref[...],
                           lg_ref[...], lb_ref[...])
        cat_ref[...] = cat.astype(BF16)
        dout = (mod_ref[2:3, :] * dzo).astype(BF16)
        dout_ref[...] = dout
        dgate = jnp.sum(out_ref[...] * dzo, axis=0, keepdims=True)
        dcat = _nt(dout, w_ref[...])
        dpo, doa, _, dgg, dco, dgn, dlg, dlb = vjp(dcat)
        dpo_ref[...] = dpo
        do_ref[...] = doa
        dgg_ref[...] = dgg
        dco_ref[...] = dco
        zero = jnp.zeros_like(dgate)
        _acc(dmod_ref, jnp.concatenate([zero, zero, dgate], axis=0), t <= 1)
        first = jnp.logical_and(b == 0, t == 0)
        _acc(dgn_ref, dgn, first)
        _acc(dlg_ref, dlg, first)
        _acc(dlb_ref, dlb, first)

    return pl.pallas_call(
        body, name="mix_out_bwd", grid=(B, S // TM),
        in_specs=[_tok(D), _tok(D), _tok(PW), _tok(RW), _tok(RW), _tok(RW), _tok(PW), _const((1, RW)), _const((1, PW)),
                  _const((1, PW)), _modspec(), _wspec(wout)],
        out_specs=[_tok(PW), _tok(RW), _tok(RW), _tok(PW), _tok(D), _tok(D), _modspec(), _const((1, RW)),
                   _const((1, PW)), _const((1, PW))],
        out_shape=[jax.ShapeDtypeStruct((B, S, PW), F32), jax.ShapeDtypeStruct((B, S, RW), F32),
                   jax.ShapeDtypeStruct((B, S, RW), F32), jax.ShapeDtypeStruct((B, S, PW), F32),
                   jax.ShapeDtypeStruct((B, S, D), BF16), jax.ShapeDtypeStruct((B, S, D), BF16),
                   jax.ShapeDtypeStruct((B, 2, 3, D), F32), jax.ShapeDtypeStruct((1, RW), F32),
                   jax.ShapeDtypeStruct((1, PW), F32), jax.ShapeDtypeStruct((1, PW), F32)],
        compiler_params=_params(None, ("arbitrary", "arbitrary")),
    )(dzo, out, po, oa, ob, gg, co, *ro, mod, wout[0])


def _rms(z, g):
    return z * lax.rsqrt(jnp.mean(z * z, axis=-1, keepdims=True) + EPS) * g


def _head(z, target, fg):
    B, S, _ = z.shape

    def body(z_ref, t_ref, g_ref, dz_ref, dg_ref, loss_ref):
        b, t = pl.program_id(0), pl.program_id(1)
        first = jnp.logical_and(b == 0, t == 0)

        @pl.when(t == 0)
        def _():
            dz_ref[...] = jnp.zeros((TM, D), F32)

        @pl.when(first)
        def _():
            dg_ref[...] = jnp.zeros((1, D), F32)
            loss_ref[...] = jnp.zeros((8, 128), F32)

        @pl.when(t > 0)
        def _():
            y, vjp = jax.vjp(_rms, z_ref[...], g_ref[...])
            err = y - t_ref[...]
            dz, dg = vjp(err * (1.0 / D))
            dz_ref[...] = dz
            dg_ref[...] += dg
            loss_ref[...] += 0.5 * jnp.sum(err * err) * (1.0 / D)

    return pl.pallas_call(
        body, name="head", grid=(B, S // TM),
        in_specs=[_tok(D), pl.BlockSpec((None, TM, D), lambda b, t: (b, jnp.maximum(t - 1, 0), 0)), _const((1, D))],
        out_specs=[_tok(D), _const((1, D)), _const((8, 128))],
        out_shape=[jax.ShapeDtypeStruct((B, S, D), F32), jax.ShapeDtypeStruct((1, D), F32),
                   jax.ShapeDtypeStruct((8, 128), F32)],
        compiler_params=_params(None, ("arbitrary", "arbitrary")),
    )(z, target, fg)


MROWS = 24
MCOL = 768


def _silu(x):
    return x * jax.nn.sigmoid(x)


def _mod_fwd(c24, wmod, bmod):
    ncol = wmod.shape[2]

    def body(c_ref, w_ref, b_ref, o_ref):
        sc = _silu(c_ref[...]).astype(BF16)
        o_ref[...] = _nn(sc, w_ref[...].astype(BF16)) + b_ref[...]

    return pl.pallas_call(
        body, name="mod_fwd", grid=(2, ncol // MCOL),
        in_specs=[pl.BlockSpec((MROWS, D), lambda l, j: (0, 0)), pl.BlockSpec((None, D, MCOL), lambda l, j: (l, 0, j)),
                  pl.BlockSpec((None, 1, MCOL), lambda l, j: (l, 0, j))],
        out_specs=pl.BlockSpec((None, MROWS, MCOL), lambda l, j: (l, 0, j)),
        out_shape=jax.ShapeDtypeStruct((2, MROWS, ncol), F32),
        compiler_params=_params(None, ("arbitrary", "arbitrary")),
    )(c24, wmod, bmod)


def _mod_bwd(c24, dmod, wmod):
    ncol = wmod.shape[2]

    def body(c_ref, d_ref, w_ref, dw_ref, dsc_ref):
        l, j = pl.program_id(0), pl.program_id(1)
        sc = _silu(c_ref[...]).astype(BF16)
        dm = d_ref[...].astype(BF16)
        dw_ref[...] = _tn(sc, dm)
        _acc(dsc_ref, _nt(dm, w_ref[...].astype(BF16)), jnp.logical_and(l == 0, j == 0))

    return pl.pallas_call(
        body, name="mod_bwd", grid=(2, ncol // MCOL),
        in_specs=[pl.BlockSpec((MROWS, D), lambda l, j: (0, 0)), pl.BlockSpec((None, MROWS, MCOL), lambda l, j: (l, 0, j)),
                  pl.BlockSpec((None, D, MCOL), lambda l, j: (l, 0, j))],
        out_specs=[pl.BlockSpec((None, D, MCOL), lambda l, j: (l, 0, j)), pl.BlockSpec((MROWS, D), lambda l, j: (0, 0))],
        out_shape=[jax.ShapeDtypeStruct((2, D, ncol), F32), jax.ShapeDtypeStruct((MROWS, D), F32)],
        compiler_params=_params(None, ("arbitrary", "arbitrary")),
    )(c24, dmod, wmod)


def _bmod_cctx_grad(dmod_full, dsc_parts, cctx):
    def body(d_ref, p_ref, c_ref, db_ref, dc_ref):
        db_ref[...] = jnp.sum(d_ref[...], axis=1, keepdims=True)
        tot = jnp.zeros((8, D), F32)
        for s in range(N_CHIP):
            tot = tot + p_ref[s]
        x = c_ref[...]
        sg = jax.nn.sigmoid(x)
        dc_ref[...] = jnp.sum(tot, axis=0, keepdims=True) * (sg * (1.0 + x * (1.0 - sg)))

    return pl.pallas_call(
        body, name="bmod_cctx_grad",
        out_shape=[jax.ShapeDtypeStruct((2, 1, N_MOD * D), F32), jax.ShapeDtypeStruct((1, D), F32)],
    )(dmod_full, dsc_parts, cctx)


def _adam_math(w, g, m, v):
    m = ADAM_B1 * m + (1.0 - ADAM_B1) * g
    v = ADAM_B2 * v + (1.0 - ADAM_B2) * (g * g)
    m_hat = m / (1.0 - ADAM_B1 ** ADAM_STEP)
    v_hat = v / (1.0 - ADAM_B2 ** ADAM_STEP)
    delta = -ADAM_LR * (m_hat / (jnp.sqrt(v_hat) + ADAM_EPS) + ADAM_WD * w)
    return delta, m, v


def _adam(w, g, m, v):
    R, Cc = w.shape
    if R * Cc * 4 <= (1 << 20):
        RB = R
    else:
        RB = 1 << (((1 << 18) // Cc).bit_length() - 1)
        assert R % RB == 0

    def body(w_ref, g_ref, m_ref, v_ref, d_ref, mo_ref, vo_ref):
        d, mn, vn = _adam_math(w_ref[...], g_ref[...], m_ref[...], v_ref[...])
        d_ref[...] = d
        mo_ref[...] = mn
        vo_ref[...] = vn

    spec = pl.BlockSpec((RB, Cc), lambda i: (i, 0))
    return pl.pallas_call(
        body, name="adam", grid=(R // RB,), in_specs=[spec] * 4, out_specs=[spec] * 3,
        out_shape=[jax.ShapeDtypeStruct((R, Cc), F32)] * 3,
        compiler_params=_params(None, ("arbitrary",)),
    )(w, g, m, v)


def _adam_block(w, g, m, v, block, n, prev):
    shape = w.shape
    cols = shape[-1]
    w3, m3, v3 = (t.reshape(n, -1, cols) for t in (w, m, v))
    g2 = g.reshape(-1, cols)
    R = g2.shape[0]
    RB = max(r for r in range(8, (1 << 18) // cols + 1, 8) if R % r == 0)

    def body(w_ref, g_ref, m_ref, v_ref, *refs):
        go_ref, d_ref, mo_ref, vo_ref = refs[-4:]
        gt = g_ref[...]
        d, mn, vn = _adam_math(w_ref[...], gt, m_ref[...], v_ref[...])
        go_ref[...] = gt
        d_ref[...] = d
        mo_ref[...] = mn
        vo_ref[...] = vn

    lay = pl.BlockSpec((None, RB, cols), lambda i: (block, i, 0))
    flat = pl.BlockSpec((RB, cols), lambda i: (i, 0))
    hold = [] if prev is None else [t.reshape(n, -1, cols) for t in prev]
    outs = pl.pallas_call(
        body, name="adam_block", grid=(R // RB,),
        in_specs=[lay, flat, lay, lay] + [pl.BlockSpec(memory_space=pl.ANY)] * len(hold), out_specs=[lay] * 4,
        out_shape=[jax.ShapeDtypeStruct(w3.shape, F32)] * 4,
        input_output_aliases={4 + k: k for k in range(len(hold))},
        compiler_params=_params(None, ("arbitrary",)),
    )(w3, g2, m3, v3, *hold)
    return [o.reshape(shape) for o in outs]


def _sum_devices(parts):
    K = parts.shape[1]

    def body(p_ref, o_ref):
        tot = p_ref[0]
        for i in range(1, N_DEV):
            tot = tot + p_ref[i]
        o_ref[...] = tot

    return pl.pallas_call(body, name="sum_devices", out_shape=jax.ShapeDtypeStruct((K, 128), F32))(parts)


def _sum_pieces(own, others):
    R = own.shape[0]
    RB = 96 if R % 96 == 0 else 32

    def body(a_ref, r_ref, o_ref):
        tot = a_ref[...].astype(F32)
        for i in range(N_DEV - 1):
            tot = tot + r_ref[i].astype(F32)
        o_ref[...] = tot

    return pl.pallas_call(
        body, name="sum_pieces", grid=(R // RB,),
        in_specs=[pl.BlockSpec((RB, D), lambda i: (i, 0)), pl.BlockSpec((N_DEV - 1, RB, D), lambda i: (0, i, 0))],
        out_specs=pl.BlockSpec((RB, D), lambda i: (i, 0)),
        out_shape=jax.ShapeDtypeStruct((R, D), F32),
        compiler_params=_params(None, ("arbitrary",)),
    )(own, others)


def _coords():
    return lax.axis_index("x"), lax.axis_index("y"), lax.axis_index("c")


_FLIPS = [(fx, fy, fc) for fx in (0, 1) for fy in (0, 1) for fc in (0, 1)][1:]


def _all_gather_small(buf):
    K = buf.shape[0]

    def body(in_ref, out_ref, send_sems, recv_sems, local_sem):
        x, y, c = _coords()
        me = 4 * x + 2 * y + c
        mine = pltpu.make_async_copy(in_ref, out_ref.at[me], local_sem)
        mine.start()
        sends = []
        for k, (fx, fy, fc) in enumerate(_FLIPS):
            peer = (x ^ fx, y ^ fy, c ^ fc)
            cp = pltpu.make_async_remote_copy(src_ref=in_ref, dst_ref=out_ref.at[me], send_sem=send_sems.at[k],
                                              recv_sem=recv_sems.at[k], device_id=peer, device_id_type=MESH)
            cp.start()
            sends.append(cp)
        for k, (fx, fy, fc) in enumerate(_FLIPS):
            src = 4 * (x ^ fx) + 2 * (y ^ fy) + (c ^ fc)
            pltpu.make_async_remote_copy(src_ref=in_ref, dst_ref=out_ref.at[src], send_sem=send_sems.at[k],
                                         recv_sem=recv_sems.at[k], device_id=(x, y, c), device_id_type=MESH).wait_recv()
        for cp in sends:
            cp.wait_send()
        mine.wait()

    return pl.pallas_call(
        body, name="all_gather_small",
        in_specs=[pl.BlockSpec(memory_space=pltpu.VMEM)], out_specs=pl.BlockSpec(memory_space=pltpu.VMEM),
        out_shape=jax.ShapeDtypeStruct((N_DEV, K, 128), F32),
        scratch_shapes=[pltpu.SemaphoreType.DMA((7,)), pltpu.SemaphoreType.DMA((7,)), pltpu.SemaphoreType.DMA],
        compiler_params=_params(VMEM_BIG),
    )(buf)


_CHIP_FLIPS = [(1, 0), (0, 1), (1, 1)]


def _exchange_rows(buf, B):
    K = buf.shape[1]
    nex = N_DEV * B

    def body(in_ref, out_ref, send_sems, recv_sems, local_sems):
        x, y, c = _coords()
        s_me = 2 * x + y

        def parts(dev, chip_slot):
            return ((in_ref.at[pl.ds(B * dev, B)], out_ref.at[chip_slot, pl.ds(0, B)]),
                    (in_ref.at[pl.ds(nex, 1)], out_ref.at[chip_slot, pl.ds(B, 1)]))

        locals_ = [pltpu.make_async_copy(src, dst, local_sems.at[i]) for i, (src, dst) in enumerate(parts(4 * x + 2 * y + c, s_me))]
        for cp in locals_:
            cp.start()
        sends = []
        for j, (fx, fy) in enumerate(_CHIP_FLIPS):
            px, py = x ^ fx, y ^ fy
            for i, (src, dst) in enumerate(parts(4 * px + 2 * py + c, s_me)):
                cp = pltpu.make_async_remote_copy(src_ref=src, dst_ref=dst, send_sem=send_sems.at[2 * j + i],
                                                  recv_sem=recv_sems.at[2 * j + i], device_id=(px, py, c), device_id_type=MESH)
                cp.start()
                sends.append(cp)
        for j, (fx, fy) in enumerate(_CHIP_FLIPS):
            for i, (src, dst) in enumerate(parts(0, 2 * (x ^ fx) + (y ^ fy))):
                pltpu.make_async_remote_copy(src_ref=src, dst_ref=dst, send_sem=send_sems.at[2 * j + i],
                                             recv_sem=recv_sems.at[2 * j + i], device_id=(x, y, c), device_id_type=MESH).wait_recv()
        for cp in sends:
            cp.wait_send()
        for cp in locals_:
            cp.wait()

    return pl.pallas_call(
        body, name="exchange_rows",
        in_specs=[pl.BlockSpec(memory_space=pltpu.VMEM)], out_specs=pl.BlockSpec(memory_space=pltpu.VMEM),
        out_shape=jax.ShapeDtypeStruct((N_CHIP, B + 1, K, 128), F32),
        scratch_shapes=[pltpu.SemaphoreType.DMA((6,)), pltpu.SemaphoreType.DMA((6,)), pltpu.SemaphoreType.DMA((2,))],
    )(buf)


_HBM = pl.BlockSpec(memory_space=pltpu.HBM)
_SEMS = pl.BlockSpec(memory_space=pltpu.SEMAPHORE)
_EFFECT = pltpu.SideEffectType.DATAFLOW_SIDE_EFFECTING


def _in_hbm(v):
    return pltpu.with_memory_space_constraint(v, pltpu.HBM)


def _copies_start(name, srcs, lands, n_sems, issue, after):
    ns, nl = len(srcs), len(lands)

    def body(*refs):
        src_refs, land_refs = refs[:ns], refs[ns:ns + nl]
        out = refs[ns + nl + 1:]
        issue(src_refs, land_refs, out[:nl], out[nl:2 * nl])
        out[-1][...] = jnp.zeros((8, 128), F32)

    outs = pl.pallas_call(
        body, name=name, in_specs=[_HBM] * (ns + nl) + [pl.BlockSpec(memory_space=pl.ANY)],
        out_specs=[_SEMS] * (2 * nl) + [_HBM] * (ns + nl) + [pl.BlockSpec(memory_space=pltpu.VMEM)],
        out_shape=[pltpu.SemaphoreType.DMA((n_sems,))] * (2 * nl) + [pltpu.HBM(v.shape, v.dtype) for v in (*srcs, *lands)]
        + [jax.ShapeDtypeStruct((8, 128), F32)],
        input_output_aliases={i: 2 * nl + i for i in range(ns + nl)},
        compiler_params=pltpu.CompilerParams(has_side_effects=_EFFECT),
    )(*[_in_hbm(v) for v in (*srcs, *lands)], after)
    return outs[:nl], outs[nl:2 * nl], outs[2 * nl:2 * nl + ns], outs[2 * nl + ns:2 * nl + ns + nl], outs[-1]


def _copies_wait(name, srcs, lands, send_sems, recv_sems, finish, after):
    after = list(after) if isinstance(after, (list, tuple)) else [after]
    ns, nl = len(srcs), len(lands)

    def body(*refs):
        src_refs, land_refs = refs[:ns], refs[ns:ns + nl]
        finish(src_refs, land_refs, refs[ns + nl:ns + 2 * nl], refs[ns + 2 * nl:ns + 3 * nl])

    outs = pl.pallas_call(
        body, name=name, in_specs=[_HBM] * (ns + nl) + [_SEMS] * (2 * nl) + [pl.BlockSpec(memory_space=pl.ANY)] * len(after),
        out_specs=[_HBM] * (ns + nl), out_shape=[pltpu.HBM(v.shape, v.dtype) for v in (*srcs, *lands)],
        input_output_aliases={i: i for i in range(ns + nl)},
        compiler_params=pltpu.CompilerParams(has_side_effects=_EFFECT),
    )(*srcs, *lands, *send_sems, *recv_sems, *after)
    return outs[:ns], outs[ns:]


def _own_slab(land, mine, index):
    return lax.dynamic_update_slice_in_dim(land, mine[:, None], index, axis=1)


def _gather_start(tag, units, after):
    x, y, c = _coords()
    chip = 2 * x + y
    lands = [_own_slab(lax.empty((u.shape[0], N_CHIP) + u.shape[1:], u.dtype), u, chip) for u in units]

    def issue(src_refs, land_refs, send_sems, recv_sems):
        x, y, c = _coords()
        s_me = 2 * x + y
        for i, (src, land) in enumerate(zip(src_refs, land_refs)):
            for j, (fx, fy) in enumerate(_CHIP_FLIPS):
                pltpu.make_async_remote_copy(src_ref=src, dst_ref=land.at[:, s_me], send_sem=send_sems[i].at[j],
                                             recv_sem=recv_sems[i].at[j], device_id=(x ^ fx, y ^ fy, c),
                                             device_id_type=MESH).start()

    return _copies_start("gather_start_" + tag, units, lands, 3, issue, after)


def _gather_wait(tag, started, which, after):
    send_sems, recv_sems, srcs, lands, _ = started

    def finish(src_refs, land_refs, ssems, rsems):
        x, y, c = _coords()
        for src, land, ss, rs in zip(src_refs, land_refs, ssems, rsems):
            for j in range(3):
                cp = pltpu.make_async_remote_copy(src_ref=src, dst_ref=land.at[:, 0], send_sem=ss.at[j], recv_sem=rs.at[j],
                                                  device_id=(x, y, c), device_id_type=MESH)
                cp.wait_send()
                cp.wait_recv()

    _, done = _copies_wait("gather_wait_" + tag, [srcs[i] for i in which], [lands[i] for i in which],
                           [send_sems[i] for i in which], [recv_sems[i] for i in which], finish, after)
    return [d.reshape(d.shape[0], N_CHIP * d.shape[2], D) for d in done]


def _piece_rows(grads):
    return [g.shape[2] for g in grads]


def _scatter_start(tag, grads, after):
    rows = _piece_rows(grads)
    land = lax.empty((N_DEV - 1, sum(rows), D), grads[0].dtype)

    def issue(src_refs, land_refs, send_sems, recv_sems):
        x, y, c = _coords()
        for k, (fx, fy, fc) in enumerate(_FLIPS):
            px, py, pc = x ^ fx, y ^ fy, c ^ fc
            off = 0
            for src, n in zip(src_refs, rows):
                pltpu.make_async_remote_copy(src_ref=src.at[2 * px + py, pc], dst_ref=land_refs[0].at[k, pl.ds(off, n)],
                                             send_sem=send_sems[0].at[k], recv_sem=recv_sems[0].at[k],
                                             device_id=(px, py, pc), device_id_type=MESH).start()
                off += n

    return _copies_start("scatter_start_" + tag, grads, [land], N_DEV - 1, issue, after)


def _scatter_wait(tag, started, after):
    send_sems, recv_sems, srcs, lands, _ = started

    def finish(src_refs, land_refs, ssems, rsems):
        x, y, c = _coords()
        for k in range(N_DEV - 1):
            cp = pltpu.make_async_remote_copy(src_ref=land_refs[0].at[0], dst_ref=land_refs[0].at[0], send_sem=ssems[0].at[k],
                                              recv_sem=rsems[0].at[k], device_id=(x, y, c), device_id_type=MESH)
            cp.wait_send()
            cp.wait_recv()

    grads, (others,) = _copies_wait("scatter_wait_" + tag, srcs, lands, send_sems, recv_sems, finish, after)
    return grads, others


def _swap_start(tag, mine, rows, after):
    x, y, c = _coords()
    offs = [sum(rows[:t]) for t in range(len(rows))]
    lands = [lax.dynamic_update_slice_in_dim(lax.empty((2, n, D), mine.dtype), mine[o:o + n][None], c, axis=0)
             for o, n in zip(offs, rows)]

    def issue(src_refs, land_refs, send_sems, recv_sems):
        x, y, c = _coords()
        for t, (o, n) in enumerate(zip(offs, rows)):
            pltpu.make_async_remote_copy(src_ref=src_refs[0].at[pl.ds(o, n)], dst_ref=land_refs[t].at[c],
                                         send_sem=send_sems[t].at[0], recv_sem=recv_sems[t].at[0],
                                         device_id=(x, y, 1 - c), device_id_type=MESH).start()

    return _copies_start("swap_start_" + tag, [mine], lands, 1, issue, after)


def _swap_wait(tag, started, after):
    send_sems, recv_sems, srcs, lands, _ = started

    def finish(src_refs, land_refs, ssems, rsems):
        x, y, c = _coords()
        for land, ss, rs in zip(land_refs, ssems, rsems):
            cp = pltpu.make_async_remote_copy(src_ref=land.at[0], dst_ref=land.at[0], send_sem=ss.at[0], recv_sem=rs.at[0],
                                              device_id=(x, y, c), device_id_type=MESH)
            cp.wait_send()
            cp.wait_recv()

    return _copies_wait("swap_wait_" + tag, srcs, lands, send_sems, recv_sems, finish, after)[1]


def _size(shape):
    n = 1
    for d in shape:
        n *= d
    return n


def _pack(arrays):
    return jnp.concatenate([jnp.pad(a.reshape(-1).astype(F32), (0, (-a.size) % 1024)).reshape(-1, 128) for a in arrays], axis=0)


def _unpack(buf, shapes):
    out, row = [], 0
    for s in shapes:
        n = _size(s)
        nrows = 8 * -(-n // 1024)
        out.append(buf[row:row + nrows].reshape(-1)[:n].reshape(s))
        row += nrows
    return out


def _block_diag(pw):
    bd = jnp.zeros((PW, PW), F32)
    g = PW // 4
    for i in range(4):
        bd = bd.at[i * g:(i + 1) * g, i * g:(i + 1) * g].set(pw[i])
    return bd


def _lanes(v):
    return jnp.broadcast_to(v.reshape(NH, 1, 1), (NH, 1, HD))


def _layer_fwd(z, mod, normg, wget, small):
    S = z.shape[1]
    cos, sin = _rope_tables(S)
    fwd_sched, _ = _schedules(S)
    wa = wget("a", z)
    z1, f_a = _ffn_fwd(z, mod[:, :, 0], normg[0], *wa)
    wm = wget("m", z1)
    pp, q, k, v, gg, pc = _mix_in_fwd(z1, mod[:, :, 1], normg[1], wm[0], cos, sin)
    po, co = _pool_conv_fwd(pp, pc, small["bd"], small["pscale"], small["dw"], small["db"])
    ro = (small["gng"], small["lng"], small["lnb"])
    oa, ob = _retention(q, k, v, small["dec_f"], small["dec_b"], *fwd_sched)
    z2, out = _mix_out_fwd(z1, po, oa, ob, gg, co, ro, mod[:, :, 1], wm[1])
    wb = wget("b", z2)
    z3, f_b = _ffn_fwd(z2, mod[:, :, 2], normg[2], *wb)
    saved = dict(z=z, f_a=f_a, z1=z1, pp=pp, q=q, k=k, v=v, gg=gg, pc=pc, po=po, co=co, oa=oa, ob=ob, out=out, z2=z2, f_b=f_b,
                 wa=wa, wm=wm, wb=wb)
    return z3, saved


def _layer_bwd(dz3, sv, mod, normg, small, emit, tok, last):
    S = dz3.shape[1]
    B = dz3.shape[0]
    T = B * S
    cos, sin = _rope_tables(S)
    fwd_sched, bwd_sched = _schedules(S)
    wa, wm, wb = sv["wa"], sv["wm"], sv["wb"]
    dz2, dmod_b, dg_b, gw1t_b, gw3t_b, gw2_b = _ffn_bwd(sv["z2"], dz3, sv["f_b"], mod[:, :, 2] + tok, normg[2], *wb)
    tok = emit("b", [gw1t_b, gw3t_b, gw2_b])
    mod_m = mod[:, :, 1] + tok
    ro = (small["gng"], small["lng"], small["lnb"])
    dpo, do, dgg, dco, cat, dout, dmod_gate, dgng, dlng, dlnb = _mix_out_bwd(
        dz2, sv["out"], sv["po"], sv["oa"], sv["ob"], sv["gg"], sv["co"], ro, mod_m, wm[1])
    gwout = _tn_matmul(cat.reshape(T, D), dout.reshape(T, D))
    dqa, dqb = _retention(do, sv["v"], sv["k"], small["dec_f"], small["dec_b"], *fwd_sched)
    dka, dkb = _retention(sv["v"], do, sv["q"], small["dec_f"], small["dec_b"], *bwd_sched)
    dva, dvb = _retention(sv["k"], sv["q"], do, small["dec_f"], small["dec_b"], *bwd_sched)
    ddec = _retention_ddecay(sv["q"], sv["k"], sv["v"], do, small["dec_f"], small["dec_b"], *fwd_sched)
    dpp, dpc, dbd, dps, ddw, ddb = _pool_conv_bwd(sv["pp"], sv["pc"], dpo, dco, small["bd"], small["pscale"],
                                                   small["dw"], small["db"])
    dz1, h, dp, dmod_m, dg_m = _mix_in_bwd(sv["z1"], dz2, dpp, dqa, dqb, dka, dkb, dva, dvb, dgg, dpc, mod_m, normg[1],
                                           wm[0], cos, sin)
    gwint = _tn_matmul(dp.reshape(T, F), h.reshape(T, D))
    tok = emit("m", [gwint, gwout])
    if last:
        dz, dmod_a, dg_a = _ffn_bwd(sv["z"], dz1, sv["f_a"], mod[:, :, 0] + tok, normg[0], *wa,
                                    each=lambda grad: emit("a", [grad]))[:3]
    else:
        dz, dmod_a, dg_a, gw1t_a, gw3t_a, gw2_a = _ffn_bwd(sv["z"], dz1, sv["f_a"], mod[:, :, 0] + tok, normg[0], *wa)
        tok = emit("a", [gw1t_a, gw3t_a, gw2_a])
    dmod = jnp.stack([dmod_a, dmod_m + dmod_gate, dmod_b], axis=2)
    dnormg = jnp.stack([dg_a, dg_m, dg_b], axis=0)
    g = PW // 4
    dpool_w = jnp.stack([dbd[i * g:(i + 1) * g, i * g:(i + 1) * g] for i in range(4)], axis=0)
    sm = dict(pool_w=dpool_w, pool_scale=dps[0], dec_f=ddec[:, 0, 0], dec_b=ddec[:, 1, 0], gng=dgng[0],
              conv_dw=ddw[0:CONV_K], conv_b=ddb[0], conv_ln_g=dlng[0], conv_ln_b=dlnb[0])
    return dz, dmod, dnormg, sm, tok


def _small_params(pool_w, pool_scale, dec_f, dec_b, gng, conv_dw, conv_b, lng, lnb):
    return dict(bd=_block_diag(pool_w), pscale=pool_scale.reshape(1, PW), dec_f=_lanes(dec_f), dec_b=_lanes(dec_b),
                gng=gng.reshape(1, RW), dw=jnp.pad(conv_dw, ((0, 1), (0, 0))), db=conv_b.reshape(1, PW),
                lng=lng.reshape(1, PW), lnb=lnb.reshape(1, PW))


_WEIGHTS = ["c_ctx", "w_mod", "b_mod", "norm_g", "ffn_w1", "ffn_w3", "ffn_w2", "w_in", "w_out", "pool_w", "pool_scale",
            "ret_decay_fwd", "ret_decay_bwd", "ret_gn_g", "conv_dw", "conv_b", "conv_ln_g", "conv_ln_b", "final_g"]
_BIG = ["w_mod", "ffn_w1", "ffn_w3", "ffn_w2", "w_in", "w_out"]
_SMALL = [n for n in _WEIGHTS if n not in _BIG]


def _adam_any(w, g, m, v):
    shape = w.shape
    cols = shape[-1] if w.ndim >= 2 else 128
    outs = _adam(w.reshape(-1, cols), g.reshape(-1, cols), m.reshape(-1, cols), v.reshape(-1, cols))
    return [o.reshape(shape) for o in outs]


def _step(a):
    x, c, ctx = a["x"], a["c"], a["ctx"]
    B = x.shape[0]
    nex = N_DEV * B
    assert nex + B <= MROWS and ctx.shape[1] == LC and x.shape[1] % TM == 0
    xi, yi, ci = _coords()
    me = 4 * xi + 2 * yi + ci
    chip = 2 * xi + yi
    ncol = a["w_mod"].shape[2]

    def ffn_unit(l, i, tok):
        return jnp.stack([(jnp.swapaxes(a["ffn_w1"][l, i], -1, -2) + tok).astype(BF16),
                          (jnp.swapaxes(a["ffn_w3"][l, i], -1, -2) + tok).astype(BF16), (a["ffn_w2"][l, i] + tok).astype(BF16)])

    def later_units(tok):
        units = []
        for l in range(2):
            units += [ffn_unit(l, 0, tok), (jnp.swapaxes(a["w_in"][l], -1, -2) + tok).astype(BF16)[None],
                      (a["w_out"][l] + tok).astype(BF16)[None], ffn_unit(l, 1, tok)]
        return units[1:]

    shapes1 = [(B, D), (2, 3, D // N_CHIP), (2, CONV_K, PW // N_CHIP)]
    g1 = _all_gather_small(_pack([c, a["norm_g"], a["conv_dw"]]))
    per = [_unpack(g1[d], shapes1) for d in range(N_DEV)]
    c_all = jnp.concatenate([per[d][0] for d in range(N_DEV)], axis=0)
    norm_g_full = jnp.concatenate([per[2 * s][1] for s in range(N_CHIP)], axis=-1)
    conv_dw_full = jnp.concatenate([per[2 * s][2] for s in range(N_CHIP)], axis=-1)
    cctx = a["c_ctx"].reshape(1, D)
    c24 = jnp.concatenate([c_all] + [cctx] * B + [jnp.zeros((MROWS - nex - B, D), F32)], axis=0)

    bsh = lax.dynamic_slice(a["b_mod"], (0, chip * ncol), (2, ncol)).reshape(2, 1, ncol)
    mod_raw = _mod_fwd(c24, a["w_mod"], bsh)
    mine = _exchange_rows(jnp.swapaxes(mod_raw, 0, 1).reshape(MROWS, 2 * ncol // 128, 128), B)
    mod_mine = jnp.concatenate([mine[s].reshape(B + 1, 2, ncol) for s in range(N_CHIP)], axis=-1)
    mods = []
    for l in range(2):
        cx = jnp.broadcast_to(mod_mine[B, l][None], (B, N_MOD * D))
        mods.append(jnp.stack([cx, mod_mine[:B, l]], axis=1).reshape(B, 2, 3, 3, D))

    first = _gather_start("first", [ffn_unit(0, 0, jnp.zeros((), F32))], mod_mine)
    rest = _gather_start("rest", later_units(first[4][0, 0]), first[4])
    started = tuple(x + y for x, y in zip(first[:4], rest[:4])) + (rest[4],)

    def wget_of(l):
        def wget(stage, after):
            if stage == "m":
                win, wo = _gather_wait(f"m{l}", started, [4 * l + 1, 4 * l + 2], after)
                return (win, 0), (wo, 0)
            (g,) = _gather_wait(f"{stage}{l}", started, [4 * l + (0 if stage == "a" else 3)], after)
            return (g, 0), (g, 1), (g, 2)
        return wget

    smalls = [_small_params(a["pool_w"][l], a["pool_scale"][l], a["ret_decay_fwd"][l], a["ret_decay_bwd"][l],
                            a["ret_gn_g"][l], conv_dw_full[l], a["conv_b"][l], a["conv_ln_g"][l], a["conv_ln_b"][l])
              for l in range(2)]
    normgs = [norm_g_full[l].reshape(3, 1, D) for l in range(2)]
    z = jnp.concatenate([ctx, x], axis=1)
    saved = []
    for l in range(2):
        z, sv = _layer_fwd(z, mods[l], normgs[l], wget_of(l), smalls[l])
        saved.append(sv)
    dz, dfinal_g, loss_part = _head(z, a["loss_target"], a["final_g"].reshape(1, D))

    scattering, swapping, reduced, newest = [], [], {}, []

    def reduce_oldest(after):
        tag, st, rows = scattering.pop(0)
        grads, others = _scatter_wait(tag, st, after)
        own = jnp.concatenate([lax.dynamic_slice(g, (chip, ci, 0, 0), (1, 1) + g.shape[2:]).reshape(g.shape[2:])
                               for g in grads], axis=0)
        mine = _sum_pieces(own, others)
        sw = _swap_start(tag, mine, rows, mine)
        if swapping:
            ptag, psw = swapping.pop()
            reduced[ptag] = _swap_wait(ptag, psw, sw[4])
        swapping.append((tag, sw))
        return sw[4]

    def emit_of(l):
        def emit(stage, grads):
            grads = [g.reshape(N_CHIP, 2, g.shape[0] // (2 * N_CHIP), D) for g in grads]
            stem = f"{stage}{l}"
            same = sum(1 for t, _, _ in scattering if t.split("_")[0] == stem)
            tag = stem if same == 0 else f"{stem}_{same}"
            st = _scatter_start(tag, grads, grads[0])
            tok = st[4][0, 0]
            while scattering and scattering[0][0].split("_")[0] != stem:
                tok = tok + reduce_oldest(st[4])[0, 0]
            scattering.append((tag, st, _piece_rows(grads)))
            newest[:] = [st[4]] + (newest if same else [])
            return tok
        return emit

    back = [None, None]
    tok = jnp.zeros((), F32)
    for l in (1, 0):
        dz, dmod, dnormg, sm, tok = _layer_bwd(dz, saved[l], mods[l], normgs[l], smalls[l], emit_of(l), tok, l == 0)
        back[l] = (dmod, dnormg, None, sm)
    grad_x = dz[:, LC:]
    grads = {}

    dmods = [back[l][0].reshape(B, 2, N_MOD * D) for l in range(2)]
    pack_a = _pack([jnp.stack([dm[:, 1] for dm in dmods])])
    ka = pack_a.shape[0]
    sm = [back[l][3] for l in range(2)]
    sum_list = [jnp.stack([dm[:, 0] for dm in dmods]), jnp.stack([back[l][1][:, 0] for l in range(2)])]
    sm_keys = ["pool_w", "pool_scale", "dec_f", "dec_b", "gng", "conv_dw", "conv_b", "conv_ln_g", "conv_ln_b"]
    sum_list += [jnp.stack([sm[l][k] for l in range(2)]) for k in sm_keys]
    sum_list += [dfinal_g[0], loss_part[0, 0:1]]
    sum_shapes = [s.shape for s in sum_list]
    g3 = _all_gather_small(jnp.concatenate([pack_a, _pack(sum_list)], axis=0))
    dmx_all = jnp.concatenate([_unpack(g3[d, :ka], [(2, B, N_MOD * D)])[0] for d in range(N_DEV)], axis=1)
    summed = _unpack(_sum_devices(g3[:, ka:]), sum_shapes)
    dmy, dnorm_full = summed[0], summed[1]
    sgrad = dict(zip(sm_keys, summed[2:2 + len(sm_keys)]))
    loss = summed[-1].reshape(())

    dmod24 = jnp.concatenate([dmx_all, dmy, jnp.zeros((2, MROWS - nex - B, N_MOD * D), F32)], axis=1)
    dmod_my = lax.dynamic_slice(dmod24, (0, 0, chip * ncol), (2, MROWS, ncol))
    grads["w_mod"], dsc = _mod_bwd(c24, dmod_my, a["w_mod"])
    g4 = _all_gather_small(_pack([dsc[nex:nex + 8]]))
    dsc_parts = jnp.stack([_unpack(g4[2 * s], [(8, D)])[0] for s in range(N_CHIP)])
    dbmod, dcctx = _bmod_cctx_grad(dmod24, dsc_parts, cctx)

    grads["c_ctx"] = dcctx[0]
    grads["b_mod"] = dbmod.reshape(2, N_MOD * D)
    grads["norm_g"] = lax.dynamic_slice(dnorm_full, (0, 0, chip * (D // N_CHIP)), (2, 3, D // N_CHIP))
    grads["pool_w"] = sgrad["pool_w"]
    grads["pool_scale"] = sgrad["pool_scale"]
    grads["ret_decay_fwd"] = sgrad["dec_f"]
    grads["ret_decay_bwd"] = sgrad["dec_b"]
    grads["ret_gn_g"] = sgrad["gng"]
    grads["conv_dw"] = lax.dynamic_slice(sgrad["conv_dw"], (0, 0, chip * (PW // N_CHIP)), (2, CONV_K, PW // N_CHIP))
    grads["conv_b"] = sgrad["conv_b"]
    grads["conv_ln_g"] = sgrad["conv_ln_g"]
    grads["conv_ln_b"] = sgrad["conv_ln_b"]
    grads["final_g"] = summed[-2]

    delta, new_m, new_v = {}, {}, {}
    delta["w_mod"], new_m["w_mod"], new_v["w_mod"] = _adam_any(a["w_mod"], grads["w_mod"], a["m_w_mod"], a["v_w_mod"])
    shapes_s = [a[n].shape for n in _SMALL]
    packed = _adam(_pack([a[n] for n in _SMALL]), _pack([grads[n] for n in _SMALL]),
                   _pack([a["m_" + n] for n in _SMALL]), _pack([a["v_" + n] for n in _SMALL]))
    for res, out in zip(packed, (delta, new_m, new_v)):
        for n, val in zip(_SMALL, _unpack(res, shapes_s)):
            out[n] = val

    def group(stem):
        out, k = list(reduced[stem]), 1
        while f"{stem}_{k}" in reduced:
            out += reduced[f"{stem}_{k}"]
            k += 1
        return [h.reshape(-1, D) for h in out]

    turned = ("ffn_w1", "ffn_w3", "w_in")
    done = {n: None for n in _BIG[1:]}

    def adam_of(n, block, nblocks, g):
        t = (lambda u: jnp.swapaxes(u, -1, -2)) if n in turned else (lambda u: u)
        done[n] = _adam_block(t(a[n]), g, t(a["m_" + n]), t(a["v_" + n]), block, nblocks, done[n])

    def adam_stage(stage, l):
        mats = group(f"{stage}{l}")
        if stage == "m":
            adam_of("w_in", l, 2, mats[0])
            adam_of("w_out", l, 2, mats[1])
        else:
            for n, g in zip(("ffn_w1", "ffn_w3", "ffn_w2"), mats):
                adam_of(n, 2 * l + (stage == "b"), 4, g)

    for stage, l in (("b", 1), ("m", 1), ("a", 1), ("b", 0)):
        adam_stage(stage, l)

    last = [delta["w_mod"], packed[0]] + [done[n][1] for n in _BIG[1:]] + newest
    while scattering:
        last = reduce_oldest(last)
    tag, sw = swapping.pop()
    reduced[tag] = _swap_wait(tag, sw, last)
    adam_stage("m", 0)
    adam_stage("a", 0)
    for n in _BIG[1:]:
        grads[n], delta[n], new_m[n], new_v[n] = [jnp.swapaxes(r, -1, -2) for r in done[n]] if n in turned else done[n]
    return (loss, grad_x, *[grads[n] for n in _WEIGHTS], *[delta[n] for n in _WEIGHTS],
            *[new_m[n] for n in _WEIGHTS], *[new_v[n] for n in _WEIGHTS])


def kernel(x, c, ctx, c_ctx, w_mod, b_mod, norm_g, ffn_w1, ffn_w3, ffn_w2, w_in, w_out, pool_w, pool_scale, ret_decay_fwd, ret_decay_bwd, ret_gn_g, conv_dw, conv_b, conv_ln_g, conv_ln_b, final_g, loss_target, m_c_ctx, m_w_mod, m_b_mod, m_norm_g, m_ffn_w1, m_ffn_w3, m_ffn_w2, m_w_in, m_w_out, m_pool_w, m_pool_scale, m_ret_decay_fwd, m_ret_decay_bwd, m_ret_gn_g, m_conv_dw, m_conv_b, m_conv_ln_g, m_conv_ln_b, m_final_g, v_c_ctx, v_w_mod, v_b_mod, v_norm_g, v_ffn_w1, v_ffn_w3, v_ffn_w2, v_w_in, v_w_out, v_pool_w, v_pool_scale, v_ret_decay_fwd, v_ret_decay_bwd, v_ret_gn_g, v_conv_dw, v_conv_b, v_conv_ln_g, v_conv_ln_b, v_final_g):
    return _step(dict(locals()))
```
